```python
import jax, jax.numpy as jnp
from jax import lax
import numpy as np

D_MODEL = 1024
BATCH = 8
SEQ = 4096
DEPTH = 2

HEAD_DIM = 64
ROPE_THETA = 500000.0
PARTIAL_ROT = HEAD_DIM // 4
NORM_EPS = 1e-6
Q_BLOCK = 128

NSA_HEADS = 8
NSA_GROUPS = 2
NSA_REP = NSA_HEADS // NSA_GROUPS
CMP_LEN = 32
CMP_STRIDE = 16
CMP_HID = 4 * HEAD_DIM
SLC_LEN = 64
SLC_TOPK = 16
WIN = 512
NSA_Q_BLOCK = 32

FOX_HEADS = 8

MLA_HEADS = 8
MLA_Q_LORA = 384
MLA_KV_LORA = 256
MLA_NOPE = 64
MLA_ROPE = 32
MLA_V = 64

D_FF = 2816
CONV_W = 3

NSA_W = NSA_HEADS * HEAD_DIM
NSA_KV_W = NSA_GROUPS * HEAD_DIM
FOX_W = FOX_HEADS * HEAD_DIM
MLA_W = MLA_HEADS * MLA_V

IN_SIZES = (
    NSA_W,
    6 * NSA_KV_W,
    3 * NSA_HEADS,
    3 * FOX_W,
    FOX_HEADS,
    MLA_Q_LORA,
    MLA_KV_LORA,
    MLA_ROPE,
    3 * D_MODEL,
)
N_IN = NSA_W + 6 * NSA_KV_W + 3 * NSA_HEADS + 3 * FOX_W + FOX_HEADS + MLA_Q_LORA + MLA_KV_LORA + MLA_ROPE + 3 * D_MODEL

kernel_name = "hybrid_nsa_fox_mla_gated_convffn"


def rms_norm(x, g):
    xf = x.astype(jnp.float32)
    y = xf * lax.rsqrt(jnp.mean(xf * xf, axis=-1, keepdims=True) + NORM_EPS)
    return (y * g.astype(jnp.float32)).astype(x.dtype)


def rotary(x, positions, rot_dim):
    half = rot_dim // 2
    inv_freq = ROPE_THETA ** (-jnp.arange(half, dtype=jnp.float32) / half)
    ang = positions.astype(jnp.float32)[..., None] * inv_freq
    ang = ang.reshape(ang.shape[:2] + (1,) * (x.ndim - 3) + (half,))
    cos, sin = jnp.cos(ang), jnp.sin(ang)
    xr = x[..., :rot_dim].astype(jnp.float32)
    x1, x2 = xr[..., :half], xr[..., half:]
    rot = jnp.concatenate([x1 * cos - x2 * sin, x2 * cos + x1 * sin], axis=-1).astype(x.dtype)
    return jnp.concatenate([rot, x[..., rot_dim:]], axis=-1)


def masked_softmax(s, mask):
    s = jnp.where(mask, s.astype(jnp.float32), -jnp.inf)
    m = jnp.max(s, axis=-1, keepdims=True)
    m = jnp.where(jnp.isfinite(m), m, 0.0)
    p = jnp.exp(s - m)
    return p / jnp.maximum(jnp.sum(p, axis=-1, keepdims=True), 1e-30)


def causal_attention_blocked(q, k, v, scale, log_decay=None):
    B, S, H, _ = q.shape
    dv = v.shape[-1]
    k_pos = jnp.arange(S)
    F = None if log_decay is None else jnp.transpose(log_decay, (0, 2, 1))

    def block(i):
        t0 = i * Q_BLOCK
        qb = lax.dynamic_slice_in_dim(q, t0, Q_BLOCK, axis=1)
        s = jnp.einsum('bqhd,bshd->bhqs', qb, k).astype(jnp.float32) * scale
        if F is not None:
            fq = lax.dynamic_slice_in_dim(F, t0, Q_BLOCK, axis=2)
            s = s + fq[..., None] - F[:, :, None, :]
        q_pos = t0 + jnp.arange(Q_BLOCK)
        p = masked_softmax(s, k_pos[None, :] <= q_pos[:, None])
        return jnp.einsum('bhqs,bshd->bqhd', p.astype(v.dtype), v)

    out = lax.map(block, jnp.arange(S // Q_BLOCK))
    return jnp.moveaxis(out, 0, 1).reshape(B, S, H, dv)


def nsa_mixer(q, kv, gate_logits, positions, pe_k, w1_k, w2_k, pe_v, w1_v, w2_v):
    B, S, _ = q.shape
    G, R, Dh = NSA_GROUPS, NSA_REP, HEAD_DIM
    scale = Dh ** -0.5
    q = q.reshape(B, S, G, R, Dh)
    k_cmp, v_cmp, k_slc, v_slc, k_win, v_win = [t.reshape(B, S, G, Dh) for t in jnp.split(kv, 6, axis=-1)]

    n_cmp = (S - CMP_LEN) // CMP_STRIDE + 1
    blk_idx = CMP_STRIDE * np.arange(n_cmp)[:, None] + np.arange(CMP_LEN)[None, :]

    def compress(t, pe, w1, w2):
        blocks = t[:, blk_idx] + pe[:, None, :]
        blocks = jnp.transpose(blocks, (0, 1, 3, 2, 4)).reshape(B, n_cmp, G, CMP_LEN * Dh)
        return jax.nn.gelu(blocks @ w1) @ w2

    kc = compress(k_cmp, pe_k, w1_k, w2_k)
    vc = compress(v_cmp, pe_v, w1_v, w2_v)
    t_np = np.arange(S)
    cmp_end = CMP_STRIDE * np.arange(n_cmp) + CMP_LEN - 1
    s_cmp = jnp.einsum('bsgrd,bngd->bgrsn', q, kc).astype(jnp.float32) * scale
    p_cmp = masked_softmax(s_cmp, cmp_end[None, :] <= t_np[:, None])
    o_cmp = jnp.einsum('bgrsn,bngd->bsgrd', p_cmp.astype(vc.dtype), vc)

    n_slc = S // SLC_LEN
    c0 = CMP_STRIDE * np.arange(n_cmp)[:, None]
    s0 = SLC_LEN * np.arange(n_slc)[None, :]
    overlap = np.clip(np.minimum(c0 + CMP_LEN, s0 + SLC_LEN) - np.maximum(c0, s0), 0, None) / CMP_LEN
    imp = jnp.einsum('bgrsn,nj->bgsj', p_cmp, jnp.asarray(overlap, jnp.float32))
    cur = t_np // SLC_LEN
    j = np.arange(n_slc)
    forced = (j[None, :] == 0) | (j[None, :] == cur[:, None]) | (j[None, :] == cur[:, None] - 1)
    future = j[None, :] > cur[:, None]
    imp = jnp.where(forced, 1e9, jnp.where(future, -1e9, imp))
    n_top = min(SLC_TOPK, n_slc)
    _, sel = lax.top_k(imp, n_top)

    q_r = rotary(q, positions, PARTIAL_ROT)
    k_slc = rotary(k_slc, positions, PARTIAL_ROT)
    k_win = rotary(k_win, positions, PARTIAL_ROT)
    k_slc_b = jnp.transpose(k_slc.reshape(B, n_slc, SLC_LEN, G, Dh), (0, 3, 1, 2, 4))
    v_slc_b = jnp.transpose(v_slc.reshape(B, n_slc, SLC_LEN, G, Dh), (0, 3, 1, 2, 4))
    pad = ((0, 0), (WIN, 0), (0, 0), (0, 0))
    k_win_p = jnp.pad(k_win, pad)
    v_win_p = jnp.pad(v_win, pad)
    b_ix = jnp.arange(B)[:, None, None, None]
    g_ix = jnp.arange(G)[None, :, None, None]
    QB = NSA_Q_BLOCK

    def block(i):
        t0 = i * QB
        qb = lax.dynamic_slice_in_dim(q_r, t0, QB, axis=1)
        tq = t0 + jnp.arange(QB)
        idx = lax.dynamic_slice_in_dim(sel, t0, QB, axis=2)
        kg = k_slc_b[b_ix, g_ix, idx]
        vg = v_slc_b[b_ix, g_ix, idx]
        s = jnp.einsum('bqgrd,bgqnld->bgrqnl', qb, kg).astype(jnp.float32) * scale
        s_pos = idx[..., None] * SLC_LEN + jnp.arange(SLC_LEN)
        m_sel = (s_pos <= tq[None, None, :, None, None]).reshape(B, G, 1, QB, n_top * SLC_LEN)
        p = masked_softmax(s.reshape(B, G, R, QB, n_top * SLC_LEN), m_sel)
        o_s = jnp.einsum('bgrqk,bgqkd->bqgrd', p.astype(vg.dtype), vg.reshape(B, G, QB, n_top * SLC_LEN, Dh))
        kw = lax.dynamic_slice_in_dim(k_win_p, t0, QB + WIN, axis=1)
        vw = lax.dynamic_slice_in_dim(v_win_p, t0, QB + WIN, axis=1)
        s_w = jnp.einsum('bqgrd,bkgd->bgrqk', qb, kw).astype(jnp.float32) * scale
        kpos = t0 - WIN + jnp.arange(QB + WIN)
        diff = tq[:, None] - kpos[None, :]
        m_win = (kpos[None, :] >= 0) & (diff >= 0) & (diff < WIN)
        p_w = masked_softmax(s_w, m_win)
        o_w = jnp.einsum('bgrqk,bkgd->bqgrd', p_w.astype(vw.dtype), vw)
        return o_s, o_w

    o_slc, o_win = lax.map(block, jnp.arange(S // QB))
    o_slc = jnp.moveaxis(o_slc, 0, 1).reshape(B, S, G, R, Dh)
    o_win = jnp.moveaxis(o_win, 0, 1).reshape(B, S, G, R, Dh)
    g = jax.nn.sigmoid(gate_logits.astype(jnp.float32)).astype(q.dtype).reshape(B, S, G, R, 3)
    o = g[..., 0:1] * o_cmp + g[..., 1:2] * o_slc + g[..., 2:3] * o_win
    return o.reshape(B, S, NSA_W)


def fox_mixer(qkv, f_logits, b_forget):
    B, S, _ = qkv.shape
    q, k, v = [t.reshape(B, S, FOX_HEADS, HEAD_DIM) for t in jnp.split(qkv, 3, axis=-1)]
    log_f = jax.nn.log_sigmoid(f_logits.astype(jnp.float32) + b_forget.astype(jnp.float32))
    F = jnp.cumsum(log_f, axis=1)
    o = causal_attention_blocked(q, k, v, HEAD_DIM ** -0.5, F)
    return o.reshape(B, S, FOX_W)


def mla_mixer(c_q, c_kv, k_rope, positions, q_norm, w_uq, kv_norm, w_ukv):
    B, S, _ = c_q.shape
    H = MLA_HEADS
    q = (rms_norm(c_q, q_norm) @ w_uq).reshape(B, S, H, MLA_NOPE + MLA_ROPE)
    kv = (rms_norm(c_kv, kv_norm) @ w_ukv).reshape(B, S, H, MLA_NOPE + MLA_V)
    q_nope, q_pe = q[..., :MLA_NOPE], rotary(q[..., MLA_NOPE:], positions, MLA_ROPE)
    k_nope, v = kv[..., :MLA_NOPE], kv[..., MLA_NOPE:]
    k_pe = rotary(k_rope, positions, MLA_ROPE)
    qf = jnp.concatenate([q_nope, q_pe], axis=-1)
    kf = jnp.concatenate([k_nope, jnp.broadcast_to(k_pe[:, :, None, :], (B, S, H, MLA_ROPE))], axis=-1)
    o = causal_attention_blocked(qf, kf, v, (MLA_NOPE + MLA_ROPE) ** -0.5)
    return o.reshape(B, S, MLA_W)


def hybrid_mixer(x, positions, norm, w_in, b_forget, pe_k, w1_k, w2_k, pe_v, w1_v, w2_v,
                 q_norm, w_uq, kv_norm, w_ukv, w_br_nsa, w_br_fox, w_br_mla, w_out):
    B, S, D = x.shape
    h = rms_norm(x, norm)
    proj = h @ w_in
    (nsa_q, nsa_kv, nsa_g, fox_qkv, fox_f, mla_cq, mla_ckv, mla_kr, merge_g) = jnp.split(
        proj, np.cumsum(IN_SIZES)[:-1].tolist(), axis=-1)
    o_nsa = nsa_mixer(nsa_q, nsa_kv, nsa_g, positions, pe_k, w1_k, w2_k, pe_v, w1_v, w2_v)
    o_fox = fox_mixer(fox_qkv, fox_f, b_forget)
    o_mla = mla_mixer(mla_cq, mla_ckv, mla_kr, positions, q_norm, w_uq, kv_norm, w_ukv)
    gates = jax.nn.sigmoid(merge_g.astype(jnp.float32)).astype(x.dtype).reshape(B, S, 3, D)
    merged = (gates[:, :, 0] * (o_nsa @ w_br_nsa)
              + gates[:, :, 1] * (o_fox @ w_br_fox)
              + gates[:, :, 2] * (o_mla @ w_br_mla))
    return merged @ w_out


def conv_ffn(x, norm, w_up, conv_w, conv_b, w_down):
    S = x.shape[1]
    h = rms_norm(x, norm)
    u, v = jnp.split(h @ w_up, 2, axis=-1)
    up = jnp.pad(u, ((0, 0), (CONV_W - 1, 0), (0, 0)))
    uc = sum(conv_w[k] * up[:, k:k + S] for k in range(CONV_W)) + conv_b
    return (jax.nn.silu(uc) * v) @ w_down


def setup_inputs(seed: int = 0) -> dict:
    key = jax.random.key(seed)
    ks = jax.random.split(key, 32)
    f32 = jnp.float32
    L = DEPTH

    def nrm(i, shape, scale):
        return jax.random.normal(ks[i], shape, f32) * scale

    x = nrm(0, (BATCH, SEQ, D_MODEL), 1.0)
    positions = (jax.random.randint(ks[1], (BATCH, 1), 0, 1024, dtype=jnp.int32)
                 + jnp.arange(SEQ, dtype=jnp.int32)[None, :])
    return {
        "x": x,
        "positions": positions,
        "mix_norm": 1.0 + nrm(2, (L, D_MODEL), 0.02),
        "w_in": nrm(3, (L, D_MODEL, N_IN), D_MODEL ** -0.5),
        "b_forget": jax.random.uniform(ks[4], (L, FOX_HEADS), f32, 2.0, 6.0),
        "cmp_pe_k": nrm(5, (L, CMP_LEN, HEAD_DIM), 0.1),
        "cmp_w1_k": nrm(6, (L, CMP_LEN * HEAD_DIM, CMP_HID), (CMP_LEN * HEAD_DIM) ** -0.5),
        "cmp_w2_k": nrm(7, (L, CMP_HID, HEAD_DIM), CMP_HID ** -0.5),
        "cmp_pe_v": nrm(8, (L, CMP_LEN, HEAD_DIM), 0.1),
        "cmp_w1_v": nrm(9, (L, CMP_LEN * HEAD_DIM, CMP_HID), (CMP_LEN * HEAD_DIM) ** -0.5),
        "cmp_w2_v": nrm(10, (L, CMP_HID, HEAD_DIM), CMP_HID ** -0.5),
        "mla_q_norm": 1.0 + nrm(11, (L, MLA_Q_LORA), 0.02),
        "mla_w_uq": nrm(12, (L, MLA_Q_LORA, MLA_HEADS * (MLA_NOPE + MLA_ROPE)), MLA_Q_LORA ** -0.5),
        "mla_kv_norm": 1.0 + nrm(13, (L, MLA_KV_LORA), 0.02),
        "mla_w_ukv": nrm(14, (L, MLA_KV_LORA, MLA_HEADS * (MLA_NOPE + MLA_V)), MLA_KV_LORA ** -0.5),
        "w_br_nsa": nrm(15, (L, NSA_W, D_MODEL), NSA_W ** -0.5),
        "w_br_fox": nrm(16, (L, FOX_W, D_MODEL), FOX_W ** -0.5),
        "w_br_mla": nrm(17, (L, MLA_W, D_MODEL), MLA_W ** -0.5),
        "w_out": nrm(18, (L, D_MODEL, D_MODEL), D_MODEL ** -0.5),
        "ffn_norm": 1.0 + nrm(19, (L, D_MODEL), 0.02),
        "w_up": nrm(20, (L, D_MODEL, 2 * D_FF), D_MODEL ** -0.5),
        "conv_w": nrm(21, (L, CONV_W, D_FF), CONV_W ** -0.5),
        "conv_b": nrm(22, (L, D_FF), 0.02),
        "w_down": nrm(23, (L, D_FF, D_MODEL), D_FF ** -0.5),
        "final_norm": 1.0 + nrm(24, (D_MODEL,), 0.02),
    }


def reference(x, positions, mix_norm, w_in, b_forget, cmp_pe_k, cmp_w1_k, cmp_w2_k,
              cmp_pe_v, cmp_w1_v, cmp_w2_v, mla_q_norm, mla_w_uq, mla_kv_norm, mla_w_ukv,
              w_br_nsa, w_br_fox, w_br_mla, w_out, ffn_norm, w_up, conv_w, conv_b, w_down,
              final_norm):
    h = x
    for l in range(DEPTH):
        h = h + hybrid_mixer(h, positions, mix_norm[l], w_in[l], b_forget[l],
                             cmp_pe_k[l], cmp_w1_k[l], cmp_w2_k[l],
                             cmp_pe_v[l], cmp_w1_v[l], cmp_w2_v[l],
                             mla_q_norm[l], mla_w_uq[l], mla_kv_norm[l], mla_w_ukv[l],
                             w_br_nsa[l], w_br_fox[l], w_br_mla[l], w_out[l])
        h = h + conv_ffn(h, ffn_norm[l], w_up[l], conv_w[l], conv_b[l], w_down[l])
    return rms_norm(h, final_norm)
```

```python
import functools

import numpy as np
import jax
import jax.numpy as jnp
from jax import lax
from jax.experimental import pallas as pl
from jax.experimental.pallas import tpu as pltpu

F32 = jnp.float32
BF16 = jnp.bfloat16

HEAD_DIM = 64
ROPE_THETA = 500000.0
PARTIAL_ROT = HEAD_DIM // 4
NORM_EPS = 1e-6
NSA_HEADS = 8
NSA_GROUPS = 2
NSA_REP = NSA_HEADS // NSA_GROUPS
CMP_LEN = 32
CMP_STRIDE = 16
CMP_HID = 4 * HEAD_DIM
SLC_LEN = 64
SLC_TOPK = 16
WIN = 512
FOX_HEADS = 8
MLA_HEADS = 8
MLA_Q_LORA = 384
MLA_KV_LORA = 256
MLA_NOPE = 64
MLA_ROPE = 32
MLA_V = 64
D_FF = 2816
CONV_W = 3

LANES = 128
NEG = -1e30
SEL_NEG = -1e9
VMEM_LIMIT = 52 * 1024 * 1024

TM = 256
TQ = 256


def _cparams(n_axes):
    return pltpu.CompilerParams(dimension_semantics=("arbitrary",) * n_axes,
                                vmem_limit_bytes=VMEM_LIMIT)


def _const_spec(shape):
    nd = len(shape)
    return pl.BlockSpec(shape, lambda *_: (0,) * nd, pipeline_mode=pl.Buffered(1))


def _rms(x, g):
    return x * lax.rsqrt(jnp.mean(x * x, axis=-1, keepdims=True) + NORM_EPS) * g


def _dot(a, b):
    return jnp.dot(a, b, preferred_element_type=F32)


def _dot_nt(a, b):
    return lax.dot_general(a, b, (((1,), (1,)), ((), ())), preferred_element_type=F32)


def _rope_tables_kernel(pos_ref, f16_ref, g16_ref, f32_ref, g32_ref, c16_o, s16_o, c32_o, s32_o):
    p = pos_ref[...].astype(F32)
    a16 = p * f16_ref[...]
    c16_o[...] = jnp.cos(a16)
    s16_o[...] = jnp.sin(a16) * g16_ref[...]
    a32 = p * f32_ref[...]
    c32_o[...] = jnp.cos(a32)
    s32_o[...] = jnp.sin(a32) * g32_ref[...]


def _rope_tables(positions):
    T = positions.size
    half16, half32 = PARTIAL_ROT // 2, MLA_ROPE // 2
    inv16 = ROPE_THETA ** (-jnp.arange(half16, dtype=F32) / half16)
    inv32 = ROPE_THETA ** (-jnp.arange(half32, dtype=F32) / half32)
    lane = np.arange(LANES)
    d = lane % HEAD_DIM
    on16 = d < PARTIAL_ROT
    f16 = jnp.where(on16, inv16[d % half16], 0.0)[None, :]
    g16 = np.where(on16, np.where(d < half16, -1.0, 1.0), 0.0).astype(np.float32)[None, :]
    r = lane - MLA_NOPE
    on32 = (r >= 0) & (r < MLA_ROPE)
    f32 = jnp.where(on32, inv32[r % half32], 0.0)[None, :]
    g32 = np.where(on32, np.where(r < half32, -1.0, 1.0), 0.0).astype(np.float32)[None, :]
    pos = jnp.broadcast_to(positions.reshape(T, 1), (T, LANES))
    tt = 1024
    row = pl.BlockSpec((tt, LANES), lambda i: (i, 0))
    vec = _const_spec((1, LANES))
    out = jax.ShapeDtypeStruct((T, LANES), F32)
    return pl.pallas_call(
        _rope_tables_kernel, grid=(T // tt,),
        in_specs=[row, vec, vec, vec, vec], out_specs=[row] * 4, out_shape=[out] * 4,
        compiler_params=_cparams(1), name="rope_tables",
    )(pos, f16, jnp.asarray(g16), f32, jnp.asarray(g32))


def _rot16(x, cos, sin):
    lane = lax.broadcasted_iota(jnp.int32, x.shape, 1)
    first = (lane % HEAD_DIM) < (PARTIAL_ROT // 2)
    sw = jnp.where(first, pltpu.roll(x, LANES - PARTIAL_ROT // 2, 1), pltpu.roll(x, PARTIAL_ROT // 2, 1))
    return x * cos + sw * sin


def _rot32(x, cos, sin):
    lane = lax.broadcasted_iota(jnp.int32, x.shape, 1)
    first = lane < (MLA_NOPE + MLA_ROPE // 2)
    sw = jnp.where(first, pltpu.roll(x, LANES - MLA_ROPE // 2, 1), pltpu.roll(x, MLA_ROPE // 2, 1))
    return x * cos + sw * sin


def _inproj_kernel(x_ref, g_ref, c16_ref, s16_ref, c32_ref, s32_ref,
                   wq_ref, wkvc_ref, wksl_ref, wvsl_ref, wkwin_ref, wvwin_ref, wgate_ref,
                   wfox_ref, wft_ref, wmla_ref, qnrm_ref, wuq_ref, kvnrm_ref, wuk_ref, wuv_ref, wmg_ref,
                   qn_o, qr_o, kvc_o, ksl_o, vsl_o, kwin_o, vwin_o, gate_o,
                   fq_o, fk_o, fv_o, ft_o, mq_o, mk_o, mv_o, mg_o, *, seq):
    tm = x_ref.shape[0]
    hb = _rms(x_ref[...], g_ref[...]).astype(BF16)
    c16, s16 = c16_ref[...], s16_ref[...]
    c32, s32 = c32_ref[...], s32_ref[...]

    def slabs(z):
        return [z[:, k * LANES:(k + 1) * LANES] for k in range(z.shape[1] // LANES)]

    zq = _dot(hb, wq_ref[...])
    qn_o[...] = zq.astype(BF16)
    for k, z in enumerate(slabs(zq)):
        qr_o[:, k * LANES:(k + 1) * LANES] = _rot16(z, c16, s16).astype(BF16)

    zc = _dot(hb, wkvc_ref[...])
    for k, z in enumerate(slabs(zc)):
        kvc_o[k] = z[:, :HEAD_DIM]

    tok = (pl.program_id(0) * tm) % seq + lax.broadcasted_iota(jnp.int32, (tm, LANES), 0)
    lane = lax.broadcasted_iota(jnp.int32, (tm, LANES), 1)
    onehot = jnp.where(lane - HEAD_DIM == tok // SLC_LEN, 1.0, 0.0)
    zk = _dot(hb, wksl_ref[...])
    for k, z in enumerate(slabs(zk)):
        ksl_o[:, k * LANES:(k + 1) * LANES] = (_rot16(z, c16, s16) + onehot).astype(BF16)
    vsl_o[...] = _dot(hb, wvsl_ref[...]).astype(BF16)

    zw = _dot(hb, wkwin_ref[...])
    for k, z in enumerate(slabs(zw)):
        kwin_o[:, k * LANES:(k + 1) * LANES] = _rot16(z, c16, s16).astype(BF16)
    vwin_o[...] = _dot(hb, wvwin_ref[...]).astype(BF16)

    gate_o[...] = _dot(hb, wgate_ref[...])

    zf = _dot(hb, wfox_ref[...])
    w = fq_o.shape[1]
    fq_o[...] = zf[:, :w].astype(BF16)
    fk_o[...] = zf[:, w:2 * w].astype(BF16)
    fv_o[...] = zf[:, 2 * w:].astype(BF16)
    ft_o[...] = _dot_nt(wft_ref[...], hb)

    zm = _dot(hb, wmla_ref[...])
    cq = _rms(zm[:, :MLA_Q_LORA], qnrm_ref[...]).astype(BF16)
    ckv = _rms(zm[:, MLA_Q_LORA:MLA_Q_LORA + MLA_KV_LORA], kvnrm_ref[...]).astype(BF16)
    kpe = _rot32(zm[:, MLA_Q_LORA + MLA_KV_LORA:], c32, s32)
    zuq = _dot(cq, wuq_ref[...])
    for k, z in enumerate(slabs(zuq)):
        mq_o[:, k * LANES:(k + 1) * LANES] = _rot32(z, c32, s32).astype(BF16)
    zuk = _dot(ckv, wuk_ref[...])
    for k, z in enumerate(slabs(zuk)):
        mk_o[:, k * LANES:(k + 1) * LANES] = (z + kpe).astype(BF16)
    mv_o[...] = _dot(ckv, wuv_ref[...]).astype(BF16)

    mg_o[...] = jax.nn.sigmoid(_dot(hb, wmg_ref[...]))


def _pad_halves(w, first_only=True):
    K, n = w.shape[0], w.shape[1] // HEAD_DIM
    w = w.reshape(K, n, HEAD_DIM)
    other = jnp.zeros_like(w) if first_only else w
    return jnp.concatenate([w, other], axis=-1).reshape(K, n * LANES)


def _inproj(x2, tabs, p, seq):
    T, D = x2.shape
    w_in = p["w_in"]
    scale = HEAD_DIM ** -0.5
    o = 0

    def take(n):
        nonlocal o
        w = w_in[:, o:o + n]
        o += n
        return w

    nsa_w, kvw = NSA_HEADS * HEAD_DIM, NSA_GROUPS * HEAD_DIM
    w_q = take(nsa_w)
    w_kc, w_vc, w_ks, w_vs, w_kw, w_vw = [take(kvw) for _ in range(6)]
    w_g = take(3 * NSA_HEADS)
    w_fq, w_fk, w_fv = [take(FOX_HEADS * HEAD_DIM) for _ in range(3)]
    w_ff = take(FOX_HEADS)
    w_cq, w_ckv, w_kr = take(MLA_Q_LORA), take(MLA_KV_LORA), take(MLA_ROPE)
    w_mg = take(3 * D)

    bf = lambda w: w.astype(BF16)
    wq = bf(_pad_halves(w_q * scale))
    wkvc = bf(_pad_halves(jnp.concatenate([w_kc, w_vc], axis=1)))
    wksl = bf(_pad_halves(w_ks))
    wvsl = bf(_pad_halves(w_vs, first_only=False))
    wkwin = bf(_pad_halves(w_kw))
    wvwin = bf(_pad_halves(w_vw, first_only=False))
    wgate = bf(jnp.pad(w_g, ((0, 0), (0, LANES - w_g.shape[1]))))
    wfox = bf(jnp.concatenate([w_fq * scale, w_fk, w_fv], axis=1))
    wft = bf(w_ff.T)
    w_kr_slab = jnp.pad(w_kr, ((0, 0), (MLA_NOPE, LANES - MLA_NOPE - MLA_ROPE)))
    wmla = bf(jnp.concatenate([w_cq, w_ckv, w_kr_slab], axis=1))
    dq = MLA_NOPE + MLA_ROPE
    wuq = (p["mla_w_uq"] * dq ** -0.5).reshape(MLA_Q_LORA, MLA_HEADS, dq)
    wuq = bf(jnp.pad(wuq, ((0, 0), (0, 0), (0, LANES - dq))).reshape(MLA_Q_LORA, MLA_HEADS * LANES))
    wukv = p["mla_w_ukv"].reshape(MLA_KV_LORA, MLA_HEADS, MLA_NOPE + MLA_V)
    wuk = bf(_pad_halves(wukv[:, :, :MLA_NOPE].reshape(MLA_KV_LORA, MLA_HEADS * MLA_NOPE)))
    wuv = bf(wukv[:, :, MLA_NOPE:].reshape(MLA_KV_LORA, MLA_HEADS * MLA_V))
    wmg = bf(w_mg)
    row = lambda a: a.reshape(1, -1).astype(F32)

    weights = [wq, wkvc, wksl, wvsl, wkwin, wvwin, wgate, wfox, wft, wmla,
               row(p["mla_q_norm"]), wuq, row(p["mla_kv_norm"]), wuk, wuv, wmg]
    n = T // TM
    tok = lambda c: pl.BlockSpec((TM, c), lambda i: (i, 0))
    in_specs = ([tok(D), _const_spec((1, D))] + [tok(LANES)] * 4
                + [_const_spec(w.shape) for w in weights])
    outs = [
        (NSA_HEADS * LANES, BF16), (NSA_HEADS * LANES, BF16), None,
        (NSA_GROUPS * LANES, BF16), (NSA_GROUPS * LANES, BF16),
        (NSA_GROUPS * LANES, BF16), (NSA_GROUPS * LANES, BF16), (LANES, F32),
        (FOX_HEADS * HEAD_DIM, BF16), (FOX_HEADS * HEAD_DIM, BF16), (FOX_HEADS * HEAD_DIM, BF16), None,
        (MLA_HEADS * LANES, BF16), (MLA_HEADS * LANES, BF16), (MLA_HEADS * MLA_V, BF16), (3 * D, F32)]
    out_shape, out_specs = [], []
    for k, spec in enumerate(outs):
        if k == 2:
            out_shape.append(jax.ShapeDtypeStruct((4, T, HEAD_DIM), F32))
            out_specs.append(pl.BlockSpec((4, TM, HEAD_DIM), lambda i: (0, i, 0)))
        elif k == 11:
            out_shape.append(jax.ShapeDtypeStruct((FOX_HEADS, T), F32))
            out_specs.append(pl.BlockSpec((FOX_HEADS, TM), lambda i: (0, i)))
        else:
            out_shape.append(jax.ShapeDtypeStruct((T, spec[0]), spec[1]))
            out_specs.append(tok(spec[0]))
    return pl.pallas_call(
        functools.partial(_inproj_kernel, seq=seq), grid=(n,),
        in_specs=in_specs, out_specs=out_specs, out_shape=out_shape,
        compiler_params=_cparams(1), name="inproj",
    )(x2, row(p["mix_norm"]), *tabs, *weights)


def _fox_decay_kernel(ft_ref, b_ref, tri_ref, o_ref, *, chunk):
    seq = ft_ref.shape[1]
    tri = tri_ref[...]
    carry = jnp.zeros((ft_ref.shape[0], 1), F32)
    for c in range(seq // chunk):
        z = ft_ref[:, c * chunk:(c + 1) * chunk] + b_ref[...]
        lf = jnp.minimum(z, 0.0) - jnp.log1p(jnp.exp(-jnp.abs(z)))
        cs = jnp.dot(lf, tri, preferred_element_type=F32, precision=lax.Precision.HIGHEST) + carry
        o_ref[:, c * chunk:(c + 1) * chunk] = cs
        carry = cs[:, chunk - 1:chunk]


def _fox_decay(ft, b_forget, batch, seq):
    chunk = 512
    tri = jnp.asarray(np.triu(np.ones((chunk, chunk), np.float32)))
    H = ft.shape[0]
    return pl.pallas_call(
        functools.partial(_fox_decay_kernel, chunk=chunk), grid=(batch,),
        in_specs=[pl.BlockSpec((H, seq), lambda b: (0, b)), _const_spec((H, 1)),
                  _const_spec((chunk, chunk))],
        out_specs=pl.BlockSpec((H, seq), lambda b: (0, b)),
        out_shape=jax.ShapeDtypeStruct(ft.shape, F32),
        compiler_params=_cparams(1), name="fox_decay",
    )(ft, b_forget.reshape(H, 1).astype(F32), tri)


def _softmax_step(s, v, m_ref, l_ref, acc_ref, idx):
    m_old = m_ref[idx]
    m_new = jnp.maximum(m_old, jnp.max(s, axis=-1, keepdims=True))
    alpha = jnp.exp(m_old - m_new)
    p = jnp.exp(s - m_new)
    l_ref[idx] = alpha * l_ref[idx] + jnp.sum(p, axis=-1, keepdims=True)
    acc_ref[idx] = alpha * acc_ref[idx] + _dot(p.astype(BF16), v)
    m_ref[idx] = m_new


def _init_state(m_ref, l_ref, acc_ref):
    m_ref[...] = jnp.full(m_ref.shape, NEG, F32)
    l_ref[...] = jnp.zeros(l_ref.shape, F32)
    acc_ref[...] = jnp.zeros(acc_ref.shape, F32)


def _pair_attn_kernel(*refs, tq, per_head_slabs, has_bias):
    if has_bias:
        q_ref, k_ref, v_ref, f_ref, o_ref, m_ref, l_ref, acc_ref = refs
    else:
        q_ref, k_ref, v_ref, o_ref, m_ref, l_ref, acc_ref = refs
        f_ref = None
    pr, i = pl.program_id(1), pl.program_id(2)
    _init_state(m_ref, l_ref, acc_ref)
    lane = lax.broadcasted_iota(jnp.int32, (tq, LANES), 1)
    qs = []
    for hh in range(2):
        if per_head_slabs:
            qs.append(q_ref[:, hh * LANES:(hh + 1) * LANES])
        else:
            q = q_ref[...]
            qs.append(jnp.where((lane >= HEAD_DIM) == (hh == 1), q, jnp.zeros_like(q)))

    def tile(j, mask):
        r0 = pl.multiple_of(j * tq, tq)
        v = v_ref[pl.ds(r0, tq), :]
        for hh in range(2):
            if per_head_slabs:
                k = k_ref[pl.ds(r0, tq), hh * LANES:(hh + 1) * LANES]
            else:
                k = k_ref[pl.ds(r0, tq), :]
            s = _dot_nt(qs[hh], k)
            if has_bias:
                s = s - f_ref[pl.ds(2 * pr + hh, 1), pl.ds(r0, tq)]
            if mask is not None:
                s = jnp.where(mask, s, NEG)
            _softmax_step(s, v, m_ref, l_ref, acc_ref, hh)

    def body(j, c):
        tile(j, None)
        return c

    lax.fori_loop(0, i, body, 0)
    causal = (lax.broadcasted_iota(jnp.int32, (tq, tq), 1)
              <= lax.broadcasted_iota(jnp.int32, (tq, tq), 0))
    tile(i, causal)
    o0 = acc_ref[0] / l_ref[0]
    o1 = acc_ref[1] / l_ref[1]
    o_ref[...] = jnp.where(lane < HEAD_DIM, o0, o1).astype(o_ref.dtype)


def _pair_attention(q, k, v, fdec, batch, seq, per_head_slabs):
    T = q.shape[0]
    nq = seq // TQ
    qw = 2 * LANES if per_head_slabs else LANES
    npair = v.shape[1] // LANES
    in_specs = [pl.BlockSpec((TQ, qw), lambda b, p, i: (b * nq + i, p)),
                pl.BlockSpec((seq, qw), lambda b, p, i: (b, p)),
                pl.BlockSpec((seq, LANES), lambda b, p, i: (b, p))]
    args = [q, k, v]
    if fdec is not None:
        in_specs.append(pl.BlockSpec((fdec.shape[0], seq), lambda b, p, i: (0, b)))
        args.append(fdec)
    return pl.pallas_call(
        functools.partial(_pair_attn_kernel, tq=TQ, per_head_slabs=per_head_slabs,
                          has_bias=fdec is not None),
        grid=(batch, npair, nq), in_specs=in_specs,
        out_specs=pl.BlockSpec((TQ, LANES), lambda b, p, i: (b * nq + i, p)),
        out_shape=jax.ShapeDtypeStruct((T, npair * LANES), BF16),
        scratch_shapes=[pltpu.VMEM((2, TQ, 1), F32), pltpu.VMEM((2, TQ, 1), F32),
                        pltpu.VMEM((2, TQ, LANES), F32)],
        compiler_params=_cparams(3), name="mla_attn" if per_head_slabs else "fox_attn",
    )(*args)


def _gelu_tanh(x):
    return 0.5 * x * (1.0 + jnp.tanh(np.float32(np.sqrt(2.0 / np.pi)) * (x + 0.044715 * (x * x * x))))


def _compress_kernel(x_ref, pe_ref, w1_ref, w2_ref, o_ref):
    x = x_ref[0]
    half = x.shape[1]
    top = _dot((x + pe_ref[0, :, :half]).astype(BF16), w1_ref[0, :half, :])
    bot = _dot((x + pe_ref[0, :, half:]).astype(BF16), w1_ref[0, half:, :])
    hid = top + pltpu.roll(bot, bot.shape[0] - 1, 0)
    o_ref[0, 0] = _dot(_gelu_tanh(hid).astype(BF16), w2_ref[0]).astype(o_ref.dtype)


def _compress(kvc, p, batch, seq):
    T = kvc.shape[1]
    rows = seq // CMP_STRIDE
    x = kvc.reshape(4, T // CMP_STRIDE, CMP_STRIDE * HEAD_DIM)
    pe = jnp.stack([p["cmp_pe_k"].reshape(1, -1), p["cmp_pe_v"].reshape(1, -1)]).astype(F32)
    w1 = jnp.stack([p["cmp_w1_k"], p["cmp_w1_v"]]).astype(BF16)
    w2k = jnp.pad(p["cmp_w2_k"], ((0, 0), (0, LANES - HEAD_DIM)))
    w2v = jnp.concatenate([p["cmp_w2_v"], p["cmp_w2_v"]], axis=1)
    w2 = jnp.stack([w2k, w2v]).astype(BF16)
    return pl.pallas_call(
        _compress_kernel, grid=(4, batch),
        in_specs=[pl.BlockSpec((1, rows, x.shape[2]), lambda j, b: (j, b, 0)),
                  pl.BlockSpec((1,) + pe.shape[1:], lambda j, b: (j // 2, 0, 0)),
                  pl.BlockSpec((1,) + w1.shape[1:], lambda j, b: (j // 2, 0, 0)),
                  pl.BlockSpec((1,) + w2.shape[1:], lambda j, b: (j // 2, 0, 0))],
        out_specs=pl.BlockSpec((1, 1, rows, LANES), lambda j, b: (b, j, 0, 0)),
        out_shape=jax.ShapeDtypeStruct((batch, 4, rows, LANES), BF16),
        compiler_params=_cparams(2), name="nsa_compress",
    )(x, pe, w1, w2)


def _cmp_select_kernel(qn_ref, qr_ref, kc_ref, vc_ref, ovt_ref, o_ref, qa_ref, *, tq):
    i = pl.program_id(2)
    n_cmp = kc_ref.shape[2]
    kc, vc = kc_ref[0, 0], vc_ref[0, 0]
    t_row = i * tq + lax.broadcasted_iota(jnp.int32, (tq, n_cmp), 0)
    blk_end = CMP_STRIDE * lax.broadcasted_iota(jnp.int32, (tq, n_cmp), 1) + (CMP_LEN - 1)
    vis = blk_end <= t_row
    lane = lax.broadcasted_iota(jnp.int32, (tq, LANES), 1)
    psum = jnp.zeros((tq, n_cmp), F32)
    outs = []
    for r in range(NSA_REP):
        s = jnp.where(vis, _dot_nt(qn_ref[:, r * LANES:(r + 1) * LANES], kc), NEG)
        m = jnp.max(s, axis=-1, keepdims=True)
        e = jnp.where(vis, jnp.exp(s - m), 0.0)
        pn = e / jnp.maximum(jnp.sum(e, axis=-1, keepdims=True), 1e-30)
        psum = psum + pn
        outs.append(_dot(pn.astype(BF16), vc))
    for k in range(NSA_REP // 2):
        o_ref[:, k * LANES:(k + 1) * LANES] = jnp.where(lane < HEAD_DIM, outs[2 * k], outs[2 * k + 1])

    p_hi = psum.astype(BF16)
    p_lo = (psum - p_hi.astype(F32)).astype(BF16)
    ovt = ovt_ref[...]
    imp = _dot_nt(ovt, p_hi) + _dot_nt(ovt, p_lo)
    n_slc = LANES - HEAD_DIM
    jb = lax.broadcasted_iota(jnp.int32, (LANES, tq), 0) - HEAD_DIM
    cur = (i * tq + lax.broadcasted_iota(jnp.int32, (LANES, tq), 1)) // SLC_LEN
    forced = (jb == 0) | (jb == cur) | (jb == cur - 1)
    imp = jnp.where(forced, 1e9, jnp.where(jb > cur, -1e9, imp))
    groups = [imp[HEAD_DIM + 8 * v:HEAD_DIM + 8 * (v + 1)] for v in range(n_slc // 8)]
    sub = lax.broadcasted_iota(jnp.int32, (8, tq), 0)
    ranks = [jnp.zeros((8, tq), F32) for _ in groups]
    for a in range(n_slc):
        row = jnp.broadcast_to(groups[a // 8][a % 8:a % 8 + 1], (8, tq))
        for v in range(len(groups)):
            if v > a // 8:
                inc = jnp.where(row >= groups[v], 1.0, 0.0)
            elif v < a // 8:
                inc = jnp.where(row > groups[v], 1.0, 0.0)
            else:
                tie = jnp.where(sub > a % 8, 1.0, 0.0)
                inc = jnp.where(row > groups[v], 1.0, jnp.where(row == groups[v], tie, 0.0))
            ranks[v] = ranks[v] + inc
    bias_t = jnp.concatenate(
        [jnp.zeros((HEAD_DIM, tq), F32)]
        + [jnp.where(rk < SLC_TOPK, 0.0, SEL_NEG) for rk in ranks], axis=0)
    bias = bias_t.T
    for r in range(NSA_REP):
        qa_ref[:, r * LANES:(r + 1) * LANES] = (
            qr_ref[:, r * LANES:(r + 1) * LANES].astype(F32) + bias).astype(BF16)


def _overlap_t(seq):
    n_cmp = (seq - CMP_LEN) // CMP_STRIDE + 1
    n_slc = seq // SLC_LEN
    assert n_slc == LANES - HEAD_DIM, "selection blocks must fill the bias half of a slab"
    c0 = CMP_STRIDE * np.arange(n_cmp)[:, None]
    s0 = SLC_LEN * np.arange(n_slc)[None, :]
    ov = np.clip(np.minimum(c0 + CMP_LEN, s0 + SLC_LEN) - np.maximum(c0, s0), 0, None) / CMP_LEN
    out = np.zeros((LANES, seq // CMP_STRIDE), np.float32)
    out[HEAD_DIM:, :n_cmp] = ov.T
    return jnp.asarray(out, BF16)


def _cmp_select(qn, qr, cmp_kv, batch, seq):
    T = qn.shape[0]
    nq = seq // TQ
    rows = seq // CMP_STRIDE
    gw = NSA_REP * LANES
    qspec = pl.BlockSpec((TQ, gw), lambda b, g, i: (b * nq + i, g))
    return pl.pallas_call(
        functools.partial(_cmp_select_kernel, tq=TQ), grid=(batch, NSA_GROUPS, nq),
        in_specs=[qspec, qspec,
                  pl.BlockSpec((1, 1, rows, LANES), lambda b, g, i: (b, g, 0, 0)),
                  pl.BlockSpec((1, 1, rows, LANES), lambda b, g, i: (b, NSA_GROUPS + g, 0, 0)),
                  _const_spec((LANES, rows))],
        out_specs=[pl.BlockSpec((TQ, NSA_REP * HEAD_DIM), lambda b, g, i: (b * nq + i, g)), qspec],
        out_shape=[jax.ShapeDtypeStruct((T, NSA_HEADS * HEAD_DIM), F32),
                   jax.ShapeDtypeStruct((T, NSA_HEADS * LANES), BF16)],
        compiler_params=_cparams(3), name="nsa_cmp_select",
    )(qn, qr, cmp_kv, cmp_kv, _overlap_t(seq))


def _group_attn_kernel(q_ref, k_ref, v_ref, o_ref, m_ref, l_ref, acc_ref, *, tq, window):
    i = pl.program_id(2)
    _init_state(m_ref, l_ref, acc_ref)
    q = jnp.concatenate([q_ref[:, r * LANES:(r + 1) * LANES] for r in range(NSA_REP)], axis=0)
    rows = NSA_REP * tq
    t_q = i * tq + lax.broadcasted_iota(jnp.int32, (rows, tq), 0) % tq
    col = lax.broadcasted_iota(jnp.int32, (rows, tq), 1)

    def tile(j, mask):
        r0 = pl.multiple_of(jnp.maximum(j, 0) * tq, tq)
        s = _dot_nt(q, k_ref[pl.ds(r0, tq), :])
        if mask is not None:
            s = jnp.where(mask, s, NEG)
        _softmax_step(s, v_ref[pl.ds(r0, tq), :], m_ref, l_ref, acc_ref, 0)

    if window:
        for dj in range(WIN // tq, -1, -1):
            j = i - dj
            t_k = j * tq + col
            tile(j, (t_k >= 0) & (t_k <= t_q) & (t_q - t_k < WIN))
    else:
        def body(j, c):
            tile(j, None)
            return c
        lax.fori_loop(0, i, body, 0)
        tile(i, i * tq + col <= t_q)

    o = acc_ref[0] / l_ref[0]
    lane = lax.broadcasted_iota(jnp.int32, (tq, LANES), 1)
    for k in range(NSA_REP // 2):
        a, b = o[2 * k * tq:(2 * k + 1) * tq], o[(2 * k + 1) * tq:(2 * k + 2) * tq]
        o_ref[:, k * LANES:(k + 1) * LANES] = jnp.where(lane < HEAD_DIM, a, b)


def _group_attention(qa, k, v, batch, seq, window):
    T = qa.shape[0]
    nq = seq // TQ
    return pl.pallas_call(
        functools.partial(_group_attn_kernel, tq=TQ, window=window), grid=(batch, NSA_GROUPS, nq),
        in_specs=[pl.BlockSpec((TQ, NSA_REP * LANES), lambda b, g, i: (b * nq + i, g)),
                  pl.BlockSpec((seq, LANES), lambda b, g, i: (b, g)),
                  pl.BlockSpec((seq, LANES), lambda b, g, i: (b, g))],
        out_specs=pl.BlockSpec((TQ, NSA_REP * HEAD_DIM), lambda b, g, i: (b * nq + i, g)),
        out_shape=jax.ShapeDtypeStruct((T, NSA_HEADS * HEAD_DIM), F32),
        scratch_shapes=[pltpu.VMEM((1, NSA_REP * TQ, 1), F32), pltpu.VMEM((1, NSA_REP * TQ, 1), F32),
                        pltpu.VMEM((1, NSA_REP * TQ, LANES), F32)],
        compiler_params=_cparams(3), name="nsa_win_attn" if window else "nsa_slc_attn",
    )(qa, k, v)


def _merge_kernel(x_ref, oc_ref, os_ref, ow_ref, gl_ref, of_ref, om_ref, mg_ref,
                  e_ref, wn_ref, wf_ref, wm_ref, wo_ref, o_ref):
    d = x_ref.shape[1]
    nw = oc_ref.shape[1]
    sg = jax.nn.sigmoid(gl_ref[...])
    sg_hi = sg.astype(BF16)
    sg_lo = (sg - sg_hi.astype(F32)).astype(BF16)
    e = e_ref[...]
    g = _dot(sg_hi, e) + _dot(sg_lo, e)
    o_nsa = g[:, :nw] * oc_ref[...] + g[:, nw:2 * nw] * os_ref[...] + g[:, 2 * nw:] * ow_ref[...]
    mg = mg_ref[...]
    merged = (mg[:, :d] * _dot(o_nsa.astype(BF16), wn_ref[...])
              + mg[:, d:2 * d] * _dot(of_ref[...], wf_ref[...])
              + mg[:, 2 * d:] * _dot(om_ref[...], wm_ref[...]))
    o_ref[...] = x_ref[...] + _dot(merged.astype(BF16), wo_ref[...])


def _gate_expand():
    e = np.zeros((LANES, 3 * NSA_HEADS * HEAD_DIM), np.float32)
    for h in range(NSA_HEADS):
        for c in range(3):
            e[h * 3 + c, c * NSA_HEADS * HEAD_DIM + h * HEAD_DIM:
              c * NSA_HEADS * HEAD_DIM + (h + 1) * HEAD_DIM] = 1.0
    return jnp.asarray(e, BF16)


def _merge(x2, o_cmp, o_slc, o_win, gate_l, o_fox, o_mla, mg, p):
    T, D = x2.shape
    tok = lambda a: pl.BlockSpec((TM, a.shape[1]), lambda i: (i, 0))
    acts = [x2, o_cmp, o_slc, o_win, gate_l, o_fox, o_mla, mg]
    weights = [_gate_expand(), p["w_br_nsa"].astype(BF16), p["w_br_fox"].astype(BF16),
               p["w_br_mla"].astype(BF16), p["w_out"].astype(BF16)]
    return pl.pallas_call(
        _merge_kernel, grid=(T // TM,),
        in_specs=[tok(a) for a in acts] + [_const_spec(w.shape) for w in weights],
        out_specs=pl.BlockSpec((TM, D), lambda i: (i, 0)),
        out_shape=jax.ShapeDtypeStruct((T, D), F32),
        compiler_params=_cparams(1), name="merge_out",
    )(*acts, *weights)


def _ffn_kernel(x_ref, g_ref, wu_ref, wv_ref, cw_ref, cb_ref, wd_ref, fin_ref, o_ref, tail_ref,
                *, seq, n_chunks, final):
    tm = x_ref.shape[0]
    x = x_ref[...]
    hb = _rms(x, g_ref[...]).astype(BF16)
    @pl.when((pl.program_id(0) * tm) % seq == 0)
    def _():
        tail_ref[...] = jnp.zeros(tail_ref.shape, F32)

    cw = wu_ref.shape[1] // n_chunks
    row8 = lax.broadcasted_iota(jnp.int32, (8, cw), 0)
    acc = jnp.zeros(x.shape, F32)
    for c in range(n_chunks):
        sl = slice(c * cw, (c + 1) * cw)
        u = _dot(hb, wu_ref[:, sl])
        v = _dot(hb, wv_ref[:, sl])
        prev = tail_ref[:, sl]
        tail_ref[:, sl] = u[tm - 8:]
        u1 = pltpu.roll(u, 1, 0)
        u2 = pltpu.roll(u, 2, 0)
        h1 = jnp.where(row8 < 1, pltpu.roll(prev, 1, 0), u1[:8])
        h2 = jnp.where(row8 < 2, pltpu.roll(prev, 2, 0), u2[:8])
        u1 = jnp.concatenate([h1, u1[8:]], axis=0)
        u2 = jnp.concatenate([h2, u2[8:]], axis=0)
        uc = cw_ref[0:1, sl] * u2 + cw_ref[1:2, sl] * u1 + cw_ref[2:3, sl] * u + cb_ref[:, sl]
        act = (uc * jax.nn.sigmoid(uc) * v).astype(BF16)
        acc = acc + _dot(act, wd_ref[sl, :])
    y = x + acc
    if final:
        y = _rms(y, fin_ref[...])
    o_ref[...] = y


def _ffn(x2, p, final_norm, seq, final):
    T, D = x2.shape
    w_up = p["w_up"]
    wu, wv = w_up[:, :D_FF].astype(BF16), w_up[:, D_FF:].astype(BF16)
    row = lambda a: a.reshape(1, -1).astype(F32)
    cwp = jnp.pad(p["conv_w"].astype(F32), ((0, 8 - CONV_W), (0, 0)))
    weights = [row(p["ffn_norm"]), wu, wv, cwp, row(p["conv_b"]), p["w_down"].astype(BF16),
               row(final_norm)]
    return pl.pallas_call(
        functools.partial(_ffn_kernel, seq=seq, n_chunks=2, final=final), grid=(T // TM,),
        in_specs=[pl.BlockSpec((TM, D), lambda i: (i, 0))] + [_const_spec(w.shape) for w in weights],
        out_specs=pl.BlockSpec((TM, D), lambda i: (i, 0)),
        out_shape=jax.ShapeDtypeStruct((T, D), F32),
        scratch_shapes=[pltpu.VMEM((8, D_FF), F32)],
        compiler_params=_cparams(1), name="conv_ffn",
    )(x2, *weights)


_LAYER_PARAMS = ("mix_norm", "w_in", "b_forget", "cmp_pe_k", "cmp_w1_k", "cmp_w2_k", "cmp_pe_v",
                 "cmp_w1_v", "cmp_w2_v", "mla_q_norm", "mla_w_uq", "mla_kv_norm", "mla_w_ukv",
                 "w_br_nsa", "w_br_fox", "w_br_mla", "w_out", "ffn_norm", "w_up", "conv_w",
                 "conv_b", "w_down")


def _mixer_layer(h, tabs, p, batch, seq):
    (qn, qr, kvc, ksl, vsl, kwin, vwin, gate_l,
     fq, fk, fv, ft, mq, mk, mv, mg) = _inproj(h, tabs, p, seq)
    fdec = _fox_decay(ft, p["b_forget"], batch, seq)
    cmp_kv = _compress(kvc, p, batch, seq)
    o_cmp, qa = _cmp_select(qn, qr, cmp_kv, batch, seq)
    o_slc = _group_attention(qa, ksl, vsl, batch, seq, window=False)
    o_win = _group_attention(qa, kwin, vwin, batch, seq, window=True)
    o_fox = _pair_attention(fq, fk, fv, fdec, batch, seq, per_head_slabs=False)
    o_mla = _pair_attention(mq, mk, mv, None, batch, seq, per_head_slabs=True)
    return _merge(h, o_cmp, o_slc, o_win, gate_l, o_fox, o_mla, mg, p)


def kernel(x, positions, mix_norm, w_in, b_forget, cmp_pe_k, cmp_w1_k, cmp_w2_k, cmp_pe_v, cmp_w1_v,
           cmp_w2_v, mla_q_norm, mla_w_uq, mla_kv_norm, mla_w_ukv, w_br_nsa, w_br_fox, w_br_mla,
           w_out, ffn_norm, w_up, conv_w, conv_b, w_down, final_norm):
    batch, seq, d = x.shape
    stacked = dict(zip(_LAYER_PARAMS, (
        mix_norm, w_in, b_forget, cmp_pe_k, cmp_w1_k, cmp_w2_k, cmp_pe_v, cmp_w1_v, cmp_w2_v,
        mla_q_norm, mla_w_uq, mla_kv_norm, mla_w_ukv, w_br_nsa, w_br_fox, w_br_mla, w_out,
        ffn_norm, w_up, conv_w, conv_b, w_down)))
    depth = w_in.shape[0]
    tabs = _rope_tables(positions)
    h = x.reshape(batch * seq, d)
    for l in range(depth):
        p = {k: v[l] for k, v in stacked.items()}
        h = _mixer_layer(h, tabs, p, batch, seq)
        h = _ffn(h, p, final_norm, seq, final=(l == depth - 1))
    return h.reshape(batch, seq, d)
```

```python
import functools

import numpy as np
import jax
import jax.numpy as jnp
from jax import lax
from jax.experimental import pallas as pl
from jax.experimental.pallas import tpu as pltpu

F32 = jnp.float32
BF16 = jnp.bfloat16

HEAD_DIM = 64
ROPE_THETA = 500000.0
PARTIAL_ROT = HEAD_DIM // 4
NORM_EPS = 1e-6
NSA_HEADS = 8
NSA_GROUPS = 2
NSA_REP = NSA_HEADS // NSA_GROUPS
CMP_LEN = 32
CMP_STRIDE = 16
CMP_HID = 4 * HEAD_DIM
SLC_LEN = 64
SLC_TOPK = 16
WIN = 512
FOX_HEADS = 8
MLA_HEADS = 8
MLA_Q_LORA = 384
MLA_KV_LORA = 256
MLA_NOPE = 64
MLA_ROPE = 32
MLA_V = 64
D_FF = 2816
CONV_W = 3

LANES = 128
NEG = -1e30
SEL_NEG = -1e9
VMEM_LIMIT = 52 * 1024 * 1024

TM = 256
TQ = 512
DECAY_TERMS = 3


def _cparams(n_axes):
    return pltpu.CompilerParams(dimension_semantics=("arbitrary",) * n_axes,
                                vmem_limit_bytes=VMEM_LIMIT)


def _const_spec(shape):
    nd = len(shape)
    return pl.BlockSpec(shape, lambda *_: (0,) * nd, pipeline_mode=pl.Buffered(1))


def _rms(x, g):
    return x * lax.rsqrt(jnp.mean(x * x, axis=-1, keepdims=True) + NORM_EPS) * g


def _dot(a, b):
    return jnp.dot(a, b, preferred_element_type=F32)


def _dot_nt(a, b):
    return lax.dot_general(a, b, (((1,), (1,)), ((), ())), preferred_element_type=F32)


def _rope_tables_kernel(pos_ref, f16_ref, g16_ref, f32_ref, g32_ref, c16_o, s16_o, c32_o, s32_o):
    p = pos_ref[...].astype(F32)
    a16 = p * f16_ref[...]
    c16_o[...] = jnp.cos(a16)
    s16_o[...] = jnp.sin(a16) * g16_ref[...]
    a32 = p * f32_ref[...]
    c32_o[...] = jnp.cos(a32)
    s32_o[...] = jnp.sin(a32) * g32_ref[...]


def _rope_tables(positions):
    T = positions.size
    half16, half32 = PARTIAL_ROT // 2, MLA_ROPE // 2
    inv16 = ROPE_THETA ** (-jnp.arange(half16, dtype=F32) / half16)
    inv32 = ROPE_THETA ** (-jnp.arange(half32, dtype=F32) / half32)
    lane = np.arange(LANES)
    d = lane % HEAD_DIM
    on16 = d < PARTIAL_ROT
    f16 = jnp.where(on16, inv16[d % half16], 0.0)[None, :]
    g16 = np.where(on16, np.where(d < half16, -1.0, 1.0), 0.0).astype(np.float32)[None, :]
    r = lane - MLA_NOPE
    on32 = (r >= 0) & (r < MLA_ROPE)
    f32 = jnp.where(on32, inv32[r % half32], 0.0)[None, :]
    g32 = np.where(on32, np.where(r < half32, -1.0, 1.0), 0.0).astype(np.float32)[None, :]
    pos = jnp.broadcast_to(positions.reshape(T, 1), (T, LANES))
    tt = 1024
    row = pl.BlockSpec((tt, LANES), lambda i: (i, 0))
    vec = _const_spec((1, LANES))
    out = jax.ShapeDtypeStruct((T, LANES), F32)
    return pl.pallas_call(
        _rope_tables_kernel, grid=(T // tt,),
        in_specs=[row, vec, vec, vec, vec], out_specs=[row] * 4, out_shape=[out] * 4,
        compiler_params=_cparams(1), name="rope_tables",
    )(pos, f16, jnp.asarray(g16), f32, jnp.asarray(g32))


def _rot16(x, cos, sin):
    lane = lax.broadcasted_iota(jnp.int32, x.shape, 1)
    first = (lane % HEAD_DIM) < (PARTIAL_ROT // 2)
    sw = jnp.where(first, pltpu.roll(x, LANES - PARTIAL_ROT // 2, 1), pltpu.roll(x, PARTIAL_ROT // 2, 1))
    return x * cos + sw * sin


def _rot32(x, cos, sin):
    lane = lax.broadcasted_iota(jnp.int32, x.shape, 1)
    first = lane < (MLA_NOPE + MLA_ROPE // 2)
    sw = jnp.where(first, pltpu.roll(x, LANES - MLA_ROPE // 2, 1), pltpu.roll(x, MLA_ROPE // 2, 1))
    return x * cos + sw * sin


def _inproj_kernel(x_ref, g_ref, c16_ref, s16_ref, c32_ref, s32_ref,
                   wq_ref, wkvc_ref, wksl_ref, wvsl_ref, wkwin_ref, wvwin_ref, wgate_ref,
                   wfq_ref, wfk_ref, wfv_ref, wff_ref, bf_ref, tri_ref, place_ref,
                   wmla_ref, qnrm_ref, wuq_ref, kvnrm_ref, wuk_ref, wuv_ref, wmg_ref,
                   qn_o, qr_o, kvc_o, ksl_o, vsl_o, kwin_o, vwin_o, gate_o,
                   fq_o, fk_o, fv_o, mq_o, mk_o, mv_o, mg_o, fcarry_ref, *, seq):
    tm = x_ref.shape[0]
    seq_start = (pl.program_id(0) * tm) % seq == 0
    hb = _rms(x_ref[...], g_ref[...]).astype(BF16)
    c16, s16 = c16_ref[...], s16_ref[...]
    c32, s32 = c32_ref[...], s32_ref[...]

    def slabs(z):
        return [z[:, k * LANES:(k + 1) * LANES] for k in range(z.shape[1] // LANES)]

    zq = _dot(hb, wq_ref[...])
    qn_o[...] = zq.astype(BF16)
    for k, z in enumerate(slabs(zq)):
        qr_o[:, k * LANES:(k + 1) * LANES] = _rot16(z, c16, s16).astype(BF16)

    zc = _dot(hb, wkvc_ref[...])
    for k, z in enumerate(slabs(zc)):
        kvc_o[k] = z[:, :HEAD_DIM]

    tok = (pl.program_id(0) * tm) % seq + lax.broadcasted_iota(jnp.int32, (tm, LANES), 0)
    lane = lax.broadcasted_iota(jnp.int32, (tm, LANES), 1)
    onehot = jnp.where(lane - HEAD_DIM == tok // SLC_LEN, 1.0, 0.0)
    zk = _dot(hb, wksl_ref[...])
    for k, z in enumerate(slabs(zk)):
        ksl_o[:, k * LANES:(k + 1) * LANES] = (_rot16(z, c16, s16) + onehot).astype(BF16)
    vsl_o[0] = _dot_nt(wvsl_ref[...], hb).astype(BF16)

    zw = _dot(hb, wkwin_ref[...])
    for k, z in enumerate(slabs(zw)):
        kwin_o[:, k * LANES:(k + 1) * LANES] = _rot16(z, c16, s16).astype(BF16)
    vwin_o[0] = _dot_nt(wvwin_ref[...], hb).astype(BF16)

    gate_o[...] = _dot(hb, wgate_ref[...])

    @pl.when(seq_start)
    def _():
        fcarry_ref[...] = jnp.zeros(fcarry_ref.shape, F32)

    zf = _dot(hb, wff_ref[...]) + bf_ref[...]
    logf = jnp.minimum(zf, 0.0) - jnp.log1p(jnp.exp(-jnp.abs(zf)))
    fsum = jnp.dot(tri_ref[...], logf, preferred_element_type=F32,
                   precision=lax.Precision.HIGHEST) + fcarry_ref[0:1]
    fcarry_ref[...] = jnp.broadcast_to(fsum[tm - 1:tm], fcarry_ref.shape)
    rest = -fsum
    zfk = _dot(hb, wfk_ref[...])
    for t in range(DECAY_TERMS):
        part = rest.astype(BF16)
        rest = rest - part.astype(F32)
        zfk = zfk + _dot(part, place_ref[t])
    fk_o[...] = zfk.astype(BF16)
    ones = jnp.where((lane >= HEAD_DIM) & (lane < HEAD_DIM + DECAY_TERMS), 1.0, 0.0)
    zfq = _dot(hb, wfq_ref[...])
    for k, z in enumerate(slabs(zfq)):
        fq_o[:, k * LANES:(k + 1) * LANES] = (z + ones).astype(BF16)
    fv_o[0] = _dot_nt(wfv_ref[...], hb).astype(BF16)

    zm = _dot(hb, wmla_ref[...])
    cq = _rms(zm[:, :MLA_Q_LORA], qnrm_ref[...]).astype(BF16)
    ckv = _rms(zm[:, MLA_Q_LORA:MLA_Q_LORA + MLA_KV_LORA], kvnrm_ref[...]).astype(BF16)
    kpe = _rot32(zm[:, MLA_Q_LORA + MLA_KV_LORA:], c32, s32)
    zuq = _dot(cq, wuq_ref[...])
    for k, z in enumerate(slabs(zuq)):
        mq_o[:, k * LANES:(k + 1) * LANES] = _rot32(z, c32, s32).astype(BF16)
    zuk = _dot(ckv, wuk_ref[...])
    for k, z in enumerate(slabs(zuk)):
        mk_o[:, k * LANES:(k + 1) * LANES] = (z + kpe).astype(BF16)
    mv_o[0] = _dot_nt(wuv_ref[...], ckv).astype(BF16)

    mg_o[...] = jax.nn.sigmoid(_dot(hb, wmg_ref[...]))


def _pad_halves(w, first_only=True):
    K, n = w.shape[0], w.shape[1] // HEAD_DIM
    w = w.reshape(K, n, HEAD_DIM)
    other = jnp.zeros_like(w) if first_only else w
    return jnp.concatenate([w, other], axis=-1).reshape(K, n * LANES)


def _inproj(x2, tabs, p, seq):
    T, D = x2.shape
    w_in = p["w_in"]
    scale = HEAD_DIM ** -0.5
    o = 0

    def take(n):
        nonlocal o
        w = w_in[:, o:o + n]
        o += n
        return w

    nsa_w, kvw = NSA_HEADS * HEAD_DIM, NSA_GROUPS * HEAD_DIM
    w_q = take(nsa_w)
    w_kc, w_vc, w_ks, w_vs, w_kw, w_vw = [take(kvw) for _ in range(6)]
    w_g = take(3 * NSA_HEADS)
    w_fq, w_fk, w_fv = [take(FOX_HEADS * HEAD_DIM) for _ in range(3)]
    w_ff = take(FOX_HEADS)
    w_cq, w_ckv, w_kr = take(MLA_Q_LORA), take(MLA_KV_LORA), take(MLA_ROPE)
    w_mg = take(3 * D)

    bf = lambda w: w.astype(BF16)
    row = lambda a: a.reshape(1, -1).astype(F32)
    wq = bf(_pad_halves(w_q * scale))
    wkvc = bf(_pad_halves(jnp.concatenate([w_kc, w_vc], axis=1)))
    wksl = bf(_pad_halves(w_ks))
    wvsl = bf(_pad_halves(w_vs, first_only=False).T)
    wkwin = bf(_pad_halves(w_kw))
    wvwin = bf(_pad_halves(w_vw, first_only=False).T)
    wgate = bf(jnp.pad(w_g, ((0, 0), (0, LANES - w_g.shape[1]))))
    wfq = bf(_pad_halves(w_fq * scale))
    wfk = bf(_pad_halves(w_fk))
    wfv = bf(w_fv.T)
    wff = bf(jnp.pad(w_ff, ((0, 0), (0, LANES - FOX_HEADS))))
    b_f = jnp.pad(row(p["b_forget"]), ((0, 0), (0, LANES - FOX_HEADS)))
    tri = jnp.asarray(np.tril(np.ones((TM, TM), np.float32)))
    place = np.zeros((DECAY_TERMS, LANES, FOX_HEADS * LANES), np.float32)
    for t in range(DECAY_TERMS):
        for h in range(FOX_HEADS):
            place[t, h, h * LANES + HEAD_DIM + t] = 1.0
    place = jnp.asarray(place, BF16)
    w_kr_slab = jnp.pad(w_kr, ((0, 0), (MLA_NOPE, LANES - MLA_NOPE - MLA_ROPE)))
    wmla = bf(jnp.concatenate([w_cq, w_ckv, w_kr_slab], axis=1))
    dq = MLA_NOPE + MLA_ROPE
    wuq = (p["mla_w_uq"] * dq ** -0.5).reshape(MLA_Q_LORA, MLA_HEADS, dq)
    wuq = bf(jnp.pad(wuq, ((0, 0), (0, 0), (0, LANES - dq))).reshape(MLA_Q_LORA, MLA_HEADS * LANES))
    wukv = p["mla_w_ukv"].reshape(MLA_KV_LORA, MLA_HEADS, MLA_NOPE + MLA_V)
    wuk = bf(_pad_halves(wukv[:, :, :MLA_NOPE].reshape(MLA_KV_LORA, MLA_HEADS * MLA_NOPE)))
    wuv = bf(wukv[:, :, MLA_NOPE:].reshape(MLA_KV_LORA, MLA_HEADS * MLA_V).T)
    wmg = bf(w_mg)

    weights = [wq, wkvc, wksl, wvsl, wkwin, wvwin, wgate, wfq, wfk, wfv, wff, b_f, tri, place,
               wmla, row(p["mla_q_norm"]), wuq, row(p["mla_kv_norm"]), wuk, wuv, wmg]
    n = T // TM
    tok = lambda c: pl.BlockSpec((TM, c), lambda i: (i, 0))
    in_specs = ([tok(D), _const_spec((1, D))] + [tok(LANES)] * 4
                + [_const_spec(w.shape) for w in weights])
    tokens, transposed = "tokens", "transposed"
    outs = [
        (tokens, NSA_HEADS * LANES, BF16), (tokens, NSA_HEADS * LANES, BF16), None,
        (tokens, NSA_GROUPS * LANES, BF16), (transposed, NSA_GROUPS * LANES, BF16),
        (tokens, NSA_GROUPS * LANES, BF16), (transposed, NSA_GROUPS * LANES, BF16), (tokens, LANES, F32),
        (tokens, FOX_HEADS * LANES, BF16), (tokens, FOX_HEADS * LANES, BF16),
        (transposed, FOX_HEADS * HEAD_DIM, BF16),
        (tokens, MLA_HEADS * LANES, BF16), (tokens, MLA_HEADS * LANES, BF16),
        (transposed, MLA_HEADS * MLA_V, BF16), (tokens, 3 * D, F32)]
    out_shape, out_specs = [], []
    for spec in outs:
        if spec is None:
            out_shape.append(jax.ShapeDtypeStruct((4, T, HEAD_DIM), F32))
            out_specs.append(pl.BlockSpec((4, TM, HEAD_DIM), lambda i: (0, i, 0)))
        elif spec[0] == transposed:
            out_shape.append(jax.ShapeDtypeStruct((n, spec[1], TM), spec[2]))
            out_specs.append(pl.BlockSpec((1, spec[1], TM), lambda i: (i, 0, 0)))
        else:
            out_shape.append(jax.ShapeDtypeStruct((T, spec[1]), spec[2]))
            out_specs.append(tok(spec[1]))
    return pl.pallas_call(
        functools.partial(_inproj_kernel, seq=seq), grid=(n,),
        in_specs=in_specs, out_specs=out_specs, out_shape=out_shape,
        scratch_shapes=[pltpu.VMEM((8, LANES), F32)],
        compiler_params=_cparams(1), name="inproj",
    )(x2, row(p["mix_norm"]), *tabs, *weights)


def _attn_kernel(q_ref, k_ref, vt_ref, o_ref, qt_ref, m_ref, l_ref, acc_ref, *, tq, tk, k_of_head, window):
    i = pl.program_id(2)
    nh = len(k_of_head)
    per_q = tq // tk
    for h in range(nh):
        qt_ref[h] = q_ref[:, h * LANES:(h + 1) * LANES].astype(F32).T.astype(BF16)
    dist0 = (i * tq + lax.broadcasted_iota(jnp.int32, (tk, tq), 1)
             - lax.broadcasted_iota(jnp.int32, (tk, tq), 0))

    def tile(j, masked, first):
        r0 = pl.multiple_of(j * tk, tk)
        vt = vt_ref[j]
        if masked:
            dist = dist0 - j * tk
            mask = (lax.bitcast_convert_type(dist, jnp.uint32) < WIN) if window else (dist >= 0)
        scores = []
        for h in range(nh):
            kk = k_of_head[h]
            s = _dot(k_ref[pl.ds(r0, tk), kk * LANES:(kk + 1) * LANES], qt_ref[h])
            scores.append(jnp.where(mask, s, NEG) if masked else s)
        probs = []
        for h in range(nh):
            s = scores[h]
            m_cur = jnp.max(s, axis=0, keepdims=True)
            if first:
                m_new, alpha = m_cur, None
                p = jnp.exp(s - m_new)
                l_ref[h] = jnp.sum(p, axis=0, keepdims=True)
            else:
                m_old = m_ref[h]
                m_new = jnp.maximum(m_old, m_cur)
                alpha = jnp.exp(m_old - m_new)
                p = jnp.exp(s - m_new)
                l_ref[h] = alpha * l_ref[h] + jnp.sum(p, axis=0, keepdims=True)
            m_ref[h] = m_new
            probs.append((p.astype(BF16), alpha))
        for h in range(nh):
            p, alpha = probs[h]
            pv = _dot(vt, p)
            acc_ref[h] = pv if first else alpha * acc_ref[h] + pv

    tile(i * per_q, True, True)
    for d in range(1, per_q):
        tile(i * per_q + d, True, False)
    if window:
        for d in range(1, WIN // tk + 1):
            @pl.when(i * per_q >= d)
            def _():
                tile(i * per_q - d, True, False)
    else:
        def body(j, c):
            tile(j, False, False)
            return c
        lax.fori_loop(0, i * per_q, body, 0)

    upper = lax.broadcasted_iota(jnp.int32, (LANES, tq), 0) < HEAD_DIM
    for k in range(nh // 2):
        a = acc_ref[2 * k] * (1.0 / l_ref[2 * k])
        b = acc_ref[2 * k + 1] * (1.0 / l_ref[2 * k + 1])
        o_ref[:, k * LANES:(k + 1) * LANES] = jnp.where(upper, a, b).T.astype(o_ref.dtype)


def _attention(q, k, vt, batch, seq, k_of_head, window, out_dtype, name):
    T = q.shape[0]
    nq = seq // TQ
    nh, nk = len(k_of_head), max(k_of_head) + 1
    units = q.shape[1] // (nh * LANES)
    assert vt.shape == (T // TM, units * LANES, TM) and k.shape[1] == units * nk * LANES
    return pl.pallas_call(
        functools.partial(_attn_kernel, tq=TQ, tk=TM, k_of_head=k_of_head, window=window),
        grid=(batch, units, nq),
        in_specs=[pl.BlockSpec((TQ, nh * LANES), lambda b, u, i: (b * nq + i, u)),
                  pl.BlockSpec((seq, nk * LANES), lambda b, u, i: (b, u)),
                  pl.BlockSpec((seq // TM, LANES, TM), lambda b, u, i: (b, u, 0))],
        out_specs=pl.BlockSpec((TQ, nh // 2 * LANES), lambda b, u, i: (b * nq + i, u)),
        out_shape=jax.ShapeDtypeStruct((T, units * nh // 2 * LANES), out_dtype),
        scratch_shapes=[pltpu.VMEM((nh, LANES, TQ), BF16), pltpu.VMEM((nh, 1, TQ), F32),
                        pltpu.VMEM((nh, 1, TQ), F32), pltpu.VMEM((nh, LANES, TQ), F32)],
        compiler_params=_cparams(3), name=name,
    )(q, k, vt)


def _gelu_tanh(x):
    return 0.5 * x * (1.0 + jnp.tanh(np.float32(np.sqrt(2.0 / np.pi)) * (x + 0.044715 * (x * x * x))))


def _compress_kernel(x_ref, pe_ref, w1_ref, w2_ref, o_ref):
    x = x_ref[0]
    half = x.shape[1]
    top = _dot((x + pe_ref[0, :, :half]).astype(BF16), w1_ref[0, :half, :])
    bot = _dot((x + pe_ref[0, :, half:]).astype(BF16), w1_ref[0, half:, :])
    hid = top + pltpu.roll(bot, bot.shape[0] - 1, 0)
    o_ref[0, 0] = _dot(_gelu_tanh(hid).astype(BF16), w2_ref[0]).astype(o_ref.dtype)


def _compress(kvc, p, batch, seq):
    T = kvc.shape[1]
    rows = seq // CMP_STRIDE
    x = kvc.reshape(4, T // CMP_STRIDE, CMP_STRIDE * HEAD_DIM)
    pe = jnp.stack([p["cmp_pe_k"].reshape(1, -1), p["cmp_pe_v"].reshape(1, -1)]).astype(F32)
    w1 = jnp.stack([p["cmp_w1_k"], p["cmp_w1_v"]]).astype(BF16)
    w2k = jnp.pad(p["cmp_w2_k"], ((0, 0), (0, LANES - HEAD_DIM)))
    w2v = jnp.concatenate([p["cmp_w2_v"], p["cmp_w2_v"]], axis=1)
    w2 = jnp.stack([w2k, w2v]).astype(BF16)
    return pl.pallas_call(
        _compress_kernel, grid=(4, batch),
        in_specs=[pl.BlockSpec((1, rows, x.shape[2]), lambda j, b: (j, b, 0)),
                  pl.BlockSpec((1,) + pe.shape[1:], lambda j, b: (j // 2, 0, 0)),
                  pl.BlockSpec((1,) + w1.shape[1:], lambda j, b: (j // 2, 0, 0)),
                  pl.BlockSpec((1,) + w2.shape[1:], lambda j, b: (j // 2, 0, 0))],
        out_specs=pl.BlockSpec((1, 1, rows, LANES), lambda j, b: (b, j, 0, 0)),
        out_shape=jax.ShapeDtypeStruct((batch, 4, rows, LANES), BF16),
        compiler_params=_cparams(2), name="nsa_compress",
    )(x, pe, w1, w2)


def _cmp_select_kernel(qn_ref, qr_ref, kc_ref, vc_ref, ovt_ref, o_ref, qa_ref, *, tq):
    i = pl.program_id(2)
    n_cmp = kc_ref.shape[2]
    kc, vc = kc_ref[0, 0], vc_ref[0, 0]
    t_row = i * tq + lax.broadcasted_iota(jnp.int32, (tq, n_cmp), 0)
    blk_end = CMP_STRIDE * lax.broadcasted_iota(jnp.int32, (tq, n_cmp), 1) + (CMP_LEN - 1)
    vis = blk_end <= t_row
    lane = lax.broadcasted_iota(jnp.int32, (tq, LANES), 1)
    psum = jnp.zeros((tq, n_cmp), F32)
    outs = []
    for r in range(NSA_REP):
        s = jnp.where(vis, _dot_nt(qn_ref[:, r * LANES:(r + 1) * LANES], kc), NEG)
        m = jnp.max(s, axis=-1, keepdims=True)
        e = jnp.where(vis, jnp.exp(s - m), 0.0)
        pn = e / jnp.maximum(jnp.sum(e, axis=-1, keepdims=True), 1e-30)
        psum = psum + pn
        outs.append(_dot(pn.astype(BF16), vc))
    for k in range(NSA_REP // 2):
        o_ref[:, k * LANES:(k + 1) * LANES] = jnp.where(lane < HEAD_DIM, outs[2 * k], outs[2 * k + 1])

    p_hi = psum.astype(BF16)
    p_lo = (psum - p_hi.astype(F32)).astype(BF16)
    ovt = ovt_ref[...]
    imp = _dot_nt(ovt, p_hi) + _dot_nt(ovt, p_lo)
    n_slc = LANES - HEAD_DIM
    jb = lax.broadcasted_iota(jnp.int32, (LANES, tq), 0) - HEAD_DIM
    cur = (i * tq + lax.broadcasted_iota(jnp.int32, (LANES, tq), 1)) // SLC_LEN
    forced = (jb == 0) | (jb == cur) | (jb == cur - 1)
    imp = jnp.where(forced, 1e9, jnp.where(jb > cur, -1e9, imp))
    groups = [imp[HEAD_DIM + 8 * v:HEAD_DIM + 8 * (v + 1)] for v in range(n_slc // 8)]
    sub = lax.broadcasted_iota(jnp.int32, (8, tq), 0)
    ranks = [jnp.zeros((8, tq), F32) for _ in groups]
    for a in range(n_slc):
        row = jnp.broadcast_to(groups[a // 8][a % 8:a % 8 + 1], (8, tq))
        for v in range(len(groups)):
            if v > a // 8:
                inc = jnp.where(row >= groups[v], 1.0, 0.0)
            elif v < a // 8:
                inc = jnp.where(row > groups[v], 1.0, 0.0)
            else:
                tie = jnp.where(sub > a % 8, 1.0, 0.0)
                inc = jnp.where(row > groups[v], 1.0, jnp.where(row == groups[v], tie, 0.0))
            ranks[v] = ranks[v] + inc
    bias_t = jnp.concatenate(
        [jnp.zeros((HEAD_DIM, tq), F32)]
        + [jnp.where(rk < SLC_TOPK, 0.0, SEL_NEG) for rk in ranks], axis=0)
    bias = bias_t.T
    for r in range(NSA_REP):
        qa_ref[:, r * LANES:(r + 1) * LANES] = (
            qr_ref[:, r * LANES:(r + 1) * LANES].astype(F32) + bias).astype(BF16)


def _overlap_t(seq):
    n_cmp = (seq - CMP_LEN) // CMP_STRIDE + 1
    n_slc = seq // SLC_LEN
    assert n_slc == LANES - HEAD_DIM, "selection blocks must fill the bias half of a slab"
    c0 = CMP_STRIDE * np.arange(n_cmp)[:, None]
    s0 = SLC_LEN * np.arange(n_slc)[None, :]
    ov = np.clip(np.minimum(c0 + CMP_LEN, s0 + SLC_LEN) - np.maximum(c0, s0), 0, None) / CMP_LEN
    out = np.zeros((LANES, seq // CMP_STRIDE), np.float32)
    out[HEAD_DIM:, :n_cmp] = ov.T
    return jnp.asarray(out, BF16)


def _cmp_select(qn, qr, cmp_kv, batch, seq):
    T = qn.shape[0]
    nq = seq // TQ
    rows = seq // CMP_STRIDE
    gw = NSA_REP * LANES
    qspec = pl.BlockSpec((TQ, gw), lambda b, g, i: (b * nq + i, g))
    return pl.pallas_call(
        functools.partial(_cmp_select_kernel, tq=TQ), grid=(batch, NSA_GROUPS, nq),
        in_specs=[qspec, qspec,
                  pl.BlockSpec((1, 1, rows, LANES), lambda b, g, i: (b, g, 0, 0)),
                  pl.BlockSpec((1, 1, rows, LANES), lambda b, g, i: (b, NSA_GROUPS + g, 0, 0)),
                  _const_spec((LANES, rows))],
        out_specs=[pl.BlockSpec((TQ, NSA_REP * HEAD_DIM), lambda b, g, i: (b * nq + i, g)), qspec],
        out_shape=[jax.ShapeDtypeStruct((T, NSA_HEADS * HEAD_DIM), F32),
                   jax.ShapeDtypeStruct((T, NSA_HEADS * LANES), BF16)],
        compiler_params=_cparams(3), name="nsa_cmp_select",
    )(qn, qr, cmp_kv, cmp_kv, _overlap_t(seq))


def _merge_kernel(x_ref, oc_ref, os_ref, ow_ref, gl_ref, of_ref, om_ref, mg_ref,
                  e_ref, wn_ref, wf_ref, wm_ref, wo_ref, o_ref):
    d = x_ref.shape[1]
    nw = oc_ref.shape[1]
    sg = jax.nn.sigmoid(gl_ref[...])
    sg_hi = sg.astype(BF16)
    sg_lo = (sg - sg_hi.astype(F32)).astype(BF16)
    e = e_ref[...]
    g = _dot(sg_hi, e) + _dot(sg_lo, e)
    o_nsa = g[:, :nw] * oc_ref[...] + g[:, nw:2 * nw] * os_ref[...] + g[:, 2 * nw:] * ow_ref[...]
    mg = mg_ref[...]
    merged = (mg[:, :d] * _dot(o_nsa.astype(BF16), wn_ref[...])
              + mg[:, d:2 * d] * _dot(of_ref[...], wf_ref[...])
              + mg[:, 2 * d:] * _dot(om_ref[...], wm_ref[...]))
    o_ref[...] = x_ref[...] + _dot(merged.astype(BF16), wo_ref[...])


def _gate_expand():
    e = np.zeros((LANES, 3 * NSA_HEADS * HEAD_DIM), np.float32)
    for h in range(NSA_HEADS):
        for c in range(3):
            e[h * 3 + c, c * NSA_HEADS * HEAD_DIM + h * HEAD_DIM:
              c * NSA_HEADS * HEAD_DIM + (h + 1) * HEAD_DIM] = 1.0
    return jnp.asarray(e, BF16)


def _merge(x2, o_cmp, o_slc, o_win, gate_l, o_fox, o_mla, mg, p):
    T, D = x2.shape
    tok = lambda a: pl.BlockSpec((TM, a.shape[1]), lambda i: (i, 0))
    acts = [x2, o_cmp, o_slc, o_win, gate_l, o_fox, o_mla, mg]
    weights = [_gate_expand(), p["w_br_nsa"].astype(BF16), p["w_br_fox"].astype(BF16),
               p["w_br_mla"].astype(BF16), p["w_out"].astype(BF16)]
    return pl.pallas_call(
        _merge_kernel, grid=(T // TM,),
        in_specs=[tok(a) for a in acts] + [_const_spec(w.shape) for w in weights],
        out_specs=pl.BlockSpec((TM, D), lambda i: (i, 0)),
        out_shape=jax.ShapeDtypeStruct((T, D), F32),
        compiler_params=_cparams(1), name="merge_out",
    )(*acts, *weights)


def _ffn_kernel(x_ref, g_ref, wu_ref, wv_ref, cw_ref, cb_ref, wd_ref, fin_ref, o_ref, tail_ref,
                *, seq, n_chunks, final):
    tm = x_ref.shape[0]
    x = x_ref[...]
    hb = _rms(x, g_ref[...]).astype(BF16)

    @pl.when((pl.program_id(0) * tm) % seq == 0)
    def _():
        tail_ref[...] = jnp.zeros(tail_ref.shape, F32)

    cw = wu_ref.shape[1] // n_chunks
    row8 = lax.broadcasted_iota(jnp.int32, (8, cw), 0)
    acc = jnp.zeros(x.shape, F32)
    for c in range(n_chunks):
        sl = slice(c * cw, (c + 1) * cw)
        u = _dot(hb, wu_ref[:, sl])
        v = _dot(hb, wv_ref[:, sl])
        prev = tail_ref[:, sl]
        tail_ref[:, sl] = u[tm - 8:]
        u1 = pltpu.roll(u, 1, 0)
        u2 = pltpu.roll(u, 2, 0)
        h1 = jnp.where(row8 < 1, pltpu.roll(prev, 1, 0), u1[:8])
        h2 = jnp.where(row8 < 2, pltpu.roll(prev, 2, 0), u2[:8])
        u1 = jnp.concatenate([h1, u1[8:]], axis=0)
        u2 = jnp.concatenate([h2, u2[8:]], axis=0)
        uc = cw_ref[0:1, sl] * u2 + cw_ref[1:2, sl] * u1 + cw_ref[2:3, sl] * u + cb_ref[:, sl]
        act = (uc * jax.nn.sigmoid(uc) * v).astype(BF16)
        acc = acc + _dot(act, wd_ref[sl, :])
    y = x + acc
    if final:
        y = _rms(y, fin_ref[...])
    o_ref[...] = y


def _ffn(x2, p, final_norm, seq, final):
    T, D = x2.shape
    w_up = p["w_up"]
    wu, wv = w_up[:, :D_FF].astype(BF16), w_up[:, D_FF:].astype(BF16)
    row = lambda a: a.reshape(1, -1).astype(F32)
    cwp = jnp.pad(p["conv_w"].astype(F32), ((0, 8 - CONV_W), (0, 0)))
    weights = [row(p["ffn_norm"]), wu, wv, cwp, row(p["conv_b"]), p["w_down"].astype(BF16),
               row(final_norm)]
    return pl.pallas_call(
        functools.partial(_ffn_kernel, seq=seq, n_chunks=2, final=final), grid=(T // TM,),
        in_specs=[pl.BlockSpec((TM, D), lambda i: (i, 0))] + [_const_spec(w.shape) for w in weights],
        out_specs=pl.BlockSpec((TM, D), lambda i: (i, 0)),
        out_shape=jax.ShapeDtypeStruct((T, D), F32),
        scratch_shapes=[pltpu.VMEM((8, D_FF), F32)],
        compiler_params=_cparams(1), name="conv_ffn",
    )(x2, *weights)


_LAYER_PARAMS = ("mix_norm", "w_in", "b_forget", "cmp_pe_k", "cmp_w1_k", "cmp_w2_k", "cmp_pe_v",
                 "cmp_w1_v", "cmp_w2_v", "mla_q_norm", "mla_w_uq", "mla_kv_norm", "mla_w_ukv",
                 "w_br_nsa", "w_br_fox", "w_br_mla", "w_out", "ffn_norm", "w_up", "conv_w",
                 "conv_b", "w_down")


def _mixer_layer(h, tabs, p, batch, seq):
    (qn, qr, kvc, ksl, vsl_t, kwin, vwin_t, gate_l,
     fq, fk, fv_t, mq, mk, mv_t, mg) = _inproj(h, tabs, p, seq)
    cmp_kv = _compress(kvc, p, batch, seq)
    o_cmp, qa = _cmp_select(qn, qr, cmp_kv, batch, seq)
    group = (0,) * NSA_REP
    pair = (0, 1)
    o_slc = _attention(qa, ksl, vsl_t, batch, seq, group, False, F32, "nsa_slc_attn")
    o_win = _attention(qa, kwin, vwin_t, batch, seq, group, True, F32, "nsa_win_attn")
    o_fox = _attention(fq, fk, fv_t, batch, seq, pair, False, BF16, "fox_attn")
    o_mla = _attention(mq, mk, mv_t, batch, seq, pair, False, BF16, "mla_attn")
    return _merge(h, o_cmp, o_slc, o_win, gate_l, o_fox, o_mla, mg, p)


def kernel(x, positions, mix_norm, w_in, b_forget, cmp_pe_k, cmp_w1_k, cmp_w2_k, cmp_pe_v, cmp_w1_v,
           cmp_w2_v, mla_q_norm, mla_w_uq, mla_kv_norm, mla_w_ukv, w_br_nsa, w_br_fox, w_br_mla,
           w_out, ffn_norm, w_up, conv_w, conv_b, w_down, final_norm):
    batch, seq, d = x.shape
    stacked = dict(zip(_LAYER_PARAMS, (
        mix_norm, w_in, b_forget, cmp_pe_k, cmp_w1_k, cmp_w2_k, cmp_pe_v, cmp_w1_v, cmp_w2_v,
        mla_q_norm, mla_w_uq, mla_kv_norm, mla_w_ukv, w_br_nsa, w_br_fox, w_br_mla, w_out,
        ffn_norm, w_up, conv_w, conv_b, w_down)))
    depth = w_in.shape[0]
    tabs = _rope_tables(positions)
    h = x.reshape(batch * seq, d)
    for l in range(depth):
        p = {k: v[l] for k, v in stacked.items()}
        h = _mixer_layer(h, tabs, p, batch, seq)
        h = _ffn(h, p, final_norm, seq, final=(l == depth - 1))
    return h.reshape(batch, seq, d)
```

```python
import functools

import numpy as np
import jax
import jax.numpy as jnp
from jax import lax
from jax.experimental import pallas as pl
from jax.experimental.pallas import tpu as pltpu

F32 = jnp.float32
BF16 = jnp.bfloat16

HEAD_DIM = 64
ROPE_THETA = 500000.0
PARTIAL_ROT = HEAD_DIM // 4
NORM_EPS = 1e-6
NSA_HEADS = 8
NSA_GROUPS = 2
NSA_REP = NSA_HEADS // NSA_GROUPS
CMP_LEN = 32
CMP_STRIDE = 16
CMP_HID = 4 * HEAD_DIM
SLC_LEN = 64
SLC_TOPK = 16
WIN = 512
FOX_HEADS = 8
MLA_HEADS = 8
MLA_Q_LORA = 384
MLA_KV_LORA = 256
MLA_NOPE = 64
MLA_ROPE = 32
MLA_V = 64
D_FF = 2816
CONV_W = 3

LANES = 128
LOG2E = 1.4426950408889634
NEG = -1e30
SEL_NEG = -1e9
VMEM_LIMIT = 52 * 1024 * 1024

TM = 256
TQ = 512
TK = 512
QC = 256
DECAY_TERMS = 3


def _cparams(n_axes):
    return pltpu.CompilerParams(dimension_semantics=("arbitrary",) * n_axes,
                                vmem_limit_bytes=VMEM_LIMIT)


def _const_spec(shape):
    nd = len(shape)
    return pl.BlockSpec(shape, lambda *_: (0,) * nd, pipeline_mode=pl.Buffered(1))


def _rms(x, g):
    return x * lax.rsqrt(jnp.mean(x * x, axis=-1, keepdims=True) + NORM_EPS) * g


def _dot(a, b):
    return jnp.dot(a, b, preferred_element_type=F32)


def _dot_nt(a, b):
    return lax.dot_general(a, b, (((1,), (1,)), ((), ())), preferred_element_type=F32)


def _rope_tables_kernel(pos_ref, f16_ref, g16_ref, f32_ref, g32_ref, c16_o, s16_o, c32_o, s32_o):
    p = pos_ref[...].astype(F32)
    a16 = p * f16_ref[...]
    c16_o[...] = jnp.cos(a16)
    s16_o[...] = jnp.sin(a16) * g16_ref[...]
    a32 = p * f32_ref[...]
    c32_o[...] = jnp.cos(a32)
    s32_o[...] = jnp.sin(a32) * g32_ref[...]


def _rope_tables(positions):
    T = positions.size
    half16, half32 = PARTIAL_ROT // 2, MLA_ROPE // 2
    inv16 = ROPE_THETA ** (-jnp.arange(half16, dtype=F32) / half16)
    inv32 = ROPE_THETA ** (-jnp.arange(half32, dtype=F32) / half32)
    lane = np.arange(LANES)
    d = lane % HEAD_DIM
    on16 = d < PARTIAL_ROT
    f16 = jnp.where(on16, inv16[d % half16], 0.0)[None, :]
    g16 = np.where(on16, np.where(d < half16, -1.0, 1.0), 0.0).astype(np.float32)[None, :]
    r = lane - MLA_NOPE
    on32 = (r >= 0) & (r < MLA_ROPE)
    f32 = jnp.where(on32, inv32[r % half32], 0.0)[None, :]
    g32 = np.where(on32, np.where(r < half32, -1.0, 1.0), 0.0).astype(np.float32)[None, :]
    pos = jnp.broadcast_to(positions.reshape(T, 1), (T, LANES))
    tt = 1024
    row = pl.BlockSpec((tt, LANES), lambda i: (i, 0))
    vec = _const_spec((1, LANES))
    out = jax.ShapeDtypeStruct((T, LANES), F32)
    return pl.pallas_call(
        _rope_tables_kernel, grid=(T // tt,),
        in_specs=[row, vec, vec, vec, vec], out_specs=[row] * 4, out_shape=[out] * 4,
        compiler_params=_cparams(1), name="rope_tables",
    )(pos, f16, jnp.asarray(g16), f32, jnp.asarray(g32))


def _rot16(x, cos, sin):
    lane = lax.broadcasted_iota(jnp.int32, x.shape, 1)
    first = (lane % HEAD_DIM) < (PARTIAL_ROT // 2)
    sw = jnp.where(first, pltpu.roll(x, LANES - PARTIAL_ROT // 2, 1), pltpu.roll(x, PARTIAL_ROT // 2, 1))
    return x * cos + sw * sin


def _rot32(x, cos, sin):
    lane = lax.broadcasted_iota(jnp.int32, x.shape, 1)
    first = lane < (MLA_NOPE + MLA_ROPE // 2)
    sw = jnp.where(first, pltpu.roll(x, LANES - MLA_ROPE // 2, 1), pltpu.roll(x, MLA_ROPE // 2, 1))
    return x * cos + sw * sin


def _inproj_kernel(x_ref, g_ref, c16_ref, s16_ref, c32_ref, s32_ref,
                   wq_ref, wkvc_ref, wksl_ref, wvsl_ref, wkwin_ref, wvwin_ref, wgate_ref,
                   wfq_ref, wfk_ref, wfv_ref, wff_ref, bf_ref, tri_ref, place_ref,
                   wmla_ref, qnrm_ref, wuq_ref, kvnrm_ref, wuk_ref, wuv_ref, wmg_ref,
                   qn_o, qr_o, kvc_o, ksl_o, vsl_o, kwin_o, vwin_o, gate_o,
                   fq_o, fk_o, fv_o, mq_o, mk_o, mv_o, mg_o, fcarry_ref, *, seq):
    tm = x_ref.shape[0]
    seq_start = (pl.program_id(0) * tm) % seq == 0
    hb = _rms(x_ref[...], g_ref[...]).astype(BF16)
    c16, s16 = c16_ref[...], s16_ref[...]
    c32, s32 = c32_ref[...], s32_ref[...]

    def slabs(z):
        return [z[:, k * LANES:(k + 1) * LANES] for k in range(z.shape[1] // LANES)]

    zq = _dot(hb, wq_ref[...])
    qn_o[...] = zq.astype(BF16)
    for k, z in enumerate(slabs(zq)):
        qr_o[:, k * LANES:(k + 1) * LANES] = _rot16(z, c16, s16).astype(BF16)

    zc = _dot(hb, wkvc_ref[...])
    for k, z in enumerate(slabs(zc)):
        kvc_o[k] = z[:, :HEAD_DIM]

    tok = (pl.program_id(0) * tm) % seq + lax.broadcasted_iota(jnp.int32, (tm, LANES), 0)
    lane = lax.broadcasted_iota(jnp.int32, (tm, LANES), 1)
    onehot = jnp.where(lane - HEAD_DIM == tok // SLC_LEN, 1.0, 0.0)
    zk = _dot(hb, wksl_ref[...])
    for k, z in enumerate(slabs(zk)):
        ksl_o[:, k * LANES:(k + 1) * LANES] = (_rot16(z, c16, s16) + onehot).astype(BF16)
    vsl_o[0] = _dot_nt(wvsl_ref[...], hb).astype(BF16)

    zw = _dot(hb, wkwin_ref[...])
    for k, z in enumerate(slabs(zw)):
        kwin_o[:, k * LANES:(k + 1) * LANES] = _rot16(z, c16, s16).astype(BF16)
    vwin_o[0] = _dot_nt(wvwin_ref[...], hb).astype(BF16)

    gate_o[...] = _dot(hb, wgate_ref[...])

    @pl.when(seq_start)
    def _():
        fcarry_ref[...] = jnp.zeros(fcarry_ref.shape, F32)

    zf = _dot(hb, wff_ref[...]) + bf_ref[...]
    logf = jnp.minimum(zf, 0.0) - jnp.log1p(jnp.exp(-jnp.abs(zf)))
    fsum = jnp.dot(tri_ref[...], logf, preferred_element_type=F32,
                   precision=lax.Precision.HIGHEST) + fcarry_ref[0:1]
    fcarry_ref[...] = jnp.broadcast_to(fsum[tm - 1:tm], fcarry_ref.shape)
    rest = -LOG2E * fsum
    zfk = _dot(hb, wfk_ref[...])
    for t in range(DECAY_TERMS):
        part = rest.astype(BF16)
        rest = rest - part.astype(F32)
        zfk = zfk + _dot(part, place_ref[t])
    fk_o[...] = zfk.astype(BF16)
    ones = jnp.where((lane >= HEAD_DIM) & (lane < HEAD_DIM + DECAY_TERMS), 1.0, 0.0)
    zfq = _dot(hb, wfq_ref[...])
    for k, z in enumerate(slabs(zfq)):
        fq_o[:, k * LANES:(k + 1) * LANES] = (z + ones).astype(BF16)
    fv_o[0] = _dot_nt(wfv_ref[...], hb).astype(BF16)

    zm = _dot(hb, wmla_ref[...])
    cq = _rms(zm[:, :MLA_Q_LORA], qnrm_ref[...]).astype(BF16)
    ckv = _rms(zm[:, MLA_Q_LORA:MLA_Q_LORA + MLA_KV_LORA], kvnrm_ref[...]).astype(BF16)
    kpe = _rot32(zm[:, MLA_Q_LORA + MLA_KV_LORA:], c32, s32)
    zuq = _dot(cq, wuq_ref[...])
    for k, z in enumerate(slabs(zuq)):
        mq_o[:, k * LANES:(k + 1) * LANES] = _rot32(z, c32, s32).astype(BF16)
    zuk = _dot(ckv, wuk_ref[...])
    for k, z in enumerate(slabs(zuk)):
        mk_o[:, k * LANES:(k + 1) * LANES] = (z + kpe).astype(BF16)
    mv_o[0] = _dot_nt(wuv_ref[...], ckv).astype(BF16)

    mg_o[...] = jax.nn.sigmoid(_dot(hb, wmg_ref[...]))


def _pad_halves(w, first_only=True):
    K, n = w.shape[0], w.shape[1] // HEAD_DIM
    w = w.reshape(K, n, HEAD_DIM)
    other = jnp.zeros_like(w) if first_only else w
    return jnp.concatenate([w, other], axis=-1).reshape(K, n * LANES)


def _inproj(x2, tabs, p, seq):
    T, D = x2.shape
    w_in = p["w_in"]
    scale = HEAD_DIM ** -0.5 * LOG2E
    o = 0

    def take(n):
        nonlocal o
        w = w_in[:, o:o + n]
        o += n
        return w

    nsa_w, kvw = NSA_HEADS * HEAD_DIM, NSA_GROUPS * HEAD_DIM
    w_q = take(nsa_w)
    w_kc, w_vc, w_ks, w_vs, w_kw, w_vw = [take(kvw) for _ in range(6)]
    w_g = take(3 * NSA_HEADS)
    w_fq, w_fk, w_fv = [take(FOX_HEADS * HEAD_DIM) for _ in range(3)]
    w_ff = take(FOX_HEADS)
    w_cq, w_ckv, w_kr = take(MLA_Q_LORA), take(MLA_KV_LORA), take(MLA_ROPE)
    w_mg = take(3 * D)

    bf = lambda w: w.astype(BF16)
    row = lambda a: a.reshape(1, -1).astype(F32)
    wq = bf(_pad_halves(w_q * scale))
    wkvc = bf(_pad_halves(jnp.concatenate([w_kc, w_vc], axis=1)))
    wksl = bf(_pad_halves(w_ks))
    wvsl = bf(_pad_halves(w_vs, first_only=False).T)
    wkwin = bf(_pad_halves(w_kw))
    wvwin = bf(_pad_halves(w_vw, first_only=False).T)
    wgate = bf(jnp.pad(w_g, ((0, 0), (0, LANES - w_g.shape[1]))))
    wfq = bf(_pad_halves(w_fq * scale))
    wfk = bf(_pad_halves(w_fk))
    wfv = bf(w_fv.T)
    wff = bf(jnp.pad(w_ff, ((0, 0), (0, LANES - FOX_HEADS))))
    b_f = jnp.pad(row(p["b_forget"]), ((0, 0), (0, LANES - FOX_HEADS)))
    tri = jnp.asarray(np.tril(np.ones((TM, TM), np.float32)))
    place = np.zeros((DECAY_TERMS, LANES, FOX_HEADS * LANES), np.float32)
    for t in range(DECAY_TERMS):
        for h in range(FOX_HEADS):
            place[t, h, h * LANES + HEAD_DIM + t] = 1.0
    place = jnp.asarray(place, BF16)
    w_kr_slab = jnp.pad(w_kr, ((0, 0), (MLA_NOPE, LANES - MLA_NOPE - MLA_ROPE)))
    wmla = bf(jnp.concatenate([w_cq, w_ckv, w_kr_slab], axis=1))
    dq = MLA_NOPE + MLA_ROPE
    wuq = (p["mla_w_uq"] * (dq ** -0.5 * LOG2E)).reshape(MLA_Q_LORA, MLA_HEADS, dq)
    wuq = bf(jnp.pad(wuq, ((0, 0), (0, 0), (0, LANES - dq))).reshape(MLA_Q_LORA, MLA_HEADS * LANES))
    wukv = p["mla_w_ukv"].reshape(MLA_KV_LORA, MLA_HEADS, MLA_NOPE + MLA_V)
    wuk = bf(_pad_halves(wukv[:, :, :MLA_NOPE].reshape(MLA_KV_LORA, MLA_HEADS * MLA_NOPE)))
    wuv = bf(wukv[:, :, MLA_NOPE:].reshape(MLA_KV_LORA, MLA_HEADS * MLA_V).T)
    wmg = bf(w_mg)

    weights = [wq, wkvc, wksl, wvsl, wkwin, wvwin, wgate, wfq, wfk, wfv, wff, b_f, tri, place,
               wmla, row(p["mla_q_norm"]), wuq, row(p["mla_kv_norm"]), wuk, wuv, wmg]
    n = T // TM
    tok = lambda c: pl.BlockSpec((TM, c), lambda i: (i, 0))
    in_specs = ([tok(D), _const_spec((1, D))] + [tok(LANES)] * 4
                + [_const_spec(w.shape) for w in weights])
    tokens, transposed = "tokens", "transposed"
    outs = [
        (tokens, NSA_HEADS * LANES, BF16), (tokens, NSA_HEADS * LANES, BF16), None,
        (tokens, NSA_GROUPS * LANES, BF16), (transposed, NSA_GROUPS * LANES, BF16),
        (tokens, NSA_GROUPS * LANES, BF16), (transposed, NSA_GROUPS * LANES, BF16), (tokens, LANES, F32),
        (tokens, FOX_HEADS * LANES, BF16), (tokens, FOX_HEADS * LANES, BF16),
        (transposed, FOX_HEADS * HEAD_DIM, BF16),
        (tokens, MLA_HEADS * LANES, BF16), (tokens, MLA_HEADS * LANES, BF16),
        (transposed, MLA_HEADS * MLA_V, BF16), (tokens, 3 * D, F32)]
    out_shape, out_specs = [], []
    for spec in outs:
        if spec is None:
            out_shape.append(jax.ShapeDtypeStruct((4, T, HEAD_DIM), F32))
            out_specs.append(pl.BlockSpec((4, TM, HEAD_DIM), lambda i: (0, i, 0)))
        elif spec[0] == transposed:
            out_shape.append(jax.ShapeDtypeStruct((n, spec[1], TM), spec[2]))
            out_specs.append(pl.BlockSpec((1, spec[1], TM), lambda i: (i, 0, 0)))
        else:
            out_shape.append(jax.ShapeDtypeStruct((T, spec[1]), spec[2]))
            out_specs.append(tok(spec[1]))
    return pl.pallas_call(
        functools.partial(_inproj_kernel, seq=seq), grid=(n,),
        in_specs=in_specs, out_specs=out_specs, out_shape=out_shape,
        scratch_shapes=[pltpu.VMEM((8, LANES), F32)],
        compiler_params=_cparams(1), name="inproj",
    )(x2, row(p["mix_norm"]), *tabs, *weights)


def _attn_kernel(q_ref, k_ref, vt_ref, o_ref, qt_ref, m_ref, l_ref, acc_ref, *, tq, tk, k_of_head, window):
    i = pl.program_id(2)
    nh = len(k_of_head)
    per_q = tq // tk
    tv = vt_ref.shape[2]
    v_sub = tk // tv
    for h in range(nh):
        qt_ref[h] = q_ref[:, h * LANES:(h + 1) * LANES].astype(F32).T.astype(BF16)
    dist0 = (i * tq + lax.broadcasted_iota(jnp.int32, (tk, tq), 1)
             - lax.broadcasted_iota(jnp.int32, (tk, tq), 0))

    m_ref[...] = jnp.full(m_ref.shape, NEG, F32)
    l_ref[...] = jnp.zeros(l_ref.shape, F32)
    acc_ref[...] = jnp.zeros(acc_ref.shape, F32)

    def logits(j, h, cols):
        r0 = pl.multiple_of(j * tk, tk)
        kk = k_of_head[h]
        return _dot(k_ref[pl.ds(r0, tk), kk * LANES:(kk + 1) * LANES], qt_ref[h, :, cols])

    def softmax(s, h, cols, mask):
        if mask is not None:
            s = jnp.where(mask, s, NEG)
        m_old = m_ref[h, :, cols]
        m_new = jnp.maximum(m_old, jnp.max(s, axis=0, keepdims=True))
        alpha = jnp.exp2(m_old - m_new)
        p = jnp.exp2(s - m_new)
        l_ref[h, :, cols] = alpha * l_ref[h, :, cols] + jnp.sum(p, axis=0, keepdims=True)
        m_ref[h, :, cols] = m_new
        return p.astype(BF16), alpha

    def accumulate(j, h, cols, p, alpha):
        half = slice((h % 2) * HEAD_DIM, (h % 2 + 1) * HEAD_DIM)
        pv = sum(_dot(vt_ref[j * v_sub + c, half, :], p[c * tv:(c + 1) * tv]) for c in range(v_sub))
        acc_ref[h, :, cols] = alpha * acc_ref[h, :, cols] + pv

    def sweep(tiles):
        work = [(j, jv, mask, h, slice(c * QC, (c + 1) * QC))
                for j, jv, mask in tiles for h in range(nh) for c in range(tq // QC)]
        s_of, p_of = {}, {}
        for t in range(len(work) + 2):
            if t < len(work):
                j, _, _, h, cols = work[t]
                s_of[t] = logits(j, h, cols)
            if 0 <= t - 1 < len(work):
                _, _, mask, h, cols = work[t - 1]
                p_of[t - 1] = softmax(s_of.pop(t - 1), h, cols, None if mask is None else mask(cols))
            if 0 <= t - 2 < len(work):
                _, jv, _, h, cols = work[t - 2]
                accumulate(jv, h, cols, *p_of.pop(t - 2))

    def visible(j, valid=None):
        def mask(cols):
            dist = dist0[:, cols] - j * tk
            if not window:
                return dist >= 0
            if valid is not None:
                dist = dist + jnp.where(valid, 0, WIN)
            return lax.bitcast_convert_type(dist, jnp.uint32) < WIN
        return mask

    assert per_q == 1, "the sweeps are written for equal query and key tiles"
    if window:
        assert WIN == tk, "window sweep is written for one key tile per window"
        prev = jnp.maximum(i - 1, 0)
        sweep([(i, i, visible(i)), (prev, prev, visible(i - 1, valid=i >= 1))])
    else:
        def body(j, c):
            sweep([(j, j, None)])
            return c
        lax.fori_loop(0, i, body, 0)
        sweep([(i, i, visible(i))])

    for k in range(nh // 2):
        a = acc_ref[2 * k] * (1.0 / l_ref[2 * k])
        b = acc_ref[2 * k + 1] * (1.0 / l_ref[2 * k + 1])
        o_ref[:, k * LANES:(k + 1) * LANES] = jnp.concatenate([a, b], axis=0).T.astype(o_ref.dtype)


def _attention(q, k, vt, batch, seq, k_of_head, window, out_dtype, name):
    T = q.shape[0]
    nq = seq // TQ
    nh, nk = len(k_of_head), max(k_of_head) + 1
    units = q.shape[1] // (nh * LANES)
    assert vt.shape == (T // TM, units * LANES, TM) and k.shape[1] == units * nk * LANES
    return pl.pallas_call(
        functools.partial(_attn_kernel, tq=TQ, tk=TK, k_of_head=k_of_head, window=window),
        grid=(batch, units, nq),
        in_specs=[pl.BlockSpec((TQ, nh * LANES), lambda b, u, i: (b * nq + i, u)),
                  pl.BlockSpec((seq, nk * LANES), lambda b, u, i: (b, u)),
                  pl.BlockSpec((seq // TM, LANES, TM), lambda b, u, i: (b, u, 0))],
        out_specs=pl.BlockSpec((TQ, nh // 2 * LANES), lambda b, u, i: (b * nq + i, u)),
        out_shape=jax.ShapeDtypeStruct((T, units * nh // 2 * LANES), out_dtype),
        scratch_shapes=[pltpu.VMEM((nh, LANES, TQ), BF16), pltpu.VMEM((nh, 1, TQ), F32),
                        pltpu.VMEM((nh, 1, TQ), F32), pltpu.VMEM((nh, HEAD_DIM, TQ), F32)],
        compiler_params=_cparams(3), name=name,
    )(q, k, vt)


def _gelu_tanh(x):
    return 0.5 * x * (1.0 + jnp.tanh(np.float32(np.sqrt(2.0 / np.pi)) * (x + 0.044715 * (x * x * x))))


def _compress_kernel(x_ref, pe_ref, w1_ref, w2_ref, o_ref):
    x = x_ref[0]
    half = x.shape[1]
    top = _dot((x + pe_ref[0, :, :half]).astype(BF16), w1_ref[0, :half, :])
    bot = _dot((x + pe_ref[0, :, half:]).astype(BF16), w1_ref[0, half:, :])
    hid = top + pltpu.roll(bot, bot.shape[0] - 1, 0)
    o_ref[0, 0] = _dot(_gelu_tanh(hid).astype(BF16), w2_ref[0]).astype(o_ref.dtype)


def _compress(kvc, p, batch, seq):
    T = kvc.shape[1]
    rows = seq // CMP_STRIDE
    x = kvc.reshape(4, T // CMP_STRIDE, CMP_STRIDE * HEAD_DIM)
    pe = jnp.stack([p["cmp_pe_k"].reshape(1, -1), p["cmp_pe_v"].reshape(1, -1)]).astype(F32)
    w1 = jnp.stack([p["cmp_w1_k"], p["cmp_w1_v"]]).astype(BF16)
    w2k = jnp.pad(p["cmp_w2_k"], ((0, 0), (0, LANES - HEAD_DIM)))
    w2v = jnp.concatenate([p["cmp_w2_v"], p["cmp_w2_v"]], axis=1)
    w2 = jnp.stack([w2k, w2v]).astype(BF16)
    return pl.pallas_call(
        _compress_kernel, grid=(4, batch),
        in_specs=[pl.BlockSpec((1, rows, x.shape[2]), lambda j, b: (j, b, 0)),
                  pl.BlockSpec((1,) + pe.shape[1:], lambda j, b: (j // 2, 0, 0)),
                  pl.BlockSpec((1,) + w1.shape[1:], lambda j, b: (j // 2, 0, 0)),
                  pl.BlockSpec((1,) + w2.shape[1:], lambda j, b: (j // 2, 0, 0))],
        out_specs=pl.BlockSpec((1, 1, rows, LANES), lambda j, b: (b, j, 0, 0)),
        out_shape=jax.ShapeDtypeStruct((batch, 4, rows, LANES), BF16),
        compiler_params=_cparams(2), name="nsa_compress",
    )(x, pe, w1, w2)


def _cmp_select_kernel(qn_ref, qr_ref, kc_ref, vc_ref, ovt_ref, o_ref, qa_ref, *, tq):
    i = pl.program_id(2)
    n_cmp = kc_ref.shape[2]
    kc, vc = kc_ref[0, 0], vc_ref[0, 0]
    t_row = i * tq + lax.broadcasted_iota(jnp.int32, (tq, n_cmp), 0)
    blk_end = CMP_STRIDE * lax.broadcasted_iota(jnp.int32, (tq, n_cmp), 1) + (CMP_LEN - 1)
    vis = blk_end <= t_row
    lane = lax.broadcasted_iota(jnp.int32, (tq, LANES), 1)
    psum = jnp.zeros((tq, n_cmp), F32)
    outs = []
    for r in range(NSA_REP):
        s = jnp.where(vis, _dot_nt(qn_ref[:, r * LANES:(r + 1) * LANES], kc), NEG)
        m = jnp.max(s, axis=-1, keepdims=True)
        e = jnp.where(vis, jnp.exp2(s - m), 0.0)
        pn = e / jnp.maximum(jnp.sum(e, axis=-1, keepdims=True), 1e-30)
        psum = psum + pn
        outs.append(_dot(pn.astype(BF16), vc))
    for k in range(NSA_REP // 2):
        o_ref[:, k * LANES:(k + 1) * LANES] = jnp.where(lane < HEAD_DIM, outs[2 * k], outs[2 * k + 1])

    p_hi = psum.astype(BF16)
    p_lo = (psum - p_hi.astype(F32)).astype(BF16)
    ovt = ovt_ref[...]
    imp = _dot_nt(ovt, p_hi) + _dot_nt(ovt, p_lo)
    n_slc = LANES - HEAD_DIM
    jb = lax.broadcasted_iota(jnp.int32, (LANES, tq), 0) - HEAD_DIM
    cur = (i * tq + lax.broadcasted_iota(jnp.int32, (LANES, tq), 1)) // SLC_LEN
    forced = (jb == 0) | (jb == cur) | (jb == cur - 1)
    imp = jnp.where(forced, 1e9, jnp.where(jb > cur, -1e9, imp))
    groups = [imp[HEAD_DIM + 8 * v:HEAD_DIM + 8 * (v + 1)] for v in range(n_slc // 8)]
    sub = lax.broadcasted_iota(jnp.int32, (8, tq), 0)
    ranks = [jnp.zeros((8, tq), F32) for _ in groups]
    for a in range(n_slc):
        row = jnp.broadcast_to(groups[a // 8][a % 8:a % 8 + 1], (8, tq))
        for v in range(len(groups)):
            if v > a // 8:
                inc = jnp.where(row >= groups[v], 1.0, 0.0)
            elif v < a // 8:
                inc = jnp.where(row > groups[v], 1.0, 0.0)
            else:
                tie = jnp.where(sub > a % 8, 1.0, 0.0)
                inc = jnp.where(row > groups[v], 1.0, jnp.where(row == groups[v], tie, 0.0))
            ranks[v] = ranks[v] + inc
    bias_t = jnp.concatenate(
        [jnp.zeros((HEAD_DIM, tq), F32)]
        + [jnp.where(rk < SLC_TOPK, 0.0, SEL_NEG) for rk in ranks], axis=0)
    bias = bias_t.T
    for r in range(NSA_REP):
        qa_ref[:, r * LANES:(r + 1) * LANES] = (
            qr_ref[:, r * LANES:(r + 1) * LANES].astype(F32) + bias).astype(BF16)


def _overlap_t(seq):
    n_cmp = (seq - CMP_LEN) // CMP_STRIDE + 1
    n_slc = seq // SLC_LEN
    assert n_slc == LANES - HEAD_DIM, "selection blocks must fill the bias half of a slab"
    c0 = CMP_STRIDE * np.arange(n_cmp)[:, None]
    s0 = SLC_LEN * np.arange(n_slc)[None, :]
    ov = np.clip(np.minimum(c0 + CMP_LEN, s0 + SLC_LEN) - np.maximum(c0, s0), 0, None) / CMP_LEN
    out = np.zeros((LANES, seq // CMP_STRIDE), np.float32)
    out[HEAD_DIM:, :n_cmp] = ov.T
    return jnp.asarray(out, BF16)


def _cmp_select(qn, qr, cmp_kv, batch, seq):
    T = qn.shape[0]
    nq = seq // TQ
    rows = seq // CMP_STRIDE
    gw = NSA_REP * LANES
    qspec = pl.BlockSpec((TQ, gw), lambda b, g, i: (b * nq + i, g))
    return pl.pallas_call(
        functools.partial(_cmp_select_kernel, tq=TQ), grid=(batch, NSA_GROUPS, nq),
        in_specs=[qspec, qspec,
                  pl.BlockSpec((1, 1, rows, LANES), lambda b, g, i: (b, g, 0, 0)),
                  pl.BlockSpec((1, 1, rows, LANES), lambda b, g, i: (b, NSA_GROUPS + g, 0, 0)),
                  _const_spec((LANES, rows))],
        out_specs=[pl.BlockSpec((TQ, NSA_REP * HEAD_DIM), lambda b, g, i: (b * nq + i, g)), qspec],
        out_shape=[jax.ShapeDtypeStruct((T, NSA_HEADS * HEAD_DIM), F32),
                   jax.ShapeDtypeStruct((T, NSA_HEADS * LANES), BF16)],
        compiler_params=_cparams(3), name="nsa_cmp_select",
    )(qn, qr, cmp_kv, cmp_kv, _overlap_t(seq))


def _merge_kernel(x_ref, oc_ref, os_ref, ow_ref, gl_ref, of_ref, om_ref, mg_ref,
                  e_ref, wn_ref, wf_ref, wm_ref, wo_ref, o_ref):
    d = x_ref.shape[1]
    nw = oc_ref.shape[1]
    sg = jax.nn.sigmoid(gl_ref[...])
    sg_hi = sg.astype(BF16)
    sg_lo = (sg - sg_hi.astype(F32)).astype(BF16)
    e = e_ref[...]
    g = _dot(sg_hi, e) + _dot(sg_lo, e)
    o_nsa = g[:, :nw] * oc_ref[...] + g[:, nw:2 * nw] * os_ref[...] + g[:, 2 * nw:] * ow_ref[...]
    mg = mg_ref[...]
    merged = (mg[:, :d] * _dot(o_nsa.astype(BF16), wn_ref[...])
              + mg[:, d:2 * d] * _dot(of_ref[...], wf_ref[...])
              + mg[:, 2 * d:] * _dot(om_ref[...], wm_ref[...]))
    o_ref[...] = x_ref[...] + _dot(merged.astype(BF16), wo_ref[...])


def _gate_expand():
    e = np.zeros((LANES, 3 * NSA_HEADS * HEAD_DIM), np.float32)
    for h in range(NSA_HEADS):
        for c in range(3):
            e[h * 3 + c, c * NSA_HEADS * HEAD_DIM + h * HEAD_DIM:
              c * NSA_HEADS * HEAD_DIM + (h + 1) * HEAD_DIM] = 1.0
    return jnp.asarray(e, BF16)


def _merge(x2, o_cmp, o_slc, o_win, gate_l, o_fox, o_mla, mg, p):
    T, D = x2.shape
    tok = lambda a: pl.BlockSpec((TM, a.shape[1]), lambda i: (i, 0))
    acts = [x2, o_cmp, o_slc, o_win, gate_l, o_fox, o_mla, mg]
    weights = [_gate_expand(), p["w_br_nsa"].astype(BF16), p["w_br_fox"].astype(BF16),
               p["w_br_mla"].astype(BF16), p["w_out"].astype(BF16)]
    return pl.pallas_call(
        _merge_kernel, grid=(T // TM,),
        in_specs=[tok(a) for a in acts] + [_const_spec(w.shape) for w in weights],
        out_specs=pl.BlockSpec((TM, D), lambda i: (i, 0)),
        out_shape=jax.ShapeDtypeStruct((T, D), F32),
        compiler_params=_cparams(1), name="merge_out",
    )(*acts, *weights)


def _ffn_kernel(x_ref, g_ref, wu_ref, wv_ref, cw_ref, cb_ref, wd_ref, fin_ref, o_ref, tail_ref,
                *, seq, n_chunks, final):
    tm = x_ref.shape[0]
    x = x_ref[...]
    hb = _rms(x, g_ref[...]).astype(BF16)

    @pl.when((pl.program_id(0) * tm) % seq == 0)
    def _():
        tail_ref[...] = jnp.zeros(tail_ref.shape, F32)

    cw = wu_ref.shape[1] // n_chunks
    row8 = lax.broadcasted_iota(jnp.int32, (8, cw), 0)
    acc = jnp.zeros(x.shape, F32)
    for c in range(n_chunks):
        sl = slice(c * cw, (c + 1) * cw)
        u = _dot(hb, wu_ref[:, sl])
        v = _dot(hb, wv_ref[:, sl])
        prev = tail_ref[:, sl]
        tail_ref[:, sl] = u[tm - 8:]
        u1 = pltpu.roll(u, 1, 0)
        u2 = pltpu.roll(u, 2, 0)
        h1 = jnp.where(row8 < 1, pltpu.roll(prev, 1, 0), u1[:8])
        h2 = jnp.where(row8 < 2, pltpu.roll(prev, 2, 0), u2[:8])
        u1 = jnp.concatenate([h1, u1[8:]], axis=0)
        u2 = jnp.concatenate([h2, u2[8:]], axis=0)
        uc = cw_ref[0:1, sl] * u2 + cw_ref[1:2, sl] * u1 + cw_ref[2:3, sl] * u + cb_ref[:, sl]
        act = (uc * jax.nn.sigmoid(uc) * v).astype(BF16)
        acc = acc + _dot(act, wd_ref[sl, :])
    y = x + acc
    if final:
        y = _rms(y, fin_ref[...])
    o_ref[...] = y


def _ffn(x2, p, final_norm, seq, final):
    T, D = x2.shape
    w_up = p["w_up"]
    wu, wv = w_up[:, :D_FF].astype(BF16), w_up[:, D_FF:].astype(BF16)
    row = lambda a: a.reshape(1, -1).astype(F32)
    cwp = jnp.pad(p["conv_w"].astype(F32), ((0, 8 - CONV_W), (0, 0)))
    weights = [row(p["ffn_norm"]), wu, wv, cwp, row(p["conv_b"]), p["w_down"].astype(BF16),
               row(final_norm)]
    return pl.pallas_call(
        functools.partial(_ffn_kernel, seq=seq, n_chunks=2, final=final), grid=(T // TM,),
        in_specs=[pl.BlockSpec((TM, D), lambda i: (i, 0))] + [_const_spec(w.shape) for w in weights],
        out_specs=pl.BlockSpec((TM, D), lambda i: (i, 0)),
        out_shape=jax.ShapeDtypeStruct((T, D), F32),
        scratch_shapes=[pltpu.VMEM((8, D_FF), F32)],
        compiler_params=_cparams(1), name="conv_ffn",
    )(x2, *weights)


_LAYER_PARAMS = ("mix_norm", "w_in", "b_forget", "cmp_pe_k", "cmp_w1_k", "cmp_w2_k", "cmp_pe_v",
                 "cmp_w1_v", "cmp_w2_v", "mla_q_norm", "mla_w_uq", "mla_kv_norm", "mla_w_ukv",
                 "w_br_nsa", "w_br_fox", "w_br_mla", "w_out", "ffn_norm", "w_up", "conv_w",
                 "conv_b", "w_down")


def _mixer_layer(h, tabs, p, batch, seq):
    (qn, qr, kvc, ksl, vsl_t, kwin, vwin_t, gate_l,
     fq, fk, fv_t, mq, mk, mv_t, mg) = _inproj(h, tabs, p, seq)
    cmp_kv = _compress(kvc, p, batch, seq)
    o_cmp, qa = _cmp_select(qn, qr, cmp_kv, batch, seq)
    group = (0,) * NSA_REP
    pair = (0, 1)
    o_slc = _attention(qa, ksl, vsl_t, batch, seq, group, False, F32, "nsa_slc_attn")
    o_win = _attention(qa, kwin, vwin_t, batch, seq, group, True, F32, "nsa_win_attn")
    o_fox = _attention(fq, fk, fv_t, batch, seq, pair, False, BF16, "fox_attn")
    o_mla = _attention(mq, mk, mv_t, batch, seq, pair, False, BF16, "mla_attn")
    return _merge(h, o_cmp, o_slc, o_win, gate_l, o_fox, o_mla, mg, p)


def kernel(x, positions, mix_norm, w_in, b_forget, cmp_pe_k, cmp_w1_k, cmp_w2_k, cmp_pe_v, cmp_w1_v,
           cmp_w2_v, mla_q_norm, mla_w_uq, mla_kv_norm, mla_w_ukv, w_br_nsa, w_br_fox, w_br_mla,
           w_out, ffn_norm, w_up, conv_w, conv_b, w_down, final_norm):
    batch, seq, d = x.shape
    stacked = dict(zip(_LAYER_PARAMS, (
        mix_norm, w_in, b_forget, cmp_pe_k, cmp_w1_k, cmp_w2_k, cmp_pe_v, cmp_w1_v, cmp_w2_v,
        mla_q_norm, mla_w_uq, mla_kv_norm, mla_w_ukv, w_br_nsa, w_br_fox, w_br_mla, w_out,
        ffn_norm, w_up, conv_w, conv_b, w_down)))
    depth = w_in.shape[0]
    tabs = _rope_tables(positions)
    h = x.reshape(batch * seq, d)
    for l in range(depth):
        p = {k: v[l] for k, v in stacked.items()}
        h = _mixer_layer(h, tabs, p, batch, seq)
        h = _ffn(h, p, final_norm, seq, final=(l == depth - 1))
    return h.reshape(batch, seq, d)
```

```python
import functools

import numpy as np
import jax
import jax.numpy as jnp
from jax import lax
from jax.experimental import pallas as pl
from jax.experimental.pallas import tpu as pltpu

F32 = jnp.float32
BF16 = jnp.bfloat16

HEAD_DIM = 64
ROPE_THETA = 500000.0
PARTIAL_ROT = HEAD_DIM // 4
NORM_EPS = 1e-6
NSA_HEADS = 8
NSA_GROUPS = 2
NSA_REP = NSA_HEADS // NSA_GROUPS
CMP_LEN = 32
CMP_STRIDE = 16
CMP_HID = 4 * HEAD_DIM
SLC_LEN = 64
SLC_TOPK = 16
WIN = 512
FOX_HEADS = 8
MLA_HEADS = 8
MLA_Q_LORA = 384
MLA_KV_LORA = 256
MLA_NOPE = 64
MLA_ROPE = 32
MLA_V = 64
D_FF = 2816
D_MODEL = 1024
CONV_W = 3

LANES = 128
LOG2E = 1.4426950408889634
NEG = -1e30
SEL_NEG = -1e9
VMEM_LIMIT = 52 * 1024 * 1024

TM = 256
TQ = 512
TK = 512
QC = 256
DECAY_TERMS = 3


def _cparams(n_axes):
    return pltpu.CompilerParams(dimension_semantics=("arbitrary",) * n_axes,
                                vmem_limit_bytes=VMEM_LIMIT)


def _const_spec(shape):
    nd = len(shape)
    return pl.BlockSpec(shape, lambda *_: (0,) * nd, pipeline_mode=pl.Buffered(1))


def _rms(x, g):
    return x * lax.rsqrt(jnp.mean(x * x, axis=-1, keepdims=True) + NORM_EPS) * g


def _dot(a, b):
    return jnp.dot(a, b, preferred_element_type=F32)


def _dot_nt(a, b):
    return lax.dot_general(a, b, (((1,), (1,)), ((), ())), preferred_element_type=F32)


def _rope_tables_kernel(pos_ref, f16_ref, g16_ref, f32_ref, g32_ref, c16_o, s16_o, c32_o, s32_o):
    p = pos_ref[...].astype(F32)
    a16 = p * f16_ref[...]
    c16_o[...] = jnp.cos(a16)
    s16_o[...] = jnp.sin(a16) * g16_ref[...]
    a32 = p * f32_ref[...]
    c32_o[...] = jnp.cos(a32)
    s32_o[...] = jnp.sin(a32) * g32_ref[...]


def _rope_tables(positions):
    T = positions.size
    half16, half32 = PARTIAL_ROT // 2, MLA_ROPE // 2
    inv16 = ROPE_THETA ** (-jnp.arange(half16, dtype=F32) / half16)
    inv32 = ROPE_THETA ** (-jnp.arange(half32, dtype=F32) / half32)
    lane = np.arange(LANES)
    d = lane % HEAD_DIM
    on16 = d < PARTIAL_ROT
    f16 = jnp.where(on16, inv16[d % half16], 0.0)[None, :]
    g16 = np.where(on16, np.where(d < half16, -1.0, 1.0), 0.0).astype(np.float32)[None, :]
    r = lane - MLA_NOPE
    on32 = (r >= 0) & (r < MLA_ROPE)
    f32 = jnp.where(on32, inv32[r % half32], 0.0)[None, :]
    g32 = np.where(on32, np.where(r < half32, -1.0, 1.0), 0.0).astype(np.float32)[None, :]
    pos = jnp.broadcast_to(positions.reshape(T, 1), (T, LANES))
    tt = 1024
    row = pl.BlockSpec((tt, LANES), lambda i: (i, 0))
    vec = _const_spec((1, LANES))
    out = jax.ShapeDtypeStruct((T, LANES), F32)
    return pl.pallas_call(
        _rope_tables_kernel, grid=(T // tt,),
        in_specs=[row, vec, vec, vec, vec], out_specs=[row] * 4, out_shape=[out] * 4,
        compiler_params=_cparams(1), name="rope_tables",
    )(pos, f16, jnp.asarray(g16), f32, jnp.asarray(g32))


def _rot16(x, cos, sin):
    lane = lax.broadcasted_iota(jnp.int32, x.shape, 1)
    first = (lane % HEAD_DIM) < (PARTIAL_ROT // 2)
    sw = jnp.where(first, pltpu.roll(x, LANES - PARTIAL_ROT // 2, 1), pltpu.roll(x, PARTIAL_ROT // 2, 1))
    return x * cos + sw * sin


def _rot32(x, cos, sin):
    lane = lax.broadcasted_iota(jnp.int32, x.shape, 1)
    first = lane < (MLA_NOPE + MLA_ROPE // 2)
    sw = jnp.where(first, pltpu.roll(x, LANES - MLA_ROPE // 2, 1), pltpu.roll(x, MLA_ROPE // 2, 1))
    return x * cos + sw * sin


_FF_LANE = 3 * NSA_HEADS
_IN_COLS = {}
for _name, _n in (("q", NSA_HEADS * HEAD_DIM), ("kvc", 4 * HEAD_DIM), ("ksl", LANES), ("kwin", LANES),
                  ("misc", LANES), ("fq", FOX_HEADS * HEAD_DIM), ("fk", FOX_HEADS * HEAD_DIM),
                  ("lat", MLA_Q_LORA + MLA_KV_LORA), ("mg", 3 * D_MODEL)):
    _IN_COLS[_name] = (sum(n for _, n in _IN_COLS.values()), _n)
_IN_ROWS = {"vsl": (0, LANES), "vwin": (LANES, LANES), "fv": (2 * LANES, FOX_HEADS * HEAD_DIM)}


def _inproj_kernel(x_ref, g_ref, c16_ref, s16_ref, c32_ref, s32_ref,
                   w_ref, wt_ref, bf_ref, tri_ref, place_ref,
                   qnrm_ref, wuq_ref, kvnrm_ref, wuk_ref, wuv_ref,
                   qn_o, qr_o, kvc_o, ksl_o, vsl_o, kwin_o, vwin_o, gate_o,
                   fq_o, fk_o, fv_o, mq_o, mk_o, mv_o, mg_o, fcarry_ref, *, seq):
    tm = x_ref.shape[0]
    seq_start = (pl.program_id(0) * tm) % seq == 0
    hb = _rms(x_ref[...], g_ref[...]).astype(BF16)
    c16, s16 = c16_ref[...], s16_ref[...]
    c32, s32 = c32_ref[...], s32_ref[...]
    lane = lax.broadcasted_iota(jnp.int32, (tm, LANES), 1)
    low = lane < HEAD_DIM

    def proj(name):
        a, n = _IN_COLS[name]
        return _dot(hb, w_ref[:, a:a + n])

    def proj_t(name):
        a, n = _IN_ROWS[name]
        return _dot_nt(wt_ref[a:a + n, :], hb)

    def slabs(z):
        return [z[:, k * LANES:(k + 1) * LANES] for k in range(z.shape[1] // LANES)]

    def split(z, fill=0.0):
        return jnp.where(low, z, fill), jnp.where(low, pltpu.roll(z, HEAD_DIM, 1), fill)

    def store_heads(o_ref, pair_slabs, fill=0.0, extra=None):
        for k, z in enumerate(pair_slabs):
            for e, s in enumerate(split(z, fill)):
                h = 2 * k + e
                if extra is not None:
                    s = s + extra[:, h * LANES:(h + 1) * LANES]
                o_ref[:, h * LANES:(h + 1) * LANES] = s.astype(o_ref.dtype)

    zq = slabs(proj("q"))
    store_heads(qn_o, zq)
    store_heads(qr_o, [_rot16(z, c16, s16) for z in zq])

    for k, z in enumerate(slabs(proj("kvc"))):
        for e, s in enumerate(split(z)):
            kvc_o[2 * k + e] = s[:, :HEAD_DIM]

    tok = (pl.program_id(0) * tm) % seq + lax.broadcasted_iota(jnp.int32, (tm, LANES), 0)
    onehot = jnp.where(lane - HEAD_DIM == tok // SLC_LEN, 1.0, 0.0)
    store_heads(ksl_o, [_rot16(proj("ksl"), c16, s16)], fill=onehot)
    vsl_o[0] = proj_t("vsl").astype(BF16)

    store_heads(kwin_o, [_rot16(proj("kwin"), c16, s16)])
    vwin_o[0] = proj_t("vwin").astype(BF16)

    misc = proj("misc")
    gate_o[...] = misc

    @pl.when(seq_start)
    def _():
        fcarry_ref[...] = jnp.zeros(fcarry_ref.shape, F32)

    zf = misc + bf_ref[...]
    is_f = (lane >= _FF_LANE) & (lane < _FF_LANE + FOX_HEADS)
    logf = jnp.where(is_f, jnp.minimum(zf, 0.0) - jnp.log1p(jnp.exp(-jnp.abs(zf))), 0.0)
    fsum = jnp.dot(tri_ref[...], logf, preferred_element_type=F32,
                   precision=lax.Precision.HIGHEST) + fcarry_ref[0:1]
    fcarry_ref[...] = jnp.broadcast_to(fsum[tm - 1:tm], fcarry_ref.shape)
    rest = -LOG2E * fsum
    decay = None
    for t in range(DECAY_TERMS):
        part = rest.astype(BF16)
        rest = rest - part.astype(F32)
        term = _dot(part, place_ref[t])
        decay = term if decay is None else decay + term
    store_heads(fk_o, slabs(proj("fk")), extra=decay)
    ones = jnp.where((lane >= HEAD_DIM) & (lane < HEAD_DIM + DECAY_TERMS), 1.0, 0.0)
    store_heads(fq_o, slabs(proj("fq")), fill=ones)
    fv_o[0] = proj_t("fv").astype(BF16)

    lat = proj("lat")
    cq = _rms(lat[:, :MLA_Q_LORA], qnrm_ref[...]).astype(BF16)
    ckv = _rms(lat[:, MLA_Q_LORA:], kvnrm_ref[...]).astype(BF16)
    kpe = jnp.where(low, 0.0, _rot32(misc, c32, s32))
    zuq = _dot(cq, wuq_ref[...])
    for k, z in enumerate(slabs(zuq)):
        mq_o[:, k * LANES:(k + 1) * LANES] = _rot32(z, c32, s32).astype(BF16)
    zuk = _dot(ckv, wuk_ref[...])
    for k, z in enumerate(slabs(zuk)):
        mk_o[:, k * LANES:(k + 1) * LANES] = (z + kpe).astype(BF16)
    mv_o[0] = _dot_nt(wuv_ref[...], ckv).astype(BF16)

    mg_o[...] = jax.nn.sigmoid(proj("mg"))


def _pad_halves(w, first_only=True):
    K, n = w.shape[0], w.shape[1] // HEAD_DIM
    w = w.reshape(K, n, HEAD_DIM)
    other = jnp.zeros_like(w) if first_only else w
    return jnp.concatenate([w, other], axis=-1).reshape(K, n * LANES)


def _inproj(x2, tabs, p, seq):
    T, D = x2.shape
    w_in = p["w_in"]
    scale = HEAD_DIM ** -0.5 * LOG2E
    o = 0

    def take(n):
        nonlocal o
        w = w_in[:, o:o + n]
        o += n
        return w

    nsa_w, kvw = NSA_HEADS * HEAD_DIM, NSA_GROUPS * HEAD_DIM
    w_q = take(nsa_w)
    w_kc, w_vc, w_ks, w_vs, w_kw, w_vw = [take(kvw) for _ in range(6)]
    w_g = take(3 * NSA_HEADS)
    w_fq, w_fk, w_fv = [take(FOX_HEADS * HEAD_DIM) for _ in range(3)]
    w_ff = take(FOX_HEADS)
    w_cq, w_ckv, w_kr = take(MLA_Q_LORA), take(MLA_KV_LORA), take(MLA_ROPE)
    w_mg = take(3 * D)

    bf = lambda w: w.astype(BF16)
    row = lambda a: a.reshape(1, -1).astype(F32)
    w_misc = jnp.concatenate([w_g, w_ff, jnp.zeros((D, MLA_NOPE - _FF_LANE - FOX_HEADS), F32), w_kr,
                              jnp.zeros((D, LANES - MLA_NOPE - MLA_ROPE), F32)], axis=1)
    w_tok = bf(jnp.concatenate([w_q * scale, w_kc, w_vc, w_ks, w_kw, w_misc, w_fq * scale, w_fk,
                                w_cq, w_ckv, w_mg], axis=1))
    w_tr = bf(jnp.concatenate([w_vs, w_vw, w_fv], axis=1).T)
    b_f = jnp.pad(row(p["b_forget"]), ((0, 0), (_FF_LANE, LANES - _FF_LANE - FOX_HEADS)))
    tri = jnp.asarray(np.tril(np.ones((TM, TM), np.float32)))
    place = np.zeros((DECAY_TERMS, LANES, FOX_HEADS * LANES), np.float32)
    for t in range(DECAY_TERMS):
        for h in range(FOX_HEADS):
            place[t, _FF_LANE + h, h * LANES + HEAD_DIM + t] = 1.0
    place = jnp.asarray(place, BF16)
    dq = MLA_NOPE + MLA_ROPE
    wuq = (p["mla_w_uq"] * (dq ** -0.5 * LOG2E)).reshape(MLA_Q_LORA, MLA_HEADS, dq)
    wuq = bf(jnp.pad(wuq, ((0, 0), (0, 0), (0, LANES - dq))).reshape(MLA_Q_LORA, MLA_HEADS * LANES))
    wukv = p["mla_w_ukv"].reshape(MLA_KV_LORA, MLA_HEADS, MLA_NOPE + MLA_V)
    wuk = bf(_pad_halves(wukv[:, :, :MLA_NOPE].reshape(MLA_KV_LORA, MLA_HEADS * MLA_NOPE)))
    wuv = bf(wukv[:, :, MLA_NOPE:].reshape(MLA_KV_LORA, MLA_HEADS * MLA_V).T)
    assert w_tok.shape[1] == sum(n for _, n in _IN_COLS.values())

    weights = [w_tok, w_tr, b_f, tri, place,
               row(p["mla_q_norm"]), wuq, row(p["mla_kv_norm"]), wuk, wuv]
    n = T // TM
    tok = lambda c: pl.BlockSpec((TM, c), lambda i: (i, 0))
    in_specs = ([tok(D), _const_spec((1, D))] + [tok(LANES)] * 4
                + [_const_spec(w.shape) for w in weights])
    tokens, transposed = "tokens", "transposed"
    outs = [
        (tokens, NSA_HEADS * LANES, BF16), (tokens, NSA_HEADS * LANES, BF16), None,
        (tokens, NSA_GROUPS * LANES, BF16), (transposed, NSA_GROUPS * HEAD_DIM, BF16),
        (tokens, NSA_GROUPS * LANES, BF16), (transposed, NSA_GROUPS * HEAD_DIM, BF16), (tokens, LANES, F32),
        (tokens, FOX_HEADS * LANES, BF16), (tokens, FOX_HEADS * LANES, BF16),
        (transposed, FOX_HEADS * HEAD_DIM, BF16),
        (tokens, MLA_HEADS * LANES, BF16), (tokens, MLA_HEADS * LANES, BF16),
        (transposed, MLA_HEADS * MLA_V, BF16), (tokens, 3 * D, F32)]
    out_shape, out_specs = [], []
    for spec in outs:
        if spec is None:
            out_shape.append(jax.ShapeDtypeStruct((4, T, HEAD_DIM), F32))
            out_specs.append(pl.BlockSpec((4, TM, HEAD_DIM), lambda i: (0, i, 0)))
        elif spec[0] == transposed:
            out_shape.append(jax.ShapeDtypeStruct((n, spec[1], TM), spec[2]))
            out_specs.append(pl.BlockSpec((1, spec[1], TM), lambda i: (i, 0, 0)))
        else:
            out_shape.append(jax.ShapeDtypeStruct((T, spec[1]), spec[2]))
            out_specs.append(tok(spec[1]))
    return pl.pallas_call(
        functools.partial(_inproj_kernel, seq=seq), grid=(n,),
        in_specs=in_specs, out_specs=out_specs, out_shape=out_shape,
        scratch_shapes=[pltpu.VMEM((8, LANES), F32)],
        compiler_params=_cparams(1), name="inproj",
    )(x2, row(p["mix_norm"]), *tabs, *weights)


def _attn_kernel(q_ref, k_ref, vt_ref, o_ref, qt_ref, m_ref, l_ref, acc_ref, *, tq, tk, k_of_head, window,
                 values_by_unit):
    i = pl.program_id(2)
    nh = len(k_of_head)
    per_q = tq // tk
    tv = vt_ref.shape[2]
    v_sub = tk // tv
    for h in range(nh):
        qt_ref[h] = q_ref[:, h * LANES:(h + 1) * LANES].astype(F32).T.astype(BF16)
    dist0 = (i * tq + lax.broadcasted_iota(jnp.int32, (tk, tq), 1)
             - lax.broadcasted_iota(jnp.int32, (tk, tq), 0))

    m_ref[...] = jnp.full(m_ref.shape, NEG, F32)
    l_ref[...] = jnp.zeros(l_ref.shape, F32)
    acc_ref[...] = jnp.zeros(acc_ref.shape, F32)

    def logits(j, h, cols):
        r0 = pl.multiple_of(j * tk, tk)
        kk = k_of_head[h]
        return _dot(k_ref[pl.ds(r0, tk), kk * LANES:(kk + 1) * LANES], qt_ref[h, :, cols])

    def softmax(s, h, cols, mask):
        if mask is not None:
            s = jnp.where(mask, s, NEG)
        m_old = m_ref[h, :, cols]
        m_new = jnp.maximum(m_old, jnp.max(s, axis=0, keepdims=True))
        alpha = jnp.exp2(m_old - m_new)
        p = jnp.exp2(s - m_new)
        l_ref[h, :, cols] = alpha * l_ref[h, :, cols] + jnp.sum(p, axis=0, keepdims=True)
        m_ref[h, :, cols] = m_new
        return p.astype(BF16), alpha

    def accumulate(j, h, cols, p, alpha):
        if values_by_unit:
            half = pl.ds(pl.multiple_of(pl.program_id(1) * HEAD_DIM, HEAD_DIM), HEAD_DIM)
        else:
            half = slice((h % 2) * HEAD_DIM, (h % 2 + 1) * HEAD_DIM)
        pv = sum(_dot(vt_ref[j * v_sub + c, half, :], p[c * tv:(c + 1) * tv]) for c in range(v_sub))
        acc_ref[h, :, cols] = alpha * acc_ref[h, :, cols] + pv

    def sweep(tiles):
        work = [(j, jv, mask, h, slice(c * QC, (c + 1) * QC))
                for j, jv, mask in tiles for h in range(nh) for c in range(tq // QC)]
        s_of, p_of = {}, {}
        for t in range(len(work) + 2):
            if t < len(work):
                j, _, _, h, cols = work[t]
                s_of[t] = logits(j, h, cols)
            if 0 <= t - 1 < len(work):
                _, _, mask, h, cols = work[t - 1]
                p_of[t - 1] = softmax(s_of.pop(t - 1), h, cols, None if mask is None else mask(cols))
            if 0 <= t - 2 < len(work):
                _, jv, _, h, cols = work[t - 2]
                accumulate(jv, h, cols, *p_of.pop(t - 2))

    def visible(j, valid=None):
        def mask(cols):
            dist = dist0[:, cols] - j * tk
            if not window:
                return dist >= 0
            if valid is not None:
                dist = dist + jnp.where(valid, 0, WIN)
            return lax.bitcast_convert_type(dist, jnp.uint32) < WIN
        return mask

    assert per_q == 1, "the sweeps are written for equal query and key tiles"
    if window:
        assert WIN == tk, "window sweep is written for one key tile per window"
        prev = jnp.maximum(i - 1, 0)
        sweep([(i, i, visible(i)), (prev, prev, visible(i - 1, valid=i >= 1))])
    else:
        def body(jj, c):
            sweep([(2 * jj, 2 * jj, None), (2 * jj + 1, 2 * jj + 1, None)])
            return c
        lax.fori_loop(0, i // 2, body, 0)

        @pl.when(i % 2 == 1)
        def _():
            sweep([(i - 1, i - 1, None), (i, i, visible(i))])

        @pl.when(i % 2 == 0)
        def _():
            sweep([(i, i, visible(i))])

    for k in range(nh // 2):
        a = acc_ref[2 * k] * (1.0 / l_ref[2 * k])
        b = acc_ref[2 * k + 1] * (1.0 / l_ref[2 * k + 1])
        o_ref[:, k * LANES:(k + 1) * LANES] = jnp.concatenate([a, b], axis=0).T.astype(o_ref.dtype)


def _attention(q, k, vt, batch, seq, k_of_head, window, out_dtype, name):
    T = q.shape[0]
    nq = seq // TQ
    nh, nk = len(k_of_head), max(k_of_head) + 1
    units = q.shape[1] // (nh * LANES)
    by_unit = nk == 1
    assert k.shape[1] == units * nk * LANES
    assert vt.shape == (T // TM, units * (HEAD_DIM if by_unit else LANES), TM)
    assert not by_unit or units * HEAD_DIM == LANES, "GQA value blocks must fill one slab"
    return pl.pallas_call(
        functools.partial(_attn_kernel, tq=TQ, tk=TK, k_of_head=k_of_head, window=window,
                          values_by_unit=by_unit),
        grid=(batch, units, nq),
        in_specs=[pl.BlockSpec((TQ, nh * LANES), lambda b, u, i: (b * nq + i, u)),
                  pl.BlockSpec((seq, nk * LANES), lambda b, u, i: (b, u)),
                  pl.BlockSpec((seq // TM, LANES, TM),
                               (lambda b, u, i: (b, 0, 0)) if by_unit else (lambda b, u, i: (b, u, 0)))],
        out_specs=pl.BlockSpec((TQ, nh // 2 * LANES), lambda b, u, i: (b * nq + i, u)),
        out_shape=jax.ShapeDtypeStruct((T, units * nh // 2 * LANES), out_dtype),
        scratch_shapes=[pltpu.VMEM((nh, LANES, TQ), BF16), pltpu.VMEM((nh, 1, TQ), F32),
                        pltpu.VMEM((nh, 1, TQ), F32), pltpu.VMEM((nh, HEAD_DIM, TQ), F32)],
        compiler_params=_cparams(3), name=name,
    )(q, k, vt)


def _gelu_tanh(x):
    return 0.5 * x * (1.0 + jnp.tanh(np.float32(np.sqrt(2.0 / np.pi)) * (x + 0.044715 * (x * x * x))))


def _compress_kernel(x_ref, pe_ref, w1_ref, w2_ref, o_ref):
    x = x_ref[0]
    half = x.shape[1]
    top = _dot((x + pe_ref[0, :, :half]).astype(BF16), w1_ref[0, :half, :])
    bot = _dot((x + pe_ref[0, :, half:]).astype(BF16), w1_ref[0, half:, :])
    hid = top + pltpu.roll(bot, bot.shape[0] - 1, 0)
    o_ref[0, 0] = _dot(_gelu_tanh(hid).astype(BF16), w2_ref[0]).astype(o_ref.dtype)


def _compress(kvc, p, batch, seq):
    T = kvc.shape[1]
    rows = seq // CMP_STRIDE
    x = kvc.reshape(4, T // CMP_STRIDE, CMP_STRIDE * HEAD_DIM)
    pe = jnp.stack([p["cmp_pe_k"].reshape(1, -1), p["cmp_pe_v"].reshape(1, -1)]).astype(F32)
    w1 = jnp.stack([p["cmp_w1_k"], p["cmp_w1_v"]]).astype(BF16)
    w2k = jnp.pad(p["cmp_w2_k"], ((0, 0), (0, LANES - HEAD_DIM)))
    w2v = jnp.concatenate([p["cmp_w2_v"], p["cmp_w2_v"]], axis=1)
    w2 = jnp.stack([w2k, w2v]).astype(BF16)
    return pl.pallas_call(
        _compress_kernel, grid=(4, batch),
        in_specs=[pl.BlockSpec((1, rows, x.shape[2]), lambda j, b: (j, b, 0)),
                  pl.BlockSpec((1,) + pe.shape[1:], lambda j, b: (j // 2, 0, 0)),
                  pl.BlockSpec((1,) + w1.shape[1:], lambda j, b: (j // 2, 0, 0)),
                  pl.BlockSpec((1,) + w2.shape[1:], lambda j, b: (j // 2, 0, 0))],
        out_specs=pl.BlockSpec((1, 1, rows, LANES), lambda j, b: (b, j, 0, 0)),
        out_shape=jax.ShapeDtypeStruct((batch, 4, rows, LANES), BF16),
        compiler_params=_cparams(2), name="nsa_compress",
    )(x, pe, w1, w2)


def _cmp_select_kernel(qn_ref, qr_ref, kc_ref, vc_ref, ovt_ref, o_ref, qa_ref, *, tq):
    i = pl.program_id(2)
    n_cmp = kc_ref.shape[2]
    kc, vc = kc_ref[0, 0], vc_ref[0, 0]
    t_row = i * tq + lax.broadcasted_iota(jnp.int32, (tq, n_cmp), 0)
    blk_end = CMP_STRIDE * lax.broadcasted_iota(jnp.int32, (tq, n_cmp), 1) + (CMP_LEN - 1)
    vis = blk_end <= t_row
    lane = lax.broadcasted_iota(jnp.int32, (tq, LANES), 1)
    psum = jnp.zeros((tq, n_cmp), F32)
    outs = []
    for r in range(NSA_REP):
        s = jnp.where(vis, _dot_nt(qn_ref[:, r * LANES:(r + 1) * LANES], kc), NEG)
        m = jnp.max(s, axis=-1, keepdims=True)
        e = jnp.where(vis, jnp.exp2(s - m), 0.0)
        pn = e / jnp.maximum(jnp.sum(e, axis=-1, keepdims=True), 1e-30)
        psum = psum + pn
        outs.append(_dot(pn.astype(BF16), vc))
    for k in range(NSA_REP // 2):
        o_ref[:, k * LANES:(k + 1) * LANES] = jnp.where(lane < HEAD_DIM, outs[2 * k], outs[2 * k + 1])

    p_hi = psum.astype(BF16)
    p_lo = (psum - p_hi.astype(F32)).astype(BF16)
    ovt = ovt_ref[...]
    imp = _dot_nt(ovt, p_hi) + _dot_nt(ovt, p_lo)
    n_slc = LANES - HEAD_DIM
    jb = lax.broadcasted_iota(jnp.int32, (LANES, tq), 0) - HEAD_DIM
    cur = (i * tq + lax.broadcasted_iota(jnp.int32, (LANES, tq), 1)) // SLC_LEN
    forced = (jb == 0) | (jb == cur) | (jb == cur - 1)
    imp = jnp.where(forced, 1e9, jnp.where(jb > cur, -1e9, imp))
    groups = [imp[HEAD_DIM + 8 * v:HEAD_DIM + 8 * (v + 1)] for v in range(n_slc // 8)]
    sub = lax.broadcasted_iota(jnp.int32, (8, tq), 0)
    ranks = [jnp.zeros((8, tq), F32) for _ in groups]
    for a in range(n_slc):
        row = jnp.broadcast_to(groups[a // 8][a % 8:a % 8 + 1], (8, tq))
        for v in range(len(groups)):
            if v > a // 8:
                inc = jnp.where(row >= groups[v], 1.0, 0.0)
            elif v < a // 8:
                inc = jnp.where(row > groups[v], 1.0, 0.0)
            else:
                tie = jnp.where(sub > a % 8, 1.0, 0.0)
                inc = jnp.where(row > groups[v], 1.0, jnp.where(row == groups[v], tie, 0.0))
            ranks[v] = ranks[v] + inc
    bias_t = jnp.concatenate(
        [jnp.zeros((HEAD_DIM, tq), F32)]
        + [jnp.where(rk < SLC_TOPK, 0.0, SEL_NEG) for rk in ranks], axis=0)
    bias = bias_t.T
    for r in range(NSA_REP):
        qa_ref[:, r * LANES:(r + 1) * LANES] = (
            qr_ref[:, r * LANES:(r + 1) * LANES].astype(F32) + bias).astype(BF16)


def _overlap_t(seq):
    n_cmp = (seq - CMP_LEN) // CMP_STRIDE + 1
    n_slc = seq // SLC_LEN
    assert n_slc == LANES - HEAD_DIM, "selection blocks must fill the bias half of a slab"
    c0 = CMP_STRIDE * np.arange(n_cmp)[:, None]
    s0 = SLC_LEN * np.arange(n_slc)[None, :]
    ov = np.clip(np.minimum(c0 + CMP_LEN, s0 + SLC_LEN) - np.maximum(c0, s0), 0, None) / CMP_LEN
    out = np.zeros((LANES, seq // CMP_STRIDE), np.float32)
    out[HEAD_DIM:, :n_cmp] = ov.T
    return jnp.asarray(out, BF16)


def _cmp_select(qn, qr, cmp_kv, batch, seq):
    T = qn.shape[0]
    nq = seq // TQ
    rows = seq // CMP_STRIDE
    gw = NSA_REP * LANES
    qspec = pl.BlockSpec((TQ, gw), lambda b, g, i: (b * nq + i, g))
    return pl.pallas_call(
        functools.partial(_cmp_select_kernel, tq=TQ), grid=(batch, NSA_GROUPS, nq),
        in_specs=[qspec, qspec,
                  pl.BlockSpec((1, 1, rows, LANES), lambda b, g, i: (b, g, 0, 0)),
                  pl.BlockSpec((1, 1, rows, LANES), lambda b, g, i: (b, NSA_GROUPS + g, 0, 0)),
                  _const_spec((LANES, rows))],
        out_specs=[pl.BlockSpec((TQ, NSA_REP * HEAD_DIM), lambda b, g, i: (b * nq + i, g)), qspec],
        out_shape=[jax.ShapeDtypeStruct((T, NSA_HEADS * HEAD_DIM), F32),
                   jax.ShapeDtypeStruct((T, NSA_HEADS * LANES), BF16)],
        compiler_params=_cparams(3), name="nsa_cmp_select",
    )(qn, qr, cmp_kv, cmp_kv, _overlap_t(seq))


def _merge_kernel(x_ref, oc_ref, os_ref, ow_ref, gl_ref, of_ref, om_ref, mg_ref,
                  e_ref, wn_ref, wf_ref, wm_ref, wo_ref, o_ref):
    d = x_ref.shape[1]
    nw = oc_ref.shape[1]
    sg = jax.nn.sigmoid(gl_ref[...])
    sg_hi = sg.astype(BF16)
    sg_lo = (sg - sg_hi.astype(F32)).astype(BF16)
    e = e_ref[...]
    g = _dot(sg_hi, e) + _dot(sg_lo, e)
    o_nsa = g[:, :nw] * oc_ref[...] + g[:, nw:2 * nw] * os_ref[...] + g[:, 2 * nw:] * ow_ref[...]
    mg = mg_ref[...]
    merged = (mg[:, :d] * _dot(o_nsa.astype(BF16), wn_ref[...])
              + mg[:, d:2 * d] * _dot(of_ref[...], wf_ref[...])
              + mg[:, 2 * d:] * _dot(om_ref[...], wm_ref[...]))
    o_ref[...] = x_ref[...] + _dot(merged.astype(BF16), wo_ref[...])


def _gate_expand():
    e = np.zeros((LANES, 3 * NSA_HEADS * HEAD_DIM), np.float32)
    for h in range(NSA_HEADS):
        for c in range(3):
            e[h * 3 + c, c * NSA_HEADS * HEAD_DIM + h * HEAD_DIM:
              c * NSA_HEADS * HEAD_DIM + (h + 1) * HEAD_DIM] = 1.0
    return jnp.asarray(e, BF16)


def _merge(x2, o_cmp, o_slc, o_win, gate_l, o_fox, o_mla, mg, p):
    T, D = x2.shape
    tok = lambda a: pl.BlockSpec((TM, a.shape[1]), lambda i: (i, 0))
    acts = [x2, o_cmp, o_slc, o_win, gate_l, o_fox, o_mla, mg]
    weights = [_gate_expand(), p["w_br_nsa"].astype(BF16), p["w_br_fox"].astype(BF16),
               p["w_br_mla"].astype(BF16), p["w_out"].astype(BF16)]
    return pl.pallas_call(
        _merge_kernel, grid=(T // TM,),
        in_specs=[tok(a) for a in acts] + [_const_spec(w.shape) for w in weights],
        out_specs=pl.BlockSpec((TM, D), lambda i: (i, 0)),
        out_shape=jax.ShapeDtypeStruct((T, D), F32),
        compiler_params=_cparams(1), name="merge_out",
    )(*acts, *weights)


def _ffn_kernel(x_ref, g_ref, wu_ref, wv_ref, cw_ref, cb_ref, wd_ref, fin_ref, o_ref, tail_ref,
                *, seq, n_chunks, final):
    tm = x_ref.shape[0]
    x = x_ref[...]
    hb = _rms(x, g_ref[...]).astype(BF16)

    @pl.when((pl.program_id(0) * tm) % seq == 0)
    def _():
        tail_ref[...] = jnp.zeros(tail_ref.shape, F32)

    cw = wu_ref.shape[1] // n_chunks
    row8 = lax.broadcasted_iota(jnp.int32, (8, cw), 0)
    acc = jnp.zeros(x.shape, F32)
    for c in range(n_chunks):
        sl = slice(c * cw, (c + 1) * cw)
        u = _dot(hb, wu_ref[:, sl])
        v = _dot(hb, wv_ref[:, sl])
        prev = tail_ref[:, sl]
        tail_ref[:, sl] = u[tm - 8:]
        u1 = pltpu.roll(u, 1, 0)
        u2 = pltpu.roll(u, 2, 0)
        h1 = jnp.where(row8 < 1, pltpu.roll(prev, 1, 0), u1[:8])
        h2 = jnp.where(row8 < 2, pltpu.roll(prev, 2, 0), u2[:8])
        u1 = jnp.concatenate([h1, u1[8:]], axis=0)
        u2 = jnp.concatenate([h2, u2[8:]], axis=0)
        uc = cw_ref[0:1, sl] * u2 + cw_ref[1:2, sl] * u1 + cw_ref[2:3, sl] * u + cb_ref[:, sl]
        act = (uc * jax.nn.sigmoid(uc) * v).astype(BF16)
        acc = acc + _dot(act, wd_ref[sl, :])
    y = x + acc
    if final:
        y = _rms(y, fin_ref[...])
    o_ref[...] = y


def _ffn(x2, p, final_norm, seq, final):
    T, D = x2.shape
    w_up = p["w_up"]
    wu, wv = w_up[:, :D_FF].astype(BF16), w_up[:, D_FF:].astype(BF16)
    row = lambda a: a.reshape(1, -1).astype(F32)
    cwp = jnp.pad(p["conv_w"].astype(F32), ((0, 8 - CONV_W), (0, 0)))
    weights = [row(p["ffn_norm"]), wu, wv, cwp, row(p["conv_b"]), p["w_down"].astype(BF16),
               row(final_norm)]
    return pl.pallas_call(
        functools.partial(_ffn_kernel, seq=seq, n_chunks=2, final=final), grid=(T // TM,),
        in_specs=[pl.BlockSpec((TM, D), lambda i: (i, 0))] + [_const_spec(w.shape) for w in weights],
        out_specs=pl.BlockSpec((TM, D), lambda i: (i, 0)),
        out_shape=jax.ShapeDtypeStruct((T, D), F32),
        scratch_shapes=[pltpu.VMEM((8, D_FF), F32)],
        compiler_params=_cparams(1), name="conv_ffn",
    )(x2, *weights)


_LAYER_PARAMS = ("mix_norm", "w_in", "b_forget", "cmp_pe_k", "cmp_w1_k", "cmp_w2_k", "cmp_pe_v",
                 "cmp_w1_v", "cmp_w2_v", "mla_q_norm", "mla_w_uq", "mla_kv_norm", "mla_w_ukv",
                 "w_br_nsa", "w_br_fox", "w_br_mla", "w_out", "ffn_norm", "w_up", "conv_w",
                 "conv_b", "w_down")


def _mixer_layer(h, tabs, p, batch, seq):
    (qn, qr, kvc, ksl, vsl_t, kwin, vwin_t, gate_l,
     fq, fk, fv_t, mq, mk, mv_t, mg) = _inproj(h, tabs, p, seq)
    cmp_kv = _compress(kvc, p, batch, seq)
    o_cmp, qa = _cmp_select(qn, qr, cmp_kv, batch, seq)
    group = (0,) * NSA_REP
    pair = (0, 1)
    o_slc = _attention(qa, ksl, vsl_t, batch, seq, group, False, F32, "nsa_slc_attn")
    o_win = _attention(qa, kwin, vwin_t, batch, seq, group, True, F32, "nsa_win_attn")
    o_fox = _attention(fq, fk, fv_t, batch, seq, pair, False, BF16, "fox_attn")
    o_mla = _attention(mq, mk, mv_t, batch, seq, pair, False, BF16, "mla_attn")
    return _merge(h, o_cmp, o_slc, o_win, gate_l, o_fox, o_mla, mg, p)


def kernel(x, positions, mix_norm, w_in, b_forget, cmp_pe_k, cmp_w1_k, cmp_w2_k, cmp_pe_v, cmp_w1_v,
           cmp_w2_v, mla_q_norm, mla_w_uq, mla_kv_norm, mla_w_ukv, w_br_nsa, w_br_fox, w_br_mla,
           w_out, ffn_norm, w_up, conv_w, conv_b, w_down, final_norm):
    batch, seq, d = x.shape
    stacked = dict(zip(_LAYER_PARAMS, (
        mix_norm, w_in, b_forget, cmp_pe_k, cmp_w1_k, cmp_w2_k, cmp_pe_v, cmp_w1_v, cmp_w2_v,
        mla_q_norm, mla_w_uq, mla_kv_norm, mla_w_ukv, w_br_nsa, w_br_fox, w_br_mla, w_out,
        ffn_norm, w_up, conv_w, conv_b, w_down)))
    depth = w_in.shape[0]
    tabs = _rope_tables(positions)
    h = x.reshape(batch * seq, d)
    for l in range(depth):
        p = {k: v[l] for k, v in stacked.items()}
        h = _mixer_layer(h, tabs, p, batch, seq)
        h = _ffn(h, p, final_norm, seq, final=(l == depth - 1))
    return h.reshape(batch, seq, d)
```

```python
import functools

import numpy as np
import jax
import jax.numpy as jnp
from jax import lax
from jax.experimental import pallas as pl
from jax.experimental.pallas import tpu as pltpu

F32 = jnp.float32
BF16 = jnp.bfloat16

HEAD_DIM = 64
ROPE_THETA = 500000.0
PARTIAL_ROT = HEAD_DIM // 4
NORM_EPS = 1e-6
NSA_HEADS = 8
NSA_GROUPS = 2
NSA_REP = NSA_HEADS // NSA_GROUPS
CMP_LEN = 32
CMP_STRIDE = 16
CMP_HID = 4 * HEAD_DIM
SLC_LEN = 64
SLC_TOPK = 16
WIN = 512
FOX_HEADS = 8
MLA_HEADS = 8
MLA_Q_LORA = 384
MLA_KV_LORA = 256
MLA_NOPE = 64
MLA_ROPE = 32
MLA_V = 64
D_FF = 2816
D_MODEL = 1024
CONV_W = 3

LANES = 128
LOG2E = 1.4426950408889634
NEG = -1e30
SEL_NEG = -1e9
VMEM_LIMIT = 52 * 1024 * 1024

TM = 256
TQ = 512
TK = 512
QC = 256
DECAY_TERMS = 3


def _cparams(n_axes):
    return pltpu.CompilerParams(dimension_semantics=("arbitrary",) * n_axes,
                                vmem_limit_bytes=VMEM_LIMIT)


def _const_spec(shape):
    nd = len(shape)
    return pl.BlockSpec(shape, lambda *_: (0,) * nd, pipeline_mode=pl.Buffered(1))


def _rms(x, g):
    return x * lax.rsqrt(jnp.mean(x * x, axis=-1, keepdims=True) + NORM_EPS) * g


def _dot(a, b):
    return jnp.dot(a, b, preferred_element_type=F32)


def _dot_nt(a, b):
    return lax.dot_general(a, b, (((1,), (1,)), ((), ())), preferred_element_type=F32)


def _rope_tables_kernel(pos_ref, f16_ref, g16_ref, f32_ref, g32_ref, c16_o, s16_o, c32_o, s32_o):
    p = pos_ref[...].astype(F32)
    a16 = p * f16_ref[...]
    c16_o[...] = jnp.cos(a16)
    s16_o[...] = jnp.sin(a16) * g16_ref[...]
    a32 = p * f32_ref[...]
    c32_o[...] = jnp.cos(a32)
    s32_o[...] = jnp.sin(a32) * g32_ref[...]


def _rope_tables(positions):
    T = positions.size
    half16, half32 = PARTIAL_ROT // 2, MLA_ROPE // 2
    inv16 = ROPE_THETA ** (-jnp.arange(half16, dtype=F32) / half16)
    inv32 = ROPE_THETA ** (-jnp.arange(half32, dtype=F32) / half32)
    lane = np.arange(LANES)
    d = lane % HEAD_DIM
    on16 = d < PARTIAL_ROT
    f16 = jnp.where(on16, inv16[d % half16], 0.0)[None, :]
    g16 = np.where(on16, np.where(d < half16, -1.0, 1.0), 0.0).astype(np.float32)[None, :]
    r = lane - MLA_NOPE
    on32 = (r >= 0) & (r < MLA_ROPE)
    f32 = jnp.where(on32, inv32[r % half32], 0.0)[None, :]
    g32 = np.where(on32, np.where(r < half32, -1.0, 1.0), 0.0).astype(np.float32)[None, :]
    pos = jnp.broadcast_to(positions.reshape(T, 1), (T, LANES))
    tt = 1024
    row = pl.BlockSpec((tt, LANES), lambda i: (i, 0))
    vec = _const_spec((1, LANES))
    out = jax.ShapeDtypeStruct((T, LANES), F32)
    return pl.pallas_call(
        _rope_tables_kernel, grid=(T // tt,),
        in_specs=[row, vec, vec, vec, vec], out_specs=[row] * 4, out_shape=[out] * 4,
        compiler_params=_cparams(1), name="rope_tables",
    )(pos, f16, jnp.asarray(g16), f32, jnp.asarray(g32))


def _rot16(x, cos, sin):
    lane = lax.broadcasted_iota(jnp.int32, x.shape, 1)
    first = (lane % HEAD_DIM) < (PARTIAL_ROT // 2)
    sw = jnp.where(first, pltpu.roll(x, LANES - PARTIAL_ROT // 2, 1), pltpu.roll(x, PARTIAL_ROT // 2, 1))
    return x * cos + sw * sin


def _rot32(x, cos, sin):
    lane = lax.broadcasted_iota(jnp.int32, x.shape, 1)
    first = lane < (MLA_NOPE + MLA_ROPE // 2)
    sw = jnp.where(first, pltpu.roll(x, LANES - MLA_ROPE // 2, 1), pltpu.roll(x, MLA_ROPE // 2, 1))
    return x * cos + sw * sin


_FF_LANE = 3 * NSA_HEADS
_IN_COLS = {}
for _name, _n in (("q", NSA_HEADS * HEAD_DIM), ("kvc", 4 * HEAD_DIM), ("ksl", LANES), ("kwin", LANES),
                  ("misc", LANES), ("fq", FOX_HEADS * HEAD_DIM), ("fk", FOX_HEADS * HEAD_DIM),
                  ("lat", MLA_Q_LORA + MLA_KV_LORA), ("mg", 3 * D_MODEL)):
    _IN_COLS[_name] = (sum(n for _, n in _IN_COLS.values()), _n)
_IN_ROWS = {"vsl": (0, LANES), "vwin": (LANES, LANES), "fv": (2 * LANES, FOX_HEADS * HEAD_DIM)}


def _inproj_kernel(x_ref, g_ref, c16_ref, s16_ref, c32_ref, s32_ref,
                   w_ref, wt_ref, bf_ref, tri_ref, place_ref,
                   qnrm_ref, wuq_ref, kvnrm_ref, wuk_ref, wuv_ref,
                   qn_o, qr_o, kvc_o, ksl_o, vsl_o, kwin_o, vwin_o, gate_o,
                   fq_o, fk_o, fv_o, mq_o, mk_o, mv_o, mg_o, fcarry_ref, *, seq):
    tm = x_ref.shape[0]
    seq_start = (pl.program_id(0) * tm) % seq == 0
    hb = _rms(x_ref[...], g_ref[...]).astype(BF16)
    c16, s16 = c16_ref[...], s16_ref[...]
    c32, s32 = c32_ref[...], s32_ref[...]
    lane = lax.broadcasted_iota(jnp.int32, (tm, LANES), 1)
    low = lane < HEAD_DIM

    def proj(name):
        a, n = _IN_COLS[name]
        return _dot(hb, w_ref[:, a:a + n])

    def proj_t(name):
        a, n = _IN_ROWS[name]
        return _dot_nt(wt_ref[a:a + n, :], hb)

    def slabs(z):
        return [z[:, k * LANES:(k + 1) * LANES] for k in range(z.shape[1] // LANES)]

    def split(z, fill=0.0):
        return jnp.where(low, z, fill), jnp.where(low, pltpu.roll(z, HEAD_DIM, 1), fill)

    def store_heads(o_ref, pair_slabs, fill=0.0, extra=None):
        for k, z in enumerate(pair_slabs):
            for e, s in enumerate(split(z, fill)):
                h = 2 * k + e
                if extra is not None:
                    s = s + extra[:, h * LANES:(h + 1) * LANES]
                o_ref[:, h * LANES:(h + 1) * LANES] = s.astype(o_ref.dtype)

    zq = slabs(proj("q"))
    store_heads(qn_o, zq)
    store_heads(qr_o, [_rot16(z, c16, s16) for z in zq])

    for k, z in enumerate(slabs(proj("kvc"))):
        for e, s in enumerate(split(z)):
            kvc_o[2 * k + e] = s[:, :HEAD_DIM]

    tok = (pl.program_id(0) * tm) % seq + lax.broadcasted_iota(jnp.int32, (tm, LANES), 0)
    onehot = jnp.where(lane - HEAD_DIM == tok // SLC_LEN, 1.0, 0.0)
    store_heads(ksl_o, [_rot16(proj("ksl"), c16, s16)], fill=onehot)
    vsl_o[0] = proj_t("vsl").astype(BF16)

    store_heads(kwin_o, [_rot16(proj("kwin"), c16, s16)])
    vwin_o[0] = proj_t("vwin").astype(BF16)

    misc = proj("misc")
    gate_o[...] = misc

    @pl.when(seq_start)
    def _():
        fcarry_ref[...] = jnp.zeros(fcarry_ref.shape, F32)

    zf = misc + bf_ref[...]
    is_f = (lane >= _FF_LANE) & (lane < _FF_LANE + FOX_HEADS)
    logf = jnp.where(is_f, jnp.minimum(zf, 0.0) - jnp.log1p(jnp.exp(-jnp.abs(zf))), 0.0)
    fsum = jnp.dot(tri_ref[...], logf, preferred_element_type=F32,
                   precision=lax.Precision.HIGHEST) + fcarry_ref[0:1]
    fcarry_ref[...] = jnp.broadcast_to(fsum[tm - 1:tm], fcarry_ref.shape)
    rest = -LOG2E * fsum
    decay = None
    for t in range(DECAY_TERMS):
        part = rest.astype(BF16)
        rest = rest - part.astype(F32)
        term = _dot(part, place_ref[t])
        decay = term if decay is None else decay + term
    store_heads(fk_o, slabs(proj("fk")), extra=decay)
    ones = jnp.where((lane >= HEAD_DIM) & (lane < HEAD_DIM + DECAY_TERMS), 1.0, 0.0)
    store_heads(fq_o, slabs(proj("fq")), fill=ones)
    fv_o[0] = proj_t("fv").astype(BF16)

    lat = proj("lat")
    cq = _rms(lat[:, :MLA_Q_LORA], qnrm_ref[...]).astype(BF16)
    ckv = _rms(lat[:, MLA_Q_LORA:], kvnrm_ref[...]).astype(BF16)
    kpe = jnp.where(low, 0.0, _rot32(misc, c32, s32))
    zuq = _dot(cq, wuq_ref[...])
    for k, z in enumerate(slabs(zuq)):
        mq_o[:, k * LANES:(k + 1) * LANES] = _rot32(z, c32, s32).astype(BF16)
    zuk = _dot(ckv, wuk_ref[...])
    for k, z in enumerate(slabs(zuk)):
        mk_o[:, k * LANES:(k + 1) * LANES] = (z + kpe).astype(BF16)
    mv_o[0] = _dot_nt(wuv_ref[...], ckv).astype(BF16)

    mg_o[...] = jax.nn.sigmoid(proj("mg"))


def _pad_halves(w, first_only=True):
    K, n = w.shape[0], w.shape[1] // HEAD_DIM
    w = w.reshape(K, n, HEAD_DIM)
    other = jnp.zeros_like(w) if first_only else w
    return jnp.concatenate([w, other], axis=-1).reshape(K, n * LANES)


def _inproj(x2, tabs, p, seq):
    T, D = x2.shape
    w_in = p["w_in"]
    scale = HEAD_DIM ** -0.5 * LOG2E
    o = 0

    def take(n):
        nonlocal o
        w = w_in[:, o:o + n]
        o += n
        return w

    nsa_w, kvw = NSA_HEADS * HEAD_DIM, NSA_GROUPS * HEAD_DIM
    w_q = take(nsa_w)
    w_kc, w_vc, w_ks, w_vs, w_kw, w_vw = [take(kvw) for _ in range(6)]
    w_g = take(3 * NSA_HEADS)
    w_fq, w_fk, w_fv = [take(FOX_HEADS * HEAD_DIM) for _ in range(3)]
    w_ff = take(FOX_HEADS)
    w_cq, w_ckv, w_kr = take(MLA_Q_LORA), take(MLA_KV_LORA), take(MLA_ROPE)
    w_mg = take(3 * D)

    bf = lambda w: w.astype(BF16)
    row = lambda a: a.reshape(1, -1).astype(F32)
    w_misc = jnp.concatenate([w_g, w_ff, jnp.zeros((D, MLA_NOPE - _FF_LANE - FOX_HEADS), F32), w_kr,
                              jnp.zeros((D, LANES - MLA_NOPE - MLA_ROPE), F32)], axis=1)
    w_tok = bf(jnp.concatenate([w_q * scale, w_kc, w_vc, w_ks, w_kw, w_misc, w_fq * scale, w_fk,
                                w_cq, w_ckv, w_mg], axis=1))
    w_tr = bf(jnp.concatenate([w_vs, w_vw, w_fv], axis=1).T)
    b_f = jnp.pad(row(p["b_forget"]), ((0, 0), (_FF_LANE, LANES - _FF_LANE - FOX_HEADS)))
    tri = jnp.asarray(np.tril(np.ones((TM, TM), np.float32)))
    place = np.zeros((DECAY_TERMS, LANES, FOX_HEADS * LANES), np.float32)
    for t in range(DECAY_TERMS):
        for h in range(FOX_HEADS):
            place[t, _FF_LANE + h, h * LANES + HEAD_DIM + t] = 1.0
    place = jnp.asarray(place, BF16)
    dq = MLA_NOPE + MLA_ROPE
    wuq = (p["mla_w_uq"] * (dq ** -0.5 * LOG2E)).reshape(MLA_Q_LORA, MLA_HEADS, dq)
    wuq = bf(jnp.pad(wuq, ((0, 0), (0, 0), (0, LANES - dq))).reshape(MLA_Q_LORA, MLA_HEADS * LANES))
    wukv = p["mla_w_ukv"].reshape(MLA_KV_LORA, MLA_HEADS, MLA_NOPE + MLA_V)
    wuk = bf(_pad_halves(wukv[:, :, :MLA_NOPE].reshape(MLA_KV_LORA, MLA_HEADS * MLA_NOPE)))
    wuv = bf(wukv[:, :, MLA_NOPE:].reshape(MLA_KV_LORA, MLA_HEADS * MLA_V).T)
    assert w_tok.shape[1] == sum(n for _, n in _IN_COLS.values())

    weights = [w_tok, w_tr, b_f, tri, place,
               row(p["mla_q_norm"]), wuq, row(p["mla_kv_norm"]), wuk, wuv]
    n = T // TM
    tok = lambda c: pl.BlockSpec((TM, c), lambda i: (i, 0))
    in_specs = ([tok(D), _const_spec((1, D))] + [tok(LANES)] * 4
                + [_const_spec(w.shape) for w in weights])
    tokens, transposed = "tokens", "transposed"
    outs = [
        (tokens, NSA_HEADS * LANES, BF16), (tokens, NSA_HEADS * LANES, BF16), None,
        (tokens, NSA_GROUPS * LANES, BF16), (transposed, NSA_GROUPS * HEAD_DIM, BF16),
        (tokens, NSA_GROUPS * LANES, BF16), (transposed, NSA_GROUPS * HEAD_DIM, BF16), (tokens, LANES, F32),
        (tokens, FOX_HEADS * LANES, BF16), (tokens, FOX_HEADS * LANES, BF16),
        (transposed, FOX_HEADS * HEAD_DIM, BF16),
        (tokens, MLA_HEADS * LANES, BF16), (tokens, MLA_HEADS * LANES, BF16),
        (transposed, MLA_HEADS * MLA_V, BF16), (tokens, 3 * D, F32)]
    out_shape, out_specs = [], []
    for spec in outs:
        if spec is None:
            out_shape.append(jax.ShapeDtypeStruct((4, T, HEAD_DIM), F32))
            out_specs.append(pl.BlockSpec((4, TM, HEAD_DIM), lambda i: (0, i, 0)))
        elif spec[0] == transposed:
            out_shape.append(jax.ShapeDtypeStruct((n, spec[1], TM), spec[2]))
            out_specs.append(pl.BlockSpec((1, spec[1], TM), lambda i: (i, 0, 0)))
        else:
            out_shape.append(jax.ShapeDtypeStruct((T, spec[1]), spec[2]))
            out_specs.append(tok(spec[1]))
    return pl.pallas_call(
        functools.partial(_inproj_kernel, seq=seq), grid=(n,),
        in_specs=in_specs, out_specs=out_specs, out_shape=out_shape,
        scratch_shapes=[pltpu.VMEM((8, LANES), F32)],
        compiler_params=_cparams(1), name="inproj",
    )(x2, row(p["mix_norm"]), *tabs, *weights)


def _attn_kernel(*refs, tq, tk, n_streams, k_of_head, window, values_by_unit):
    ins, outs = refs[:3 * n_streams], refs[3 * n_streams:4 * n_streams]
    qt_ref, m_ref, l_ref, acc_ref = refs[4 * n_streams:]
    i = pl.program_id(2)
    hs = len(k_of_head)
    nh = n_streams * hs
    q_of = lambda h: ins[3 * (h // hs)]
    k_of = lambda h: ins[3 * (h // hs) + 1]
    vt_of = lambda h: ins[3 * (h // hs) + 2]
    per_q = tq // tk
    tv = ins[2].shape[2]
    v_sub = tk // tv
    for h in range(nh):
        qt_ref[h] = q_of(h)[:, (h % hs) * LANES:(h % hs + 1) * LANES].astype(F32).T.astype(BF16)
    dist0 = (i * tq + lax.broadcasted_iota(jnp.int32, (tk, tq), 1)
             - lax.broadcasted_iota(jnp.int32, (tk, tq), 0))

    m_ref[...] = jnp.full(m_ref.shape, NEG, F32)
    l_ref[...] = jnp.zeros(l_ref.shape, F32)
    acc_ref[...] = jnp.zeros(acc_ref.shape, F32)

    def logits(j, lo, n, h, cols):
        r0 = pl.multiple_of(j * tk + lo, tv)
        kk = k_of_head[h % hs]
        return _dot(k_of(h)[pl.ds(r0, n), kk * LANES:(kk + 1) * LANES], qt_ref[h, :, cols])

    def softmax(s, h, cols, mask):
        if mask is not None:
            s = jnp.where(mask, s, NEG)
        m_old = m_ref[h, :, cols]
        m_new = jnp.maximum(m_old, jnp.max(s, axis=0, keepdims=True))
        alpha = jnp.exp2(m_old - m_new)
        p = jnp.exp2(s - m_new)
        l_ref[h, :, cols] = alpha * l_ref[h, :, cols] + jnp.sum(p, axis=0, keepdims=True)
        m_ref[h, :, cols] = m_new
        return p.astype(BF16), alpha

    def accumulate(j, lo, n, h, cols, p, alpha):
        if values_by_unit:
            half = pl.ds(pl.multiple_of(pl.program_id(1) * HEAD_DIM, HEAD_DIM), HEAD_DIM)
        else:
            half = slice((h % 2) * HEAD_DIM, (h % 2 + 1) * HEAD_DIM)
        pv = sum(_dot(vt_of(h)[j * v_sub + lo // tv + c, half, :], p[c * tv:(c + 1) * tv])
                 for c in range(n // tv))
        acc_ref[h, :, cols] = alpha * acc_ref[h, :, cols] + pv

    def sweep(tiles):
        work = []
        for j, n_tiles, mask, span in tiles:
            for h in range(nh):
                for c in range(tq // QC):
                    lo, n = (0, n_tiles * tk) if span is None else span(c)
                    work.append((j, lo, n, mask, h, slice(c * QC, (c + 1) * QC)))
        s_of, p_of = {}, {}
        for t in range(len(work) + 2):
            if t < len(work):
                j, lo, n, _, h, cols = work[t]
                s_of[t] = logits(j, lo, n, h, cols)
            if 0 <= t - 1 < len(work):
                _, lo, n, mask, h, cols = work[t - 1]
                p_of[t - 1] = softmax(s_of.pop(t - 1), h, cols,
                                      None if mask is None else mask(slice(lo, lo + n), cols))
            if 0 <= t - 2 < len(work):
                j, lo, n, _, h, cols = work[t - 2]
                accumulate(j, lo, n, h, cols, *p_of.pop(t - 2))

    def visible(j, valid=None):
        def mask(keys, cols):
            dist = dist0[keys, cols] - j * tk
            if not window:
                return dist >= 0
            if valid is not None:
                dist = dist + jnp.where(valid, 0, WIN)
            return lax.bitcast_convert_type(dist, jnp.uint32) < WIN
        return mask

    assert per_q == 1 and tk == 2 * QC, "the sweeps are written for tq == tk == two query chunks"
    own_span = lambda c: (0, (c + 1) * QC)
    if window:
        assert WIN == tk, "window sweep is written for one key tile per window"
        prev = jnp.maximum(i - 1, 0)
        sweep([(i, 1, visible(i), own_span),
               (prev, 1, visible(i - 1, valid=i >= 1), lambda c: (c * QC, tk - c * QC))])
    else:
        one_step = nh * (tq // QC) >= 8

        def body(jj, c):
            sweep([(2 * jj, 2, None, None)] if one_step else
                  [(2 * jj, 1, None, None), (2 * jj + 1, 1, None, None)])
            return c
        lax.fori_loop(0, i // 2, body, 0)

        @pl.when(i % 2 == 1)
        def _():
            sweep([(i - 1, 1, None, None), (i, 1, visible(i), own_span)])

        @pl.when(i % 2 == 0)
        def _():
            sweep([(i, 1, visible(i), own_span)])

    for h in range(0, nh, 2):
        a = acc_ref[h] * (1.0 / l_ref[h])
        b = acc_ref[h + 1] * (1.0 / l_ref[h + 1])
        o_ref, k = outs[h // hs], (h % hs) // 2
        o_ref[:, k * LANES:(k + 1) * LANES] = jnp.concatenate([a, b], axis=0).T.astype(o_ref.dtype)


def _attention(streams, batch, seq, k_of_head, window, out_dtype, name):
    T = streams[0][0].shape[0]
    nq = seq // TQ
    hs, nk = len(k_of_head), max(k_of_head) + 1
    nh = hs * len(streams)
    units = streams[0][0].shape[1] // (hs * LANES)
    by_unit = nk == 1
    assert not by_unit or units * HEAD_DIM == LANES, "GQA value blocks must fill one slab"
    in_specs, args = [], []
    for q, k, vt in streams:
        assert q.shape[1] == units * hs * LANES and k.shape[1] == units * nk * LANES
        assert vt.shape == (T // TM, units * (HEAD_DIM if by_unit else LANES), TM)
        in_specs += [pl.BlockSpec((TQ, hs * LANES), lambda b, u, i: (b * nq + i, u)),
                     pl.BlockSpec((seq, nk * LANES), lambda b, u, i: (b, u)),
                     pl.BlockSpec((seq // TM, LANES, TM),
                                  (lambda b, u, i: (b, 0, 0)) if by_unit else (lambda b, u, i: (b, u, 0)))]
        args += [q, k, vt]
    return pl.pallas_call(
        functools.partial(_attn_kernel, tq=TQ, tk=TK, n_streams=len(streams), k_of_head=k_of_head,
                          window=window, values_by_unit=by_unit),
        grid=(batch, units, nq), in_specs=in_specs,
        out_specs=[pl.BlockSpec((TQ, hs // 2 * LANES), lambda b, u, i: (b * nq + i, u))] * len(streams),
        out_shape=[jax.ShapeDtypeStruct((T, units * hs // 2 * LANES), out_dtype)] * len(streams),
        scratch_shapes=[pltpu.VMEM((nh, LANES, TQ), BF16), pltpu.VMEM((nh, 1, TQ), F32),
                        pltpu.VMEM((nh, 1, TQ), F32), pltpu.VMEM((nh, HEAD_DIM, TQ), F32)],
        compiler_params=_cparams(3), name=name,
    )(*args)


def _gelu_tanh(x):
    return 0.5 * x * (1.0 + jnp.tanh(np.float32(np.sqrt(2.0 / np.pi)) * (x + 0.044715 * (x * x * x))))


def _compress_kernel(x_ref, pe_ref, w1_ref, w2_ref, o_ref):
    x = x_ref[0]
    half = x.shape[1]
    top = _dot((x + pe_ref[0, :, :half]).astype(BF16), w1_ref[0, :half, :])
    bot = _dot((x + pe_ref[0, :, half:]).astype(BF16), w1_ref[0, half:, :])
    hid = top + pltpu.roll(bot, bot.shape[0] - 1, 0)
    o_ref[0, 0] = _dot(_gelu_tanh(hid).astype(BF16), w2_ref[0]).astype(o_ref.dtype)


def _compress(kvc, p, batch, seq):
    T = kvc.shape[1]
    rows = seq // CMP_STRIDE
    x = kvc.reshape(4, T // CMP_STRIDE, CMP_STRIDE * HEAD_DIM)
    pe = jnp.stack([p["cmp_pe_k"].reshape(1, -1), p["cmp_pe_v"].reshape(1, -1)]).astype(F32)
    w1 = jnp.stack([p["cmp_w1_k"], p["cmp_w1_v"]]).astype(BF16)
    w2k = jnp.pad(p["cmp_w2_k"], ((0, 0), (0, LANES - HEAD_DIM)))
    w2v = jnp.concatenate([p["cmp_w2_v"], p["cmp_w2_v"]], axis=1)
    w2 = jnp.stack([w2k, w2v]).astype(BF16)
    return pl.pallas_call(
        _compress_kernel, grid=(4, batch),
        in_specs=[pl.BlockSpec((1, rows, x.shape[2]), lambda j, b: (j, b, 0)),
                  pl.BlockSpec((1,) + pe.shape[1:], lambda j, b: (j // 2, 0, 0)),
                  pl.BlockSpec((1,) + w1.shape[1:], lambda j, b: (j // 2, 0, 0)),
                  pl.BlockSpec((1,) + w2.shape[1:], lambda j, b: (j // 2, 0, 0))],
        out_specs=pl.BlockSpec((1, 1, rows, LANES), lambda j, b: (b, j, 0, 0)),
        out_shape=jax.ShapeDtypeStruct((batch, 4, rows, LANES), BF16),
        compiler_params=_cparams(2), name="nsa_compress",
    )(x, pe, w1, w2)


def _cmp_select_kernel(qn_ref, qr_ref, kc_ref, vc_ref, ovt_ref, o_ref, qa_ref, *, tq):
    i = pl.program_id(2)
    n_cmp = kc_ref.shape[2]
    kc, vc = kc_ref[0, 0], vc_ref[0, 0]
    t_row = i * tq + lax.broadcasted_iota(jnp.int32, (tq, n_cmp), 0)
    blk_end = CMP_STRIDE * lax.broadcasted_iota(jnp.int32, (tq, n_cmp), 1) + (CMP_LEN - 1)
    vis = blk_end <= t_row
    lane = lax.broadcasted_iota(jnp.int32, (tq, LANES), 1)
    psum = jnp.zeros((tq, n_cmp), F32)
    outs = []
    for r in range(NSA_REP):
        s = jnp.where(vis, _dot_nt(qn_ref[:, r * LANES:(r + 1) * LANES], kc), NEG)
        m = jnp.max(s, axis=-1, keepdims=True)
        e = jnp.where(vis, jnp.exp2(s - m), 0.0)
        pn = e / jnp.maximum(jnp.sum(e, axis=-1, keepdims=True), 1e-30)
        psum = psum + pn
        outs.append(_dot(pn.astype(BF16), vc))
    for k in range(NSA_REP // 2):
        o_ref[:, k * LANES:(k + 1) * LANES] = jnp.where(lane < HEAD_DIM, outs[2 * k], outs[2 * k + 1])

    p_hi = psum.astype(BF16)
    p_lo = (psum - p_hi.astype(F32)).astype(BF16)
    ovt = ovt_ref[...]
    imp = _dot_nt(ovt, p_hi) + _dot_nt(ovt, p_lo)
    n_slc = LANES - HEAD_DIM
    jb = lax.broadcasted_iota(jnp.int32, (LANES, tq), 0) - HEAD_DIM
    cur = (i * tq + lax.broadcasted_iota(jnp.int32, (LANES, tq), 1)) // SLC_LEN
    forced = (jb == 0) | (jb == cur) | (jb == cur - 1)
    imp = jnp.where(forced, 1e9, jnp.where(jb > cur, -1e9, imp))
    groups = [imp[HEAD_DIM + 8 * v:HEAD_DIM + 8 * (v + 1)] for v in range(n_slc // 8)]
    sub = lax.broadcasted_iota(jnp.int32, (8, tq), 0)
    ranks = [jnp.zeros((8, tq), F32) for _ in groups]
    for a in range(n_slc):
        row = jnp.broadcast_to(groups[a // 8][a % 8:a % 8 + 1], (8, tq))
        for v in range(len(groups)):
            if v > a // 8:
                inc = jnp.where(row >= groups[v], 1.0, 0.0)
            elif v < a // 8:
                inc = jnp.where(row > groups[v], 1.0, 0.0)
            else:
                tie = jnp.where(sub > a % 8, 1.0, 0.0)
                inc = jnp.where(row > groups[v], 1.0, jnp.where(row == groups[v], tie, 0.0))
            ranks[v] = ranks[v] + inc
    bias_t = jnp.concatenate(
        [jnp.zeros((HEAD_DIM, tq), F32)]
        + [jnp.where(rk < SLC_TOPK, 0.0, SEL_NEG) for rk in ranks], axis=0)
    bias = bias_t.T
    for r in range(NSA_REP):
        qa_ref[:, r * LANES:(r + 1) * LANES] = (
            qr_ref[:, r * LANES:(r + 1) * LANES].astype(F32) + bias).astype(BF16)


def _overlap_t(seq):
    n_cmp = (seq - CMP_LEN) // CMP_STRIDE + 1
    n_slc = seq // SLC_LEN
    assert n_slc == LANES - HEAD_DIM, "selection blocks must fill the bias half of a slab"
    c0 = CMP_STRIDE * np.arange(n_cmp)[:, None]
    s0 = SLC_LEN * np.arange(n_slc)[None, :]
    ov = np.clip(np.minimum(c0 + CMP_LEN, s0 + SLC_LEN) - np.maximum(c0, s0), 0, None) / CMP_LEN
    out = np.zeros((LANES, seq // CMP_STRIDE), np.float32)
    out[HEAD_DIM:, :n_cmp] = ov.T
    return jnp.asarray(out, BF16)


def _cmp_select(qn, qr, cmp_kv, batch, seq):
    T = qn.shape[0]
    nq = seq // TQ
    rows = seq // CMP_STRIDE
    gw = NSA_REP * LANES
    qspec = pl.BlockSpec((TQ, gw), lambda b, g, i: (b * nq + i, g))
    return pl.pallas_call(
        functools.partial(_cmp_select_kernel, tq=TQ), grid=(batch, NSA_GROUPS, nq),
        in_specs=[qspec, qspec,
                  pl.BlockSpec((1, 1, rows, LANES), lambda b, g, i: (b, g, 0, 0)),
                  pl.BlockSpec((1, 1, rows, LANES), lambda b, g, i: (b, NSA_GROUPS + g, 0, 0)),
                  _const_spec((LANES, rows))],
        out_specs=[pl.BlockSpec((TQ, NSA_REP * HEAD_DIM), lambda b, g, i: (b * nq + i, g)), qspec],
        out_shape=[jax.ShapeDtypeStruct((T, NSA_HEADS * HEAD_DIM), F32),
                   jax.ShapeDtypeStruct((T, NSA_HEADS * LANES), BF16)],
        compiler_params=_cparams(3), name="nsa_cmp_select",
    )(qn, qr, cmp_kv, cmp_kv, _overlap_t(seq))


def _merge_kernel(x_ref, oc_ref, os_ref, ow_ref, gl_ref, of_ref, om_ref, mg_ref,
                  e_ref, wn_ref, wf_ref, wm_ref, wo_ref, o_ref):
    d = x_ref.shape[1]
    nw = oc_ref.shape[1]
    sg = jax.nn.sigmoid(gl_ref[...])
    sg_hi = sg.astype(BF16)
    sg_lo = (sg - sg_hi.astype(F32)).astype(BF16)
    e = e_ref[...]
    g = _dot(sg_hi, e) + _dot(sg_lo, e)
    o_nsa = g[:, :nw] * oc_ref[...] + g[:, nw:2 * nw] * os_ref[...] + g[:, 2 * nw:] * ow_ref[...]
    mg = mg_ref[...]
    merged = (mg[:, :d] * _dot(o_nsa.astype(BF16), wn_ref[...])
              + mg[:, d:2 * d] * _dot(of_ref[...], wf_ref[...])
              + mg[:, 2 * d:] * _dot(om_ref[...], wm_ref[...]))
    o_ref[...] = x_ref[...] + _dot(merged.astype(BF16), wo_ref[...])


def _gate_expand():
    e = np.zeros((LANES, 3 * NSA_HEADS * HEAD_DIM), np.float32)
    for h in range(NSA_HEADS):
        for c in range(3):
            e[h * 3 + c, c * NSA_HEADS * HEAD_DIM + h * HEAD_DIM:
              c * NSA_HEADS * HEAD_DIM + (h + 1) * HEAD_DIM] = 1.0
    return jnp.asarray(e, BF16)


def _merge(x2, o_cmp, o_slc, o_win, gate_l, o_fox, o_mla, mg, p):
    T, D = x2.shape
    tok = lambda a: pl.BlockSpec((TM, a.shape[1]), lambda i: (i, 0))
    acts = [x2, o_cmp, o_slc, o_win, gate_l, o_fox, o_mla, mg]
    weights = [_gate_expand(), p["w_br_nsa"].astype(BF16), p["w_br_fox"].astype(BF16),
               p["w_br_mla"].astype(BF16), p["w_out"].astype(BF16)]
    return pl.pallas_call(
        _merge_kernel, grid=(T // TM,),
        in_specs=[tok(a) for a in acts] + [_const_spec(w.shape) for w in weights],
        out_specs=pl.BlockSpec((TM, D), lambda i: (i, 0)),
        out_shape=jax.ShapeDtypeStruct((T, D), F32),
        compiler_params=_cparams(1), name="merge_out",
    )(*acts, *weights)


def _ffn_kernel(x_ref, g_ref, wu_ref, wv_ref, cw_ref, cb_ref, wd_ref, fin_ref, o_ref, tail_ref,
                *, seq, n_chunks, final):
    tm = x_ref.shape[0]
    x = x_ref[...]
    hb = _rms(x, g_ref[...]).astype(BF16)

    @pl.when((pl.program_id(0) * tm) % seq == 0)
    def _():
        tail_ref[...] = jnp.zeros(tail_ref.shape, F32)

    cw = wu_ref.shape[1] // n_chunks
    row8 = lax.broadcasted_iota(jnp.int32, (8, cw), 0)
    acc = jnp.zeros(x.shape, F32)
    for c in range(n_chunks):
        sl = slice(c * cw, (c + 1) * cw)
        u = _dot(hb, wu_ref[:, sl])
        v = _dot(hb, wv_ref[:, sl])
        prev = tail_ref[:, sl]
        tail_ref[:, sl] = u[tm - 8:]
        u1 = pltpu.roll(u, 1, 0)
        u2 = pltpu.roll(u, 2, 0)
        h1 = jnp.where(row8 < 1, pltpu.roll(prev, 1, 0), u1[:8])
        h2 = jnp.where(row8 < 2, pltpu.roll(prev, 2, 0), u2[:8])
        u1 = jnp.concatenate([h1, u1[8:]], axis=0)
        u2 = jnp.concatenate([h2, u2[8:]], axis=0)
        uc = cw_ref[0:1, sl] * u2 + cw_ref[1:2, sl] * u1 + cw_ref[2:3, sl] * u + cb_ref[:, sl]
        act = (uc * jax.nn.sigmoid(uc) * v).astype(BF16)
        acc = acc + _dot(act, wd_ref[sl, :])
    y = x + acc
    if final:
        y = _rms(y, fin_ref[...])
    o_ref[...] = y


def _ffn(x2, p, final_norm, seq, final):
    T, D = x2.shape
    w_up = p["w_up"]
    wu, wv = w_up[:, :D_FF].astype(BF16), w_up[:, D_FF:].astype(BF16)
    row = lambda a: a.reshape(1, -1).astype(F32)
    cwp = jnp.pad(p["conv_w"].astype(F32), ((0, 8 - CONV_W), (0, 0)))
    weights = [row(p["ffn_norm"]), wu, wv, cwp, row(p["conv_b"]), p["w_down"].astype(BF16),
               row(final_norm)]
    return pl.pallas_call(
        functools.partial(_ffn_kernel, seq=seq, n_chunks=2, final=final), grid=(T // TM,),
        in_specs=[pl.BlockSpec((TM, D), lambda i: (i, 0))] + [_const_spec(w.shape) for w in weights],
        out_specs=pl.BlockSpec((TM, D), lambda i: (i, 0)),
        out_shape=jax.ShapeDtypeStruct((T, D), F32),
        scratch_shapes=[pltpu.VMEM((8, D_FF), F32)],
        compiler_params=_cparams(1), name="conv_ffn",
    )(x2, *weights)


_LAYER_PARAMS = ("mix_norm", "w_in", "b_forget", "cmp_pe_k", "cmp_w1_k", "cmp_w2_k", "cmp_pe_v",
                 "cmp_w1_v", "cmp_w2_v", "mla_q_norm", "mla_w_uq", "mla_kv_norm", "mla_w_ukv",
                 "w_br_nsa", "w_br_fox", "w_br_mla", "w_out", "ffn_norm", "w_up", "conv_w",
                 "conv_b", "w_down")


def _mixer_layer(h, tabs, p, batch, seq):
    (qn, qr, kvc, ksl, vsl_t, kwin, vwin_t, gate_l,
     fq, fk, fv_t, mq, mk, mv_t, mg) = _inproj(h, tabs, p, seq)
    cmp_kv = _compress(kvc, p, batch, seq)
    o_cmp, qa = _cmp_select(qn, qr, cmp_kv, batch, seq)
    group = (0,) * NSA_REP
    pair = (0, 1)
    o_slc, = _attention([(qa, ksl, vsl_t)], batch, seq, group, False, F32, "nsa_slc_attn")
    o_win, = _attention([(qa, kwin, vwin_t)], batch, seq, group, True, F32, "nsa_win_attn")
    o_fox, o_mla = _attention([(fq, fk, fv_t), (mq, mk, mv_t)], batch, seq, pair, False, BF16,
                              "fox_mla_attn")
    return _merge(h, o_cmp, o_slc, o_win, gate_l, o_fox, o_mla, mg, p)


def kernel(x, positions, mix_norm, w_in, b_forget, cmp_pe_k, cmp_w1_k, cmp_w2_k, cmp_pe_v, cmp_w1_v,
           cmp_w2_v, mla_q_norm, mla_w_uq, mla_kv_norm, mla_w_ukv, w_br_nsa, w_br_fox, w_br_mla,
           w_out, ffn_norm, w_up, conv_w, conv_b, w_down, final_norm):
    batch, seq, d = x.shape
    stacked = dict(zip(_LAYER_PARAMS, (
        mix_norm, w_in, b_forget, cmp_pe_k, cmp_w1_k, cmp_w2_k, cmp_pe_v, cmp_w1_v, cmp_w2_v,
        mla_q_norm, mla_w_uq, mla_kv_norm, mla_w_ukv, w_br_nsa, w_br_fox, w_br_mla, w_out,
        ffn_norm, w_up, conv_w, conv_b, w_down)))
    depth = w_in.shape[0]
    tabs = _rope_tables(positions)
    h = x.reshape(batch * seq, d)
    for l in range(depth):
        p = {k: v[l] for k, v in stacked.items()}
        h = _mixer_layer(h, tabs, p, batch, seq)
        h = _ffn(h, p, final_norm, seq, final=(l == depth - 1))
    return h.reshape(batch, seq, d)
```

```python
import functools

import numpy as np
import jax
import jax.numpy as jnp
from jax import lax
from jax.experimental import pallas as pl
from jax.experimental.pallas import tpu as pltpu

F32 = jnp.float32
BF16 = jnp.bfloat16

HEAD_DIM = 64
ROPE_THETA = 500000.0
PARTIAL_ROT = HEAD_DIM // 4
NORM_EPS = 1e-6
NSA_HEADS = 8
NSA_GROUPS = 2
NSA_REP = NSA_HEADS // NSA_GROUPS
CMP_LEN = 32
CMP_STRIDE = 16
CMP_HID = 4 * HEAD_DIM
SLC_LEN = 64
SLC_TOPK = 16
WIN = 512
FOX_HEADS = 8
MLA_HEADS = 8
MLA_Q_LORA = 384
MLA_KV_LORA = 256
MLA_NOPE = 64
MLA_ROPE = 32
MLA_V = 64
D_FF = 2816
D_MODEL = 1024
CONV_W = 3

LANES = 128
LOG2E = 1.4426950408889634
NEG = -1e30
SEL_NEG = -1e9
VMEM_LIMIT = 52 * 1024 * 1024

TM = 256
TQ = 512
TK = 512
QC = 256
LOOP_TILES = 4
DECAY_TERMS = 3


def _cparams(n_axes):
    return pltpu.CompilerParams(dimension_semantics=("arbitrary",) * n_axes,
                                vmem_limit_bytes=VMEM_LIMIT)


def _const_spec(shape):
    nd = len(shape)
    return pl.BlockSpec(shape, lambda *_: (0,) * nd, pipeline_mode=pl.Buffered(1))


def _rms(x, g):
    return x * lax.rsqrt(jnp.mean(x * x, axis=-1, keepdims=True) + NORM_EPS) * g


def _dot(a, b):
    return jnp.dot(a, b, preferred_element_type=F32)


def _dot_nt(a, b):
    return lax.dot_general(a, b, (((1,), (1,)), ((), ())), preferred_element_type=F32)


def _rope_tables_kernel(pos_ref, f16_ref, g16_ref, f32_ref, g32_ref, c16_o, s16_o, c32_o, s32_o):
    p = pos_ref[...].astype(F32)
    a16 = p * f16_ref[...]
    c16_o[...] = jnp.cos(a16)
    s16_o[...] = jnp.sin(a16) * g16_ref[...]
    a32 = p * f32_ref[...]
    c32_o[...] = jnp.cos(a32)
    s32_o[...] = jnp.sin(a32) * g32_ref[...]


def _rope_tables(positions):
    T = positions.size
    half16, half32 = PARTIAL_ROT // 2, MLA_ROPE // 2
    inv16 = ROPE_THETA ** (-jnp.arange(half16, dtype=F32) / half16)
    inv32 = ROPE_THETA ** (-jnp.arange(half32, dtype=F32) / half32)
    lane = np.arange(LANES)
    d = lane % HEAD_DIM
    on16 = d < PARTIAL_ROT
    f16 = jnp.where(on16, inv16[d % half16], 0.0)[None, :]
    g16 = np.where(on16, np.where(d < half16, -1.0, 1.0), 0.0).astype(np.float32)[None, :]
    r = lane - MLA_NOPE
    on32 = (r >= 0) & (r < MLA_ROPE)
    f32 = jnp.where(on32, inv32[r % half32], 0.0)[None, :]
    g32 = np.where(on32, np.where(r < half32, -1.0, 1.0), 0.0).astype(np.float32)[None, :]
    pos = jnp.broadcast_to(positions.reshape(T, 1), (T, LANES))
    tt = 1024
    row = pl.BlockSpec((tt, LANES), lambda i: (i, 0))
    vec = _const_spec((1, LANES))
    out = jax.ShapeDtypeStruct((T, LANES), F32)
    return pl.pallas_call(
        _rope_tables_kernel, grid=(T // tt,),
        in_specs=[row, vec, vec, vec, vec], out_specs=[row] * 4, out_shape=[out] * 4,
        compiler_params=_cparams(1), name="rope_tables",
    )(pos, f16, jnp.asarray(g16), f32, jnp.asarray(g32))


def _rot16(x, cos, sin):
    lane = lax.broadcasted_iota(jnp.int32, x.shape, 1)
    first = (lane % HEAD_DIM) < (PARTIAL_ROT // 2)
    sw = jnp.where(first, pltpu.roll(x, LANES - PARTIAL_ROT // 2, 1), pltpu.roll(x, PARTIAL_ROT // 2, 1))
    return x * cos + sw * sin


def _rot32(x, cos, sin):
    lane = lax.broadcasted_iota(jnp.int32, x.shape, 1)
    first = lane < (MLA_NOPE + MLA_ROPE // 2)
    sw = jnp.where(first, pltpu.roll(x, LANES - MLA_ROPE // 2, 1), pltpu.roll(x, MLA_ROPE // 2, 1))
    return x * cos + sw * sin


_FF_LANE = 3 * NSA_HEADS
_IN_COLS = {}
for _name, _n in (("q", NSA_HEADS * HEAD_DIM), ("kvc", 4 * HEAD_DIM), ("ksl", LANES), ("kwin", LANES),
                  ("misc", LANES), ("fq", FOX_HEADS * HEAD_DIM), ("fk", FOX_HEADS * HEAD_DIM),
                  ("lat", MLA_Q_LORA + MLA_KV_LORA), ("mg", 3 * D_MODEL)):
    _IN_COLS[_name] = (sum(n for _, n in _IN_COLS.values()), _n)
_IN_ROWS = {"vsl": (0, LANES), "vwin": (LANES, LANES), "fv": (2 * LANES, FOX_HEADS * HEAD_DIM)}


def _inproj_kernel(x_ref, g_ref, c16_ref, s16_ref, c32_ref, s32_ref,
                   w_ref, wt_ref, bf_ref, tri_ref, place_ref,
                   qnrm_ref, wuq_ref, kvnrm_ref, wuk_ref, wuv_ref,
                   qn_o, qr_o, kvc_o, ksl_o, vsl_o, kwin_o, vwin_o, gate_o,
                   fq_o, fk_o, fv_o, mq_o, mk_o, mv_o, mg_o, fcarry_ref, *, seq):
    tm = x_ref.shape[0]
    seq_start = (pl.program_id(0) * tm) % seq == 0
    hb = _rms(x_ref[...], g_ref[...]).astype(BF16)
    c16, s16 = c16_ref[...], s16_ref[...]
    c32, s32 = c32_ref[...], s32_ref[...]
    lane = lax.broadcasted_iota(jnp.int32, (tm, LANES), 1)
    low = lane < HEAD_DIM

    def proj(name):
        a, n = _IN_COLS[name]
        return _dot(hb, w_ref[:, a:a + n])

    def proj_t(name):
        a, n = _IN_ROWS[name]
        return _dot_nt(wt_ref[a:a + n, :], hb)

    def slabs(z):
        return [z[:, k * LANES:(k + 1) * LANES] for k in range(z.shape[1] // LANES)]

    def split(z, fill=0.0):
        return jnp.where(low, z, fill), jnp.where(low, pltpu.roll(z, HEAD_DIM, 1), fill)

    def store_heads(o_ref, pair_slabs, fill=0.0, extra=None):
        for k, z in enumerate(pair_slabs):
            for e, s in enumerate(split(z, fill)):
                h = 2 * k + e
                if extra is not None:
                    s = s + extra[:, h * LANES:(h + 1) * LANES]
                o_ref[:, h * LANES:(h + 1) * LANES] = s.astype(o_ref.dtype)

    zq = slabs(proj("q"))
    store_heads(qn_o, zq)
    store_heads(qr_o, [_rot16(z, c16, s16) for z in zq])

    for k, z in enumerate(slabs(proj("kvc"))):
        for e, s in enumerate(split(z)):
            kvc_o[2 * k + e] = s[:, :HEAD_DIM]

    tok = (pl.program_id(0) * tm) % seq + lax.broadcasted_iota(jnp.int32, (tm, LANES), 0)
    onehot = jnp.where(lane - HEAD_DIM == tok // SLC_LEN, 1.0, 0.0)
    store_heads(ksl_o, [_rot16(proj("ksl"), c16, s16)], fill=onehot)
    vsl_o[0] = proj_t("vsl").astype(BF16)

    store_heads(kwin_o, [_rot16(proj("kwin"), c16, s16)])
    vwin_o[0] = proj_t("vwin").astype(BF16)

    misc = proj("misc")
    gate_o[...] = misc

    @pl.when(seq_start)
    def _():
        fcarry_ref[...] = jnp.zeros(fcarry_ref.shape, F32)

    zf = misc + bf_ref[...]
    is_f = (lane >= _FF_LANE) & (lane < _FF_LANE + FOX_HEADS)
    logf = jnp.where(is_f, jnp.minimum(zf, 0.0) - jnp.log1p(jnp.exp(-jnp.abs(zf))), 0.0)
    fsum = jnp.dot(tri_ref[...], logf, preferred_element_type=F32,
                   precision=lax.Precision.HIGHEST) + fcarry_ref[0:1]
    fcarry_ref[...] = jnp.broadcast_to(fsum[tm - 1:tm], fcarry_ref.shape)
    rest = -LOG2E * fsum
    decay = None
    for t in range(DECAY_TERMS):
        part = rest.astype(BF16)
        rest = rest - part.astype(F32)
        term = _dot(part, place_ref[t])
        decay = term if decay is None else decay + term
    store_heads(fk_o, slabs(proj("fk")), extra=decay)
    ones = jnp.where((lane >= HEAD_DIM) & (lane < HEAD_DIM + DECAY_TERMS), 1.0, 0.0)
    store_heads(fq_o, slabs(proj("fq")), fill=ones)
    fv_o[0] = proj_t("fv").astype(BF16)

    lat = proj("lat")
    cq = _rms(lat[:, :MLA_Q_LORA], qnrm_ref[...]).astype(BF16)
    ckv = _rms(lat[:, MLA_Q_LORA:], kvnrm_ref[...]).astype(BF16)
    kpe = jnp.where(low, 0.0, _rot32(misc, c32, s32))
    zuq = _dot(cq, wuq_ref[...])
    for k, z in enumerate(slabs(zuq)):
        mq_o[:, k * LANES:(k + 1) * LANES] = _rot32(z, c32, s32).astype(BF16)
    zuk = _dot(ckv, wuk_ref[...])
    for k, z in enumerate(slabs(zuk)):
        mk_o[:, k * LANES:(k + 1) * LANES] = (z + kpe).astype(BF16)
    mv_o[0] = _dot_nt(wuv_ref[...], ckv).astype(BF16)

    mg_o[...] = jax.nn.sigmoid(proj("mg"))


def _pad_halves(w, first_only=True):
    K, n = w.shape[0], w.shape[1] // HEAD_DIM
    w = w.reshape(K, n, HEAD_DIM)
    other = jnp.zeros_like(w) if first_only else w
    return jnp.concatenate([w, other], axis=-1).reshape(K, n * LANES)


def _inproj(x2, tabs, p, seq):
    T, D = x2.shape
    w_in = p["w_in"]
    scale = HEAD_DIM ** -0.5 * LOG2E
    o = 0

    def take(n):
        nonlocal o
        w = w_in[:, o:o + n]
        o += n
        return w

    nsa_w, kvw = NSA_HEADS * HEAD_DIM, NSA_GROUPS * HEAD_DIM
    w_q = take(nsa_w)
    w_kc, w_vc, w_ks, w_vs, w_kw, w_vw = [take(kvw) for _ in range(6)]
    w_g = take(3 * NSA_HEADS)
    w_fq, w_fk, w_fv = [take(FOX_HEADS * HEAD_DIM) for _ in range(3)]
    w_ff = take(FOX_HEADS)
    w_cq, w_ckv, w_kr = take(MLA_Q_LORA), take(MLA_KV_LORA), take(MLA_ROPE)
    w_mg = take(3 * D)

    bf = lambda w: w.astype(BF16)
    row = lambda a: a.reshape(1, -1).astype(F32)
    w_misc = jnp.concatenate([w_g, w_ff, jnp.zeros((D, MLA_NOPE - _FF_LANE - FOX_HEADS), F32), w_kr,
                              jnp.zeros((D, LANES - MLA_NOPE - MLA_ROPE), F32)], axis=1)
    w_tok = bf(jnp.concatenate([w_q * scale, w_kc, w_vc, w_ks, w_kw, w_misc, w_fq * scale, w_fk,
                                w_cq, w_ckv, w_mg], axis=1))
    w_tr = bf(jnp.concatenate([w_vs, w_vw, w_fv], axis=1).T)
    b_f = jnp.pad(row(p["b_forget"]), ((0, 0), (_FF_LANE, LANES - _FF_LANE - FOX_HEADS)))
    tri = jnp.asarray(np.tril(np.ones((TM, TM), np.float32)))
    place = np.zeros((DECAY_TERMS, LANES, FOX_HEADS * LANES), np.float32)
    for t in range(DECAY_TERMS):
        for h in range(FOX_HEADS):
            place[t, _FF_LANE + h, h * LANES + HEAD_DIM + t] = 1.0
    place = jnp.asarray(place, BF16)
    dq = MLA_NOPE + MLA_ROPE
    wuq = (p["mla_w_uq"] * (dq ** -0.5 * LOG2E)).reshape(MLA_Q_LORA, MLA_HEADS, dq)
    wuq = bf(jnp.pad(wuq, ((0, 0), (0, 0), (0, LANES - dq))).reshape(MLA_Q_LORA, MLA_HEADS * LANES))
    wukv = p["mla_w_ukv"].reshape(MLA_KV_LORA, MLA_HEADS, MLA_NOPE + MLA_V)
    wuk = bf(_pad_halves(wukv[:, :, :MLA_NOPE].reshape(MLA_KV_LORA, MLA_HEADS * MLA_NOPE)))
    wuv = bf(wukv[:, :, MLA_NOPE:].reshape(MLA_KV_LORA, MLA_HEADS * MLA_V).T)
    assert w_tok.shape[1] == sum(n for _, n in _IN_COLS.values())

    weights = [w_tok, w_tr, b_f, tri, place,
               row(p["mla_q_norm"]), wuq, row(p["mla_kv_norm"]), wuk, wuv]
    n = T // TM
    tok = lambda c: pl.BlockSpec((TM, c), lambda i: (i, 0))
    in_specs = ([tok(D), _const_spec((1, D))] + [tok(LANES)] * 4
                + [_const_spec(w.shape) for w in weights])
    tokens, transposed = "tokens", "transposed"
    outs = [
        (tokens, NSA_HEADS * LANES, BF16), (tokens, NSA_HEADS * LANES, BF16), None,
        (tokens, NSA_GROUPS * LANES, BF16), (transposed, NSA_GROUPS * HEAD_DIM, BF16),
        (tokens, NSA_GROUPS * LANES, BF16), (transposed, NSA_GROUPS * HEAD_DIM, BF16), (tokens, LANES, F32),
        (tokens, FOX_HEADS * LANES, BF16), (tokens, FOX_HEADS * LANES, BF16),
        (transposed, FOX_HEADS * HEAD_DIM, BF16),
        (tokens, MLA_HEADS * LANES, BF16), (tokens, MLA_HEADS * LANES, BF16),
        (transposed, MLA_HEADS * MLA_V, BF16), (tokens, 3 * D, F32)]
    out_shape, out_specs = [], []
    for spec in outs:
        if spec is None:
            out_shape.append(jax.ShapeDtypeStruct((4, T, HEAD_DIM), F32))
            out_specs.append(pl.BlockSpec((4, TM, HEAD_DIM), lambda i: (0, i, 0)))
        elif spec[0] == transposed:
            out_shape.append(jax.ShapeDtypeStruct((n, spec[1], TM), spec[2]))
            out_specs.append(pl.BlockSpec((1, spec[1], TM), lambda i: (i, 0, 0)))
        else:
            out_shape.append(jax.ShapeDtypeStruct((T, spec[1]), spec[2]))
            out_specs.append(tok(spec[1]))
    return pl.pallas_call(
        functools.partial(_inproj_kernel, seq=seq), grid=(n,),
        in_specs=in_specs, out_specs=out_specs, out_shape=out_shape,
        scratch_shapes=[pltpu.VMEM((8, LANES), F32)],
        compiler_params=_cparams(1), name="inproj",
    )(x2, row(p["mix_norm"]), *tabs, *weights)


def _attn_kernel(*refs, tq, tk, q_of_stream, window_of_stream, k_of_head, values_by_unit):
    n_streams, n_q = len(q_of_stream), max(q_of_stream) + 1
    q_refs, kv_refs = refs[:n_q], refs[n_q:n_q + 2 * n_streams]
    outs = refs[n_q + 2 * n_streams:n_q + 3 * n_streams]
    qt_ref, m_ref, l_ref, acc_ref = refs[n_q + 3 * n_streams:]
    i = pl.program_id(2)
    hs = len(k_of_head)
    nh = n_streams * hs
    k_of = lambda h: kv_refs[2 * (h // hs)]
    vt_of = lambda h: kv_refs[2 * (h // hs) + 1]
    qt_of = lambda h: q_of_stream[h // hs] * hs + h % hs
    per_q = tq // tk
    tv = kv_refs[1].shape[2]
    v_sub = tk // tv
    for qi in range(n_q):
        for r in range(hs):
            qt_ref[qi * hs + r] = q_refs[qi][:, r * LANES:(r + 1) * LANES].astype(F32).T.astype(BF16)
    dist0 = (i * tq + lax.broadcasted_iota(jnp.int32, (tk, tq), 1)
             - lax.broadcasted_iota(jnp.int32, (tk, tq), 0))

    m_ref[...] = jnp.full(m_ref.shape, NEG, F32)
    l_ref[...] = jnp.zeros(l_ref.shape, F32)
    acc_ref[...] = jnp.zeros(acc_ref.shape, F32)

    def logits(j, lo, n, h, cols):
        r0 = pl.multiple_of(j * tk + lo, tv)
        kk = k_of_head[h % hs]
        return _dot(k_of(h)[pl.ds(r0, n), kk * LANES:(kk + 1) * LANES], qt_ref[qt_of(h), :, cols])

    def softmax(s, h, cols, mask):
        if mask is not None:
            s = jnp.where(mask, s, NEG)
        m_old = m_ref[h, :, cols]
        m_new = jnp.maximum(m_old, jnp.max(s, axis=0, keepdims=True))
        alpha = jnp.exp2(m_old - m_new)
        p = jnp.exp2(s - m_new)
        l_ref[h, :, cols] = alpha * l_ref[h, :, cols] + jnp.sum(p, axis=0, keepdims=True)
        m_ref[h, :, cols] = m_new
        return p.astype(BF16), alpha

    def accumulate(j, lo, n, h, cols, p, alpha):
        if values_by_unit:
            half = pl.ds(pl.multiple_of(pl.program_id(1) * HEAD_DIM, HEAD_DIM), HEAD_DIM)
        else:
            half = slice((h % 2) * HEAD_DIM, (h % 2 + 1) * HEAD_DIM)
        pv = sum(_dot(vt_of(h)[j * v_sub + lo // tv + c, half, :], p[c * tv:(c + 1) * tv])
                 for c in range(n // tv))
        acc_ref[h, :, cols] = alpha * acc_ref[h, :, cols] + pv

    def sweep(tiles):
        work = []
        for heads, j, n_tiles, mask, span in tiles:
            for h in heads:
                for c in range(tq // QC):
                    lo, n = (0, n_tiles * tk) if span is None else span(c)
                    work.append((j, lo, n, mask, h, slice(c * QC, (c + 1) * QC)))
        s_of, p_of = {}, {}
        for t in range(len(work) + 2):
            if t < len(work):
                j, lo, n, _, h, cols = work[t]
                s_of[t] = logits(j, lo, n, h, cols)
            if 0 <= t - 1 < len(work):
                _, lo, n, mask, h, cols = work[t - 1]
                p_of[t - 1] = softmax(s_of.pop(t - 1), h, cols,
                                      None if mask is None else mask(slice(lo, lo + n), cols))
            if 0 <= t - 2 < len(work):
                j, lo, n, _, h, cols = work[t - 2]
                accumulate(j, lo, n, h, cols, *p_of.pop(t - 2))

    def visible(j, window, valid=None):
        def mask(keys, cols):
            dist = dist0[keys, cols] - j * tk
            if not window:
                return dist >= 0
            if valid is not None:
                dist = dist + jnp.where(valid, 0, WIN)
            return lax.bitcast_convert_type(dist, jnp.uint32) < WIN
        return mask

    assert per_q == 1 and tk == 2 * QC and WIN == tk, "sweeps assume tq == tk == WIN == two query chunks"
    causal = [h for h in range(nh) if not window_of_stream[h // hs]]
    windowed = [h for h in range(nh) if window_of_stream[h // hs]]
    prev = jnp.maximum(i - 1, 0)

    one_step = len(causal) * (tq // QC) >= 8

    def unmasked(j0, n):
        tiles = []
        while n > 0:
            k = 2 if (one_step and n >= 2) else 1
            tiles.append((causal, j0, k, None, None))
            j0, n = j0 + k, n - k
        return tiles

    def last_tiles(n_left):
        own_span = lambda c: (0, (c + 1) * QC)
        tiles = []
        if causal:
            tiles += unmasked(i - n_left, n_left) + [(causal, i, 1, visible(i, False), own_span)]
        if windowed:
            tiles += [(windowed, i, 1, visible(i, True), own_span),
                      (windowed, prev, 1, visible(i - 1, True, valid=i >= 1),
                       lambda c: (c * QC, tk - c * QC))]
        return tiles

    if causal:
        def body(jj, c):
            sweep(unmasked(LOOP_TILES * jj, LOOP_TILES))
            return c
        lax.fori_loop(0, i // LOOP_TILES, body, 0)
        for n_left in range(LOOP_TILES):
            @pl.when(i % LOOP_TILES == n_left)
            def _():
                sweep(last_tiles(n_left))
    else:
        sweep(last_tiles(0))

    for h in range(0, nh, 2):
        a = acc_ref[h] * (1.0 / l_ref[h])
        b = acc_ref[h + 1] * (1.0 / l_ref[h + 1])
        o_ref, k = outs[h // hs], (h % hs) // 2
        o_ref[:, k * LANES:(k + 1) * LANES] = jnp.concatenate([a, b], axis=0).T.astype(o_ref.dtype)


def _attention(streams, batch, seq, k_of_head, out_dtype, name):
    qs = []
    for q, _, _, _ in streams:
        if not any(q is x for x in qs):
            qs.append(q)
    q_of_stream = tuple(next(n for n, x in enumerate(qs) if x is q) for q, _, _, _ in streams)
    T = qs[0].shape[0]
    nq = seq // TQ
    hs, nk = len(k_of_head), max(k_of_head) + 1
    nh = hs * len(streams)
    units = qs[0].shape[1] // (hs * LANES)
    by_unit = nk == 1
    assert not by_unit or units * HEAD_DIM == LANES, "GQA value blocks must fill one slab"
    in_specs = [pl.BlockSpec((TQ, hs * LANES), lambda b, u, i: (b * nq + i, u)) for _ in qs]
    args = list(qs)
    for q, k, vt, _ in streams:
        assert q.shape[1] == units * hs * LANES and k.shape[1] == units * nk * LANES
        assert vt.shape == (T // TM, units * (HEAD_DIM if by_unit else LANES), TM)
        in_specs += [pl.BlockSpec((seq, nk * LANES), lambda b, u, i: (b, u)),
                     pl.BlockSpec((seq // TM, LANES, TM),
                                  (lambda b, u, i: (b, 0, 0)) if by_unit else (lambda b, u, i: (b, u, 0)))]
        args += [k, vt]
    return pl.pallas_call(
        functools.partial(_attn_kernel, tq=TQ, tk=TK, q_of_stream=q_of_stream,
                          window_of_stream=tuple(w for _, _, _, w in streams),
                          k_of_head=k_of_head, values_by_unit=by_unit),
        grid=(batch, units, nq), in_specs=in_specs,
        out_specs=[pl.BlockSpec((TQ, hs // 2 * LANES), lambda b, u, i: (b * nq + i, u))] * len(streams),
        out_shape=[jax.ShapeDtypeStruct((T, units * hs // 2 * LANES), out_dtype)] * len(streams),
        scratch_shapes=[pltpu.VMEM((len(qs) * hs, LANES, TQ), BF16), pltpu.VMEM((nh, 1, TQ), F32),
                        pltpu.VMEM((nh, 1, TQ), F32), pltpu.VMEM((nh, HEAD_DIM, TQ), F32)],
        compiler_params=_cparams(3), name=name,
    )(*args)


def _gelu_tanh(x):
    return 0.5 * x * (1.0 + jnp.tanh(np.float32(np.sqrt(2.0 / np.pi)) * (x + 0.044715 * (x * x * x))))


def _compress_kernel(x_ref, pe_ref, w1_ref, w2_ref, o_ref):
    x = x_ref[0]
    half = x.shape[1]
    top = _dot((x + pe_ref[0, :, :half]).astype(BF16), w1_ref[0, :half, :])
    bot = _dot((x + pe_ref[0, :, half:]).astype(BF16), w1_ref[0, half:, :])
    hid = top + pltpu.roll(bot, bot.shape[0] - 1, 0)
    o_ref[0, 0] = _dot(_gelu_tanh(hid).astype(BF16), w2_ref[0]).astype(o_ref.dtype)


def _compress(kvc, p, batch, seq):
    T = kvc.shape[1]
    rows = seq // CMP_STRIDE
    x = kvc.reshape(4, T // CMP_STRIDE, CMP_STRIDE * HEAD_DIM)
    pe = jnp.stack([p["cmp_pe_k"].reshape(1, -1), p["cmp_pe_v"].reshape(1, -1)]).astype(F32)
    w1 = jnp.stack([p["cmp_w1_k"], p["cmp_w1_v"]]).astype(BF16)
    w2k = jnp.pad(p["cmp_w2_k"], ((0, 0), (0, LANES - HEAD_DIM)))
    w2v = jnp.concatenate([p["cmp_w2_v"], p["cmp_w2_v"]], axis=1)
    w2 = jnp.stack([w2k, w2v]).astype(BF16)
    return pl.pallas_call(
        _compress_kernel, grid=(4, batch),
        in_specs=[pl.BlockSpec((1, rows, x.shape[2]), lambda j, b: (j, b, 0)),
                  pl.BlockSpec((1,) + pe.shape[1:], lambda j, b: (j // 2, 0, 0)),
                  pl.BlockSpec((1,) + w1.shape[1:], lambda j, b: (j // 2, 0, 0)),
                  pl.BlockSpec((1,) + w2.shape[1:], lambda j, b: (j // 2, 0, 0))],
        out_specs=pl.BlockSpec((1, 1, rows, LANES), lambda j, b: (b, j, 0, 0)),
        out_shape=jax.ShapeDtypeStruct((batch, 4, rows, LANES), BF16),
        compiler_params=_cparams(2), name="nsa_compress",
    )(x, pe, w1, w2)


def _cmp_select_kernel(qn_ref, qr_ref, kc_ref, vc_ref, ovt_ref, o_ref, qa_ref, *, tq):
    i = pl.program_id(2)
    n_cmp = kc_ref.shape[2]
    kc, vc = kc_ref[0, 0], vc_ref[0, 0]
    t_row = i * tq + lax.broadcasted_iota(jnp.int32, (tq, n_cmp), 0)
    blk_end = CMP_STRIDE * lax.broadcasted_iota(jnp.int32, (tq, n_cmp), 1) + (CMP_LEN - 1)
    vis = blk_end <= t_row
    lane = lax.broadcasted_iota(jnp.int32, (tq, LANES), 1)
    psum = jnp.zeros((tq, n_cmp), F32)
    outs = []
    for r in range(NSA_REP):
        s = jnp.where(vis, _dot_nt(qn_ref[:, r * LANES:(r + 1) * LANES], kc), NEG)
        m = jnp.max(s, axis=-1, keepdims=True)
        e = jnp.where(vis, jnp.exp2(s - m), 0.0)
        pn = e / jnp.maximum(jnp.sum(e, axis=-1, keepdims=True), 1e-30)
        psum = psum + pn
        outs.append(_dot(pn.astype(BF16), vc))
    for k in range(NSA_REP // 2):
        o_ref[:, k * LANES:(k + 1) * LANES] = jnp.where(lane < HEAD_DIM, outs[2 * k], outs[2 * k + 1])

    p_hi = psum.astype(BF16)
    p_lo = (psum - p_hi.astype(F32)).astype(BF16)
    ovt = ovt_ref[...]
    imp = _dot_nt(ovt, p_hi) + _dot_nt(ovt, p_lo)
    n_slc = LANES - HEAD_DIM
    jb = lax.broadcasted_iota(jnp.int32, (LANES, tq), 0) - HEAD_DIM
    cur = (i * tq + lax.broadcasted_iota(jnp.int32, (LANES, tq), 1)) // SLC_LEN
    forced = (jb == 0) | (jb == cur) | (jb == cur - 1)
    imp = jnp.where(forced, 1e9, jnp.where(jb > cur, -1e9, imp))
    groups = [imp[HEAD_DIM + 8 * v:HEAD_DIM + 8 * (v + 1)] for v in range(n_slc // 8)]
    sub = lax.broadcasted_iota(jnp.int32, (8, tq), 0)
    ranks = [jnp.zeros((8, tq), F32) for _ in groups]
    for a in range(n_slc):
        row = jnp.broadcast_to(groups[a // 8][a % 8:a % 8 + 1], (8, tq))
        for v in range(len(groups)):
            if v > a // 8:
                inc = jnp.where(row >= groups[v], 1.0, 0.0)
            elif v < a // 8:
                inc = jnp.where(row > groups[v], 1.0, 0.0)
            else:
                tie = jnp.where(sub > a % 8, 1.0, 0.0)
                inc = jnp.where(row > groups[v], 1.0, jnp.where(row == groups[v], tie, 0.0))
            ranks[v] = ranks[v] + inc
    bias_t = jnp.concatenate(
        [jnp.zeros((HEAD_DIM, tq), F32)]
        + [jnp.where(rk < SLC_TOPK, 0.0, SEL_NEG) for rk in ranks], axis=0)
    bias = bias_t.T
    for r in range(NSA_REP):
        qa_ref[:, r * LANES:(r + 1) * LANES] = (
            qr_ref[:, r * LANES:(r + 1) * LANES].astype(F32) + bias).astype(BF16)


def _overlap_t(seq):
    n_cmp = (seq - CMP_LEN) // CMP_STRIDE + 1
    n_slc = seq // SLC_LEN
    assert n_slc == LANES - HEAD_DIM, "selection blocks must fill the bias half of a slab"
    c0 = CMP_STRIDE * np.arange(n_cmp)[:, None]
    s0 = SLC_LEN * np.arange(n_slc)[None, :]
    ov = np.clip(np.minimum(c0 + CMP_LEN, s0 + SLC_LEN) - np.maximum(c0, s0), 0, None) / CMP_LEN
    out = np.zeros((LANES, seq // CMP_STRIDE), np.float32)
    out[HEAD_DIM:, :n_cmp] = ov.T
    return jnp.asarray(out, BF16)


def _cmp_select(qn, qr, cmp_kv, batch, seq):
    T = qn.shape[0]
    nq = seq // TQ
    rows = seq // CMP_STRIDE
    gw = NSA_REP * LANES
    qspec = pl.BlockSpec((TQ, gw), lambda b, g, i: (b * nq + i, g))
    return pl.pallas_call(
        functools.partial(_cmp_select_kernel, tq=TQ), grid=(batch, NSA_GROUPS, nq),
        in_specs=[qspec, qspec,
                  pl.BlockSpec((1, 1, rows, LANES), lambda b, g, i: (b, g, 0, 0)),
                  pl.BlockSpec((1, 1, rows, LANES), lambda b, g, i: (b, NSA_GROUPS + g, 0, 0)),
                  _const_spec((LANES, rows))],
        out_specs=[pl.BlockSpec((TQ, NSA_REP * HEAD_DIM), lambda b, g, i: (b * nq + i, g)), qspec],
        out_shape=[jax.ShapeDtypeStruct((T, NSA_HEADS * HEAD_DIM), F32),
                   jax.ShapeDtypeStruct((T, NSA_HEADS * LANES), BF16)],
        compiler_params=_cparams(3), name="nsa_cmp_select",
    )(qn, qr, cmp_kv, cmp_kv, _overlap_t(seq))


def _merge_kernel(x_ref, oc_ref, os_ref, ow_ref, gl_ref, of_ref, om_ref, mg_ref,
                  e_ref, wn_ref, wf_ref, wm_ref, wo_ref, o_ref):
    d = x_ref.shape[1]
    nw = oc_ref.shape[1]
    sg = jax.nn.sigmoid(gl_ref[...])
    sg_hi = sg.astype(BF16)
    sg_lo = (sg - sg_hi.astype(F32)).astype(BF16)
    e = e_ref[...]
    g = _dot(sg_hi, e) + _dot(sg_lo, e)
    o_nsa = g[:, :nw] * oc_ref[...] + g[:, nw:2 * nw] * os_ref[...] + g[:, 2 * nw:] * ow_ref[...]
    mg = mg_ref[...]
    merged = (mg[:, :d] * _dot(o_nsa.astype(BF16), wn_ref[...])
              + mg[:, d:2 * d] * _dot(of_ref[...], wf_ref[...])
              + mg[:, 2 * d:] * _dot(om_ref[...], wm_ref[...]))
    o_ref[...] = x_ref[...] + _dot(merged.astype(BF16), wo_ref[...])


def _gate_expand():
    e = np.zeros((LANES, 3 * NSA_HEADS * HEAD_DIM), np.float32)
    for h in range(NSA_HEADS):
        for c in range(3):
            e[h * 3 + c, c * NSA_HEADS * HEAD_DIM + h * HEAD_DIM:
              c * NSA_HEADS * HEAD_DIM + (h + 1) * HEAD_DIM] = 1.0
    return jnp.asarray(e, BF16)


def _merge(x2, o_cmp, o_slc, o_win, gate_l, o_fox, o_mla, mg, p):
    T, D = x2.shape
    tok = lambda a: pl.BlockSpec((TM, a.shape[1]), lambda i: (i, 0))
    acts = [x2, o_cmp, o_slc, o_win, gate_l, o_fox, o_mla, mg]
    weights = [_gate_expand(), p["w_br_nsa"].astype(BF16), p["w_br_fox"].astype(BF16),
               p["w_br_mla"].astype(BF16), p["w_out"].astype(BF16)]
    return pl.pallas_call(
        _merge_kernel, grid=(T // TM,),
        in_specs=[tok(a) for a in acts] + [_const_spec(w.shape) for w in weights],
        out_specs=pl.BlockSpec((TM, D), lambda i: (i, 0)),
        out_shape=jax.ShapeDtypeStruct((T, D), F32),
        compiler_params=_cparams(1), name="merge_out",
    )(*acts, *weights)


def _ffn_kernel(x_ref, g_ref, wu_ref, wv_ref, cw_ref, cb_ref, wd_ref, fin_ref, o_ref, tail_ref,
                *, seq, n_chunks, final):
    tm = x_ref.shape[0]
    x = x_ref[...]
    hb = _rms(x, g_ref[...]).astype(BF16)

    @pl.when((pl.program_id(0) * tm) % seq == 0)
    def _():
        tail_ref[...] = jnp.zeros(tail_ref.shape, F32)

    cw = wu_ref.shape[1] // n_chunks
    row8 = lax.broadcasted_iota(jnp.int32, (8, cw), 0)
    acc = jnp.zeros(x.shape, F32)
    for c in range(n_chunks):
        sl = slice(c * cw, (c + 1) * cw)
        u = _dot(hb, wu_ref[:, sl])
        v = _dot(hb, wv_ref[:, sl])
        prev = tail_ref[:, sl]
        tail_ref[:, sl] = u[tm - 8:]
        u1 = pltpu.roll(u, 1, 0)
        u2 = pltpu.roll(u, 2, 0)
        h1 = jnp.where(row8 < 1, pltpu.roll(prev, 1, 0), u1[:8])
        h2 = jnp.where(row8 < 2, pltpu.roll(prev, 2, 0), u2[:8])
        u1 = jnp.concatenate([h1, u1[8:]], axis=0)
        u2 = jnp.concatenate([h2, u2[8:]], axis=0)
        uc = cw_ref[0:1, sl] * u2 + cw_ref[1:2, sl] * u1 + cw_ref[2:3, sl] * u + cb_ref[:, sl]
        act = (uc * jax.nn.sigmoid(uc) * v).astype(BF16)
        acc = acc + _dot(act, wd_ref[sl, :])
    y = x + acc
    if final:
        y = _rms(y, fin_ref[...])
    o_ref[...] = y


def _ffn(x2, p, final_norm, seq, final):
    T, D = x2.shape
    w_up = p["w_up"]
    wu, wv = w_up[:, :D_FF].astype(BF16), w_up[:, D_FF:].astype(BF16)
    row = lambda a: a.reshape(1, -1).astype(F32)
    cwp = jnp.pad(p["conv_w"].astype(F32), ((0, 8 - CONV_W), (0, 0)))
    weights = [row(p["ffn_norm"]), wu, wv, cwp, row(p["conv_b"]), p["w_down"].astype(BF16),
               row(final_norm)]
    return pl.pallas_call(
        functools.partial(_ffn_kernel, seq=seq, n_chunks=2, final=final), grid=(T // TM,),
        in_specs=[pl.BlockSpec((TM, D), lambda i: (i, 0))] + [_const_spec(w.shape) for w in weights],
        out_specs=pl.BlockSpec((TM, D), lambda i: (i, 0)),
        out_shape=jax.ShapeDtypeStruct((T, D), F32),
        scratch_shapes=[pltpu.VMEM((8, D_FF), F32)],
        compiler_params=_cparams(1), name="conv_ffn",
    )(x2, *weights)


_LAYER_PARAMS = ("mix_norm", "w_in", "b_forget", "cmp_pe_k", "cmp_w1_k", "cmp_w2_k", "cmp_pe_v",
                 "cmp_w1_v", "cmp_w2_v", "mla_q_norm", "mla_w_uq", "mla_kv_norm", "mla_w_ukv",
                 "w_br_nsa", "w_br_fox", "w_br_mla", "w_out", "ffn_norm", "w_up", "conv_w",
                 "conv_b", "w_down")


def _mixer_layer(h, tabs, p, batch, seq):
    (qn, qr, kvc, ksl, vsl_t, kwin, vwin_t, gate_l,
     fq, fk, fv_t, mq, mk, mv_t, mg) = _inproj(h, tabs, p, seq)
    cmp_kv = _compress(kvc, p, batch, seq)
    o_cmp, qa = _cmp_select(qn, qr, cmp_kv, batch, seq)
    group = (0,) * NSA_REP
    pair = (0, 1)
    o_slc, o_win = _attention([(qa, ksl, vsl_t, False), (qa, kwin, vwin_t, True)], batch, seq, group,
                              F32, "nsa_slc_win_attn")
    o_fox, o_mla = _attention([(fq, fk, fv_t, False), (mq, mk, mv_t, False)], batch, seq, pair, BF16,
                              "fox_mla_attn")
    return _merge(h, o_cmp, o_slc, o_win, gate_l, o_fox, o_mla, mg, p)


def kernel(x, positions, mix_norm, w_in, b_forget, cmp_pe_k, cmp_w1_k, cmp_w2_k, cmp_pe_v, cmp_w1_v,
           cmp_w2_v, mla_q_norm, mla_w_uq, mla_kv_norm, mla_w_ukv, w_br_nsa, w_br_fox, w_br_mla,
           w_out, ffn_norm, w_up, conv_w, conv_b, w_down, final_norm):
    batch, seq, d = x.shape
    stacked = dict(zip(_LAYER_PARAMS, (
        mix_norm, w_in, b_forget, cmp_pe_k, cmp_w1_k, cmp_w2_k, cmp_pe_v, cmp_w1_v, cmp_w2_v,
        mla_q_norm, mla_w_uq, mla_kv_norm, mla_w_ukv, w_br_nsa, w_br_fox, w_br_mla, w_out,
        ffn_norm, w_up, conv_w, conv_b, w_down)))
    depth = w_in.shape[0]
    tabs = _rope_tables(positions)
    h = x.reshape(batch * seq, d)
    for l in range(depth):
        p = {k: v[l] for k, v in stacked.items()}
        h = _mixer_layer(h, tabs, p, batch, seq)
        h = _ffn(h, p, final_norm, seq, final=(l == depth - 1))
    return h.reshape(batch, seq, d)
```

```python
import functools

import numpy as np
import jax
import jax.numpy as jnp
from jax import lax
from jax.experimental import pallas as pl
from jax.experimental.pallas import tpu as pltpu

F32 = jnp.float32
BF16 = jnp.bfloat16

HEAD_DIM = 64
ROPE_THETA = 500000.0
PARTIAL_ROT = HEAD_DIM // 4
NORM_EPS = 1e-6
NSA_HEADS = 8
NSA_GROUPS = 2
NSA_REP = NSA_HEADS // NSA_GROUPS
CMP_LEN = 32
CMP_STRIDE = 16
CMP_HID = 4 * HEAD_DIM
SLC_LEN = 64
SLC_TOPK = 16
WIN = 512
FOX_HEADS = 8
MLA_HEADS = 8
MLA_Q_LORA = 384
MLA_KV_LORA = 256
MLA_NOPE = 64
MLA_ROPE = 32
MLA_V = 64
D_FF = 2816
D_MODEL = 1024
CONV_W = 3

LANES = 128
LOG2E = 1.4426950408889634
NEG = -1e30
SEL_NEG = -1e9
VMEM_LIMIT = 52 * 1024 * 1024

TM = 256
TQ = 512
TK = 512
QC = 256
LOOP_TILES = 4
DECAY_TERMS = 3


def _cparams(n_axes):
    return pltpu.CompilerParams(dimension_semantics=("arbitrary",) * n_axes,
                                vmem_limit_bytes=VMEM_LIMIT)


def _const_spec(shape):
    nd = len(shape)
    return pl.BlockSpec(shape, lambda *_: (0,) * nd, pipeline_mode=pl.Buffered(1))


def _rms(x, g):
    return x * lax.rsqrt(jnp.mean(x * x, axis=-1, keepdims=True) + NORM_EPS) * g


def _dot(a, b):
    return jnp.dot(a, b, preferred_element_type=F32)


def _dot_nt(a, b):
    return lax.dot_general(a, b, (((1,), (1,)), ((), ())), preferred_element_type=F32)


def _rope_tables_kernel(pos_ref, freq_ref, g16_ref, g32_ref, c16_o, s16_o, c32_o, s32_o):
    ang = pos_ref[...].astype(F32) * freq_ref[...]
    cos, sin = jnp.cos(ang), jnp.sin(ang)
    lane = lax.broadcasted_iota(jnp.int32, ang.shape, 1)
    low = lane < PARTIAL_ROT
    high = (lane >= HEAD_DIM) & (lane < HEAD_DIM + PARTIAL_ROT)
    mla = (lane >= MLA_NOPE) & (lane < MLA_NOPE + MLA_ROPE)
    c16_o[...] = jnp.where(low, cos, jnp.where(high, pltpu.roll(cos, HEAD_DIM, 1), 1.0))
    s16_o[...] = jnp.where(low, sin, jnp.where(high, pltpu.roll(sin, HEAD_DIM, 1), 0.0)) * g16_ref[...]
    c32_o[...] = jnp.where(mla, pltpu.roll(cos, MLA_ROPE, 1), 1.0)
    s32_o[...] = jnp.where(mla, pltpu.roll(sin, MLA_ROPE, 1), 0.0) * g32_ref[...]


def _rope_tables(positions):
    T = positions.size
    half16, half32 = PARTIAL_ROT // 2, MLA_ROPE // 2
    inv16 = ROPE_THETA ** (-jnp.arange(half16, dtype=F32) / half16)
    inv32 = ROPE_THETA ** (-jnp.arange(half32, dtype=F32) / half32)
    lane = np.arange(LANES)
    freq = jnp.where(lane < PARTIAL_ROT, inv16[lane % half16],
                     jnp.where((lane >= MLA_ROPE) & (lane < 2 * MLA_ROPE), inv32[lane % half32], 0.0))[None, :]
    d = lane % HEAD_DIM
    g16 = np.where(d < PARTIAL_ROT, np.where(d < half16, -1.0, 1.0), 0.0).astype(np.float32)[None, :]
    r = lane - MLA_NOPE
    g32 = np.where((r >= 0) & (r < MLA_ROPE), np.where(r < half32, -1.0, 1.0), 0.0).astype(np.float32)[None, :]
    pos = jnp.broadcast_to(positions.reshape(T, 1), (T, LANES))
    tt = 1024
    row = pl.BlockSpec((tt, LANES), lambda i: (i, 0))
    vec = _const_spec((1, LANES))
    out = jax.ShapeDtypeStruct((T, LANES), F32)
    return pl.pallas_call(
        _rope_tables_kernel, grid=(T // tt,),
        in_specs=[row, vec, vec, vec], out_specs=[row] * 4, out_shape=[out] * 4,
        compiler_params=_cparams(1), name="rope_tables",
    )(pos, freq, jnp.asarray(g16), jnp.asarray(g32))


def _rot16(x, cos, sin):
    lane = lax.broadcasted_iota(jnp.int32, x.shape, 1)
    first = (lane % HEAD_DIM) < (PARTIAL_ROT // 2)
    sw = jnp.where(first, pltpu.roll(x, LANES - PARTIAL_ROT // 2, 1), pltpu.roll(x, PARTIAL_ROT // 2, 1))
    return x * cos + sw * sin


def _rot32(x, cos, sin):
    lane = lax.broadcasted_iota(jnp.int32, x.shape, 1)
    first = lane < (MLA_NOPE + MLA_ROPE // 2)
    sw = jnp.where(first, pltpu.roll(x, LANES - MLA_ROPE // 2, 1), pltpu.roll(x, MLA_ROPE // 2, 1))
    return x * cos + sw * sin


_FF_LANE = 3 * NSA_HEADS
_IN_COLS = {}
for _name, _n in (("q", NSA_HEADS * HEAD_DIM), ("kvc", 4 * HEAD_DIM), ("ksl", LANES), ("kwin", LANES),
                  ("misc", LANES), ("fq", FOX_HEADS * HEAD_DIM), ("fk", FOX_HEADS * HEAD_DIM),
                  ("lat", MLA_Q_LORA + MLA_KV_LORA), ("mg", 3 * D_MODEL)):
    _IN_COLS[_name] = (sum(n for _, n in _IN_COLS.values()), _n)
_IN_ROWS = {"vsl": (0, LANES), "vwin": (LANES, LANES), "fv": (2 * LANES, FOX_HEADS * HEAD_DIM)}


def _inproj_kernel(x_ref, g_ref, c16_ref, s16_ref, c32_ref, s32_ref,
                   w_ref, wt_ref, bf_ref, tri_ref, place_ref,
                   qnrm_ref, wuq_ref, kvnrm_ref, wuk_ref, wuv_ref,
                   qn_o, qr_o, kvc_o, ksl_o, vsl_o, kwin_o, vwin_o, gate_o,
                   fq_o, fk_o, fv_o, mq_o, mk_o, mv_o, mg_o, fcarry_ref, *, seq):
    tm = x_ref.shape[0]
    seq_start = (pl.program_id(0) * tm) % seq == 0
    hb = _rms(x_ref[...], g_ref[...]).astype(BF16)
    c16, s16 = c16_ref[...], s16_ref[...]
    c32, s32 = c32_ref[...], s32_ref[...]
    lane = lax.broadcasted_iota(jnp.int32, (tm, LANES), 1)
    low = lane < HEAD_DIM

    def proj(name):
        a, n = _IN_COLS[name]
        return _dot(hb, w_ref[:, a:a + n])

    def proj_t(name):
        a, n = _IN_ROWS[name]
        return _dot_nt(wt_ref[a:a + n, :], hb)

    def slabs(z):
        return [z[:, k * LANES:(k + 1) * LANES] for k in range(z.shape[1] // LANES)]

    def split(z, fill=0.0):
        return jnp.where(low, z, fill), jnp.where(low, pltpu.roll(z, HEAD_DIM, 1), fill)

    def store_heads(o_ref, pair_slabs, fill=0.0, extra=None):
        for k, z in enumerate(pair_slabs):
            for e, s in enumerate(split(z, fill)):
                h = 2 * k + e
                if extra is not None:
                    s = s + extra[:, h * LANES:(h + 1) * LANES]
                o_ref[:, h * LANES:(h + 1) * LANES] = s.astype(o_ref.dtype)

    zq = slabs(proj("q"))
    store_heads(qn_o, zq)
    store_heads(qr_o, [_rot16(z, c16, s16) for z in zq])

    for k, z in enumerate(slabs(proj("kvc"))):
        for e, s in enumerate(split(z)):
            kvc_o[2 * k + e] = s[:, :HEAD_DIM]

    tok = (pl.program_id(0) * tm) % seq + lax.broadcasted_iota(jnp.int32, (tm, LANES), 0)
    onehot = jnp.where(lane - HEAD_DIM == tok // SLC_LEN, 1.0, 0.0)
    store_heads(ksl_o, [_rot16(proj("ksl"), c16, s16)], fill=onehot)
    vsl_o[0] = proj_t("vsl").astype(BF16)

    store_heads(kwin_o, [_rot16(proj("kwin"), c16, s16)])
    vwin_o[0] = proj_t("vwin").astype(BF16)

    misc = proj("misc")
    gate_o[...] = misc

    @pl.when(seq_start)
    def _():
        fcarry_ref[...] = jnp.zeros(fcarry_ref.shape, F32)

    zf = misc + bf_ref[...]
    is_f = (lane >= _FF_LANE) & (lane < _FF_LANE + FOX_HEADS)
    logf = jnp.where(is_f, jnp.minimum(zf, 0.0) - jnp.log1p(jnp.exp(-jnp.abs(zf))), 0.0)

    def bf16_terms(v):
        packed = None
        for t in range(DECAY_TERMS):
            part = v.astype(BF16).astype(F32)
            v = v - part
            part = part if t == 0 else pltpu.roll(part, FOX_HEADS * t, 1)
            packed = part if packed is None else packed + part
        return packed.astype(BF16)

    terms = _dot(tri_ref[...], bf16_terms(logf))
    fsum = sum(terms if t == 0 else pltpu.roll(terms, LANES - FOX_HEADS * t, 1)
               for t in range(DECAY_TERMS))
    fsum = jnp.where(is_f, fsum, 0.0) + fcarry_ref[0:1]
    fcarry_ref[...] = jnp.broadcast_to(fsum[tm - 1:tm], fcarry_ref.shape)
    decay = _dot(bf16_terms(jnp.where(is_f, -LOG2E * fsum, 0.0)), place_ref[...])
    store_heads(fk_o, slabs(proj("fk")), extra=decay)
    ones = jnp.where((lane >= HEAD_DIM) & (lane < HEAD_DIM + DECAY_TERMS), 1.0, 0.0)
    store_heads(fq_o, slabs(proj("fq")), fill=ones)
    fv_o[0] = proj_t("fv").astype(BF16)

    lat = proj("lat")
    cq = _rms(lat[:, :MLA_Q_LORA], qnrm_ref[...]).astype(BF16)
    ckv = _rms(lat[:, MLA_Q_LORA:], kvnrm_ref[...]).astype(BF16)
    kpe = jnp.where(low, 0.0, _rot32(misc, c32, s32))
    zuq = _dot(cq, wuq_ref[...])
    for k, z in enumerate(slabs(zuq)):
        mq_o[:, k * LANES:(k + 1) * LANES] = _rot32(z, c32, s32).astype(BF16)
    zuk = _dot(ckv, wuk_ref[...])
    for k, z in enumerate(slabs(zuk)):
        mk_o[:, k * LANES:(k + 1) * LANES] = (z + kpe).astype(BF16)
    mv_o[0] = _dot_nt(wuv_ref[...], ckv).astype(BF16)

    mg_o[...] = jax.nn.sigmoid(proj("mg"))


def _pad_halves(w, first_only=True):
    K, n = w.shape[0], w.shape[1] // HEAD_DIM
    w = w.reshape(K, n, HEAD_DIM)
    other = jnp.zeros_like(w) if first_only else w
    return jnp.concatenate([w, other], axis=-1).reshape(K, n * LANES)


def _inproj(x2, tabs, p, seq):
    T, D = x2.shape
    w_in = p["w_in"]
    scale = HEAD_DIM ** -0.5 * LOG2E
    o = 0

    def take(n):
        nonlocal o
        w = w_in[:, o:o + n]
        o += n
        return w

    nsa_w, kvw = NSA_HEADS * HEAD_DIM, NSA_GROUPS * HEAD_DIM
    w_q = take(nsa_w)
    w_kc, w_vc, w_ks, w_vs, w_kw, w_vw = [take(kvw) for _ in range(6)]
    w_g = take(3 * NSA_HEADS)
    w_fq, w_fk, w_fv = [take(FOX_HEADS * HEAD_DIM) for _ in range(3)]
    w_ff = take(FOX_HEADS)
    w_cq, w_ckv, w_kr = take(MLA_Q_LORA), take(MLA_KV_LORA), take(MLA_ROPE)
    w_mg = take(3 * D)

    bf = lambda w: w.astype(BF16)
    row = lambda a: a.reshape(1, -1).astype(F32)
    w_misc = jnp.concatenate([w_g, w_ff, jnp.zeros((D, MLA_NOPE - _FF_LANE - FOX_HEADS), F32), w_kr,
                              jnp.zeros((D, LANES - MLA_NOPE - MLA_ROPE), F32)], axis=1)
    w_tok = bf(jnp.concatenate([w_q * scale, w_kc, w_vc, w_ks, w_kw, w_misc, w_fq * scale, w_fk,
                                w_cq, w_ckv, w_mg], axis=1))
    w_tr = bf(jnp.concatenate([w_vs, w_vw, w_fv], axis=1).T)
    b_f = jnp.pad(row(p["b_forget"]), ((0, 0), (_FF_LANE, LANES - _FF_LANE - FOX_HEADS)))
    tri = jnp.asarray(np.tril(np.ones((TM, TM), np.float32)), BF16)
    place = np.zeros((LANES, FOX_HEADS * LANES), np.float32)
    for t in range(DECAY_TERMS):
        for h in range(FOX_HEADS):
            place[_FF_LANE + FOX_HEADS * t + h, h * LANES + HEAD_DIM + t] = 1.0
    place = jnp.asarray(place, BF16)
    dq = MLA_NOPE + MLA_ROPE
    wuq = (p["mla_w_uq"] * (dq ** -0.5 * LOG2E)).reshape(MLA_Q_LORA, MLA_HEADS, dq)
    wuq = bf(jnp.pad(wuq, ((0, 0), (0, 0), (0, LANES - dq))).reshape(MLA_Q_LORA, MLA_HEADS * LANES))
    wukv = p["mla_w_ukv"].reshape(MLA_KV_LORA, MLA_HEADS, MLA_NOPE + MLA_V)
    wuk = bf(_pad_halves(wukv[:, :, :MLA_NOPE].reshape(MLA_KV_LORA, MLA_HEADS * MLA_NOPE)))
    wuv = bf(wukv[:, :, MLA_NOPE:].reshape(MLA_KV_LORA, MLA_HEADS * MLA_V).T)
    assert w_tok.shape[1] == sum(n for _, n in _IN_COLS.values())

    weights = [w_tok, w_tr, b_f, tri, place,
               row(p["mla_q_norm"]), wuq, row(p["mla_kv_norm"]), wuk, wuv]
    n = T // TM
    tok = lambda c: pl.BlockSpec((TM, c), lambda i: (i, 0))
    in_specs = ([tok(D), _const_spec((1, D))] + [tok(LANES)] * 4
                + [_const_spec(w.shape) for w in weights])
    tokens, transposed = "tokens", "transposed"
    outs = [
        (tokens, NSA_HEADS * LANES, BF16), (tokens, NSA_HEADS * LANES, BF16), None,
        (tokens, NSA_GROUPS * LANES, BF16), (transposed, NSA_GROUPS * HEAD_DIM, BF16),
        (tokens, NSA_GROUPS * LANES, BF16), (transposed, NSA_GROUPS * HEAD_DIM, BF16), (tokens, LANES, F32),
        (tokens, FOX_HEADS * LANES, BF16), (tokens, FOX_HEADS * LANES, BF16),
        (transposed, FOX_HEADS * HEAD_DIM, BF16),
        (tokens, MLA_HEADS * LANES, BF16), (tokens, MLA_HEADS * LANES, BF16),
        (transposed, MLA_HEADS * MLA_V, BF16), (tokens, 3 * D, F32)]
    out_shape, out_specs = [], []
    for spec in outs:
        if spec is None:
            out_shape.append(jax.ShapeDtypeStruct((4, T, HEAD_DIM), F32))
            out_specs.append(pl.BlockSpec((4, TM, HEAD_DIM), lambda i: (0, i, 0)))
        elif spec[0] == transposed:
            out_shape.append(jax.ShapeDtypeStruct((n, spec[1], TM), spec[2]))
            out_specs.append(pl.BlockSpec((1, spec[1], TM), lambda i: (i, 0, 0)))
        else:
            out_shape.append(jax.ShapeDtypeStruct((T, spec[1]), spec[2]))
            out_specs.append(tok(spec[1]))
    return pl.pallas_call(
        functools.partial(_inproj_kernel, seq=seq), grid=(n,),
        in_specs=in_specs, out_specs=out_specs, out_shape=out_shape,
        scratch_shapes=[pltpu.VMEM((8, LANES), F32)],
        compiler_params=_cparams(1), name="inproj",
    )(x2, row(p["mix_norm"]), *tabs, *weights)


def _attn_kernel(*refs, tq, tk, q_of_stream, window_of_stream, k_of_head, values_by_unit):
    n_streams, n_q = len(q_of_stream), max(q_of_stream) + 1
    q_refs, kv_refs = refs[:n_q], refs[n_q:n_q + 2 * n_streams]
    outs = refs[n_q + 2 * n_streams:n_q + 3 * n_streams]
    qt_ref, m_ref, l_ref, acc_ref = refs[n_q + 3 * n_streams:]
    i = pl.program_id(2)
    hs = len(k_of_head)
    nh = n_streams * hs
    k_of = lambda h: kv_refs[2 * (h // hs)]
    vt_of = lambda h: kv_refs[2 * (h // hs) + 1]
    qt_of = lambda h: q_of_stream[h // hs] * hs + h % hs
    per_q = tq // tk
    tv = kv_refs[1].shape[2]
    v_sub = tk // tv
    for qi in range(n_q):
        for r in range(hs):
            qt_ref[qi * hs + r] = q_refs[qi][:, r * LANES:(r + 1) * LANES].astype(F32).T.astype(BF16)
    dist0 = (i * tq + lax.broadcasted_iota(jnp.int32, (tk, tq), 1)
             - lax.broadcasted_iota(jnp.int32, (tk, tq), 0))

    m_ref[...] = jnp.full(m_ref.shape, NEG, F32)
    l_ref[...] = jnp.zeros(l_ref.shape, F32)
    acc_ref[...] = jnp.zeros(acc_ref.shape, F32)

    def logits(j, lo, n, h, cols):
        r0 = pl.multiple_of(j * tk + lo, tv)
        kk = k_of_head[h % hs]
        return _dot(k_of(h)[pl.ds(r0, n), kk * LANES:(kk + 1) * LANES], qt_ref[qt_of(h), :, cols])

    def softmax(s, h, cols, mask):
        if mask is not None:
            s = jnp.where(mask, s, NEG)
        m_old = m_ref[h, :, cols]
        m_new = jnp.maximum(m_old, jnp.max(s, axis=0, keepdims=True))
        alpha = jnp.exp2(m_old - m_new)
        p = jnp.exp2(s - m_new)
        l_ref[h, :, cols] = alpha * l_ref[h, :, cols] + jnp.sum(p, axis=0, keepdims=True)
        m_ref[h, :, cols] = m_new
        return p.astype(BF16), alpha

    def accumulate(j, lo, n, h, cols, p, alpha):
        if values_by_unit:
            half = pl.ds(pl.multiple_of(pl.program_id(1) * HEAD_DIM, HEAD_DIM), HEAD_DIM)
        else:
            half = slice((h % 2) * HEAD_DIM, (h % 2 + 1) * HEAD_DIM)
        pv = sum(_dot(vt_of(h)[j * v_sub + lo // tv + c, half, :], p[c * tv:(c + 1) * tv])
                 for c in range(n // tv))
        acc_ref[h, :, cols] = alpha * acc_ref[h, :, cols] + pv

    def sweep(tiles):
        work = []
        for heads, j, n_tiles, mask, span in tiles:
            for h in heads:
                for c in range(tq // QC):
                    lo, n = (0, n_tiles * tk) if span is None else span(c)
                    work.append((j, lo, n, mask, h, slice(c * QC, (c + 1) * QC)))
        s_of, p_of = {}, {}
        for t in range(len(work) + 2):
            if t < len(work):
                j, lo, n, _, h, cols = work[t]
                s_of[t] = logits(j, lo, n, h, cols)
            if 0 <= t - 1 < len(work):
                _, lo, n, mask, h, cols = work[t - 1]
                p_of[t - 1] = softmax(s_of.pop(t - 1), h, cols,
                                      None if mask is None else mask(slice(lo, lo + n), cols))
            if 0 <= t - 2 < len(work):
                j, lo, n, _, h, cols = work[t - 2]
                accumulate(j, lo, n, h, cols, *p_of.pop(t - 2))

    def visible(j, window, valid=None):
        def mask(keys, cols):
            dist = dist0[keys, cols] - j * tk
            if not window:
                return dist >= 0
            if valid is not None:
                dist = dist + jnp.where(valid, 0, WIN)
            return lax.bitcast_convert_type(dist, jnp.uint32) < WIN
        return mask

    assert per_q == 1 and tk == 2 * QC and WIN == tk, "sweeps assume tq == tk == WIN == two query chunks"
    causal = [h for h in range(nh) if not window_of_stream[h // hs]]
    windowed = [h for h in range(nh) if window_of_stream[h // hs]]
    prev = jnp.maximum(i - 1, 0)

    one_step = len(causal) * (tq // QC) >= 8

    def unmasked(j0, n):
        tiles = []
        while n > 0:
            k = 2 if (one_step and n >= 2) else 1
            tiles.append((causal, j0, k, None, None))
            j0, n = j0 + k, n - k
        return tiles

    def last_tiles(n_left):
        own_span = lambda c: (0, (c + 1) * QC)
        tiles = []
        if causal:
            tiles += unmasked(i - n_left, n_left) + [(causal, i, 1, visible(i, False), own_span)]
        if windowed:
            tiles += [(windowed, i, 1, visible(i, True), own_span),
                      (windowed, prev, 1, visible(i - 1, True, valid=i >= 1),
                       lambda c: (c * QC, tk - c * QC))]
        return tiles

    if causal:
        def body(jj, c):
            sweep(unmasked(LOOP_TILES * jj, LOOP_TILES))
            return c
        lax.fori_loop(0, i // LOOP_TILES, body, 0)
        for n_left in range(LOOP_TILES):
            @pl.when(i % LOOP_TILES == n_left)
            def _():
                sweep(last_tiles(n_left))
    else:
        sweep(last_tiles(0))

    for h in range(0, nh, 2):
        a = acc_ref[h] * (1.0 / l_ref[h])
        b = acc_ref[h + 1] * (1.0 / l_ref[h + 1])
        o_ref, k = outs[h // hs], (h % hs) // 2
        o_ref[:, k * LANES:(k + 1) * LANES] = jnp.concatenate([a, b], axis=0).T.astype(o_ref.dtype)


def _attention(streams, batch, seq, k_of_head, out_dtype, name):
    qs = []
    for q, _, _, _ in streams:
        if not any(q is x for x in qs):
            qs.append(q)
    q_of_stream = tuple(next(n for n, x in enumerate(qs) if x is q) for q, _, _, _ in streams)
    T = qs[0].shape[0]
    nq = seq // TQ
    hs, nk = len(k_of_head), max(k_of_head) + 1
    nh = hs * len(streams)
    units = qs[0].shape[1] // (hs * LANES)
    by_unit = nk == 1
    assert not by_unit or units * HEAD_DIM == LANES, "GQA value blocks must fill one slab"
    in_specs = [pl.BlockSpec((TQ, hs * LANES), lambda b, u, i: (b * nq + i, u)) for _ in qs]
    args = list(qs)
    for q, k, vt, _ in streams:
        assert q.shape[1] == units * hs * LANES and k.shape[1] == units * nk * LANES
        assert vt.shape == (T // TM, units * (HEAD_DIM if by_unit else LANES), TM)
        in_specs += [pl.BlockSpec((seq, nk * LANES), lambda b, u, i: (b, u)),
                     pl.BlockSpec((seq // TM, LANES, TM),
                                  (lambda b, u, i: (b, 0, 0)) if by_unit else (lambda b, u, i: (b, u, 0)))]
        args += [k, vt]
    return pl.pallas_call(
        functools.partial(_attn_kernel, tq=TQ, tk=TK, q_of_stream=q_of_stream,
                          window_of_stream=tuple(w for _, _, _, w in streams),
                          k_of_head=k_of_head, values_by_unit=by_unit),
        grid=(batch, units, nq), in_specs=in_specs,
        out_specs=[pl.BlockSpec((TQ, hs // 2 * LANES), lambda b, u, i: (b * nq + i, u))] * len(streams),
        out_shape=[jax.ShapeDtypeStruct((T, units * hs // 2 * LANES), out_dtype)] * len(streams),
        scratch_shapes=[pltpu.VMEM((len(qs) * hs, LANES, TQ), BF16), pltpu.VMEM((nh, 1, TQ), F32),
                        pltpu.VMEM((nh, 1, TQ), F32), pltpu.VMEM((nh, HEAD_DIM, TQ), F32)],
        compiler_params=_cparams(3), name=name,
    )(*args)


def _gelu_tanh(x):
    return 0.5 * x * (1.0 + jnp.tanh(np.float32(np.sqrt(2.0 / np.pi)) * (x + 0.044715 * (x * x * x))))


def _compress_kernel(x_ref, pe_ref, w1_ref, w2_ref, o_ref):
    x = x_ref[0]
    half = x.shape[1]
    top = _dot((x + pe_ref[0, :, :half]).astype(BF16), w1_ref[0, :half, :])
    bot = _dot((x + pe_ref[0, :, half:]).astype(BF16), w1_ref[0, half:, :])
    hid = top + pltpu.roll(bot, bot.shape[0] - 1, 0)
    o_ref[0, 0] = _dot(_gelu_tanh(hid).astype(BF16), w2_ref[0]).astype(o_ref.dtype)


def _compress(kvc, p, batch, seq):
    T = kvc.shape[1]
    rows = seq // CMP_STRIDE
    x = kvc.reshape(4, T // CMP_STRIDE, CMP_STRIDE * HEAD_DIM)
    pe = jnp.stack([p["cmp_pe_k"].reshape(1, -1), p["cmp_pe_v"].reshape(1, -1)]).astype(F32)
    w1 = jnp.stack([p["cmp_w1_k"], p["cmp_w1_v"]]).astype(BF16)
    w2k = jnp.pad(p["cmp_w2_k"], ((0, 0), (0, LANES - HEAD_DIM)))
    w2v = jnp.concatenate([p["cmp_w2_v"], p["cmp_w2_v"]], axis=1)
    w2 = jnp.stack([w2k, w2v]).astype(BF16)
    return pl.pallas_call(
        _compress_kernel, grid=(4, batch),
        in_specs=[pl.BlockSpec((1, rows, x.shape[2]), lambda j, b: (j, b, 0)),
                  pl.BlockSpec((1,) + pe.shape[1:], lambda j, b: (j // 2, 0, 0)),
                  pl.BlockSpec((1,) + w1.shape[1:], lambda j, b: (j // 2, 0, 0)),
                  pl.BlockSpec((1,) + w2.shape[1:], lambda j, b: (j // 2, 0, 0))],
        out_specs=pl.BlockSpec((1, 1, rows, LANES), lambda j, b: (b, j, 0, 0)),
        out_shape=jax.ShapeDtypeStruct((batch, 4, rows, LANES), BF16),
        compiler_params=_cparams(2), name="nsa_compress",
    )(x, pe, w1, w2)


def _cmp_select_kernel(qn_ref, qr_ref, kc_ref, vc_ref, ovt_ref, o_ref, qa_ref, *, tq):
    i = pl.program_id(2)
    n_cmp = kc_ref.shape[2]
    kc, vc = kc_ref[0, 0], vc_ref[0, 0]
    t_row = i * tq + lax.broadcasted_iota(jnp.int32, (tq, n_cmp), 0)
    blk_end = CMP_STRIDE * lax.broadcasted_iota(jnp.int32, (tq, n_cmp), 1) + (CMP_LEN - 1)
    vis = blk_end <= t_row
    lane = lax.broadcasted_iota(jnp.int32, (tq, LANES), 1)
    psum = jnp.zeros((tq, n_cmp), F32)
    outs = []
    for r in range(NSA_REP):
        s = jnp.where(vis, _dot_nt(qn_ref[:, r * LANES:(r + 1) * LANES], kc), NEG)
        m = jnp.max(s, axis=-1, keepdims=True)
        e = jnp.where(vis, jnp.exp2(s - m), 0.0)
        pn = e / jnp.maximum(jnp.sum(e, axis=-1, keepdims=True), 1e-30)
        psum = psum + pn
        outs.append(_dot(pn.astype(BF16), vc))
    for k in range(NSA_REP // 2):
        o_ref[:, k * LANES:(k + 1) * LANES] = jnp.where(lane < HEAD_DIM, outs[2 * k], outs[2 * k + 1])

    p_hi = psum.astype(BF16)
    p_lo = (psum - p_hi.astype(F32)).astype(BF16)
    ovt = ovt_ref[...]
    imp = _dot_nt(ovt, p_hi) + _dot_nt(ovt, p_lo)
    n_slc = LANES - HEAD_DIM
    jb = lax.broadcasted_iota(jnp.int32, (LANES, tq), 0) - HEAD_DIM
    cur = (i * tq + lax.broadcasted_iota(jnp.int32, (LANES, tq), 1)) // SLC_LEN
    forced = (jb == 0) | (jb == cur) | (jb == cur - 1)
    imp = jnp.where(forced, 1e9, jnp.where(jb > cur, -1e9, imp))
    groups = [imp[HEAD_DIM + 8 * v:HEAD_DIM + 8 * (v + 1)] for v in range(n_slc // 8)]
    sub = lax.broadcasted_iota(jnp.int32, (8, tq), 0)
    ranks = [jnp.zeros((8, tq), F32) for _ in groups]
    for a in range(n_slc):
        row = jnp.broadcast_to(groups[a // 8][a % 8:a % 8 + 1], (8, tq))
        for v in range(len(groups)):
            if v > a // 8:
                inc = jnp.where(row >= groups[v], 1.0, 0.0)
            elif v < a // 8:
                inc = jnp.where(row > groups[v], 1.0, 0.0)
            else:
                tie = jnp.where(sub > a % 8, 1.0, 0.0)
                inc = jnp.where(row > groups[v], 1.0, jnp.where(row == groups[v], tie, 0.0))
            ranks[v] = ranks[v] + inc
    bias_t = jnp.concatenate(
        [jnp.zeros((HEAD_DIM, tq), F32)]
        + [jnp.where(rk < SLC_TOPK, 0.0, SEL_NEG) for rk in ranks], axis=0)
    bias = bias_t.T
    for r in range(NSA_REP):
        qa_ref[:, r * LANES:(r + 1) * LANES] = (
            qr_ref[:, r * LANES:(r + 1) * LANES].astype(F32) + bias).astype(BF16)


def _overlap_t(seq):
    n_cmp = (seq - CMP_LEN) // CMP_STRIDE + 1
    n_slc = seq // SLC_LEN
    assert n_slc == LANES - HEAD_DIM, "selection blocks must fill the bias half of a slab"
    c0 = CMP_STRIDE * np.arange(n_cmp)[:, None]
    s0 = SLC_LEN * np.arange(n_slc)[None, :]
    ov = np.clip(np.minimum(c0 + CMP_LEN, s0 + SLC_LEN) - np.maximum(c0, s0), 0, None) / CMP_LEN
    out = np.zeros((LANES, seq // CMP_STRIDE), np.float32)
    out[HEAD_DIM:, :n_cmp] = ov.T
    return jnp.asarray(out, BF16)


def _cmp_select(qn, qr, cmp_kv, batch, seq):
    T = qn.shape[0]
    nq = seq // TQ
    rows = seq // CMP_STRIDE
    gw = NSA_REP * LANES
    qspec = pl.BlockSpec((TQ, gw), lambda b, g, i: (b * nq + i, g))
    return pl.pallas_call(
        functools.partial(_cmp_select_kernel, tq=TQ), grid=(batch, NSA_GROUPS, nq),
        in_specs=[qspec, qspec,
                  pl.BlockSpec((1, 1, rows, LANES), lambda b, g, i: (b, g, 0, 0)),
                  pl.BlockSpec((1, 1, rows, LANES), lambda b, g, i: (b, NSA_GROUPS + g, 0, 0)),
                  _const_spec((LANES, rows))],
        out_specs=[pl.BlockSpec((TQ, NSA_REP * HEAD_DIM), lambda b, g, i: (b * nq + i, g)), qspec],
        out_shape=[jax.ShapeDtypeStruct((T, NSA_HEADS * HEAD_DIM), F32),
                   jax.ShapeDtypeStruct((T, NSA_HEADS * LANES), BF16)],
        compiler_params=_cparams(3), name="nsa_cmp_select",
    )(qn, qr, cmp_kv, cmp_kv, _overlap_t(seq))


def _merge_kernel(x_ref, oc_ref, os_ref, ow_ref, gl_ref, of_ref, om_ref, mg_ref,
                  e_ref, wn_ref, wf_ref, wm_ref, wo_ref, o_ref):
    d = x_ref.shape[1]
    nw = oc_ref.shape[1]
    sg = jax.nn.sigmoid(gl_ref[...])
    sg_hi = sg.astype(BF16)
    sg_lo = (sg - sg_hi.astype(F32)).astype(BF16)
    e = e_ref[...]
    g = _dot(sg_hi, e) + _dot(sg_lo, e)
    o_nsa = g[:, :nw] * oc_ref[...] + g[:, nw:2 * nw] * os_ref[...] + g[:, 2 * nw:] * ow_ref[...]
    mg = mg_ref[...]
    merged = (mg[:, :d] * _dot(o_nsa.astype(BF16), wn_ref[...])
              + mg[:, d:2 * d] * _dot(of_ref[...], wf_ref[...])
              + mg[:, 2 * d:] * _dot(om_ref[...], wm_ref[...]))
    o_ref[...] = x_ref[...] + _dot(merged.astype(BF16), wo_ref[...])


def _gate_expand():
    e = np.zeros((LANES, 3 * NSA_HEADS * HEAD_DIM), np.float32)
    for h in range(NSA_HEADS):
        for c in range(3):
            e[h * 3 + c, c * NSA_HEADS * HEAD_DIM + h * HEAD_DIM:
              c * NSA_HEADS * HEAD_DIM + (h + 1) * HEAD_DIM] = 1.0
    return jnp.asarray(e, BF16)


def _merge(x2, o_cmp, o_slc, o_win, gate_l, o_fox, o_mla, mg, p):
    T, D = x2.shape
    tok = lambda a: pl.BlockSpec((TM, a.shape[1]), lambda i: (i, 0))
    acts = [x2, o_cmp, o_slc, o_win, gate_l, o_fox, o_mla, mg]
    weights = [_gate_expand(), p["w_br_nsa"].astype(BF16), p["w_br_fox"].astype(BF16),
               p["w_br_mla"].astype(BF16), p["w_out"].astype(BF16)]
    return pl.pallas_call(
        _merge_kernel, grid=(T // TM,),
        in_specs=[tok(a) for a in acts] + [_const_spec(w.shape) for w in weights],
        out_specs=pl.BlockSpec((TM, D), lambda i: (i, 0)),
        out_shape=jax.ShapeDtypeStruct((T, D), F32),
        compiler_params=_cparams(1), name="merge_out",
    )(*acts, *weights)


def _ffn_kernel(x_ref, g_ref, wu_ref, wv_ref, cw_ref, cb_ref, wd_ref, fin_ref, o_ref, tail_ref,
                *, seq, n_chunks, final):
    tm = x_ref.shape[0]
    x = x_ref[...]
    hb = _rms(x, g_ref[...]).astype(BF16)

    @pl.when((pl.program_id(0) * tm) % seq == 0)
    def _():
        tail_ref[...] = jnp.zeros(tail_ref.shape, F32)

    cw = wu_ref.shape[1] // n_chunks
    row8 = lax.broadcasted_iota(jnp.int32, (8, cw), 0)
    acc = jnp.zeros(x.shape, F32)
    for c in range(n_chunks):
        sl = slice(c * cw, (c + 1) * cw)
        u = _dot(hb, wu_ref[:, sl])
        v = _dot(hb, wv_ref[:, sl])
        prev = tail_ref[:, sl]
        tail_ref[:, sl] = u[tm - 8:]
        u1 = pltpu.roll(u, 1, 0)
        u2 = pltpu.roll(u, 2, 0)
        h1 = jnp.where(row8 < 1, pltpu.roll(prev, 1, 0), u1[:8])
        h2 = jnp.where(row8 < 2, pltpu.roll(prev, 2, 0), u2[:8])
        u1 = jnp.concatenate([h1, u1[8:]], axis=0)
        u2 = jnp.concatenate([h2, u2[8:]], axis=0)
        uc = cw_ref[0:1, sl] * u2 + cw_ref[1:2, sl] * u1 + cw_ref[2:3, sl] * u + cb_ref[:, sl]
        act = (uc * jax.nn.sigmoid(uc) * v).astype(BF16)
        acc = acc + _dot(act, wd_ref[sl, :])
    y = x + acc
    if final:
        y = _rms(y, fin_ref[...])
    o_ref[...] = y


def _ffn(x2, p, final_norm, seq, final):
    T, D = x2.shape
    w_up = p["w_up"]
    wu, wv = w_up[:, :D_FF].astype(BF16), w_up[:, D_FF:].astype(BF16)
    row = lambda a: a.reshape(1, -1).astype(F32)
    cwp = jnp.pad(p["conv_w"].astype(F32), ((0, 8 - CONV_W), (0, 0)))
    weights = [row(p["ffn_norm"]), wu, wv, cwp, row(p["conv_b"]), p["w_down"].astype(BF16),
               row(final_norm)]
    return pl.pallas_call(
        functools.partial(_ffn_kernel, seq=seq, n_chunks=2, final=final), grid=(T // TM,),
        in_specs=[pl.BlockSpec((TM, D), lambda i: (i, 0))] + [_const_spec(w.shape) for w in weights],
        out_specs=pl.BlockSpec((TM, D), lambda i: (i, 0)),
        out_shape=jax.ShapeDtypeStruct((T, D), F32),
        scratch_shapes=[pltpu.VMEM((8, D_FF), F32)],
        compiler_params=_cparams(1), name="conv_ffn",
    )(x2, *weights)


_LAYER_PARAMS = ("mix_norm", "w_in", "b_forget", "cmp_pe_k", "cmp_w1_k", "cmp_w2_k", "cmp_pe_v",
                 "cmp_w1_v", "cmp_w2_v", "mla_q_norm", "mla_w_uq", "mla_kv_norm", "mla_w_ukv",
                 "w_br_nsa", "w_br_fox", "w_br_mla", "w_out", "ffn_norm", "w_up", "conv_w",
                 "conv_b", "w_down")


def _mixer_layer(h, tabs, p, batch, seq):
    (qn, qr, kvc, ksl, vsl_t, kwin, vwin_t, gate_l,
     fq, fk, fv_t, mq, mk, mv_t, mg) = _inproj(h, tabs, p, seq)
    cmp_kv = _compress(kvc, p, batch, seq)
    o_cmp, qa = _cmp_select(qn, qr, cmp_kv, batch, seq)
    group = (0,) * NSA_REP
    pair = (0, 1)
    o_slc, o_win = _attention([(qa, ksl, vsl_t, False), (qa, kwin, vwin_t, True)], batch, seq, group,
                              F32, "nsa_slc_win_attn")
    o_fox, o_mla = _attention([(fq, fk, fv_t, False), (mq, mk, mv_t, False)], batch, seq, pair, BF16,
                              "fox_mla_attn")
    return _merge(h, o_cmp, o_slc, o_win, gate_l, o_fox, o_mla, mg, p)


def kernel(x, positions, mix_norm, w_in, b_forget, cmp_pe_k, cmp_w1_k, cmp_w2_k, cmp_pe_v, cmp_w1_v,
           cmp_w2_v, mla_q_norm, mla_w_uq, mla_kv_norm, mla_w_ukv, w_br_nsa, w_br_fox, w_br_mla,
           w_out, ffn_norm, w_up, conv_w, conv_b, w_down, final_norm):
    batch, seq, d = x.shape
    stacked = dict(zip(_LAYER_PARAMS, (
        mix_norm, w_in, b_forget, cmp_pe_k, cmp_w1_k, cmp_w2_k, cmp_pe_v, cmp_w1_v, cmp_w2_v,
        mla_q_norm, mla_w_uq, mla_kv_norm, mla_w_ukv, w_br_nsa, w_br_fox, w_br_mla, w_out,
        ffn_norm, w_up, conv_w, conv_b, w_down)))
    depth = w_in.shape[0]
    tabs = _rope_tables(positions)
    h = x.reshape(batch * seq, d)
    for l in range(depth):
        p = {k: v[l] for k, v in stacked.items()}
        h = _mixer_layer(h, tabs, p, batch, seq)
        h = _ffn(h, p, final_norm, seq, final=(l == depth - 1))
    return h.reshape(batch, seq, d)
```

```python
import functools

import numpy as np
import jax
import jax.numpy as jnp
from jax import lax
from jax.experimental import pallas as pl
from jax.experimental.pallas import tpu as pltpu

F32 = jnp.float32
BF16 = jnp.bfloat16

HEAD_DIM = 64
ROPE_THETA = 500000.0
PARTIAL_ROT = HEAD_DIM // 4
NORM_EPS = 1e-6
NSA_HEADS = 8
NSA_GROUPS = 2
NSA_REP = NSA_HEADS // NSA_GROUPS
CMP_LEN = 32
CMP_STRIDE = 16
CMP_HID = 4 * HEAD_DIM
SLC_LEN = 64
SLC_TOPK = 16
WIN = 512
FOX_HEADS = 8
MLA_HEADS = 8
MLA_Q_LORA = 384
MLA_KV_LORA = 256
MLA_NOPE = 64
MLA_ROPE = 32
MLA_V = 64
D_FF = 2816
D_MODEL = 1024
CONV_W = 3

LANES = 128
LOG2E = 1.4426950408889634
NEG = -1e30
SEL_NEG = -1e9
VMEM_LIMIT = 52 * 1024 * 1024

TM = 256
TQ = 1024
TQ_SELECT = 512
TK = 512
QC = 256
LOOP_TILES = 4
DECAY_TERMS = 3


def _cparams(n_axes):
    return pltpu.CompilerParams(dimension_semantics=("arbitrary",) * n_axes,
                                vmem_limit_bytes=VMEM_LIMIT)


def _const_spec(shape):
    nd = len(shape)
    return pl.BlockSpec(shape, lambda *_: (0,) * nd, pipeline_mode=pl.Buffered(1))


def _rms(x, g):
    return x * lax.rsqrt(jnp.mean(x * x, axis=-1, keepdims=True) + NORM_EPS) * g


def _dot(a, b):
    return jnp.dot(a, b, preferred_element_type=F32)


def _dot_nt(a, b):
    return lax.dot_general(a, b, (((1,), (1,)), ((), ())), preferred_element_type=F32)


def _rope_tables_kernel(pos_ref, freq_ref, g16_ref, g32_ref, c16_o, s16_o, c32_o, s32_o):
    ang = pos_ref[...].astype(F32) * freq_ref[...]
    cos, sin = jnp.cos(ang), jnp.sin(ang)
    lane = lax.broadcasted_iota(jnp.int32, ang.shape, 1)
    low = lane < PARTIAL_ROT
    high = (lane >= HEAD_DIM) & (lane < HEAD_DIM + PARTIAL_ROT)
    mla = (lane >= MLA_NOPE) & (lane < MLA_NOPE + MLA_ROPE)
    c16_o[...] = jnp.where(low, cos, jnp.where(high, pltpu.roll(cos, HEAD_DIM, 1), 1.0))
    s16_o[...] = jnp.where(low, sin, jnp.where(high, pltpu.roll(sin, HEAD_DIM, 1), 0.0)) * g16_ref[...]
    c32_o[...] = jnp.where(mla, pltpu.roll(cos, MLA_ROPE, 1), 1.0)
    s32_o[...] = jnp.where(mla, pltpu.roll(sin, MLA_ROPE, 1), 0.0) * g32_ref[...]


def _rope_tables(positions):
    T = positions.size
    half16, half32 = PARTIAL_ROT // 2, MLA_ROPE // 2
    inv16 = ROPE_THETA ** (-jnp.arange(half16, dtype=F32) / half16)
    inv32 = ROPE_THETA ** (-jnp.arange(half32, dtype=F32) / half32)
    lane = np.arange(LANES)
    freq = jnp.where(lane < PARTIAL_ROT, inv16[lane % half16],
                     jnp.where((lane >= MLA_ROPE) & (lane < 2 * MLA_ROPE), inv32[lane % half32], 0.0))[None, :]
    d = lane % HEAD_DIM
    g16 = np.where(d < PARTIAL_ROT, np.where(d < half16, -1.0, 1.0), 0.0).astype(np.float32)[None, :]
    r = lane - MLA_NOPE
    g32 = np.where((r >= 0) & (r < MLA_ROPE), np.where(r < half32, -1.0, 1.0), 0.0).astype(np.float32)[None, :]
    pos = jnp.broadcast_to(positions.reshape(T, 1), (T, LANES))
    tt = 1024
    row = pl.BlockSpec((tt, LANES), lambda i: (i, 0))
    vec = _const_spec((1, LANES))
    out = jax.ShapeDtypeStruct((T, LANES), F32)
    return pl.pallas_call(
        _rope_tables_kernel, grid=(T // tt,),
        in_specs=[row, vec, vec, vec], out_specs=[row] * 4, out_shape=[out] * 4,
        compiler_params=_cparams(1), name="rope_tables",
    )(pos, freq, jnp.asarray(g16), jnp.asarray(g32))


def _rot16(x, cos, sin):
    lane = lax.broadcasted_iota(jnp.int32, x.shape, 1)
    first = (lane % HEAD_DIM) < (PARTIAL_ROT // 2)
    sw = jnp.where(first, pltpu.roll(x, LANES - PARTIAL_ROT // 2, 1), pltpu.roll(x, PARTIAL_ROT // 2, 1))
    return x * cos + sw * sin


def _rot32(x, cos, sin):
    lane = lax.broadcasted_iota(jnp.int32, x.shape, 1)
    first = lane < (MLA_NOPE + MLA_ROPE // 2)
    sw = jnp.where(first, pltpu.roll(x, LANES - MLA_ROPE // 2, 1), pltpu.roll(x, MLA_ROPE // 2, 1))
    return x * cos + sw * sin


_FF_LANE = 3 * NSA_HEADS
_IN_COLS = {}
for _name, _n in (("q", NSA_HEADS * HEAD_DIM), ("kvc", 4 * HEAD_DIM), ("ksl", LANES), ("kwin", LANES),
                  ("misc", LANES), ("fq", FOX_HEADS * HEAD_DIM), ("fk", FOX_HEADS * HEAD_DIM),
                  ("lat", MLA_Q_LORA + MLA_KV_LORA), ("mg", 3 * D_MODEL)):
    _IN_COLS[_name] = (sum(n for _, n in _IN_COLS.values()), _n)
_IN_ROWS = {"vsl": (0, LANES), "vwin": (LANES, LANES), "fv": (2 * LANES, FOX_HEADS * HEAD_DIM)}


def _inproj_kernel(x_ref, g_ref, c16_ref, s16_ref, c32_ref, s32_ref,
                   w_ref, wt_ref, bf_ref, tri_ref, place_ref,
                   qnrm_ref, wuq_ref, kvnrm_ref, wuk_ref, wuv_ref,
                   qn_o, qr_o, kvc_o, ksl_o, vsl_o, kwin_o, vwin_o, gate_o,
                   fq_o, fk_o, fv_o, mq_o, mk_o, mv_o, mg_o, fcarry_ref, *, seq):
    tm = x_ref.shape[0]
    seq_start = (pl.program_id(0) * tm) % seq == 0
    hb = _rms(x_ref[...], g_ref[...]).astype(BF16)
    c16, s16 = c16_ref[...], s16_ref[...]
    c32, s32 = c32_ref[...], s32_ref[...]
    lane = lax.broadcasted_iota(jnp.int32, (tm, LANES), 1)
    low = lane < HEAD_DIM

    def proj(name):
        a, n = _IN_COLS[name]
        return _dot(hb, w_ref[:, a:a + n])

    def proj_t(name):
        a, n = _IN_ROWS[name]
        return _dot_nt(wt_ref[a:a + n, :], hb)

    def slabs(z):
        return [z[:, k * LANES:(k + 1) * LANES] for k in range(z.shape[1] // LANES)]

    def split(z, fill=0.0):
        return jnp.where(low, z, fill), jnp.where(low, pltpu.roll(z, HEAD_DIM, 1), fill)

    def store_heads(o_ref, pair_slabs, fill=0.0, extra=None):
        for k, z in enumerate(pair_slabs):
            for e, s in enumerate(split(z, fill)):
                h = 2 * k + e
                if extra is not None:
                    s = s + extra[:, h * LANES:(h + 1) * LANES]
                o_ref[:, h * LANES:(h + 1) * LANES] = s.astype(o_ref.dtype)

    zq = slabs(proj("q"))
    store_heads(qn_o, zq)
    store_heads(qr_o, [_rot16(z, c16, s16) for z in zq])

    for k, z in enumerate(slabs(proj("kvc"))):
        for e, s in enumerate(split(z)):
            kvc_o[2 * k + e] = s[:, :HEAD_DIM]

    tok = (pl.program_id(0) * tm) % seq + lax.broadcasted_iota(jnp.int32, (tm, LANES), 0)
    onehot = jnp.where(lane - HEAD_DIM == tok // SLC_LEN, 1.0, 0.0)
    store_heads(ksl_o, [_rot16(proj("ksl"), c16, s16)], fill=onehot)
    vsl_o[0] = proj_t("vsl").astype(BF16)

    store_heads(kwin_o, [_rot16(proj("kwin"), c16, s16)])
    vwin_o[0] = proj_t("vwin").astype(BF16)

    misc = proj("misc")
    gate_o[...] = misc

    @pl.when(seq_start)
    def _():
        fcarry_ref[...] = jnp.zeros(fcarry_ref.shape, F32)

    zf = misc + bf_ref[...]
    is_f = (lane >= _FF_LANE) & (lane < _FF_LANE + FOX_HEADS)
    logf = jnp.where(is_f, jnp.minimum(zf, 0.0) - jnp.log1p(jnp.exp(-jnp.abs(zf))), 0.0)

    def bf16_terms(v):
        packed = None
        for t in range(DECAY_TERMS):
            part = v.astype(BF16).astype(F32)
            v = v - part
            part = part if t == 0 else pltpu.roll(part, FOX_HEADS * t, 1)
            packed = part if packed is None else packed + part
        return packed.astype(BF16)

    terms = _dot(tri_ref[...], bf16_terms(logf))
    fsum = sum(terms if t == 0 else pltpu.roll(terms, LANES - FOX_HEADS * t, 1)
               for t in range(DECAY_TERMS))
    fsum = jnp.where(is_f, fsum, 0.0) + fcarry_ref[0:1]
    fcarry_ref[...] = jnp.broadcast_to(fsum[tm - 1:tm], fcarry_ref.shape)
    decay = _dot(bf16_terms(jnp.where(is_f, -LOG2E * fsum, 0.0)), place_ref[...])
    store_heads(fk_o, slabs(proj("fk")), extra=decay)
    ones = jnp.where((lane >= HEAD_DIM) & (lane < HEAD_DIM + DECAY_TERMS), 1.0, 0.0)
    store_heads(fq_o, slabs(proj("fq")), fill=ones)
    fv_o[0] = proj_t("fv").astype(BF16)

    lat = proj("lat")
    cq = _rms(lat[:, :MLA_Q_LORA], qnrm_ref[...]).astype(BF16)
    ckv = _rms(lat[:, MLA_Q_LORA:], kvnrm_ref[...]).astype(BF16)
    kpe = jnp.where(low, 0.0, _rot32(misc, c32, s32))
    zuq = _dot(cq, wuq_ref[...])
    for k, z in enumerate(slabs(zuq)):
        mq_o[:, k * LANES:(k + 1) * LANES] = _rot32(z, c32, s32).astype(BF16)
    zuk = _dot(ckv, wuk_ref[...])
    for k, z in enumerate(slabs(zuk)):
        mk_o[:, k * LANES:(k + 1) * LANES] = (z + kpe).astype(BF16)
    mv_o[0] = _dot_nt(wuv_ref[...], ckv).astype(BF16)

    mg_o[...] = jax.nn.sigmoid(proj("mg"))


def _pad_halves(w, first_only=True):
    K, n = w.shape[0], w.shape[1] // HEAD_DIM
    w = w.reshape(K, n, HEAD_DIM)
    other = jnp.zeros_like(w) if first_only else w
    return jnp.concatenate([w, other], axis=-1).reshape(K, n * LANES)


def _inproj(x2, tabs, p, seq):
    T, D = x2.shape
    w_in = p["w_in"]
    scale = HEAD_DIM ** -0.5 * LOG2E
    o = 0

    def take(n):
        nonlocal o
        w = w_in[:, o:o + n]
        o += n
        return w

    nsa_w, kvw = NSA_HEADS * HEAD_DIM, NSA_GROUPS * HEAD_DIM
    w_q = take(nsa_w)
    w_kc, w_vc, w_ks, w_vs, w_kw, w_vw = [take(kvw) for _ in range(6)]
    w_g = take(3 * NSA_HEADS)
    w_fq, w_fk, w_fv = [take(FOX_HEADS * HEAD_DIM) for _ in range(3)]
    w_ff = take(FOX_HEADS)
    w_cq, w_ckv, w_kr = take(MLA_Q_LORA), take(MLA_KV_LORA), take(MLA_ROPE)
    w_mg = take(3 * D)

    bf = lambda w: w.astype(BF16)
    row = lambda a: a.reshape(1, -1).astype(F32)
    w_misc = jnp.concatenate([w_g, w_ff, jnp.zeros((D, MLA_NOPE - _FF_LANE - FOX_HEADS), F32), w_kr,
                              jnp.zeros((D, LANES - MLA_NOPE - MLA_ROPE), F32)], axis=1)
    w_tok = bf(jnp.concatenate([w_q * scale, w_kc, w_vc, w_ks, w_kw, w_misc, w_fq * scale, w_fk,
                                w_cq, w_ckv, w_mg], axis=1))
    w_tr = bf(jnp.concatenate([w_vs, w_vw, w_fv], axis=1).T)
    b_f = jnp.pad(row(p["b_forget"]), ((0, 0), (_FF_LANE, LANES - _FF_LANE - FOX_HEADS)))
    tri = jnp.asarray(np.tril(np.ones((TM, TM), np.float32)), BF16)
    place = np.zeros((LANES, FOX_HEADS * LANES), np.float32)
    for t in range(DECAY_TERMS):
        for h in range(FOX_HEADS):
            place[_FF_LANE + FOX_HEADS * t + h, h * LANES + HEAD_DIM + t] = 1.0
    place = jnp.asarray(place, BF16)
    dq = MLA_NOPE + MLA_ROPE
    wuq = (p["mla_w_uq"] * (dq ** -0.5 * LOG2E)).reshape(MLA_Q_LORA, MLA_HEADS, dq)
    wuq = bf(jnp.pad(wuq, ((0, 0), (0, 0), (0, LANES - dq))).reshape(MLA_Q_LORA, MLA_HEADS * LANES))
    wukv = p["mla_w_ukv"].reshape(MLA_KV_LORA, MLA_HEADS, MLA_NOPE + MLA_V)
    wuk = bf(_pad_halves(wukv[:, :, :MLA_NOPE].reshape(MLA_KV_LORA, MLA_HEADS * MLA_NOPE)))
    wuv = bf(wukv[:, :, MLA_NOPE:].reshape(MLA_KV_LORA, MLA_HEADS * MLA_V).T)
    assert w_tok.shape[1] == sum(n for _, n in _IN_COLS.values())

    weights = [w_tok, w_tr, b_f, tri, place,
               row(p["mla_q_norm"]), wuq, row(p["mla_kv_norm"]), wuk, wuv]
    n = T // TM
    tok = lambda c: pl.BlockSpec((TM, c), lambda i: (i, 0))
    in_specs = ([tok(D), _const_spec((1, D))] + [tok(LANES)] * 4
                + [_const_spec(w.shape) for w in weights])
    tokens, transposed = "tokens", "transposed"
    outs = [
        (tokens, NSA_HEADS * LANES, BF16), (tokens, NSA_HEADS * LANES, BF16), None,
        (tokens, NSA_GROUPS * LANES, BF16), (transposed, NSA_GROUPS * HEAD_DIM, BF16),
        (tokens, NSA_GROUPS * LANES, BF16), (transposed, NSA_GROUPS * HEAD_DIM, BF16), (tokens, LANES, F32),
        (tokens, FOX_HEADS * LANES, BF16), (tokens, FOX_HEADS * LANES, BF16),
        (transposed, FOX_HEADS * HEAD_DIM, BF16),
        (tokens, MLA_HEADS * LANES, BF16), (tokens, MLA_HEADS * LANES, BF16),
        (transposed, MLA_HEADS * MLA_V, BF16), (tokens, 3 * D, F32)]
    out_shape, out_specs = [], []
    for spec in outs:
        if spec is None:
            out_shape.append(jax.ShapeDtypeStruct((4, T, HEAD_DIM), F32))
            out_specs.append(pl.BlockSpec((4, TM, HEAD_DIM), lambda i: (0, i, 0)))
        elif spec[0] == transposed:
            out_shape.append(jax.ShapeDtypeStruct((n, spec[1], TM), spec[2]))
            out_specs.append(pl.BlockSpec((1, spec[1], TM), lambda i: (i, 0, 0)))
        else:
            out_shape.append(jax.ShapeDtypeStruct((T, spec[1]), spec[2]))
            out_specs.append(tok(spec[1]))
    return pl.pallas_call(
        functools.partial(_inproj_kernel, seq=seq), grid=(n,),
        in_specs=in_specs, out_specs=out_specs, out_shape=out_shape,
        scratch_shapes=[pltpu.VMEM((8, LANES), F32)],
        compiler_params=_cparams(1), name="inproj",
    )(x2, row(p["mix_norm"]), *tabs, *weights)


def _attn_kernel(*refs, tq, tk, q_of_stream, window_of_stream, k_of_head, values_by_unit):
    n_streams, n_q = len(q_of_stream), max(q_of_stream) + 1
    q_refs, kv_refs = refs[:n_q], refs[n_q:n_q + 2 * n_streams]
    outs = refs[n_q + 2 * n_streams:n_q + 3 * n_streams]
    qt_ref, m_ref, l_ref, acc_ref = refs[n_q + 3 * n_streams:]
    i = pl.program_id(2)
    hs = len(k_of_head)
    nh = n_streams * hs
    k_of = lambda h: kv_refs[2 * (h // hs)]
    vt_of = lambda h: kv_refs[2 * (h // hs) + 1]
    qt_of = lambda h: q_of_stream[h // hs] * hs + h % hs
    per_q = tq // tk
    tv = kv_refs[1].shape[2]
    v_sub = tk // tv
    for qi in range(n_q):
        for r in range(hs):
            qt_ref[qi * hs + r] = q_refs[qi][:, r * LANES:(r + 1) * LANES].astype(F32).T.astype(BF16)
    dist0 = (i * tq + lax.broadcasted_iota(jnp.int32, (tk, tq), 1)
             - lax.broadcasted_iota(jnp.int32, (tk, tq), 0))

    m_ref[...] = jnp.full(m_ref.shape, NEG, F32)
    l_ref[...] = jnp.zeros(l_ref.shape, F32)
    acc_ref[...] = jnp.zeros(acc_ref.shape, F32)

    def logits(j, lo, n, h, cols):
        r0 = pl.multiple_of(j * tk + lo, tv)
        kk = k_of_head[h % hs]
        return _dot(k_of(h)[pl.ds(r0, n), kk * LANES:(kk + 1) * LANES], qt_ref[qt_of(h), :, cols])

    def softmax(s, h, cols, mask):
        if mask is not None:
            s = jnp.where(mask, s, NEG)
        m_old = m_ref[h, :, cols]
        m_new = jnp.maximum(m_old, jnp.max(s, axis=0, keepdims=True))
        alpha = jnp.exp2(m_old - m_new)
        p = jnp.exp2(s - m_new)
        l_ref[h, :, cols] = alpha * l_ref[h, :, cols] + jnp.sum(p, axis=0, keepdims=True)
        m_ref[h, :, cols] = m_new
        return p.astype(BF16), alpha

    def accumulate(j, lo, n, h, cols, p, alpha):
        if values_by_unit:
            half = pl.ds(pl.multiple_of(pl.program_id(1) * HEAD_DIM, HEAD_DIM), HEAD_DIM)
        else:
            half = slice((h % 2) * HEAD_DIM, (h % 2 + 1) * HEAD_DIM)
        pv = sum(_dot(vt_of(h)[j * v_sub + lo // tv + c, half, :], p[c * tv:(c + 1) * tv])
                 for c in range(n // tv))
        acc_ref[h, :, cols] = alpha * acc_ref[h, :, cols] + pv

    def sweep(tiles):
        work = []
        for heads, j, n_tiles, mask, span in tiles:
            for h in heads:
                for c in range(tq // QC):
                    lo, n, masked = (0, n_tiles * tk, False) if span is None else span(c)
                    if n > 0:
                        work.append((j, lo, n, mask if masked else None, h, slice(c * QC, (c + 1) * QC)))
        s_of, p_of = {}, {}
        for t in range(len(work) + 2):
            if t < len(work):
                j, lo, n, _, h, cols = work[t]
                s_of[t] = logits(j, lo, n, h, cols)
            if 0 <= t - 1 < len(work):
                _, lo, n, mask, h, cols = work[t - 1]
                p_of[t - 1] = softmax(s_of.pop(t - 1), h, cols,
                                      None if mask is None else mask(slice(lo, lo + n), cols))
            if 0 <= t - 2 < len(work):
                j, lo, n, _, h, cols = work[t - 2]
                accumulate(j, lo, n, h, cols, *p_of.pop(t - 2))

    def visible(j, window, valid=None):
        def mask(keys, cols):
            dist = dist0[keys, cols] - j * tk
            if not window:
                return dist >= 0
            if valid is not None:
                dist = dist + jnp.where(valid, 0, WIN)
            return lax.bitcast_convert_type(dist, jnp.uint32) < WIN
        return mask

    assert tk == 2 * QC and WIN % tk == 0 and LOOP_TILES % per_q == 0
    causal = [h for h in range(nh) if not window_of_stream[h // hs]]
    windowed = [h for h in range(nh) if window_of_stream[h // hs]]
    first = i * per_q

    def causal_span(d):
        def span(c):
            hi = min(tk, (c + 1) * QC - d * tk)
            return 0, hi, d * tk + hi > c * QC + 1
        return span

    def window_span(d):
        def span(c):
            lo = max(0, (c * QC - WIN + 1) // tv * tv - d * tk)
            return lo, min(tk, (c + 1) * QC - d * tk) - lo, True
        return span

    one_step = len(causal) * (tq // QC) >= 8

    def unmasked(j0, n):
        tiles = []
        while n > 0:
            k = 2 if (one_step and n >= 2) else 1
            tiles.append((causal, j0, k, None, None))
            j0, n = j0 + k, n - k
        return tiles

    def last_tiles(n_left):
        tiles = []
        if causal:
            tiles += unmasked(first - n_left, n_left)
            tiles += [(causal, first + d, 1, visible(first + d, False), causal_span(d)) for d in range(per_q)]
        if windowed:
            tiles += [(windowed, first + d, 1, visible(first + d, True), window_span(d))
                      for d in reversed(range(per_q))]
            tiles += [(windowed, jnp.maximum(first + d, 0), 1,
                       visible(first + d, True, valid=first + d >= 0), window_span(d))
                      for d in range(-(WIN // tk), 0)]
        return tiles

    if causal:
        def body(jj, c):
            sweep(unmasked(LOOP_TILES * jj, LOOP_TILES))
            return c
        lax.fori_loop(0, first // LOOP_TILES, body, 0)
        for n_left in range(0, LOOP_TILES, per_q):
            @pl.when(first % LOOP_TILES == n_left)
            def _():
                sweep(last_tiles(n_left))
    else:
        sweep(last_tiles(0))

    for h in range(0, nh, 2):
        a = acc_ref[h] * (1.0 / l_ref[h])
        b = acc_ref[h + 1] * (1.0 / l_ref[h + 1])
        o_ref, k = outs[h // hs], (h % hs) // 2
        o_ref[:, k * LANES:(k + 1) * LANES] = jnp.concatenate([a, b], axis=0).T.astype(o_ref.dtype)


def _attention(streams, batch, seq, k_of_head, out_dtype, name):
    qs = []
    for q, _, _, _ in streams:
        if not any(q is x for x in qs):
            qs.append(q)
    q_of_stream = tuple(next(n for n, x in enumerate(qs) if x is q) for q, _, _, _ in streams)
    T = qs[0].shape[0]
    nq = seq // TQ
    hs, nk = len(k_of_head), max(k_of_head) + 1
    nh = hs * len(streams)
    units = qs[0].shape[1] // (hs * LANES)
    by_unit = nk == 1
    assert not by_unit or units * HEAD_DIM == LANES, "GQA value blocks must fill one slab"
    in_specs = [pl.BlockSpec((TQ, hs * LANES), lambda b, u, i: (b * nq + i, u)) for _ in qs]
    args = list(qs)
    for q, k, vt, _ in streams:
        assert q.shape[1] == units * hs * LANES and k.shape[1] == units * nk * LANES
        assert vt.shape == (T // TM, units * (HEAD_DIM if by_unit else LANES), TM)
        in_specs += [pl.BlockSpec((seq, nk * LANES), lambda b, u, i: (b, u)),
                     pl.BlockSpec((seq // TM, LANES, TM),
                                  (lambda b, u, i: (b, 0, 0)) if by_unit else (lambda b, u, i: (b, u, 0)))]
        args += [k, vt]
    return pl.pallas_call(
        functools.partial(_attn_kernel, tq=TQ, tk=TK, q_of_stream=q_of_stream,
                          window_of_stream=tuple(w for _, _, _, w in streams),
                          k_of_head=k_of_head, values_by_unit=by_unit),
        grid=(batch, units, nq), in_specs=in_specs,
        out_specs=[pl.BlockSpec((TQ, hs // 2 * LANES), lambda b, u, i: (b * nq + i, u))] * len(streams),
        out_shape=[jax.ShapeDtypeStruct((T, units * hs // 2 * LANES), out_dtype)] * len(streams),
        scratch_shapes=[pltpu.VMEM((len(qs) * hs, LANES, TQ), BF16), pltpu.VMEM((nh, 1, TQ), F32),
                        pltpu.VMEM((nh, 1, TQ), F32), pltpu.VMEM((nh, HEAD_DIM, TQ), F32)],
        compiler_params=_cparams(3), name=name,
    )(*args)


def _gelu_tanh(x):
    return 0.5 * x * (1.0 + jnp.tanh(np.float32(np.sqrt(2.0 / np.pi)) * (x + 0.044715 * (x * x * x))))


def _compress_kernel(x_ref, pe_ref, w1_ref, w2_ref, o_ref):
    x = x_ref[0]
    half = x.shape[1]
    top = _dot((x + pe_ref[0, :, :half]).astype(BF16), w1_ref[0, :half, :])
    bot = _dot((x + pe_ref[0, :, half:]).astype(BF16), w1_ref[0, half:, :])
    hid = top + pltpu.roll(bot, bot.shape[0] - 1, 0)
    o_ref[0, 0] = _dot(_gelu_tanh(hid).astype(BF16), w2_ref[0]).astype(o_ref.dtype)


def _compress(kvc, p, batch, seq):
    T = kvc.shape[1]
    rows = seq // CMP_STRIDE
    x = kvc.reshape(4, T // CMP_STRIDE, CMP_STRIDE * HEAD_DIM)
    pe = jnp.stack([p["cmp_pe_k"].reshape(1, -1), p["cmp_pe_v"].reshape(1, -1)]).astype(F32)
    w1 = jnp.stack([p["cmp_w1_k"], p["cmp_w1_v"]]).astype(BF16)
    w2k = jnp.pad(p["cmp_w2_k"], ((0, 0), (0, LANES - HEAD_DIM)))
    w2v = jnp.concatenate([p["cmp_w2_v"], p["cmp_w2_v"]], axis=1)
    w2 = jnp.stack([w2k, w2v]).astype(BF16)
    return pl.pallas_call(
        _compress_kernel, grid=(4, batch),
        in_specs=[pl.BlockSpec((1, rows, x.shape[2]), lambda j, b: (j, b, 0)),
                  pl.BlockSpec((1,) + pe.shape[1:], lambda j, b: (j // 2, 0, 0)),
                  pl.BlockSpec((1,) + w1.shape[1:], lambda j, b: (j // 2, 0, 0)),
                  pl.BlockSpec((1,) + w2.shape[1:], lambda j, b: (j // 2, 0, 0))],
        out_specs=pl.BlockSpec((1, 1, rows, LANES), lambda j, b: (b, j, 0, 0)),
        out_shape=jax.ShapeDtypeStruct((batch, 4, rows, LANES), BF16),
        compiler_params=_cparams(2), name="nsa_compress",
    )(x, pe, w1, w2)


def _cmp_select_kernel(qn_ref, qr_ref, kc_ref, vc_ref, ovt_ref, o_ref, qa_ref, *, tq):
    i = pl.program_id(2)
    n_cmp = kc_ref.shape[2]
    kc, vc = kc_ref[0, 0], vc_ref[0, 0]
    t_row = i * tq + lax.broadcasted_iota(jnp.int32, (tq, n_cmp), 0)
    blk_end = CMP_STRIDE * lax.broadcasted_iota(jnp.int32, (tq, n_cmp), 1) + (CMP_LEN - 1)
    vis = blk_end <= t_row
    lane = lax.broadcasted_iota(jnp.int32, (tq, LANES), 1)
    psum = jnp.zeros((tq, n_cmp), F32)
    outs = []
    for r in range(NSA_REP):
        s = jnp.where(vis, _dot_nt(qn_ref[:, r * LANES:(r + 1) * LANES], kc), NEG)
        m = jnp.max(s, axis=-1, keepdims=True)
        e = jnp.where(vis, jnp.exp2(s - m), 0.0)
        pn = e / jnp.maximum(jnp.sum(e, axis=-1, keepdims=True), 1e-30)
        psum = psum + pn
        outs.append(_dot(pn.astype(BF16), vc))
    for k in range(NSA_REP // 2):
        o_ref[:, k * LANES:(k + 1) * LANES] = jnp.where(lane < HEAD_DIM, outs[2 * k], outs[2 * k + 1])

    p_hi = psum.astype(BF16)
    p_lo = (psum - p_hi.astype(F32)).astype(BF16)
    ovt = ovt_ref[...]
    imp = _dot_nt(ovt, p_hi) + _dot_nt(ovt, p_lo)
    n_slc = LANES - HEAD_DIM
    jb = lax.broadcasted_iota(jnp.int32, (LANES, tq), 0) - HEAD_DIM
    cur = (i * tq + lax.broadcasted_iota(jnp.int32, (LANES, tq), 1)) // SLC_LEN
    forced = (jb == 0) | (jb == cur) | (jb == cur - 1)
    imp = jnp.where(forced, 1e9, jnp.where(jb > cur, -1e9, imp))
    groups = [imp[HEAD_DIM + 8 * v:HEAD_DIM + 8 * (v + 1)] for v in range(n_slc // 8)]
    sub = lax.broadcasted_iota(jnp.int32, (8, tq), 0)
    ranks = [jnp.zeros((8, tq), F32) for _ in groups]
    for a in range(n_slc):
        row = jnp.broadcast_to(groups[a // 8][a % 8:a % 8 + 1], (8, tq))
        for v in range(len(groups)):
            if v > a // 8:
                inc = jnp.where(row >= groups[v], 1.0, 0.0)
            elif v < a // 8:
                inc = jnp.where(row > groups[v], 1.0, 0.0)
            else:
                tie = jnp.where(sub > a % 8, 1.0, 0.0)
                inc = jnp.where(row > groups[v], 1.0, jnp.where(row == groups[v], tie, 0.0))
            ranks[v] = ranks[v] + inc
    bias_t = jnp.concatenate(
        [jnp.zeros((HEAD_DIM, tq), F32)]
        + [jnp.where(rk < SLC_TOPK, 0.0, SEL_NEG) for rk in ranks], axis=0)
    bias = bias_t.T
    for r in range(NSA_REP):
        qa_ref[:, r * LANES:(r + 1) * LANES] = (
            qr_ref[:, r * LANES:(r + 1) * LANES].astype(F32) + bias).astype(BF16)


def _overlap_t(seq):
    n_cmp = (seq - CMP_LEN) // CMP_STRIDE + 1
    n_slc = seq // SLC_LEN
    assert n_slc == LANES - HEAD_DIM, "selection blocks must fill the bias half of a slab"
    c0 = CMP_STRIDE * np.arange(n_cmp)[:, None]
    s0 = SLC_LEN * np.arange(n_slc)[None, :]
    ov = np.clip(np.minimum(c0 + CMP_LEN, s0 + SLC_LEN) - np.maximum(c0, s0), 0, None) / CMP_LEN
    out = np.zeros((LANES, seq // CMP_STRIDE), np.float32)
    out[HEAD_DIM:, :n_cmp] = ov.T
    return jnp.asarray(out, BF16)


def _cmp_select(qn, qr, cmp_kv, batch, seq):
    T = qn.shape[0]
    nq = seq // TQ_SELECT
    rows = seq // CMP_STRIDE
    gw = NSA_REP * LANES
    qspec = pl.BlockSpec((TQ_SELECT, gw), lambda b, g, i: (b * nq + i, g))
    return pl.pallas_call(
        functools.partial(_cmp_select_kernel, tq=TQ_SELECT), grid=(batch, NSA_GROUPS, nq),
        in_specs=[qspec, qspec,
                  pl.BlockSpec((1, 1, rows, LANES), lambda b, g, i: (b, g, 0, 0)),
                  pl.BlockSpec((1, 1, rows, LANES), lambda b, g, i: (b, NSA_GROUPS + g, 0, 0)),
                  _const_spec((LANES, rows))],
        out_specs=[pl.BlockSpec((TQ_SELECT, NSA_REP * HEAD_DIM), lambda b, g, i: (b * nq + i, g)), qspec],
        out_shape=[jax.ShapeDtypeStruct((T, NSA_HEADS * HEAD_DIM), F32),
                   jax.ShapeDtypeStruct((T, NSA_HEADS * LANES), BF16)],
        compiler_params=_cparams(3), name="nsa_cmp_select",
    )(qn, qr, cmp_kv, cmp_kv, _overlap_t(seq))


def _merge_kernel(x_ref, oc_ref, os_ref, ow_ref, gl_ref, of_ref, om_ref, mg_ref,
                  e_ref, wn_ref, wf_ref, wm_ref, wo_ref, o_ref):
    d = x_ref.shape[1]
    nw = oc_ref.shape[1]
    sg = jax.nn.sigmoid(gl_ref[...])
    sg_hi = sg.astype(BF16)
    sg_lo = (sg - sg_hi.astype(F32)).astype(BF16)
    e = e_ref[...]
    g = _dot(sg_hi, e) + _dot(sg_lo, e)
    o_nsa = g[:, :nw] * oc_ref[...] + g[:, nw:2 * nw] * os_ref[...] + g[:, 2 * nw:] * ow_ref[...]
    mg = mg_ref[...]
    merged = (mg[:, :d] * _dot(o_nsa.astype(BF16), wn_ref[...])
              + mg[:, d:2 * d] * _dot(of_ref[...], wf_ref[...])
              + mg[:, 2 * d:] * _dot(om_ref[...], wm_ref[...]))
    o_ref[...] = x_ref[...] + _dot(merged.astype(BF16), wo_ref[...])


def _gate_expand():
    e = np.zeros((LANES, 3 * NSA_HEADS * HEAD_DIM), np.float32)
    for h in range(NSA_HEADS):
        for c in range(3):
            e[h * 3 + c, c * NSA_HEADS * HEAD_DIM + h * HEAD_DIM:
              c * NSA_HEADS * HEAD_DIM + (h + 1) * HEAD_DIM] = 1.0
    return jnp.asarray(e, BF16)


def _merge(x2, o_cmp, o_slc, o_win, gate_l, o_fox, o_mla, mg, p):
    T, D = x2.shape
    tok = lambda a: pl.BlockSpec((TM, a.shape[1]), lambda i: (i, 0))
    acts = [x2, o_cmp, o_slc, o_win, gate_l, o_fox, o_mla, mg]
    weights = [_gate_expand(), p["w_br_nsa"].astype(BF16), p["w_br_fox"].astype(BF16),
               p["w_br_mla"].astype(BF16), p["w_out"].astype(BF16)]
    return pl.pallas_call(
        _merge_kernel, grid=(T // TM,),
        in_specs=[tok(a) for a in acts] + [_const_spec(w.shape) for w in weights],
        out_specs=pl.BlockSpec((TM, D), lambda i: (i, 0)),
        out_shape=jax.ShapeDtypeStruct((T, D), F32),
        compiler_params=_cparams(1), name="merge_out",
    )(*acts, *weights)


def _ffn_kernel(x_ref, g_ref, wu_ref, wv_ref, cw_ref, cb_ref, wd_ref, fin_ref, o_ref, tail_ref,
                *, seq, n_chunks, final):
    tm = x_ref.shape[0]
    x = x_ref[...]
    hb = _rms(x, g_ref[...]).astype(BF16)

    @pl.when((pl.program_id(0) * tm) % seq == 0)
    def _():
        tail_ref[...] = jnp.zeros(tail_ref.shape, F32)

    cw = wu_ref.shape[1] // n_chunks
    row8 = lax.broadcasted_iota(jnp.int32, (8, cw), 0)
    acc = jnp.zeros(x.shape, F32)
    for c in range(n_chunks):
        sl = slice(c * cw, (c + 1) * cw)
        u = _dot(hb, wu_ref[:, sl])
        v = _dot(hb, wv_ref[:, sl])
        prev = tail_ref[:, sl]
        tail_ref[:, sl] = u[tm - 8:]
        u1 = pltpu.roll(u, 1, 0)
        u2 = pltpu.roll(u, 2, 0)
        h1 = jnp.where(row8 < 1, pltpu.roll(prev, 1, 0), u1[:8])
        h2 = jnp.where(row8 < 2, pltpu.roll(prev, 2, 0), u2[:8])
        u1 = jnp.concatenate([h1, u1[8:]], axis=0)
        u2 = jnp.concatenate([h2, u2[8:]], axis=0)
        uc = cw_ref[0:1, sl] * u2 + cw_ref[1:2, sl] * u1 + cw_ref[2:3, sl] * u + cb_ref[:, sl]
        act = (uc * jax.nn.sigmoid(uc) * v).astype(BF16)
        acc = acc + _dot(act, wd_ref[sl, :])
    y = x + acc
    if final:
        y = _rms(y, fin_ref[...])
    o_ref[...] = y


def _ffn(x2, p, final_norm, seq, final):
    T, D = x2.shape
    w_up = p["w_up"]
    wu, wv = w_up[:, :D_FF].astype(BF16), w_up[:, D_FF:].astype(BF16)
    row = lambda a: a.reshape(1, -1).astype(F32)
    cwp = jnp.pad(p["conv_w"].astype(F32), ((0, 8 - CONV_W), (0, 0)))
    weights = [row(p["ffn_norm"]), wu, wv, cwp, row(p["conv_b"]), p["w_down"].astype(BF16),
               row(final_norm)]
    return pl.pallas_call(
        functools.partial(_ffn_kernel, seq=seq, n_chunks=2, final=final), grid=(T // TM,),
        in_specs=[pl.BlockSpec((TM, D), lambda i: (i, 0))] + [_const_spec(w.shape) for w in weights],
        out_specs=pl.BlockSpec((TM, D), lambda i: (i, 0)),
        out_shape=jax.ShapeDtypeStruct((T, D), F32),
        scratch_shapes=[pltpu.VMEM((8, D_FF), F32)],
        compiler_params=_cparams(1), name="conv_ffn",
    )(x2, *weights)


_LAYER_PARAMS = ("mix_norm", "w_in", "b_forget", "cmp_pe_k", "cmp_w1_k", "cmp_w2_k", "cmp_pe_v",
                 "cmp_w1_v", "cmp_w2_v", "mla_q_norm", "mla_w_uq", "mla_kv_norm", "mla_w_ukv",
                 "w_br_nsa", "w_br_fox", "w_br_mla", "w_out", "ffn_norm", "w_up", "conv_w",
                 "conv_b", "w_down")


def _mixer_layer(h, tabs, p, batch, seq):
    (qn, qr, kvc, ksl, vsl_t, kwin, vwin_t, gate_l,
     fq, fk, fv_t, mq, mk, mv_t, mg) = _inproj(h, tabs, p, seq)
    cmp_kv = _compress(kvc, p, batch, seq)
    o_cmp, qa = _cmp_select(qn, qr, cmp_kv, batch, seq)
    group = (0,) * NSA_REP
    pair = (0, 1)
    o_slc, o_win = _attention([(qa, ksl, vsl_t, False), (qa, kwin, vwin_t, True)], batch, seq, group,
                              F32, "nsa_slc_win_attn")
    o_fox, o_mla = _attention([(fq, fk, fv_t, False), (mq, mk, mv_t, False)], batch, seq, pair, BF16,
                              "fox_mla_attn")
    return _merge(h, o_cmp, o_slc, o_win, gate_l, o_fox, o_mla, mg, p)


def kernel(x, positions, mix_norm, w_in, b_forget, cmp_pe_k, cmp_w1_k, cmp_w2_k, cmp_pe_v, cmp_w1_v,
           cmp_w2_v, mla_q_norm, mla_w_uq, mla_kv_norm, mla_w_ukv, w_br_nsa, w_br_fox, w_br_mla,
           w_out, ffn_norm, w_up, conv_w, conv_b, w_down, final_norm):
    batch, seq, d = x.shape
    stacked = dict(zip(_LAYER_PARAMS, (
        mix_norm, w_in, b_forget, cmp_pe_k, cmp_w1_k, cmp_w2_k, cmp_pe_v, cmp_w1_v, cmp_w2_v,
        mla_q_norm, mla_w_uq, mla_kv_norm, mla_w_ukv, w_br_nsa, w_br_fox, w_br_mla, w_out,
        ffn_norm, w_up, conv_w, conv_b, w_down)))
    depth = w_in.shape[0]
    tabs = _rope_tables(positions)
    h = x.reshape(batch * seq, d)
    for l in range(depth):
        p = {k: v[l] for k, v in stacked.items()}
        h = _mixer_layer(h, tabs, p, batch, seq)
        h = _ffn(h, p, final_norm, seq, final=(l == depth - 1))
    return h.reshape(batch, seq, d)
```

```python
import functools

import numpy as np
import jax
import jax.numpy as jnp
from jax import lax
from jax.experimental import pallas as pl
from jax.experimental.pallas import tpu as pltpu

F32 = jnp.float32
BF16 = jnp.bfloat16

HEAD_DIM = 64
ROPE_THETA = 500000.0
PARTIAL_ROT = HEAD_DIM // 4
NORM_EPS = 1e-6
NSA_HEADS = 8
NSA_GROUPS = 2
NSA_REP = NSA_HEADS // NSA_GROUPS
CMP_LEN = 32
CMP_STRIDE = 16
CMP_HID = 4 * HEAD_DIM
SLC_LEN = 64
SLC_TOPK = 16
WIN = 512
FOX_HEADS = 8
MLA_HEADS = 8
MLA_Q_LORA = 384
MLA_KV_LORA = 256
MLA_NOPE = 64
MLA_ROPE = 32
MLA_V = 64
D_FF = 2816
D_MODEL = 1024
CONV_W = 3

LANES = 128
LOG2E = 1.4426950408889634
NEG = -1e30
SEL_NEG = -1e9
VMEM_LIMIT = 52 * 1024 * 1024

TM = 256
TQ = 1024
TQ_SELECT = 512
TK = 512
QC = 256
LOOP_TILES = 4
SKEW = 2
DECAY_TERMS = 3


def _cparams(n_axes):
    return pltpu.CompilerParams(dimension_semantics=("arbitrary",) * n_axes,
                                vmem_limit_bytes=VMEM_LIMIT)


def _const_spec(shape):
    nd = len(shape)
    return pl.BlockSpec(shape, lambda *_: (0,) * nd, pipeline_mode=pl.Buffered(1))


def _rms(x, g):
    return x * lax.rsqrt(jnp.mean(x * x, axis=-1, keepdims=True) + NORM_EPS) * g


def _dot(a, b):
    return jnp.dot(a, b, preferred_element_type=F32)


def _dot_nt(a, b):
    return lax.dot_general(a, b, (((1,), (1,)), ((), ())), preferred_element_type=F32)


def _rope_tables_kernel(pos_ref, freq_ref, g16_ref, g32_ref, c16_o, s16_o, c32_o, s32_o):
    ang = pos_ref[...].astype(F32) * freq_ref[...]
    cos, sin = jnp.cos(ang), jnp.sin(ang)
    lane = lax.broadcasted_iota(jnp.int32, ang.shape, 1)
    low = lane < PARTIAL_ROT
    high = (lane >= HEAD_DIM) & (lane < HEAD_DIM + PARTIAL_ROT)
    mla = (lane >= MLA_NOPE) & (lane < MLA_NOPE + MLA_ROPE)
    c16_o[...] = jnp.where(low, cos, jnp.where(high, pltpu.roll(cos, HEAD_DIM, 1), 1.0))
    s16_o[...] = jnp.where(low, sin, jnp.where(high, pltpu.roll(sin, HEAD_DIM, 1), 0.0)) * g16_ref[...]
    c32_o[...] = jnp.where(mla, pltpu.roll(cos, MLA_ROPE, 1), 1.0)
    s32_o[...] = jnp.where(mla, pltpu.roll(sin, MLA_ROPE, 1), 0.0) * g32_ref[...]


def _rope_tables(positions):
    T = positions.size
    half16, half32 = PARTIAL_ROT // 2, MLA_ROPE // 2
    inv16 = ROPE_THETA ** (-jnp.arange(half16, dtype=F32) / half16)
    inv32 = ROPE_THETA ** (-jnp.arange(half32, dtype=F32) / half32)
    lane = np.arange(LANES)
    freq = jnp.where(lane < PARTIAL_ROT, inv16[lane % half16],
                     jnp.where((lane >= MLA_ROPE) & (lane < 2 * MLA_ROPE), inv32[lane % half32], 0.0))[None, :]
    d = lane % HEAD_DIM
    g16 = np.where(d < PARTIAL_ROT, np.where(d < half16, -1.0, 1.0), 0.0).astype(np.float32)[None, :]
    r = lane - MLA_NOPE
    g32 = np.where((r >= 0) & (r < MLA_ROPE), np.where(r < half32, -1.0, 1.0), 0.0).astype(np.float32)[None, :]
    pos = jnp.broadcast_to(positions.reshape(T, 1), (T, LANES))
    tt = 1024
    row = pl.BlockSpec((tt, LANES), lambda i: (i, 0))
    vec = _const_spec((1, LANES))
    out = jax.ShapeDtypeStruct((T, LANES), F32)
    return pl.pallas_call(
        _rope_tables_kernel, grid=(T // tt,),
        in_specs=[row, vec, vec, vec], out_specs=[row] * 4, out_shape=[out] * 4,
        compiler_params=_cparams(1), name="rope_tables",
    )(pos, freq, jnp.asarray(g16), jnp.asarray(g32))


def _rot16(x, cos, sin):
    lane = lax.broadcasted_iota(jnp.int32, x.shape, 1)
    first = (lane % HEAD_DIM) < (PARTIAL_ROT // 2)
    sw = jnp.where(first, pltpu.roll(x, LANES - PARTIAL_ROT // 2, 1), pltpu.roll(x, PARTIAL_ROT // 2, 1))
    return x * cos + sw * sin


def _rot32(x, cos, sin):
    lane = lax.broadcasted_iota(jnp.int32, x.shape, 1)
    first = lane < (MLA_NOPE + MLA_ROPE // 2)
    sw = jnp.where(first, pltpu.roll(x, LANES - MLA_ROPE // 2, 1), pltpu.roll(x, MLA_ROPE // 2, 1))
    return x * cos + sw * sin


_FF_LANE = 3 * NSA_HEADS
_IN_COLS = {}
for _name, _n in (("q", NSA_HEADS * HEAD_DIM), ("kvc", 4 * HEAD_DIM), ("ksl", LANES), ("kwin", LANES),
                  ("misc", LANES), ("fq", FOX_HEADS * HEAD_DIM), ("fk", FOX_HEADS * HEAD_DIM),
                  ("lat", MLA_Q_LORA + MLA_KV_LORA), ("mg", 3 * D_MODEL)):
    _IN_COLS[_name] = (sum(n for _, n in _IN_COLS.values()), _n)
_IN_ROWS = {"vsl": (0, LANES), "vwin": (LANES, LANES), "fv": (2 * LANES, FOX_HEADS * HEAD_DIM)}


def _inproj_kernel(x_ref, g_ref, c16_ref, s16_ref, c32_ref, s32_ref,
                   w_ref, wt_ref, bf_ref, tri_ref, place_ref,
                   qnrm_ref, wuq_ref, kvnrm_ref, wuk_ref, wuv_ref,
                   qn_o, qr_o, kvc_o, ksl_o, vsl_o, kwin_o, vwin_o, gate_o,
                   fq_o, fk_o, fv_o, mq_o, mk_o, mv_o, mg_o, fcarry_ref, *, seq):
    tm = x_ref.shape[0]
    seq_start = (pl.program_id(0) * tm) % seq == 0
    hb = _rms(x_ref[...], g_ref[...]).astype(BF16)
    c16, s16 = c16_ref[...], s16_ref[...]
    c32, s32 = c32_ref[...], s32_ref[...]
    lane = lax.broadcasted_iota(jnp.int32, (tm, LANES), 1)
    low = lane < HEAD_DIM

    def proj(name):
        a, n = _IN_COLS[name]
        return _dot(hb, w_ref[:, a:a + n])

    def proj_t(name):
        a, n = _IN_ROWS[name]
        return _dot_nt(wt_ref[a:a + n, :], hb)

    def slabs(z):
        return [z[:, k * LANES:(k + 1) * LANES] for k in range(z.shape[1] // LANES)]

    def split(z, fill=0.0):
        return jnp.where(low, z, fill), jnp.where(low, pltpu.roll(z, HEAD_DIM, 1), fill)

    def store_heads(o_ref, pair_slabs, fill=0.0, extra=None):
        for k, z in enumerate(pair_slabs):
            for e, s in enumerate(split(z, fill)):
                h = 2 * k + e
                if extra is not None:
                    s = s + extra[:, h * LANES:(h + 1) * LANES]
                o_ref[:, h * LANES:(h + 1) * LANES] = s.astype(o_ref.dtype)

    zq = slabs(proj("q"))
    store_heads(qn_o, zq)
    store_heads(qr_o, [_rot16(z, c16, s16) for z in zq])

    for k, z in enumerate(slabs(proj("kvc"))):
        for e, s in enumerate(split(z)):
            kvc_o[2 * k + e] = s[:, :HEAD_DIM]

    tok = (pl.program_id(0) * tm) % seq + lax.broadcasted_iota(jnp.int32, (tm, LANES), 0)
    onehot = jnp.where(lane - HEAD_DIM == tok // SLC_LEN, 1.0, 0.0)
    store_heads(ksl_o, [_rot16(proj("ksl"), c16, s16)], fill=onehot)
    vsl_o[0] = proj_t("vsl").astype(BF16)

    store_heads(kwin_o, [_rot16(proj("kwin"), c16, s16)])
    vwin_o[0] = proj_t("vwin").astype(BF16)

    misc = proj("misc")
    gate_o[...] = misc

    @pl.when(seq_start)
    def _():
        fcarry_ref[...] = jnp.zeros(fcarry_ref.shape, F32)

    zf = misc + bf_ref[...]
    is_f = (lane >= _FF_LANE) & (lane < _FF_LANE + FOX_HEADS)
    logf = jnp.where(is_f, jnp.minimum(zf, 0.0) - jnp.log1p(jnp.exp(-jnp.abs(zf))), 0.0)

    def bf16_terms(v):
        packed = None
        for t in range(DECAY_TERMS):
            part = v.astype(BF16).astype(F32)
            v = v - part
            part = part if t == 0 else pltpu.roll(part, FOX_HEADS * t, 1)
            packed = part if packed is None else packed + part
        return packed.astype(BF16)

    terms = _dot(tri_ref[...], bf16_terms(logf))
    fsum = sum(terms if t == 0 else pltpu.roll(terms, LANES - FOX_HEADS * t, 1)
               for t in range(DECAY_TERMS))
    fsum = jnp.where(is_f, fsum, 0.0) + fcarry_ref[0:1]
    fcarry_ref[...] = jnp.broadcast_to(fsum[tm - 1:tm], fcarry_ref.shape)
    decay = _dot(bf16_terms(jnp.where(is_f, -LOG2E * fsum, 0.0)), place_ref[...])
    store_heads(fk_o, slabs(proj("fk")), extra=decay)
    ones = jnp.where((lane >= HEAD_DIM) & (lane < HEAD_DIM + DECAY_TERMS), 1.0, 0.0)
    store_heads(fq_o, slabs(proj("fq")), fill=ones)
    fv_o[0] = proj_t("fv").astype(BF16)

    lat = proj("lat")
    cq = _rms(lat[:, :MLA_Q_LORA], qnrm_ref[...]).astype(BF16)
    ckv = _rms(lat[:, MLA_Q_LORA:], kvnrm_ref[...]).astype(BF16)
    kpe = jnp.where(low, 0.0, _rot32(misc, c32, s32))
    zuq = _dot(cq, wuq_ref[...])
    for k, z in enumerate(slabs(zuq)):
        mq_o[:, k * LANES:(k + 1) * LANES] = _rot32(z, c32, s32).astype(BF16)
    zuk = _dot(ckv, wuk_ref[...])
    for k, z in enumerate(slabs(zuk)):
        mk_o[:, k * LANES:(k + 1) * LANES] = (z + kpe).astype(BF16)
    mv_o[0] = _dot_nt(wuv_ref[...], ckv).astype(BF16)

    mg_o[...] = jax.nn.sigmoid(proj("mg"))


def _pad_halves(w, first_only=True):
    K, n = w.shape[0], w.shape[1] // HEAD_DIM
    w = w.reshape(K, n, HEAD_DIM)
    other = jnp.zeros_like(w) if first_only else w
    return jnp.concatenate([w, other], axis=-1).reshape(K, n * LANES)


def _inproj(x2, tabs, p, seq):
    T, D = x2.shape
    w_in = p["w_in"]
    scale = HEAD_DIM ** -0.5 * LOG2E
    o = 0

    def take(n):
        nonlocal o
        w = w_in[:, o:o + n]
        o += n
        return w

    nsa_w, kvw = NSA_HEADS * HEAD_DIM, NSA_GROUPS * HEAD_DIM
    w_q = take(nsa_w)
    w_kc, w_vc, w_ks, w_vs, w_kw, w_vw = [take(kvw) for _ in range(6)]
    w_g = take(3 * NSA_HEADS)
    w_fq, w_fk, w_fv = [take(FOX_HEADS * HEAD_DIM) for _ in range(3)]
    w_ff = take(FOX_HEADS)
    w_cq, w_ckv, w_kr = take(MLA_Q_LORA), take(MLA_KV_LORA), take(MLA_ROPE)
    w_mg = take(3 * D)

    bf = lambda w: w.astype(BF16)
    row = lambda a: a.reshape(1, -1).astype(F32)
    w_misc = jnp.concatenate([w_g, w_ff, jnp.zeros((D, MLA_NOPE - _FF_LANE - FOX_HEADS), F32), w_kr,
                              jnp.zeros((D, LANES - MLA_NOPE - MLA_ROPE), F32)], axis=1)
    w_tok = bf(jnp.concatenate([w_q * scale, w_kc, w_vc, w_ks, w_kw, w_misc, w_fq * scale, w_fk,
                                w_cq, w_ckv, w_mg], axis=1))
    w_tr = bf(jnp.concatenate([w_vs, w_vw, w_fv], axis=1).T)
    b_f = jnp.pad(row(p["b_forget"]), ((0, 0), (_FF_LANE, LANES - _FF_LANE - FOX_HEADS)))
    tri = jnp.asarray(np.tril(np.ones((TM, TM), np.float32)), BF16)
    place = np.zeros((LANES, FOX_HEADS * LANES), np.float32)
    for t in range(DECAY_TERMS):
        for h in range(FOX_HEADS):
            place[_FF_LANE + FOX_HEADS * t + h, h * LANES + HEAD_DIM + t] = 1.0
    place = jnp.asarray(place, BF16)
    dq = MLA_NOPE + MLA_ROPE
    wuq = (p["mla_w_uq"] * (dq ** -0.5 * LOG2E)).reshape(MLA_Q_LORA, MLA_HEADS, dq)
    wuq = bf(jnp.pad(wuq, ((0, 0), (0, 0), (0, LANES - dq))).reshape(MLA_Q_LORA, MLA_HEADS * LANES))
    wukv = p["mla_w_ukv"].reshape(MLA_KV_LORA, MLA_HEADS, MLA_NOPE + MLA_V)
    wuk = bf(_pad_halves(wukv[:, :, :MLA_NOPE].reshape(MLA_KV_LORA, MLA_HEADS * MLA_NOPE)))
    wuv = bf(wukv[:, :, MLA_NOPE:].reshape(MLA_KV_LORA, MLA_HEADS * MLA_V).T)
    assert w_tok.shape[1] == sum(n for _, n in _IN_COLS.values())

    weights = [w_tok, w_tr, b_f, tri, place,
               row(p["mla_q_norm"]), wuq, row(p["mla_kv_norm"]), wuk, wuv]
    n = T // TM
    tok = lambda c: pl.BlockSpec((TM, c), lambda i: (i, 0))
    in_specs = ([tok(D), _const_spec((1, D))] + [tok(LANES)] * 4
                + [_const_spec(w.shape) for w in weights])
    tokens, transposed = "tokens", "transposed"
    outs = [
        (tokens, NSA_HEADS * LANES, BF16), (tokens, NSA_HEADS * LANES, BF16), None,
        (tokens, NSA_GROUPS * LANES, BF16), (transposed, NSA_GROUPS * HEAD_DIM, BF16),
        (tokens, NSA_GROUPS * LANES, BF16), (transposed, NSA_GROUPS * HEAD_DIM, BF16), (tokens, LANES, F32),
        (tokens, FOX_HEADS * LANES, BF16), (tokens, FOX_HEADS * LANES, BF16),
        (transposed, FOX_HEADS * HEAD_DIM, BF16),
        (tokens, MLA_HEADS * LANES, BF16), (tokens, MLA_HEADS * LANES, BF16),
        (transposed, MLA_HEADS * MLA_V, BF16), (tokens, 3 * D, F32)]
    out_shape, out_specs = [], []
    for spec in outs:
        if spec is None:
            out_shape.append(jax.ShapeDtypeStruct((4, T, HEAD_DIM), F32))
            out_specs.append(pl.BlockSpec((4, TM, HEAD_DIM), lambda i: (0, i, 0)))
        elif spec[0] == transposed:
            out_shape.append(jax.ShapeDtypeStruct((n, spec[1], TM), spec[2]))
            out_specs.append(pl.BlockSpec((1, spec[1], TM), lambda i: (i, 0, 0)))
        else:
            out_shape.append(jax.ShapeDtypeStruct((T, spec[1]), spec[2]))
            out_specs.append(tok(spec[1]))
    return pl.pallas_call(
        functools.partial(_inproj_kernel, seq=seq), grid=(n,),
        in_specs=in_specs, out_specs=out_specs, out_shape=out_shape,
        scratch_shapes=[pltpu.VMEM((8, LANES), F32)],
        compiler_params=_cparams(1), name="inproj",
    )(x2, row(p["mix_norm"]), *tabs, *weights)


def _attn_kernel(*refs, tq, tk, q_of_stream, window_of_stream, k_of_head, values_by_unit):
    n_streams, n_q = len(q_of_stream), max(q_of_stream) + 1
    q_refs, kv_refs = refs[:n_q], refs[n_q:n_q + 2 * n_streams]
    outs = refs[n_q + 2 * n_streams:n_q + 3 * n_streams]
    qt_ref, m_ref, l_ref, acc_ref = refs[n_q + 3 * n_streams:]
    i = pl.program_id(2)
    hs = len(k_of_head)
    nh = n_streams * hs
    k_of = lambda h: kv_refs[2 * (h // hs)]
    vt_of = lambda h: kv_refs[2 * (h // hs) + 1]
    qt_of = lambda h: q_of_stream[h // hs] * hs + h % hs
    per_q = tq // tk
    tv = kv_refs[1].shape[2]
    v_sub = tk // tv
    for qi in range(n_q):
        for r in range(hs):
            qt_ref[qi * hs + r] = q_refs[qi][:, r * LANES:(r + 1) * LANES].astype(F32).T.astype(BF16)
    dist0 = (i * tq + lax.broadcasted_iota(jnp.int32, (tk, tq), 1)
             - lax.broadcasted_iota(jnp.int32, (tk, tq), 0))

    m_ref[...] = jnp.full(m_ref.shape, NEG, F32)
    l_ref[...] = jnp.zeros(l_ref.shape, F32)
    acc_ref[...] = jnp.zeros(acc_ref.shape, F32)

    def logits(j, lo, n, h, cols):
        r0 = pl.multiple_of(j * tk + lo, tv)
        kk = k_of_head[h % hs]
        return _dot(k_of(h)[pl.ds(r0, n), kk * LANES:(kk + 1) * LANES], qt_ref[qt_of(h), :, cols])

    def softmax(s, h, cols, mask):
        if mask is not None:
            s = jnp.where(mask, s, NEG)
        m_old = m_ref[h, :, cols]
        m_new = jnp.maximum(m_old, jnp.max(s, axis=0, keepdims=True))
        alpha = jnp.exp2(m_old - m_new)
        p = jnp.exp2(s - m_new)
        l_ref[h, :, cols] = alpha * l_ref[h, :, cols] + jnp.sum(p, axis=0, keepdims=True)
        m_ref[h, :, cols] = m_new
        return p.astype(BF16), alpha

    def accumulate(j, lo, n, h, cols, p, alpha):
        if values_by_unit:
            half = pl.ds(pl.multiple_of(pl.program_id(1) * HEAD_DIM, HEAD_DIM), HEAD_DIM)
        else:
            half = slice((h % 2) * HEAD_DIM, (h % 2 + 1) * HEAD_DIM)
        pv = sum(_dot(vt_of(h)[j * v_sub + lo // tv + c, half, :], p[c * tv:(c + 1) * tv])
                 for c in range(n // tv))
        acc_ref[h, :, cols] = alpha * acc_ref[h, :, cols] + pv

    def sweep(tiles):
        work = []
        for heads, j, n_tiles, mask, span in tiles:
            for h in heads:
                for c in range(tq // QC):
                    lo, n, masked = (0, n_tiles * tk, False) if span is None else span(c)
                    if n > 0:
                        work.append((j, lo, n, mask if masked else None, h, slice(c * QC, (c + 1) * QC)))
        s_of, p_of = {}, {}
        for t in range(len(work) + 2 * SKEW):
            if t < len(work):
                j, lo, n, _, h, cols = work[t]
                s_of[t] = logits(j, lo, n, h, cols)
            if 0 <= t - SKEW < len(work):
                _, lo, n, mask, h, cols = work[t - SKEW]
                p_of[t - SKEW] = softmax(s_of.pop(t - SKEW), h, cols,
                                         None if mask is None else mask(slice(lo, lo + n), cols))
            if 0 <= t - 2 * SKEW < len(work):
                j, lo, n, _, h, cols = work[t - 2 * SKEW]
                accumulate(j, lo, n, h, cols, *p_of.pop(t - 2 * SKEW))

    def visible(j, window, valid=None):
        def mask(keys, cols):
            dist = dist0[keys, cols] - j * tk
            if not window:
                return dist >= 0
            if valid is not None:
                dist = dist + jnp.where(valid, 0, WIN)
            return lax.bitcast_convert_type(dist, jnp.uint32) < WIN
        return mask

    assert tk == 2 * QC and WIN % tk == 0 and LOOP_TILES % per_q == 0
    causal = [h for h in range(nh) if not window_of_stream[h // hs]]
    windowed = [h for h in range(nh) if window_of_stream[h // hs]]
    first = i * per_q

    def causal_span(d):
        def span(c):
            hi = min(tk, (c + 1) * QC - d * tk)
            return 0, hi, d * tk + hi > c * QC + 1
        return span

    def window_span(d):
        def span(c):
            lo = max(0, (c * QC - WIN + 1) // tv * tv - d * tk)
            return lo, min(tk, (c + 1) * QC - d * tk) - lo, True
        return span

    one_step = len(causal) * (tq // QC) >= 8

    def unmasked(j0, n):
        tiles = []
        while n > 0:
            k = 2 if (one_step and n >= 2) else 1
            tiles.append((causal, j0, k, None, None))
            j0, n = j0 + k, n - k
        return tiles

    def last_tiles(n_left):
        tiles = []
        if causal:
            tiles += unmasked(first - n_left, n_left)
            tiles += [(causal, first + d, 1, visible(first + d, False), causal_span(d)) for d in range(per_q)]
        if windowed:
            tiles += [(windowed, first + d, 1, visible(first + d, True), window_span(d))
                      for d in reversed(range(per_q))]
            tiles += [(windowed, jnp.maximum(first + d, 0), 1,
                       visible(first + d, True, valid=first + d >= 0), window_span(d))
                      for d in range(-(WIN // tk), 0)]
        return tiles

    if causal:
        def body(jj, c):
            sweep(unmasked(LOOP_TILES * jj, LOOP_TILES))
            return c
        lax.fori_loop(0, first // LOOP_TILES, body, 0)
        for n_left in range(0, LOOP_TILES, per_q):
            @pl.when(first % LOOP_TILES == n_left)
            def _():
                sweep(last_tiles(n_left))
    else:
        sweep(last_tiles(0))

    for h in range(0, nh, 2):
        a = acc_ref[h] * (1.0 / l_ref[h])
        b = acc_ref[h + 1] * (1.0 / l_ref[h + 1])
        o_ref, k = outs[h // hs], (h % hs) // 2
        o_ref[:, k * LANES:(k + 1) * LANES] = jnp.concatenate([a, b], axis=0).T.astype(o_ref.dtype)


def _attention(streams, batch, seq, k_of_head, out_dtype, name):
    qs = []
    for q, _, _, _ in streams:
        if not any(q is x for x in qs):
            qs.append(q)
    q_of_stream = tuple(next(n for n, x in enumerate(qs) if x is q) for q, _, _, _ in streams)
    T = qs[0].shape[0]
    nq = seq // TQ
    hs, nk = len(k_of_head), max(k_of_head) + 1
    nh = hs * len(streams)
    units = qs[0].shape[1] // (hs * LANES)
    by_unit = nk == 1
    assert not by_unit or units * HEAD_DIM == LANES, "GQA value blocks must fill one slab"
    in_specs = [pl.BlockSpec((TQ, hs * LANES), lambda b, u, i: (b * nq + i, u)) for _ in qs]
    args = list(qs)
    for q, k, vt, _ in streams:
        assert q.shape[1] == units * hs * LANES and k.shape[1] == units * nk * LANES
        assert vt.shape == (T // TM, units * (HEAD_DIM if by_unit else LANES), TM)
        in_specs += [pl.BlockSpec((seq, nk * LANES), lambda b, u, i: (b, u)),
                     pl.BlockSpec((seq // TM, LANES, TM),
                                  (lambda b, u, i: (b, 0, 0)) if by_unit else (lambda b, u, i: (b, u, 0)))]
        args += [k, vt]
    return pl.pallas_call(
        functools.partial(_attn_kernel, tq=TQ, tk=TK, q_of_stream=q_of_stream,
                          window_of_stream=tuple(w for _, _, _, w in streams),
                          k_of_head=k_of_head, values_by_unit=by_unit),
        grid=(batch, units, nq), in_specs=in_specs,
        out_specs=[pl.BlockSpec((TQ, hs // 2 * LANES), lambda b, u, i: (b * nq + i, u))] * len(streams),
        out_shape=[jax.ShapeDtypeStruct((T, units * hs // 2 * LANES), out_dtype)] * len(streams),
        scratch_shapes=[pltpu.VMEM((len(qs) * hs, LANES, TQ), BF16), pltpu.VMEM((nh, 1, TQ), F32),
                        pltpu.VMEM((nh, 1, TQ), F32), pltpu.VMEM((nh, HEAD_DIM, TQ), F32)],
        compiler_params=_cparams(3), name=name,
    )(*args)


def _gelu_tanh(x):
    return 0.5 * x * (1.0 + jnp.tanh(np.float32(np.sqrt(2.0 / np.pi)) * (x + 0.044715 * (x * x * x))))


def _compress_kernel(x_ref, pe_ref, w1_ref, w2_ref, o_ref):
    x = x_ref[0]
    half = x.shape[1]
    top = _dot((x + pe_ref[0, :, :half]).astype(BF16), w1_ref[0, :half, :])
    bot = _dot((x + pe_ref[0, :, half:]).astype(BF16), w1_ref[0, half:, :])
    hid = top + pltpu.roll(bot, bot.shape[0] - 1, 0)
    o_ref[0, 0] = _dot(_gelu_tanh(hid).astype(BF16), w2_ref[0]).astype(o_ref.dtype)


def _compress(kvc, p, batch, seq):
    T = kvc.shape[1]
    rows = seq // CMP_STRIDE
    x = kvc.reshape(4, T // CMP_STRIDE, CMP_STRIDE * HEAD_DIM)
    pe = jnp.stack([p["cmp_pe_k"].reshape(1, -1), p["cmp_pe_v"].reshape(1, -1)]).astype(F32)
    w1 = jnp.stack([p["cmp_w1_k"], p["cmp_w1_v"]]).astype(BF16)
    w2k = jnp.pad(p["cmp_w2_k"], ((0, 0), (0, LANES - HEAD_DIM)))
    w2v = jnp.concatenate([p["cmp_w2_v"], p["cmp_w2_v"]], axis=1)
    w2 = jnp.stack([w2k, w2v]).astype(BF16)
    return pl.pallas_call(
        _compress_kernel, grid=(4, batch),
        in_specs=[pl.BlockSpec((1, rows, x.shape[2]), lambda j, b: (j, b, 0)),
                  pl.BlockSpec((1,) + pe.shape[1:], lambda j, b: (j // 2, 0, 0)),
                  pl.BlockSpec((1,) + w1.shape[1:], lambda j, b: (j // 2, 0, 0)),
                  pl.BlockSpec((1,) + w2.shape[1:], lambda j, b: (j // 2, 0, 0))],
        out_specs=pl.BlockSpec((1, 1, rows, LANES), lambda j, b: (b, j, 0, 0)),
        out_shape=jax.ShapeDtypeStruct((batch, 4, rows, LANES), BF16),
        compiler_params=_cparams(2), name="nsa_compress",
    )(x, pe, w1, w2)


def _cmp_select_kernel(qn_ref, qr_ref, kc_ref, vc_ref, ovt_ref, o_ref, qa_ref, *, tq):
    i = pl.program_id(2)
    n_cmp = kc_ref.shape[2]
    kc, vc = kc_ref[0, 0], vc_ref[0, 0]
    t_row = i * tq + lax.broadcasted_iota(jnp.int32, (tq, n_cmp), 0)
    blk_end = CMP_STRIDE * lax.broadcasted_iota(jnp.int32, (tq, n_cmp), 1) + (CMP_LEN - 1)
    vis = blk_end <= t_row
    lane = lax.broadcasted_iota(jnp.int32, (tq, LANES), 1)
    sees_any = t_row[:, :1] >= CMP_LEN - 1
    psum = jnp.zeros((tq, n_cmp), F32)
    outs = []
    for r in range(NSA_REP):
        s = jnp.where(vis, _dot_nt(qn_ref[:, r * LANES:(r + 1) * LANES], kc), NEG)
        m = jnp.max(s, axis=-1, keepdims=True)
        e = jnp.exp2(s - m)
        pn = e * jnp.where(sees_any, 1.0 / jnp.sum(e, axis=-1, keepdims=True), 0.0)
        psum = psum + pn
        outs.append(_dot(pn.astype(BF16), vc))
    for k in range(NSA_REP // 2):
        o_ref[:, k * LANES:(k + 1) * LANES] = jnp.where(lane < HEAD_DIM, outs[2 * k], outs[2 * k + 1])

    p_hi = psum.astype(BF16)
    p_lo = (psum - p_hi.astype(F32)).astype(BF16)
    ovt = ovt_ref[...]
    imp = _dot_nt(ovt, p_hi) + _dot_nt(ovt, p_lo)
    n_slc = LANES - HEAD_DIM
    jb = lax.broadcasted_iota(jnp.int32, (LANES, tq), 0) - HEAD_DIM
    cur = (i * tq + lax.broadcasted_iota(jnp.int32, (LANES, tq), 1)) // SLC_LEN
    forced = (jb == 0) | (jb == cur) | (jb == cur - 1)
    imp = jnp.where(forced, 1e9, jnp.where(jb > cur, -1e9, imp))
    groups = [imp[HEAD_DIM + 8 * v:HEAD_DIM + 8 * (v + 1)] for v in range(n_slc // 8)]
    sub = lax.broadcasted_iota(jnp.int32, (8, tq), 0)
    ranks = [jnp.zeros((8, tq), F32) for _ in groups]
    for a in range(n_slc):
        row = jnp.broadcast_to(groups[a // 8][a % 8:a % 8 + 1], (8, tq))
        for v in range(len(groups)):
            if v > a // 8:
                inc = jnp.where(row >= groups[v], 1.0, 0.0)
            elif v < a // 8:
                inc = jnp.where(row > groups[v], 1.0, 0.0)
            else:
                tie = jnp.where(sub > a % 8, 1.0, 0.0)
                inc = jnp.where(row > groups[v], 1.0, jnp.where(row == groups[v], tie, 0.0))
            ranks[v] = ranks[v] + inc
    bias_t = jnp.concatenate(
        [jnp.zeros((HEAD_DIM, tq), F32)]
        + [jnp.where(rk < SLC_TOPK, 0.0, SEL_NEG) for rk in ranks], axis=0)
    bias = bias_t.T
    for r in range(NSA_REP):
        qa_ref[:, r * LANES:(r + 1) * LANES] = (
            qr_ref[:, r * LANES:(r + 1) * LANES].astype(F32) + bias).astype(BF16)


def _overlap_t(seq):
    n_cmp = (seq - CMP_LEN) // CMP_STRIDE + 1
    n_slc = seq // SLC_LEN
    assert n_slc == LANES - HEAD_DIM, "selection blocks must fill the bias half of a slab"
    c0 = CMP_STRIDE * np.arange(n_cmp)[:, None]
    s0 = SLC_LEN * np.arange(n_slc)[None, :]
    ov = np.clip(np.minimum(c0 + CMP_LEN, s0 + SLC_LEN) - np.maximum(c0, s0), 0, None) / CMP_LEN
    out = np.zeros((LANES, seq // CMP_STRIDE), np.float32)
    out[HEAD_DIM:, :n_cmp] = ov.T
    return jnp.asarray(out, BF16)


def _cmp_select(qn, qr, cmp_kv, batch, seq):
    T = qn.shape[0]
    nq = seq // TQ_SELECT
    rows = seq // CMP_STRIDE
    gw = NSA_REP * LANES
    qspec = pl.BlockSpec((TQ_SELECT, gw), lambda b, g, i: (b * nq + i, g))
    return pl.pallas_call(
        functools.partial(_cmp_select_kernel, tq=TQ_SELECT), grid=(batch, NSA_GROUPS, nq),
        in_specs=[qspec, qspec,
                  pl.BlockSpec((1, 1, rows, LANES), lambda b, g, i: (b, g, 0, 0)),
                  pl.BlockSpec((1, 1, rows, LANES), lambda b, g, i: (b, NSA_GROUPS + g, 0, 0)),
                  _const_spec((LANES, rows))],
        out_specs=[pl.BlockSpec((TQ_SELECT, NSA_REP * HEAD_DIM), lambda b, g, i: (b * nq + i, g)), qspec],
        out_shape=[jax.ShapeDtypeStruct((T, NSA_HEADS * HEAD_DIM), F32),
                   jax.ShapeDtypeStruct((T, NSA_HEADS * LANES), BF16)],
        compiler_params=_cparams(3), name="nsa_cmp_select",
    )(qn, qr, cmp_kv, cmp_kv, _overlap_t(seq))


def _merge_kernel(x_ref, oc_ref, os_ref, ow_ref, gl_ref, of_ref, om_ref, mg_ref,
                  e_ref, wn_ref, wf_ref, wm_ref, wo_ref, o_ref):
    d = x_ref.shape[1]
    nw = oc_ref.shape[1]
    g = _dot(jax.nn.sigmoid(gl_ref[...]).astype(BF16), e_ref[...])
    o_nsa = g[:, :nw] * oc_ref[...] + g[:, nw:2 * nw] * os_ref[...] + g[:, 2 * nw:] * ow_ref[...]
    mg = mg_ref[...]
    merged = (mg[:, :d] * _dot(o_nsa.astype(BF16), wn_ref[...])
              + mg[:, d:2 * d] * _dot(of_ref[...], wf_ref[...])
              + mg[:, 2 * d:] * _dot(om_ref[...], wm_ref[...]))
    o_ref[...] = x_ref[...] + _dot(merged.astype(BF16), wo_ref[...])


def _gate_expand():
    e = np.zeros((LANES, 3 * NSA_HEADS * HEAD_DIM), np.float32)
    for h in range(NSA_HEADS):
        for c in range(3):
            e[h * 3 + c, c * NSA_HEADS * HEAD_DIM + h * HEAD_DIM:
              c * NSA_HEADS * HEAD_DIM + (h + 1) * HEAD_DIM] = 1.0
    return jnp.asarray(e, BF16)


def _merge(x2, o_cmp, o_slc, o_win, gate_l, o_fox, o_mla, mg, p):
    T, D = x2.shape
    tok = lambda a: pl.BlockSpec((TM, a.shape[1]), lambda i: (i, 0))
    acts = [x2, o_cmp, o_slc, o_win, gate_l, o_fox, o_mla, mg]
    weights = [_gate_expand(), p["w_br_nsa"].astype(BF16), p["w_br_fox"].astype(BF16),
               p["w_br_mla"].astype(BF16), p["w_out"].astype(BF16)]
    return pl.pallas_call(
        _merge_kernel, grid=(T // TM,),
        in_specs=[tok(a) for a in acts] + [_const_spec(w.shape) for w in weights],
        out_specs=pl.BlockSpec((TM, D), lambda i: (i, 0)),
        out_shape=jax.ShapeDtypeStruct((T, D), F32),
        compiler_params=_cparams(1), name="merge_out",
    )(*acts, *weights)


def _ffn_kernel(x_ref, g_ref, wu_ref, wv_ref, cw_ref, cb_ref, wd_ref, fin_ref, o_ref, tail_ref,
                *, seq, n_chunks, final):
    tm = x_ref.shape[0]
    x = x_ref[...]
    hb = _rms(x, g_ref[...]).astype(BF16)

    @pl.when((pl.program_id(0) * tm) % seq == 0)
    def _():
        tail_ref[...] = jnp.zeros(tail_ref.shape, F32)

    cw = wu_ref.shape[1] // n_chunks
    row8 = lax.broadcasted_iota(jnp.int32, (8, cw), 0)
    acc = jnp.zeros(x.shape, F32)
    for c in range(n_chunks):
        sl = slice(c * cw, (c + 1) * cw)
        u = _dot(hb, wu_ref[:, sl])
        v = _dot(hb, wv_ref[:, sl])
        prev = tail_ref[:, sl]
        tail_ref[:, sl] = u[tm - 8:]
        u1 = pltpu.roll(u, 1, 0)
        u2 = pltpu.roll(u, 2, 0)
        h1 = jnp.where(row8 < 1, pltpu.roll(prev, 1, 0), u1[:8])
        h2 = jnp.where(row8 < 2, pltpu.roll(prev, 2, 0), u2[:8])
        u1 = jnp.concatenate([h1, u1[8:]], axis=0)
        u2 = jnp.concatenate([h2, u2[8:]], axis=0)
        uc = cw_ref[0:1, sl] * u2 + cw_ref[1:2, sl] * u1 + cw_ref[2:3, sl] * u + cb_ref[:, sl]
        act = (uc * jax.nn.sigmoid(uc) * v).astype(BF16)
        acc = acc + _dot(act, wd_ref[sl, :])
    y = x + acc
    if final:
        y = _rms(y, fin_ref[...])
    o_ref[...] = y


def _ffn(x2, p, final_norm, seq, final):
    T, D = x2.shape
    w_up = p["w_up"]
    wu, wv = w_up[:, :D_FF].astype(BF16), w_up[:, D_FF:].astype(BF16)
    row = lambda a: a.reshape(1, -1).astype(F32)
    cwp = jnp.pad(p["conv_w"].astype(F32), ((0, 8 - CONV_W), (0, 0)))
    weights = [row(p["ffn_norm"]), wu, wv, cwp, row(p["conv_b"]), p["w_down"].astype(BF16),
               row(final_norm)]
    return pl.pallas_call(
        functools.partial(_ffn_kernel, seq=seq, n_chunks=2, final=final), grid=(T // TM,),
        in_specs=[pl.BlockSpec((TM, D), lambda i: (i, 0))] + [_const_spec(w.shape) for w in weights],
        out_specs=pl.BlockSpec((TM, D), lambda i: (i, 0)),
        out_shape=jax.ShapeDtypeStruct((T, D), F32),
        scratch_shapes=[pltpu.VMEM((8, D_FF), F32)],
        compiler_params=_cparams(1), name="conv_ffn",
    )(x2, *weights)


_LAYER_PARAMS = ("mix_norm", "w_in", "b_forget", "cmp_pe_k", "cmp_w1_k", "cmp_w2_k", "cmp_pe_v",
                 "cmp_w1_v", "cmp_w2_v", "mla_q_norm", "mla_w_uq", "mla_kv_norm", "mla_w_ukv",
                 "w_br_nsa", "w_br_fox", "w_br_mla", "w_out", "ffn_norm", "w_up", "conv_w",
                 "conv_b", "w_down")


def _mixer_layer(h, tabs, p, batch, seq):
    (qn, qr, kvc, ksl, vsl_t, kwin, vwin_t, gate_l,
     fq, fk, fv_t, mq, mk, mv_t, mg) = _inproj(h, tabs, p, seq)
    cmp_kv = _compress(kvc, p, batch, seq)
    o_cmp, qa = _cmp_select(qn, qr, cmp_kv, batch, seq)
    group = (0,) * NSA_REP
    pair = (0, 1)
    o_slc, o_win = _attention([(qa, ksl, vsl_t, False), (qa, kwin, vwin_t, True)], batch, seq, group,
                              F32, "nsa_slc_win_attn")
    o_fox, o_mla = _attention([(fq, fk, fv_t, False), (mq, mk, mv_t, False)], batch, seq, pair, BF16,
                              "fox_mla_attn")
    return _merge(h, o_cmp, o_slc, o_win, gate_l, o_fox, o_mla, mg, p)


def kernel(x, positions, mix_norm, w_in, b_forget, cmp_pe_k, cmp_w1_k, cmp_w2_k, cmp_pe_v, cmp_w1_v,
           cmp_w2_v, mla_q_norm, mla_w_uq, mla_kv_norm, mla_w_ukv, w_br_nsa, w_br_fox, w_br_mla,
           w_out, ffn_norm, w_up, conv_w, conv_b, w_down, final_norm):
    batch, seq, d = x.shape
    stacked = dict(zip(_LAYER_PARAMS, (
        mix_norm, w_in, b_forget, cmp_pe_k, cmp_w1_k, cmp_w2_k, cmp_pe_v, cmp_w1_v, cmp_w2_v,
        mla_q_norm, mla_w_uq, mla_kv_norm, mla_w_ukv, w_br_nsa, w_br_fox, w_br_mla, w_out,
        ffn_norm, w_up, conv_w, conv_b, w_down)))
    depth = w_in.shape[0]
    tabs = _rope_tables(positions)
    h = x.reshape(batch * seq, d)
    for l in range(depth):
        p = {k: v[l] for k, v in stacked.items()}
        h = _mixer_layer(h, tabs, p, batch, seq)
        h = _ffn(h, p, final_norm, seq, final=(l == depth - 1))
    return h.reshape(batch, seq, d)
```

```python
import functools

import numpy as np
import jax
import jax.numpy as jnp
from jax import lax
from jax.experimental import pallas as pl
from jax.experimental.pallas import tpu as pltpu

F32 = jnp.float32
BF16 = jnp.bfloat16

HEAD_DIM = 64
ROPE_THETA = 500000.0
PARTIAL_ROT = HEAD_DIM // 4
NORM_EPS = 1e-6
NSA_HEADS = 8
NSA_GROUPS = 2
NSA_REP = NSA_HEADS // NSA_GROUPS
CMP_LEN = 32
CMP_STRIDE = 16
SLC_LEN = 64
SLC_TOPK = 16
WIN = 512
FOX_HEADS = 8
MLA_HEADS = 8
MLA_Q_LORA = 384
MLA_KV_LORA = 256
MLA_NOPE = 64
MLA_ROPE = 32
MLA_V = 64
D_FF = 2816
D_MODEL = 1024
CONV_W = 3

LANES = 128
LOG2E = 1.4426950408889634
NEG = -1e30
SEL_NEG = -1e9
VMEM_LIMIT = 52 * 1024 * 1024

TM = 256
TQ = 1024
TQ_SELECT = 512
TK = 512
QC = 256
LOOP_TILES = 4
SKEW = 2
DECAY_TERMS = 3


def _cparams(n_axes):
    return pltpu.CompilerParams(dimension_semantics=("arbitrary",) * n_axes,
                                vmem_limit_bytes=VMEM_LIMIT)


def _const_spec(shape):
    nd = len(shape)
    return pl.BlockSpec(shape, lambda *_: (0,) * nd, pipeline_mode=pl.Buffered(1))


def _rms(x, g):
    return x * lax.rsqrt(jnp.mean(x * x, axis=-1, keepdims=True) + NORM_EPS) * g


def _dot(a, b):
    return jnp.dot(a, b, preferred_element_type=F32)


def _dot_nt(a, b):
    return lax.dot_general(a, b, (((1,), (1,)), ((), ())), preferred_element_type=F32)


def _rope_tables_kernel(pos_ref, freq_ref, g16_ref, g32_ref, c16_o, s16_o, c32_o, s32_o):
    ang = pos_ref[...].astype(F32) * freq_ref[...]
    cos, sin = jnp.cos(ang), jnp.sin(ang)
    lane = lax.broadcasted_iota(jnp.int32, ang.shape, 1)
    low = lane < PARTIAL_ROT
    high = (lane >= HEAD_DIM) & (lane < HEAD_DIM + PARTIAL_ROT)
    mla = (lane >= MLA_NOPE) & (lane < MLA_NOPE + MLA_ROPE)
    c16_o[...] = jnp.where(low, cos, jnp.where(high, pltpu.roll(cos, HEAD_DIM, 1), 1.0))
    s16_o[...] = jnp.where(low, sin, jnp.where(high, pltpu.roll(sin, HEAD_DIM, 1), 0.0)) * g16_ref[...]
    c32_o[...] = jnp.where(mla, pltpu.roll(cos, MLA_ROPE, 1), 1.0)
    s32_o[...] = jnp.where(mla, pltpu.roll(sin, MLA_ROPE, 1), 0.0) * g32_ref[...]


def _rope_tables(positions):
    T = positions.size
    half16, half32 = PARTIAL_ROT // 2, MLA_ROPE // 2
    inv16 = ROPE_THETA ** (-jnp.arange(half16, dtype=F32) / half16)
    inv32 = ROPE_THETA ** (-jnp.arange(half32, dtype=F32) / half32)
    lane = np.arange(LANES)
    freq = jnp.where(lane < PARTIAL_ROT, inv16[lane % half16],
                     jnp.where((lane >= MLA_ROPE) & (lane < 2 * MLA_ROPE), inv32[lane % half32], 0.0))[None, :]
    d = lane % HEAD_DIM
    g16 = np.where(d < PARTIAL_ROT, np.where(d < half16, -1.0, 1.0), 0.0).astype(np.float32)[None, :]
    r = lane - MLA_NOPE
    g32 = np.where((r >= 0) & (r < MLA_ROPE), np.where(r < half32, -1.0, 1.0), 0.0).astype(np.float32)[None, :]
    pos = jnp.broadcast_to(positions.reshape(T, 1), (T, LANES))
    tt = 1024
    row = pl.BlockSpec((tt, LANES), lambda i: (i, 0))
    vec = _const_spec((1, LANES))
    out = jax.ShapeDtypeStruct((T, LANES), F32)
    return pl.pallas_call(
        _rope_tables_kernel, grid=(T // tt,),
        in_specs=[row, vec, vec, vec], out_specs=[row] * 4, out_shape=[out] * 4,
        compiler_params=_cparams(1), name="rope_tables",
    )(pos, freq, jnp.asarray(g16), jnp.asarray(g32))


def _rot16(x, cos, sin):
    lane = lax.broadcasted_iota(jnp.int32, x.shape, 1)
    first = (lane % HEAD_DIM) < (PARTIAL_ROT // 2)
    sw = jnp.where(first, pltpu.roll(x, LANES - PARTIAL_ROT // 2, 1), pltpu.roll(x, PARTIAL_ROT // 2, 1))
    return x * cos + sw * sin


def _rot32(x, cos, sin):
    lane = lax.broadcasted_iota(jnp.int32, x.shape, 1)
    first = lane < (MLA_NOPE + MLA_ROPE // 2)
    sw = jnp.where(first, pltpu.roll(x, LANES - MLA_ROPE // 2, 1), pltpu.roll(x, MLA_ROPE // 2, 1))
    return x * cos + sw * sin


_FF_LANE = 3 * NSA_HEADS
_IN_COLS = {}
for _name, _n in (("q", NSA_HEADS * HEAD_DIM), ("kvc", 4 * HEAD_DIM), ("ksl", LANES), ("kwin", LANES),
                  ("misc", LANES), ("fq", FOX_HEADS * HEAD_DIM), ("fk", FOX_HEADS * HEAD_DIM),
                  ("lat", MLA_Q_LORA + MLA_KV_LORA), ("mg", 3 * D_MODEL)):
    _IN_COLS[_name] = (sum(n for _, n in _IN_COLS.values()), _n)
_IN_ROWS = {"vsl": (0, LANES), "vwin": (LANES, LANES), "fv": (2 * LANES, FOX_HEADS * HEAD_DIM)}


def _inproj_kernel(x_ref, g_ref, c16_ref, s16_ref, c32_ref, s32_ref,
                   w_ref, wt_ref, bf_ref, tri_ref, place_ref,
                   qnrm_ref, wuq_ref, kvnrm_ref, wuk_ref, wuv_ref,
                   qn_o, qr_o, kvc_o, ksl_o, vsl_o, kwin_o, vwin_o, gate_o,
                   fq_o, fk_o, fv_o, mq_o, mk_o, mv_o, mg_o, fcarry_ref, *, seq):
    tm = x_ref.shape[0]
    seq_start = (pl.program_id(0) * tm) % seq == 0
    hb = _rms(x_ref[...], g_ref[...]).astype(BF16)
    c16, s16 = c16_ref[...], s16_ref[...]
    c32, s32 = c32_ref[...], s32_ref[...]
    lane = lax.broadcasted_iota(jnp.int32, (tm, LANES), 1)
    low = lane < HEAD_DIM

    def proj(name):
        a, n = _IN_COLS[name]
        return _dot(hb, w_ref[:, a:a + n])

    def proj_t(name):
        a, n = _IN_ROWS[name]
        return _dot_nt(wt_ref[a:a + n, :], hb)

    def slabs(z):
        return [z[:, k * LANES:(k + 1) * LANES] for k in range(z.shape[1] // LANES)]

    def split(z, fill=0.0):
        return jnp.where(low, z, fill), jnp.where(low, pltpu.roll(z, HEAD_DIM, 1), fill)

    def store_heads(o_ref, pair_slabs, fill=0.0, extra=None):
        for k, z in enumerate(pair_slabs):
            for e, s in enumerate(split(z, fill)):
                h = 2 * k + e
                if extra is not None:
                    s = s + extra[:, h * LANES:(h + 1) * LANES]
                o_ref[:, h * LANES:(h + 1) * LANES] = s.astype(o_ref.dtype)

    misc = proj("misc")
    gate_o[...] = misc

    @pl.when(seq_start)
    def _():
        fcarry_ref[...] = jnp.zeros(fcarry_ref.shape, F32)

    def bf16_terms(v):
        packed = None
        for t in range(DECAY_TERMS):
            part = v.astype(BF16).astype(F32)
            v = v - part
            part = part if t == 0 else pltpu.roll(part, FOX_HEADS * t, 1)
            packed = part if packed is None else packed + part
        return packed.astype(BF16)

    zf = misc + bf_ref[...]
    is_f = (lane >= _FF_LANE) & (lane < _FF_LANE + FOX_HEADS)
    logf_terms = bf16_terms(
        jnp.where(is_f, jnp.minimum(zf, 0.0) - jnp.log1p(jnp.exp(-jnp.abs(zf))), 0.0))

    zq = slabs(proj("q"))
    store_heads(qn_o, zq)
    store_heads(qr_o, [_rot16(z, c16, s16) for z in zq])

    terms = _dot(tri_ref[...], logf_terms)
    fsum = sum(terms if t == 0 else pltpu.roll(terms, LANES - FOX_HEADS * t, 1)
               for t in range(DECAY_TERMS))
    fsum = jnp.where(is_f, fsum, 0.0) + fcarry_ref[0:1]
    fcarry_ref[...] = jnp.broadcast_to(fsum[tm - 1:tm], fcarry_ref.shape)
    decay_terms = bf16_terms(jnp.where(is_f, -LOG2E * fsum, 0.0))

    for k, z in enumerate(slabs(proj("kvc"))):
        for e, s in enumerate(split(z)):
            kvc_o[2 * k + e] = s[:, :HEAD_DIM]

    tok = (pl.program_id(0) * tm) % seq + lax.broadcasted_iota(jnp.int32, (tm, LANES), 0)
    onehot = jnp.where(lane - HEAD_DIM == tok // SLC_LEN, 1.0, 0.0)
    store_heads(ksl_o, [_rot16(proj("ksl"), c16, s16)], fill=onehot)
    vsl_o[0] = proj_t("vsl").astype(BF16)

    store_heads(kwin_o, [_rot16(proj("kwin"), c16, s16)])
    vwin_o[0] = proj_t("vwin").astype(BF16)

    ones = jnp.where((lane >= HEAD_DIM) & (lane < HEAD_DIM + DECAY_TERMS), 1.0, 0.0)
    store_heads(fq_o, slabs(proj("fq")), fill=ones)
    fv_o[0] = proj_t("fv").astype(BF16)
    decay = _dot(decay_terms, place_ref[...])

    lat = proj("lat")
    cq = _rms(lat[:, :MLA_Q_LORA], qnrm_ref[...]).astype(BF16)
    ckv = _rms(lat[:, MLA_Q_LORA:], kvnrm_ref[...]).astype(BF16)
    kpe = jnp.where(low, 0.0, _rot32(misc, c32, s32))
    store_heads(fk_o, slabs(proj("fk")), extra=decay)
    zuq = _dot(cq, wuq_ref[...])
    for k, z in enumerate(slabs(zuq)):
        mq_o[:, k * LANES:(k + 1) * LANES] = _rot32(z, c32, s32).astype(BF16)
    zuk = _dot(ckv, wuk_ref[...])
    for k, z in enumerate(slabs(zuk)):
        mk_o[:, k * LANES:(k + 1) * LANES] = (z + kpe).astype(BF16)
    mv_o[0] = _dot_nt(wuv_ref[...], ckv).astype(BF16)

    mg_o[...] = jax.nn.sigmoid(proj("mg"))


def _pad_halves(w, first_only=True):
    K, n = w.shape[0], w.shape[1] // HEAD_DIM
    w = w.reshape(K, n, HEAD_DIM)
    other = jnp.zeros_like(w) if first_only else w
    return jnp.concatenate([w, other], axis=-1).reshape(K, n * LANES)


def _inproj(x2, tabs, p, seq):
    T, D = x2.shape
    w_in = p["w_in"]
    scale = HEAD_DIM ** -0.5 * LOG2E
    o = 0

    def take(n):
        nonlocal o
        w = w_in[:, o:o + n]
        o += n
        return w

    nsa_w, kvw = NSA_HEADS * HEAD_DIM, NSA_GROUPS * HEAD_DIM
    w_q = take(nsa_w)
    w_kc, w_vc, w_ks, w_vs, w_kw, w_vw = [take(kvw) for _ in range(6)]
    w_g = take(3 * NSA_HEADS)
    w_fq, w_fk, w_fv = [take(FOX_HEADS * HEAD_DIM) for _ in range(3)]
    w_ff = take(FOX_HEADS)
    w_cq, w_ckv, w_kr = take(MLA_Q_LORA), take(MLA_KV_LORA), take(MLA_ROPE)
    w_mg = take(3 * D)

    bf = lambda w: w.astype(BF16)
    row = lambda a: a.reshape(1, -1).astype(F32)
    w_misc = jnp.concatenate([w_g, w_ff, jnp.zeros((D, MLA_NOPE - _FF_LANE - FOX_HEADS), F32), w_kr,
                              jnp.zeros((D, LANES - MLA_NOPE - MLA_ROPE), F32)], axis=1)
    w_tok = bf(jnp.concatenate([w_q * scale, w_kc, w_vc, w_ks, w_kw, w_misc, w_fq * scale, w_fk,
                                w_cq, w_ckv, w_mg], axis=1))
    w_tr = bf(jnp.concatenate([w_vs, w_vw, w_fv], axis=1).T)
    b_f = jnp.pad(row(p["b_forget"]), ((0, 0), (_FF_LANE, LANES - _FF_LANE - FOX_HEADS)))
    tri = jnp.asarray(np.tril(np.ones((TM, TM), np.float32)), BF16)
    place = np.zeros((LANES, FOX_HEADS * LANES), np.float32)
    for t in range(DECAY_TERMS):
        for h in range(FOX_HEADS):
            place[_FF_LANE + FOX_HEADS * t + h, h * LANES + HEAD_DIM + t] = 1.0
    place = jnp.asarray(place, BF16)
    dq = MLA_NOPE + MLA_ROPE
    wuq = (p["mla_w_uq"] * (dq ** -0.5 * LOG2E)).reshape(MLA_Q_LORA, MLA_HEADS, dq)
    wuq = bf(jnp.pad(wuq, ((0, 0), (0, 0), (0, LANES - dq))).reshape(MLA_Q_LORA, MLA_HEADS * LANES))
    wukv = p["mla_w_ukv"].reshape(MLA_KV_LORA, MLA_HEADS, MLA_NOPE + MLA_V)
    wuk = bf(_pad_halves(wukv[:, :, :MLA_NOPE].reshape(MLA_KV_LORA, MLA_HEADS * MLA_NOPE)))
    wuv = bf(wukv[:, :, MLA_NOPE:].reshape(MLA_KV_LORA, MLA_HEADS * MLA_V).T)
    assert w_tok.shape[1] == sum(n for _, n in _IN_COLS.values())

    weights = [w_tok, w_tr, b_f, tri, place,
               row(p["mla_q_norm"]), wuq, row(p["mla_kv_norm"]), wuk, wuv]
    n = T // TM
    tok = lambda c: pl.BlockSpec((TM, c), lambda i: (i, 0))
    in_specs = ([tok(D), _const_spec((1, D))] + [tok(LANES)] * 4
                + [_const_spec(w.shape) for w in weights])
    tokens, transposed = "tokens", "transposed"
    outs = [
        (tokens, NSA_HEADS * LANES, BF16), (tokens, NSA_HEADS * LANES, BF16), None,
        (tokens, NSA_GROUPS * LANES, BF16), (transposed, NSA_GROUPS * HEAD_DIM, BF16),
        (tokens, NSA_GROUPS * LANES, BF16), (transposed, NSA_GROUPS * HEAD_DIM, BF16), (tokens, LANES, F32),
        (tokens, FOX_HEADS * LANES, BF16), (tokens, FOX_HEADS * LANES, BF16),
        (transposed, FOX_HEADS * HEAD_DIM, BF16),
        (tokens, MLA_HEADS * LANES, BF16), (tokens, MLA_HEADS * LANES, BF16),
        (transposed, MLA_HEADS * MLA_V, BF16), (tokens, 3 * D, F32)]
    out_shape, out_specs = [], []
    for spec in outs:
        if spec is None:
            out_shape.append(jax.ShapeDtypeStruct((4, T, HEAD_DIM), F32))
            out_specs.append(pl.BlockSpec((4, TM, HEAD_DIM), lambda i: (0, i, 0)))
        elif spec[0] == transposed:
            out_shape.append(jax.ShapeDtypeStruct((n, spec[1], TM), spec[2]))
            out_specs.append(pl.BlockSpec((1, spec[1], TM), lambda i: (i, 0, 0)))
        else:
            out_shape.append(jax.ShapeDtypeStruct((T, spec[1]), spec[2]))
            out_specs.append(tok(spec[1]))
    return pl.pallas_call(
        functools.partial(_inproj_kernel, seq=seq), grid=(n,),
        in_specs=in_specs, out_specs=out_specs, out_shape=out_shape,
        scratch_shapes=[pltpu.VMEM((8, LANES), F32)],
        compiler_params=_cparams(1), name="inproj",
    )(x2, row(p["mix_norm"]), *tabs, *weights)


def _attn_kernel(*refs, tq, tk, q_of_stream, window_of_stream, k_of_head, values_by_unit):
    n_streams, n_q = len(q_of_stream), max(q_of_stream) + 1
    q_refs, kv_refs = refs[:n_q], refs[n_q:n_q + 2 * n_streams]
    outs = refs[n_q + 2 * n_streams:n_q + 3 * n_streams]
    qt_ref, m_ref, l_ref, acc_ref = refs[n_q + 3 * n_streams:]
    i = pl.program_id(2)
    hs = len(k_of_head)
    nh = n_streams * hs
    k_of = lambda h: kv_refs[2 * (h // hs)]
    vt_of = lambda h: kv_refs[2 * (h // hs) + 1]
    qt_of = lambda h: q_of_stream[h // hs] * hs + h % hs
    per_q = tq // tk
    tv = kv_refs[1].shape[2]
    v_sub = tk // tv
    for qi in range(n_q):
        for r in range(hs):
            qt_ref[qi * hs + r] = q_refs[qi][:, r * LANES:(r + 1) * LANES].astype(F32).T.astype(BF16)
    dist0 = (i * tq + lax.broadcasted_iota(jnp.int32, (tk, tq), 1)
             - lax.broadcasted_iota(jnp.int32, (tk, tq), 0))

    m_ref[...] = jnp.full(m_ref.shape, NEG, F32)
    l_ref[...] = jnp.zeros(l_ref.shape, F32)
    acc_ref[...] = jnp.zeros(acc_ref.shape, F32)

    def logits(j, lo, n, h, cols):
        r0 = pl.multiple_of(j * tk + lo, tv)
        kk = k_of_head[h % hs]
        return _dot(k_of(h)[pl.ds(r0, n), kk * LANES:(kk + 1) * LANES], qt_ref[qt_of(h), :, cols])

    def softmax(s, h, cols, mask):
        if mask is not None:
            s = jnp.where(mask, s, NEG)
        m_old = m_ref[h, :, cols]
        m_new = jnp.maximum(m_old, jnp.max(s, axis=0, keepdims=True))
        alpha = jnp.exp2(m_old - m_new)
        p = jnp.exp2(s - m_new)
        l_ref[h, :, cols] = alpha * l_ref[h, :, cols] + jnp.sum(p, axis=0, keepdims=True)
        m_ref[h, :, cols] = m_new
        return p.astype(BF16), alpha

    def accumulate(j, lo, n, h, cols, p, alpha):
        if values_by_unit:
            half = pl.ds(pl.multiple_of(pl.program_id(1) * HEAD_DIM, HEAD_DIM), HEAD_DIM)
        else:
            half = slice((h % 2) * HEAD_DIM, (h % 2 + 1) * HEAD_DIM)
        pv = sum(_dot(vt_of(h)[j * v_sub + lo // tv + c, half, :], p[c * tv:(c + 1) * tv])
                 for c in range(n // tv))
        acc_ref[h, :, cols] = alpha * acc_ref[h, :, cols] + pv

    def sweep(tiles):
        work = []
        for heads, j, n_tiles, mask, span in tiles:
            for h in heads:
                for c in range(tq // QC):
                    lo, n, masked = (0, n_tiles * tk, False) if span is None else span(c)
                    if n > 0:
                        work.append((j, lo, n, mask if masked else None, h, slice(c * QC, (c + 1) * QC)))
        s_of, p_of = {}, {}
        for t in range(len(work) + 2 * SKEW):
            if t < len(work):
                j, lo, n, _, h, cols = work[t]
                s_of[t] = logits(j, lo, n, h, cols)
            if 0 <= t - SKEW < len(work):
                _, lo, n, mask, h, cols = work[t - SKEW]
                p_of[t - SKEW] = softmax(s_of.pop(t - SKEW), h, cols,
                                         None if mask is None else mask(slice(lo, lo + n), cols))
            if 0 <= t - 2 * SKEW < len(work):
                j, lo, n, _, h, cols = work[t - 2 * SKEW]
                accumulate(j, lo, n, h, cols, *p_of.pop(t - 2 * SKEW))

    def visible(j, window, valid=None):
        def mask(keys, cols):
            dist = dist0[keys, cols] - j * tk
            if not window:
                return dist >= 0
            if valid is not None:
                dist = dist + jnp.where(valid, 0, WIN)
            return lax.bitcast_convert_type(dist, jnp.uint32) < WIN
        return mask

    assert tk == 2 * QC and WIN % tk == 0 and LOOP_TILES % per_q == 0
    causal = [h for h in range(nh) if not window_of_stream[h // hs]]
    windowed = [h for h in range(nh) if window_of_stream[h // hs]]
    first = i * per_q

    def causal_span(d):
        def span(c):
            hi = min(tk, (c + 1) * QC - d * tk)
            return 0, hi, d * tk + hi > c * QC + 1
        return span

    def window_span(d):
        def span(c):
            lo = max(0, (c * QC - WIN + 1) // tv * tv - d * tk)
            return lo, min(tk, (c + 1) * QC - d * tk) - lo, True
        return span

    one_step = len(causal) * (tq // QC) >= 8

    def unmasked(j0, n):
        tiles = []
        while n > 0:
            k = 2 if (one_step and n >= 2) else 1
            tiles.append((causal, j0, k, None, None))
            j0, n = j0 + k, n - k
        return tiles

    def last_tiles(n_left):
        tiles = []
        if causal:
            tiles += unmasked(first - n_left, n_left)
            tiles += [(causal, first + d, 1, visible(first + d, False), causal_span(d)) for d in range(per_q)]
        if windowed:
            tiles += [(windowed, first + d, 1, visible(first + d, True), window_span(d))
                      for d in reversed(range(per_q))]
            tiles += [(windowed, jnp.maximum(first + d, 0), 1,
                       visible(first + d, True, valid=first + d >= 0), window_span(d))
                      for d in range(-(WIN // tk), 0)]
        return tiles

    if causal:
        def body(jj, c):
            sweep(unmasked(LOOP_TILES * jj, LOOP_TILES))
            return c
        lax.fori_loop(0, first // LOOP_TILES, body, 0)
        for n_left in range(0, LOOP_TILES, per_q):
            @pl.when(first % LOOP_TILES == n_left)
            def _():
                sweep(last_tiles(n_left))
    else:
        sweep(last_tiles(0))

    for h in range(0, nh, 2):
        a = acc_ref[h] * (1.0 / l_ref[h])
        b = acc_ref[h + 1] * (1.0 / l_ref[h + 1])
        o_ref, k = outs[h // hs], (h % hs) // 2
        o_ref[:, k * LANES:(k + 1) * LANES] = jnp.concatenate([a, b], axis=0).T.astype(o_ref.dtype)


def _attention(streams, batch, seq, k_of_head, out_dtype, name):
    qs = []
    for q, _, _, _ in streams:
        if not any(q is x for x in qs):
            qs.append(q)
    q_of_stream = tuple(next(n for n, x in enumerate(qs) if x is q) for q, _, _, _ in streams)
    T = qs[0].shape[0]
    nq = seq // TQ
    hs, nk = len(k_of_head), max(k_of_head) + 1
    nh = hs * len(streams)
    units = qs[0].shape[1] // (hs * LANES)
    by_unit = nk == 1
    assert not by_unit or units * HEAD_DIM == LANES, "GQA value blocks must fill one slab"
    in_specs = [pl.BlockSpec((TQ, hs * LANES), lambda b, u, i: (b * nq + i, u)) for _ in qs]
    args = list(qs)
    for q, k, vt, _ in streams:
        assert q.shape[1] == units * hs * LANES and k.shape[1] == units * nk * LANES
        assert vt.shape == (T // TM, units * (HEAD_DIM if by_unit else LANES), TM)
        in_specs += [pl.BlockSpec((seq, nk * LANES), lambda b, u, i: (b, u)),
                     pl.BlockSpec((seq // TM, LANES, TM),
                                  (lambda b, u, i: (b, 0, 0)) if by_unit else (lambda b, u, i: (b, u, 0)))]
        args += [k, vt]
    return pl.pallas_call(
        functools.partial(_attn_kernel, tq=TQ, tk=TK, q_of_stream=q_of_stream,
                          window_of_stream=tuple(w for _, _, _, w in streams),
                          k_of_head=k_of_head, values_by_unit=by_unit),
        grid=(batch, units, nq), in_specs=in_specs,
        out_specs=[pl.BlockSpec((TQ, hs // 2 * LANES), lambda b, u, i: (b * nq + i, u))] * len(streams),
        out_shape=[jax.ShapeDtypeStruct((T, units * hs // 2 * LANES), out_dtype)] * len(streams),
        scratch_shapes=[pltpu.VMEM((len(qs) * hs, LANES, TQ), BF16), pltpu.VMEM((nh, 1, TQ), F32),
                        pltpu.VMEM((nh, 1, TQ), F32), pltpu.VMEM((nh, HEAD_DIM, TQ), F32)],
        compiler_params=_cparams(3), name=name,
    )(*args)


def _gelu_tanh(x):
    return 0.5 * x * (1.0 + jnp.tanh(np.float32(np.sqrt(2.0 / np.pi)) * (x + 0.044715 * (x * x * x))))


def _compress_kernel(x_ref, pe_ref, w1_ref, w2_ref, o_ref):
    x = x_ref[0]
    half = x.shape[1]
    top = _dot((x + pe_ref[0, :, :half]).astype(BF16), w1_ref[0, :half, :])
    bot = _dot((x + pe_ref[0, :, half:]).astype(BF16), w1_ref[0, half:, :])
    hid = top + pltpu.roll(bot, bot.shape[0] - 1, 0)
    o_ref[0, 0] = _dot(_gelu_tanh(hid).astype(BF16), w2_ref[0]).astype(o_ref.dtype)


def _compress(kvc, p, batch, seq):
    T = kvc.shape[1]
    rows = seq // CMP_STRIDE
    x = kvc.reshape(4, T // CMP_STRIDE, CMP_STRIDE * HEAD_DIM)
    pe = jnp.stack([p["cmp_pe_k"].reshape(1, -1), p["cmp_pe_v"].reshape(1, -1)]).astype(F32)
    w1 = jnp.stack([p["cmp_w1_k"], p["cmp_w1_v"]]).astype(BF16)
    w2k = jnp.pad(p["cmp_w2_k"], ((0, 0), (0, LANES - HEAD_DIM)))
    w2v = jnp.concatenate([p["cmp_w2_v"], p["cmp_w2_v"]], axis=1)
    w2 = jnp.stack([w2k, w2v]).astype(BF16)
    return pl.pallas_call(
        _compress_kernel, grid=(4, batch),
        in_specs=[pl.BlockSpec((1, rows, x.shape[2]), lambda j, b: (j, b, 0)),
                  pl.BlockSpec((1,) + pe.shape[1:], lambda j, b: (j // 2, 0, 0)),
                  pl.BlockSpec((1,) + w1.shape[1:], lambda j, b: (j // 2, 0, 0)),
                  pl.BlockSpec((1,) + w2.shape[1:], lambda j, b: (j // 2, 0, 0))],
        out_specs=pl.BlockSpec((1, 1, rows, LANES), lambda j, b: (b, j, 0, 0)),
        out_shape=jax.ShapeDtypeStruct((batch, 4, rows, LANES), BF16),
        compiler_params=_cparams(2), name="nsa_compress",
    )(x, pe, w1, w2)


def _cmp_select_kernel(qn_ref, qr_ref, kc_ref, vc_ref, ovt_ref, o_ref, qa_ref, *, tq):
    i = pl.program_id(2)
    n_cmp = kc_ref.shape[2]
    kc, vc = kc_ref[0, 0], vc_ref[0, 0]
    t_row = i * tq + lax.broadcasted_iota(jnp.int32, (tq, n_cmp), 0)
    blk_end = CMP_STRIDE * lax.broadcasted_iota(jnp.int32, (tq, n_cmp), 1) + (CMP_LEN - 1)
    vis = blk_end <= t_row
    lane = lax.broadcasted_iota(jnp.int32, (tq, LANES), 1)
    sees_any = t_row[:, :1] >= CMP_LEN - 1
    psum = jnp.zeros((tq, n_cmp), F32)
    outs = []
    for r in range(NSA_REP):
        s = jnp.where(vis, _dot_nt(qn_ref[:, r * LANES:(r + 1) * LANES], kc), NEG)
        m = jnp.max(s, axis=-1, keepdims=True)
        e = jnp.exp2(s - m)
        pn = e * jnp.where(sees_any, 1.0 / jnp.sum(e, axis=-1, keepdims=True), 0.0)
        psum = psum + pn
        outs.append(_dot(pn.astype(BF16), vc))
    for k in range(NSA_REP // 2):
        o_ref[:, k * LANES:(k + 1) * LANES] = jnp.where(lane < HEAD_DIM, outs[2 * k], outs[2 * k + 1])

    p_hi = psum.astype(BF16)
    p_lo = (psum - p_hi.astype(F32)).astype(BF16)
    ovt = ovt_ref[...]
    imp = _dot_nt(ovt, p_hi) + _dot_nt(ovt, p_lo)
    n_slc = LANES - HEAD_DIM
    jb = lax.broadcasted_iota(jnp.int32, (LANES, tq), 0) - HEAD_DIM
    cur = (i * tq + lax.broadcasted_iota(jnp.int32, (LANES, tq), 1)) // SLC_LEN
    forced = (jb == 0) | (jb == cur) | (jb == cur - 1)
    imp = jnp.where(forced, 1e9, jnp.where(jb > cur, -1e9, imp))
    groups = [imp[HEAD_DIM + 8 * v:HEAD_DIM + 8 * (v + 1)] for v in range(n_slc // 8)]
    sub = lax.broadcasted_iota(jnp.int32, (8, tq), 0)
    ranks = [jnp.zeros((8, tq), F32) for _ in groups]
    for a in range(n_slc):
        row = jnp.broadcast_to(groups[a // 8][a % 8:a % 8 + 1], (8, tq))
        for v in range(len(groups)):
            if v > a // 8:
                inc = jnp.where(row >= groups[v], 1.0, 0.0)
            elif v < a // 8:
                inc = jnp.where(row > groups[v], 1.0, 0.0)
            else:
                tie = jnp.where(sub > a % 8, 1.0, 0.0)
                inc = jnp.where(row > groups[v], 1.0, jnp.where(row == groups[v], tie, 0.0))
            ranks[v] = ranks[v] + inc
    bias_t = jnp.concatenate(
        [jnp.zeros((HEAD_DIM, tq), F32)]
        + [jnp.where(rk < SLC_TOPK, 0.0, SEL_NEG) for rk in ranks], axis=0)
    bias = bias_t.T
    for r in range(NSA_REP):
        qa_ref[:, r * LANES:(r + 1) * LANES] = (
            qr_ref[:, r * LANES:(r + 1) * LANES].astype(F32) + bias).astype(BF16)


def _overlap_t(seq):
    n_cmp = (seq - CMP_LEN) // CMP_STRIDE + 1
    n_slc = seq // SLC_LEN
    assert n_slc == LANES - HEAD_DIM, "selection blocks must fill the bias half of a slab"
    c0 = CMP_STRIDE * np.arange(n_cmp)[:, None]
    s0 = SLC_LEN * np.arange(n_slc)[None, :]
    ov = np.clip(np.minimum(c0 + CMP_LEN, s0 + SLC_LEN) - np.maximum(c0, s0), 0, None) / CMP_LEN
    out = np.zeros((LANES, seq // CMP_STRIDE), np.float32)
    out[HEAD_DIM:, :n_cmp] = ov.T
    return jnp.asarray(out, BF16)


def _cmp_select(qn, qr, cmp_kv, batch, seq):
    T = qn.shape[0]
    nq = seq // TQ_SELECT
    rows = seq // CMP_STRIDE
    gw = NSA_REP * LANES
    qspec = pl.BlockSpec((TQ_SELECT, gw), lambda b, g, i: (b * nq + i, g))
    return pl.pallas_call(
        functools.partial(_cmp_select_kernel, tq=TQ_SELECT), grid=(batch, NSA_GROUPS, nq),
        in_specs=[qspec, qspec,
                  pl.BlockSpec((1, 1, rows, LANES), lambda b, g, i: (b, g, 0, 0)),
                  pl.BlockSpec((1, 1, rows, LANES), lambda b, g, i: (b, NSA_GROUPS + g, 0, 0)),
                  _const_spec((LANES, rows))],
        out_specs=[pl.BlockSpec((TQ_SELECT, NSA_REP * HEAD_DIM), lambda b, g, i: (b * nq + i, g)), qspec],
        out_shape=[jax.ShapeDtypeStruct((T, NSA_HEADS * HEAD_DIM), F32),
                   jax.ShapeDtypeStruct((T, NSA_HEADS * LANES), BF16)],
        compiler_params=_cparams(3), name="nsa_cmp_select",
    )(qn, qr, cmp_kv, cmp_kv, _overlap_t(seq))


def _merge_kernel(x_ref, oc_ref, os_ref, ow_ref, gl_ref, of_ref, om_ref, mg_ref,
                  e_ref, wn_ref, wf_ref, wm_ref, wo_ref, o_ref):
    d = x_ref.shape[1]
    nw = oc_ref.shape[1]
    mg = mg_ref[...]
    merged = mg[:, d:2 * d] * _dot(of_ref[...], wf_ref[...]) + mg[:, 2 * d:] * _dot(om_ref[...], wm_ref[...])
    g = _dot(jax.nn.sigmoid(gl_ref[...]).astype(BF16), e_ref[...])
    o_nsa = g[:, :nw] * oc_ref[...] + g[:, nw:2 * nw] * os_ref[...] + g[:, 2 * nw:] * ow_ref[...]
    merged = merged + mg[:, :d] * _dot(o_nsa.astype(BF16), wn_ref[...])
    o_ref[...] = x_ref[...] + _dot(merged.astype(BF16), wo_ref[...])


def _gate_expand():
    e = np.zeros((LANES, 3 * NSA_HEADS * HEAD_DIM), np.float32)
    for h in range(NSA_HEADS):
        for c in range(3):
            e[h * 3 + c, c * NSA_HEADS * HEAD_DIM + h * HEAD_DIM:
              c * NSA_HEADS * HEAD_DIM + (h + 1) * HEAD_DIM] = 1.0
    return jnp.asarray(e, BF16)


def _merge(x2, o_cmp, o_slc, o_win, gate_l, o_fox, o_mla, mg, p):
    T, D = x2.shape
    tok = lambda a: pl.BlockSpec((TM, a.shape[1]), lambda i: (i, 0))
    acts = [x2, o_cmp, o_slc, o_win, gate_l, o_fox, o_mla, mg]
    weights = [_gate_expand(), p["w_br_nsa"].astype(BF16), p["w_br_fox"].astype(BF16),
               p["w_br_mla"].astype(BF16), p["w_out"].astype(BF16)]
    return pl.pallas_call(
        _merge_kernel, grid=(T // TM,),
        in_specs=[tok(a) for a in acts] + [_const_spec(w.shape) for w in weights],
        out_specs=pl.BlockSpec((TM, D), lambda i: (i, 0)),
        out_shape=jax.ShapeDtypeStruct((T, D), F32),
        compiler_params=_cparams(1), name="merge_out",
    )(*acts, *weights)


def _ffn_kernel(x_ref, g_ref, wu_ref, wv_ref, cw_ref, cb_ref, wd_ref, fin_ref, o_ref, tail_ref,
                *, seq, n_chunks, final):
    tm = x_ref.shape[0]
    x = x_ref[...]
    hb = _rms(x, g_ref[...]).astype(BF16)

    @pl.when((pl.program_id(0) * tm) % seq == 0)
    def _():
        tail_ref[...] = jnp.zeros(tail_ref.shape, F32)

    cw = wu_ref.shape[1] // n_chunks
    row8 = lax.broadcasted_iota(jnp.int32, (8, cw), 0)
    acc = jnp.zeros(x.shape, F32)
    for c in range(n_chunks):
        sl = slice(c * cw, (c + 1) * cw)
        u = _dot(hb, wu_ref[:, sl])
        v = _dot(hb, wv_ref[:, sl])
        prev = tail_ref[:, sl]
        tail_ref[:, sl] = u[tm - 8:]
        u1 = pltpu.roll(u, 1, 0)
        u2 = pltpu.roll(u, 2, 0)
        h1 = jnp.where(row8 < 1, pltpu.roll(prev, 1, 0), u1[:8])
        h2 = jnp.where(row8 < 2, pltpu.roll(prev, 2, 0), u2[:8])
        u1 = jnp.concatenate([h1, u1[8:]], axis=0)
        u2 = jnp.concatenate([h2, u2[8:]], axis=0)
        uc = cw_ref[0:1, sl] * u2 + cw_ref[1:2, sl] * u1 + cw_ref[2:3, sl] * u + cb_ref[:, sl]
        act = (uc * jax.nn.sigmoid(uc) * v).astype(BF16)
        acc = acc + _dot(act, wd_ref[sl, :])
    y = x + acc
    if final:
        y = _rms(y, fin_ref[...])
    o_ref[...] = y


def _ffn(x2, p, final_norm, seq, final):
    T, D = x2.shape
    w_up = p["w_up"]
    wu, wv = w_up[:, :D_FF].astype(BF16), w_up[:, D_FF:].astype(BF16)
    row = lambda a: a.reshape(1, -1).astype(F32)
    cwp = jnp.pad(p["conv_w"].astype(F32), ((0, 8 - CONV_W), (0, 0)))
    weights = [row(p["ffn_norm"]), wu, wv, cwp, row(p["conv_b"]), p["w_down"].astype(BF16),
               row(final_norm)]
    return pl.pallas_call(
        functools.partial(_ffn_kernel, seq=seq, n_chunks=2, final=final), grid=(T // TM,),
        in_specs=[pl.BlockSpec((TM, D), lambda i: (i, 0))] + [_const_spec(w.shape) for w in weights],
        out_specs=pl.BlockSpec((TM, D), lambda i: (i, 0)),
        out_shape=jax.ShapeDtypeStruct((T, D), F32),
        scratch_shapes=[pltpu.VMEM((8, D_FF), F32)],
        compiler_params=_cparams(1), name="conv_ffn",
    )(x2, *weights)


_LAYER_PARAMS = ("mix_norm", "w_in", "b_forget", "cmp_pe_k", "cmp_w1_k", "cmp_w2_k", "cmp_pe_v",
                 "cmp_w1_v", "cmp_w2_v", "mla_q_norm", "mla_w_uq", "mla_kv_norm", "mla_w_ukv",
                 "w_br_nsa", "w_br_fox", "w_br_mla", "w_out", "ffn_norm", "w_up", "conv_w",
                 "conv_b", "w_down")


def _mixer_layer(h, tabs, p, batch, seq):
    (qn, qr, kvc, ksl, vsl_t, kwin, vwin_t, gate_l,
     fq, fk, fv_t, mq, mk, mv_t, mg) = _inproj(h, tabs, p, seq)
    cmp_kv = _compress(kvc, p, batch, seq)
    o_cmp, qa = _cmp_select(qn, qr, cmp_kv, batch, seq)
    group = (0,) * NSA_REP
    pair = (0, 1)
    o_slc, o_win = _attention([(qa, ksl, vsl_t, False), (qa, kwin, vwin_t, True)], batch, seq, group,
                              F32, "nsa_slc_win_attn")
    o_fox, o_mla = _attention([(fq, fk, fv_t, False), (mq, mk, mv_t, False)], batch, seq, pair, BF16,
                              "fox_mla_attn")
    return _merge(h, o_cmp, o_slc, o_win, gate_l, o_fox, o_mla, mg, p)


def kernel(x, positions, mix_norm, w_in, b_forget, cmp_pe_k, cmp_w1_k, cmp_w2_k, cmp_pe_v, cmp_w1_v,
           cmp_w2_v, mla_q_norm, mla_w_uq, mla_kv_norm, mla_w_ukv, w_br_nsa, w_br_fox, w_br_mla,
           w_out, ffn_norm, w_up, conv_w, conv_b, w_down, final_norm):
    batch, seq, d = x.shape
    stacked = dict(zip(_LAYER_PARAMS, (
        mix_norm, w_in, b_forget, cmp_pe_k, cmp_w1_k, cmp_w2_k, cmp_pe_v, cmp_w1_v, cmp_w2_v,
        mla_q_norm, mla_w_uq, mla_kv_norm, mla_w_ukv, w_br_nsa, w_br_fox, w_br_mla, w_out,
        ffn_norm, w_up, conv_w, conv_b, w_down)))
    depth = w_in.shape[0]
    tabs = _rope_tables(positions)
    h = x.reshape(batch * seq, d)
    for l in range(depth):
        p = {k: v[l] for k, v in stacked.items()}
        h = _mixer_layer(h, tabs, p, batch, seq)
        h = _ffn(h, p, final_norm, seq, final=(l == depth - 1))
    return h.reshape(batch, seq, d)
```

```python
import functools

import numpy as np
import jax
import jax.numpy as jnp
from jax import lax
from jax.experimental import pallas as pl
from jax.experimental.pallas import tpu as pltpu

F32 = jnp.float32
BF16 = jnp.bfloat16

HEAD_DIM = 64
ROPE_THETA = 500000.0
PARTIAL_ROT = HEAD_DIM // 4
NORM_EPS = 1e-6
NSA_HEADS = 8
NSA_GROUPS = 2
NSA_REP = NSA_HEADS // NSA_GROUPS
CMP_LEN = 32
CMP_STRIDE = 16
SLC_LEN = 64
SLC_TOPK = 16
WIN = 512
FOX_HEADS = 8
MLA_HEADS = 8
MLA_Q_LORA = 384
MLA_KV_LORA = 256
MLA_NOPE = 64
MLA_ROPE = 32
MLA_V = 64
D_FF = 2816
D_MODEL = 1024
CONV_W = 3

LANES = 128
LOG2E = 1.4426950408889634
NEG = -1e30
SEL_NEG = -1e9
VMEM_LIMIT = 52 * 1024 * 1024

TM = 256
TQ = 1024
TQ_SELECT = 512
TK = 512
QC = 256
LOOP_TILES = 4
SKEW = 2
DECAY_TERMS = 3


def _cparams(n_axes):
    return pltpu.CompilerParams(dimension_semantics=("arbitrary",) * n_axes,
                                vmem_limit_bytes=VMEM_LIMIT)


def _const_spec(shape):
    nd = len(shape)
    return pl.BlockSpec(shape, lambda *_: (0,) * nd, pipeline_mode=pl.Buffered(1))


def _rms(x, g):
    return x * lax.rsqrt(jnp.mean(x * x, axis=-1, keepdims=True) + NORM_EPS) * g


def _dot(a, b):
    return jnp.dot(a, b, preferred_element_type=F32)


def _dot_nt(a, b):
    return lax.dot_general(a, b, (((1,), (1,)), ((), ())), preferred_element_type=F32)


def _rope_tables_kernel(pos_ref, freq_ref, g16_ref, g32_ref, c16_o, s16_o, c32_o, s32_o):
    ang = pos_ref[...].astype(F32) * freq_ref[...]
    cos, sin = jnp.cos(ang), jnp.sin(ang)
    lane = lax.broadcasted_iota(jnp.int32, ang.shape, 1)
    low = lane < PARTIAL_ROT
    high = (lane >= HEAD_DIM) & (lane < HEAD_DIM + PARTIAL_ROT)
    mla = (lane >= MLA_NOPE) & (lane < MLA_NOPE + MLA_ROPE)
    c16_o[...] = jnp.where(low, cos, jnp.where(high, pltpu.roll(cos, HEAD_DIM, 1), 1.0))
    s16_o[...] = jnp.where(low, sin, jnp.where(high, pltpu.roll(sin, HEAD_DIM, 1), 0.0)) * g16_ref[...]
    c32_o[...] = jnp.where(mla, pltpu.roll(cos, MLA_ROPE, 1), 1.0)
    s32_o[...] = jnp.where(mla, pltpu.roll(sin, MLA_ROPE, 1), 0.0) * g32_ref[...]


def _rope_tables(positions):
    T = positions.size
    half16, half32 = PARTIAL_ROT // 2, MLA_ROPE // 2
    inv16 = ROPE_THETA ** (-jnp.arange(half16, dtype=F32) / half16)
    inv32 = ROPE_THETA ** (-jnp.arange(half32, dtype=F32) / half32)
    lane = np.arange(LANES)
    freq = jnp.where(lane < PARTIAL_ROT, inv16[lane % half16],
                     jnp.where((lane >= MLA_ROPE) & (lane < 2 * MLA_ROPE), inv32[lane % half32], 0.0))[None, :]
    d = lane % HEAD_DIM
    g16 = np.where(d < PARTIAL_ROT, np.where(d < half16, -1.0, 1.0), 0.0).astype(np.float32)[None, :]
    r = lane - MLA_NOPE
    g32 = np.where((r >= 0) & (r < MLA_ROPE), np.where(r < half32, -1.0, 1.0), 0.0).astype(np.float32)[None, :]
    pos = jnp.broadcast_to(positions.reshape(T, 1), (T, LANES))
    tt = 1024
    row = pl.BlockSpec((tt, LANES), lambda i: (i, 0))
    vec = _const_spec((1, LANES))
    out = jax.ShapeDtypeStruct((T, LANES), F32)
    return pl.pallas_call(
        _rope_tables_kernel, grid=(T // tt,),
        in_specs=[row, vec, vec, vec], out_specs=[row] * 4, out_shape=[out] * 4,
        compiler_params=_cparams(1), name="rope_tables",
    )(pos, freq, jnp.asarray(g16), jnp.asarray(g32))


def _rot16(x, cos, sin):
    lane = lax.broadcasted_iota(jnp.int32, x.shape, 1)
    first = (lane % HEAD_DIM) < (PARTIAL_ROT // 2)
    sw = jnp.where(first, pltpu.roll(x, LANES - PARTIAL_ROT // 2, 1), pltpu.roll(x, PARTIAL_ROT // 2, 1))
    return x * cos + sw * sin


def _rot32(x, cos, sin):
    lane = lax.broadcasted_iota(jnp.int32, x.shape, 1)
    first = lane < (MLA_NOPE + MLA_ROPE // 2)
    sw = jnp.where(first, pltpu.roll(x, LANES - MLA_ROPE // 2, 1), pltpu.roll(x, MLA_ROPE // 2, 1))
    return x * cos + sw * sin


_FF_LANE = 3 * NSA_HEADS
_IN_COLS = {}
for _name, _n in (("q", NSA_HEADS * HEAD_DIM), ("kvc", 4 * HEAD_DIM), ("ksl", LANES), ("kwin", LANES),
                  ("misc", LANES), ("fq", FOX_HEADS * HEAD_DIM), ("fk", FOX_HEADS * HEAD_DIM),
                  ("lat", MLA_Q_LORA + MLA_KV_LORA), ("mg", 3 * D_MODEL)):
    _IN_COLS[_name] = (sum(n for _, n in _IN_COLS.values()), _n)
_IN_ROWS = {"vsl": (0, LANES), "vwin": (LANES, LANES), "fv": (2 * LANES, FOX_HEADS * HEAD_DIM)}


def _inproj_kernel(x_ref, g_ref, c16_ref, s16_ref, c32_ref, s32_ref,
                   w_ref, wt_ref, bf_ref, tri_ref, place_ref,
                   qnrm_ref, wuq_ref, kvnrm_ref, wuk_ref, wuv_ref,
                   qn_o, qr_o, kvc_o, ksl_o, vsl_o, kwin_o, vwin_o, gate_o,
                   fq_o, fk_o, fv_o, mq_o, mk_o, mv_o, mg_o, fcarry_ref, *, seq):
    tm = x_ref.shape[0]
    seq_start = (pl.program_id(0) * tm) % seq == 0
    hb = _rms(x_ref[...], g_ref[...]).astype(BF16)
    c16, s16 = c16_ref[...], s16_ref[...]
    c32, s32 = c32_ref[...], s32_ref[...]
    lane = lax.broadcasted_iota(jnp.int32, (tm, LANES), 1)
    low = lane < HEAD_DIM

    def proj(name):
        a, n = _IN_COLS[name]
        return _dot(hb, w_ref[:, a:a + n])

    def proj_t(name):
        a, n = _IN_ROWS[name]
        return _dot_nt(wt_ref[a:a + n, :], hb)

    def slabs(z):
        return [z[:, k * LANES:(k + 1) * LANES] for k in range(z.shape[1] // LANES)]

    def split(z, fill=0.0):
        return jnp.where(low, z, fill), jnp.where(low, pltpu.roll(z, HEAD_DIM, 1), fill)

    def store_heads(o_ref, pair_slabs, fill=0.0, extra=None):
        for k, z in enumerate(pair_slabs):
            for e, s in enumerate(split(z, fill)):
                h = 2 * k + e
                if extra is not None:
                    s = s + extra[:, h * LANES:(h + 1) * LANES]
                o_ref[:, h * LANES:(h + 1) * LANES] = s.astype(o_ref.dtype)

    misc = proj("misc")
    gate_o[...] = misc

    @pl.when(seq_start)
    def _():
        fcarry_ref[...] = jnp.zeros(fcarry_ref.shape, F32)

    def bf16_terms(v):
        packed = None
        for t in range(DECAY_TERMS):
            part = v.astype(BF16).astype(F32)
            v = v - part
            part = part if t == 0 else pltpu.roll(part, FOX_HEADS * t, 1)
            packed = part if packed is None else packed + part
        return packed.astype(BF16)

    zf = misc + bf_ref[...]
    is_f = (lane >= _FF_LANE) & (lane < _FF_LANE + FOX_HEADS)
    logf_terms = bf16_terms(
        jnp.where(is_f, jnp.minimum(zf, 0.0) - jnp.log1p(jnp.exp(-jnp.abs(zf))), 0.0))

    zq = slabs(proj("q"))
    store_heads(qn_o, zq)
    store_heads(qr_o, [_rot16(z, c16, s16) for z in zq])

    terms = _dot(tri_ref[...], logf_terms)
    fsum = sum(terms if t == 0 else pltpu.roll(terms, LANES - FOX_HEADS * t, 1)
               for t in range(DECAY_TERMS))
    fsum = jnp.where(is_f, fsum, 0.0) + fcarry_ref[0:1]
    fcarry_ref[...] = jnp.broadcast_to(fsum[tm - 1:tm], fcarry_ref.shape)
    decay_terms = bf16_terms(jnp.where(is_f, -LOG2E * fsum, 0.0))

    for k, z in enumerate(slabs(proj("kvc"))):
        for e, s in enumerate(split(z)):
            kvc_o[2 * k + e] = s[:, :HEAD_DIM]

    tok = (pl.program_id(0) * tm) % seq + lax.broadcasted_iota(jnp.int32, (tm, LANES), 0)
    onehot = jnp.where(lane - HEAD_DIM == tok // SLC_LEN, 1.0, 0.0)
    store_heads(ksl_o, [_rot16(proj("ksl"), c16, s16)], fill=onehot)
    vsl_o[0] = proj_t("vsl").astype(BF16)

    store_heads(kwin_o, [_rot16(proj("kwin"), c16, s16)])
    vwin_o[0] = proj_t("vwin").astype(BF16)

    ones = jnp.where((lane >= HEAD_DIM) & (lane < HEAD_DIM + DECAY_TERMS), 1.0, 0.0)
    store_heads(fq_o, slabs(proj("fq")), fill=ones)
    fv_o[0] = proj_t("fv").astype(BF16)
    decay = _dot(decay_terms, place_ref[...])

    lat = proj("lat")
    cq = _rms(lat[:, :MLA_Q_LORA], qnrm_ref[...]).astype(BF16)
    ckv = _rms(lat[:, MLA_Q_LORA:], kvnrm_ref[...]).astype(BF16)
    kpe = jnp.where(low, 0.0, _rot32(misc, c32, s32))
    store_heads(fk_o, slabs(proj("fk")), extra=decay)
    zuq = _dot(cq, wuq_ref[...])
    for k, z in enumerate(slabs(zuq)):
        mq_o[:, k * LANES:(k + 1) * LANES] = _rot32(z, c32, s32).astype(BF16)
    zuk = _dot(ckv, wuk_ref[...])
    for k, z in enumerate(slabs(zuk)):
        mk_o[:, k * LANES:(k + 1) * LANES] = (z + kpe).astype(BF16)
    mv_o[0] = _dot_nt(wuv_ref[...], ckv).astype(BF16)

    mg_o[...] = jax.nn.sigmoid(proj("mg"))


def _pad_halves(w, first_only=True):
    K, n = w.shape[0], w.shape[1] // HEAD_DIM
    w = w.reshape(K, n, HEAD_DIM)
    other = jnp.zeros_like(w) if first_only else w
    return jnp.concatenate([w, other], axis=-1).reshape(K, n * LANES)


def _inproj(x2, tabs, p, seq):
    T, D = x2.shape
    w_in = p["w_in"]
    scale = HEAD_DIM ** -0.5 * LOG2E
    o = 0

    def take(n):
        nonlocal o
        w = w_in[:, o:o + n]
        o += n
        return w

    nsa_w, kvw = NSA_HEADS * HEAD_DIM, NSA_GROUPS * HEAD_DIM
    w_q = take(nsa_w)
    w_kc, w_vc, w_ks, w_vs, w_kw, w_vw = [take(kvw) for _ in range(6)]
    w_g = take(3 * NSA_HEADS)
    w_fq, w_fk, w_fv = [take(FOX_HEADS * HEAD_DIM) for _ in range(3)]
    w_ff = take(FOX_HEADS)
    w_cq, w_ckv, w_kr = take(MLA_Q_LORA), take(MLA_KV_LORA), take(MLA_ROPE)
    w_mg = take(3 * D)

    bf = lambda w: w.astype(BF16)
    row = lambda a: a.reshape(1, -1).astype(F32)
    w_misc = jnp.concatenate([w_g, w_ff, jnp.zeros((D, MLA_NOPE - _FF_LANE - FOX_HEADS), F32), w_kr,
                              jnp.zeros((D, LANES - MLA_NOPE - MLA_ROPE), F32)], axis=1)
    w_tok = bf(jnp.concatenate([w_q * scale, w_kc, w_vc, w_ks, w_kw, w_misc, w_fq * scale, w_fk,
                                w_cq, w_ckv, w_mg], axis=1))
    w_tr = bf(jnp.concatenate([w_vs, w_vw, w_fv], axis=1).T)
    b_f = jnp.pad(row(p["b_forget"]), ((0, 0), (_FF_LANE, LANES - _FF_LANE - FOX_HEADS)))
    tri = jnp.asarray(np.tril(np.ones((TM, TM), np.float32)), BF16)
    place = np.zeros((LANES, FOX_HEADS * LANES), np.float32)
    for t in range(DECAY_TERMS):
        for h in range(FOX_HEADS):
            place[_FF_LANE + FOX_HEADS * t + h, h * LANES + HEAD_DIM + t] = 1.0
    place = jnp.asarray(place, BF16)
    dq = MLA_NOPE + MLA_ROPE
    wuq = (p["mla_w_uq"] * (dq ** -0.5 * LOG2E)).reshape(MLA_Q_LORA, MLA_HEADS, dq)
    wuq = bf(jnp.pad(wuq, ((0, 0), (0, 0), (0, LANES - dq))).reshape(MLA_Q_LORA, MLA_HEADS * LANES))
    wukv = p["mla_w_ukv"].reshape(MLA_KV_LORA, MLA_HEADS, MLA_NOPE + MLA_V)
    wuk = bf(_pad_halves(wukv[:, :, :MLA_NOPE].reshape(MLA_KV_LORA, MLA_HEADS * MLA_NOPE)))
    wuv = bf(wukv[:, :, MLA_NOPE:].reshape(MLA_KV_LORA, MLA_HEADS * MLA_V).T)
    assert w_tok.shape[1] == sum(n for _, n in _IN_COLS.values())

    weights = [w_tok, w_tr, b_f, tri, place,
               row(p["mla_q_norm"]), wuq, row(p["mla_kv_norm"]), wuk, wuv]
    n = T // TM
    tok = lambda c: pl.BlockSpec((TM, c), lambda i: (i, 0))
    in_specs = ([tok(D), _const_spec((1, D))] + [tok(LANES)] * 4
                + [_const_spec(w.shape) for w in weights])
    tokens, transposed = "tokens", "transposed"
    outs = [
        (tokens, NSA_HEADS * LANES, BF16), (tokens, NSA_HEADS * LANES, BF16), None,
        (tokens, NSA_GROUPS * LANES, BF16), (transposed, NSA_GROUPS * HEAD_DIM, BF16),
        (tokens, NSA_GROUPS * LANES, BF16), (transposed, NSA_GROUPS * HEAD_DIM, BF16), (tokens, LANES, F32),
        (tokens, FOX_HEADS * LANES, BF16), (tokens, FOX_HEADS * LANES, BF16),
        (transposed, FOX_HEADS * HEAD_DIM, BF16),
        (tokens, MLA_HEADS * LANES, BF16), (tokens, MLA_HEADS * LANES, BF16),
        (transposed, MLA_HEADS * MLA_V, BF16), (tokens, 3 * D, F32)]
    out_shape, out_specs = [], []
    for spec in outs:
        if spec is None:
            out_shape.append(jax.ShapeDtypeStruct((4, T, HEAD_DIM), F32))
            out_specs.append(pl.BlockSpec((4, TM, HEAD_DIM), lambda i: (0, i, 0)))
        elif spec[0] == transposed:
            out_shape.append(jax.ShapeDtypeStruct((n, spec[1], TM), spec[2]))
            out_specs.append(pl.BlockSpec((1, spec[1], TM), lambda i: (i, 0, 0)))
        else:
            out_shape.append(jax.ShapeDtypeStruct((T, spec[1]), spec[2]))
            out_specs.append(tok(spec[1]))
    return pl.pallas_call(
        functools.partial(_inproj_kernel, seq=seq), grid=(n,),
        in_specs=in_specs, out_specs=out_specs, out_shape=out_shape,
        scratch_shapes=[pltpu.VMEM((8, LANES), F32)],
        compiler_params=_cparams(1), name="inproj",
    )(x2, row(p["mix_norm"]), *tabs, *weights)


def _attn_kernel(*refs, tq, tk, q_of_stream, window_of_stream, k_of_head, values_by_unit):
    n_streams, n_q = len(q_of_stream), max(q_of_stream) + 1
    q_refs, kv_refs = refs[:n_q], refs[n_q:n_q + 2 * n_streams]
    outs = refs[n_q + 2 * n_streams:n_q + 3 * n_streams]
    qt_ref, m_ref, l_ref, acc_ref = refs[n_q + 3 * n_streams:]
    i = pl.program_id(2)
    hs = len(k_of_head)
    nh = n_streams * hs
    k_of = lambda h: kv_refs[2 * (h // hs)]
    vt_of = lambda h: kv_refs[2 * (h // hs) + 1]
    qt_of = lambda h: q_of_stream[h // hs] * hs + h % hs
    per_q = tq // tk
    tv = kv_refs[1].shape[2]
    v_sub = tk // tv
    for qi in range(n_q):
        for r in range(hs):
            qt_ref[qi * hs + r] = q_refs[qi][:, r * LANES:(r + 1) * LANES].astype(F32).T.astype(BF16)
    dist0 = (i * tq + lax.broadcasted_iota(jnp.int32, (tk, tq), 1)
             - lax.broadcasted_iota(jnp.int32, (tk, tq), 0))

    m_ref[...] = jnp.full(m_ref.shape, NEG, F32)
    l_ref[...] = jnp.zeros(l_ref.shape, F32)
    acc_ref[...] = jnp.zeros(acc_ref.shape, F32)

    def logits(j, lo, n, h, cols):
        r0 = pl.multiple_of(j * tk + lo, tv)
        kk = k_of_head[h % hs]
        return _dot(k_of(h)[pl.ds(r0, n), kk * LANES:(kk + 1) * LANES], qt_ref[qt_of(h), :, cols])

    def softmax(s, h, cols, mask):
        if mask is not None:
            s = jnp.where(mask, s, NEG)
        m_old = m_ref[h, :, cols]
        m_new = jnp.maximum(m_old, jnp.max(s, axis=0, keepdims=True))
        alpha = jnp.exp2(m_old - m_new)
        p = jnp.exp2(s - m_new)
        l_ref[h, :, cols] = alpha * l_ref[h, :, cols] + jnp.sum(p, axis=0, keepdims=True)
        m_ref[h, :, cols] = m_new
        return p.astype(BF16), alpha

    def accumulate(j, lo, n, h, cols, p, alpha):
        if values_by_unit:
            half = pl.ds(pl.multiple_of(pl.program_id(1) * HEAD_DIM, HEAD_DIM), HEAD_DIM)
        else:
            half = slice((h % 2) * HEAD_DIM, (h % 2 + 1) * HEAD_DIM)
        pv = sum(_dot(vt_of(h)[j * v_sub + lo // tv + c, half, :], p[c * tv:(c + 1) * tv])
                 for c in range(n // tv))
        acc_ref[h, :, cols] = alpha * acc_ref[h, :, cols] + pv

    def sweep(tiles):
        work = []
        for heads, j, n_tiles, mask, span in tiles:
            for h in heads:
                for c in range(tq // QC):
                    lo, n, masked = (0, n_tiles * tk, False) if span is None else span(c)
                    if n > 0:
                        work.append((j, lo, n, mask if masked else None, h, slice(c * QC, (c + 1) * QC)))
        s_of, p_of = {}, {}
        for t in range(len(work) + 2 * SKEW):
            if 0 <= t - 2 * SKEW < len(work):
                j, lo, n, _, h, cols = work[t - 2 * SKEW]
                accumulate(j, lo, n, h, cols, *p_of.pop(t - 2 * SKEW))
            if 0 <= t - SKEW < len(work):
                _, lo, n, mask, h, cols = work[t - SKEW]
                p_of[t - SKEW] = softmax(s_of.pop(t - SKEW), h, cols,
                                         None if mask is None else mask(slice(lo, lo + n), cols))
            if t < len(work):
                j, lo, n, _, h, cols = work[t]
                s_of[t] = logits(j, lo, n, h, cols)

    def visible(j, window, valid=None):
        def mask(keys, cols):
            dist = dist0[keys, cols] - j * tk
            if not window:
                return dist >= 0
            if valid is not None:
                dist = dist + jnp.where(valid, 0, WIN)
            return lax.bitcast_convert_type(dist, jnp.uint32) < WIN
        return mask

    assert tk == 2 * QC and WIN % tk == 0 and LOOP_TILES % per_q == 0
    causal = [h for h in range(nh) if not window_of_stream[h // hs]]
    windowed = [h for h in range(nh) if window_of_stream[h // hs]]
    first = i * per_q

    def causal_span(d):
        def span(c):
            hi = min(tk, (c + 1) * QC - d * tk)
            return 0, hi, d * tk + hi > c * QC + 1
        return span

    def window_span(d):
        def span(c):
            lo = max(0, (c * QC - WIN + 1) // tv * tv - d * tk)
            return lo, min(tk, (c + 1) * QC - d * tk) - lo, True
        return span

    one_step = len(causal) * (tq // QC) >= 8

    def unmasked(j0, n):
        tiles = []
        while n > 0:
            k = 2 if (one_step and n >= 2) else 1
            tiles.append((causal, j0, k, None, None))
            j0, n = j0 + k, n - k
        return tiles

    def last_tiles(n_left):
        tiles = []
        if causal:
            tiles += unmasked(first - n_left, n_left)
            tiles += [(causal, first + d, 1, visible(first + d, False), causal_span(d)) for d in range(per_q)]
        if windowed:
            tiles += [(windowed, first + d, 1, visible(first + d, True), window_span(d))
                      for d in reversed(range(per_q))]
            tiles += [(windowed, jnp.maximum(first + d, 0), 1,
                       visible(first + d, True, valid=first + d >= 0), window_span(d))
                      for d in range(-(WIN // tk), 0)]
        return tiles

    if causal:
        def body(jj, c):
            sweep(unmasked(LOOP_TILES * jj, LOOP_TILES))
            return c
        lax.fori_loop(0, first // LOOP_TILES, body, 0)
        for n_left in range(0, LOOP_TILES, per_q):
            @pl.when(first % LOOP_TILES == n_left)
            def _():
                sweep(last_tiles(n_left))
    else:
        sweep(last_tiles(0))

    for h in range(0, nh, 2):
        a = acc_ref[h] * (1.0 / l_ref[h])
        b = acc_ref[h + 1] * (1.0 / l_ref[h + 1])
        o_ref, k = outs[h // hs], (h % hs) // 2
        o_ref[:, k * LANES:(k + 1) * LANES] = jnp.concatenate([a, b], axis=0).T.astype(o_ref.dtype)


def _attention(streams, batch, seq, k_of_head, out_dtype, name):
    qs = []
    for q, _, _, _ in streams:
        if not any(q is x for x in qs):
            qs.append(q)
    q_of_stream = tuple(next(n for n, x in enumerate(qs) if x is q) for q, _, _, _ in streams)
    T = qs[0].shape[0]
    nq = seq // TQ
    hs, nk = len(k_of_head), max(k_of_head) + 1
    nh = hs * len(streams)
    units = qs[0].shape[1] // (hs * LANES)
    by_unit = nk == 1
    assert not by_unit or units * HEAD_DIM == LANES, "GQA value blocks must fill one slab"
    in_specs = [pl.BlockSpec((TQ, hs * LANES), lambda b, u, i: (b * nq + i, u)) for _ in qs]
    args = list(qs)
    for q, k, vt, _ in streams:
        assert q.shape[1] == units * hs * LANES and k.shape[1] == units * nk * LANES
        assert vt.shape == (T // TM, units * (HEAD_DIM if by_unit else LANES), TM)
        in_specs += [pl.BlockSpec((seq, nk * LANES), lambda b, u, i: (b, u)),
                     pl.BlockSpec((seq // TM, LANES, TM),
                                  (lambda b, u, i: (b, 0, 0)) if by_unit else (lambda b, u, i: (b, u, 0)))]
        args += [k, vt]
    return pl.pallas_call(
        functools.partial(_attn_kernel, tq=TQ, tk=TK, q_of_stream=q_of_stream,
                          window_of_stream=tuple(w for _, _, _, w in streams),
                          k_of_head=k_of_head, values_by_unit=by_unit),
        grid=(batch, units, nq), in_specs=in_specs,
        out_specs=[pl.BlockSpec((TQ, hs // 2 * LANES), lambda b, u, i: (b * nq + i, u))] * len(streams),
        out_shape=[jax.ShapeDtypeStruct((T, units * hs // 2 * LANES), out_dtype)] * len(streams),
        scratch_shapes=[pltpu.VMEM((len(qs) * hs, LANES, TQ), BF16), pltpu.VMEM((nh, 1, TQ), F32),
                        pltpu.VMEM((nh, 1, TQ), F32), pltpu.VMEM((nh, HEAD_DIM, TQ), F32)],
        compiler_params=_cparams(3), name=name,
    )(*args)


def _gelu_tanh(x):
    return 0.5 * x * (1.0 + jnp.tanh(np.float32(np.sqrt(2.0 / np.pi)) * (x + 0.044715 * (x * x * x))))


def _compress_kernel(x_ref, pe_ref, w1_ref, w2_ref, o_ref):
    x = x_ref[0]
    half = x.shape[1]
    top = _dot((x + pe_ref[0, :, :half]).astype(BF16), w1_ref[0, :half, :])
    bot = _dot((x + pe_ref[0, :, half:]).astype(BF16), w1_ref[0, half:, :])
    hid = top + pltpu.roll(bot, bot.shape[0] - 1, 0)
    o_ref[0, 0] = _dot(_gelu_tanh(hid).astype(BF16), w2_ref[0]).astype(o_ref.dtype)


def _compress(kvc, p, batch, seq):
    T = kvc.shape[1]
    rows = seq // CMP_STRIDE
    x = kvc.reshape(4, T // CMP_STRIDE, CMP_STRIDE * HEAD_DIM)
    pe = jnp.stack([p["cmp_pe_k"].reshape(1, -1), p["cmp_pe_v"].reshape(1, -1)]).astype(F32)
    w1 = jnp.stack([p["cmp_w1_k"], p["cmp_w1_v"]]).astype(BF16)
    w2k = jnp.pad(p["cmp_w2_k"], ((0, 0), (0, LANES - HEAD_DIM)))
    w2v = jnp.concatenate([p["cmp_w2_v"], p["cmp_w2_v"]], axis=1)
    w2 = jnp.stack([w2k, w2v]).astype(BF16)
    return pl.pallas_call(
        _compress_kernel, grid=(4, batch),
        in_specs=[pl.BlockSpec((1, rows, x.shape[2]), lambda j, b: (j, b, 0)),
                  pl.BlockSpec((1,) + pe.shape[1:], lambda j, b: (j // 2, 0, 0)),
                  pl.BlockSpec((1,) + w1.shape[1:], lambda j, b: (j // 2, 0, 0)),
                  pl.BlockSpec((1,) + w2.shape[1:], lambda j, b: (j // 2, 0, 0))],
        out_specs=pl.BlockSpec((1, 1, rows, LANES), lambda j, b: (b, j, 0, 0)),
        out_shape=jax.ShapeDtypeStruct((batch, 4, rows, LANES), BF16),
        compiler_params=_cparams(2), name="nsa_compress",
    )(x, pe, w1, w2)


def _cmp_select_kernel(qn_ref, qr_ref, kc_ref, vc_ref, ovt_ref, o_ref, qa_ref, *, tq):
    i = pl.program_id(2)
    n_cmp = kc_ref.shape[2]
    kc, vc = kc_ref[0, 0], vc_ref[0, 0]
    t_row = i * tq + lax.broadcasted_iota(jnp.int32, (tq, n_cmp), 0)
    blk_end = CMP_STRIDE * lax.broadcasted_iota(jnp.int32, (tq, n_cmp), 1) + (CMP_LEN - 1)
    vis = blk_end <= t_row
    lane = lax.broadcasted_iota(jnp.int32, (tq, LANES), 1)
    sees_any = t_row[:, :1] >= CMP_LEN - 1
    psum = jnp.zeros((tq, n_cmp), F32)
    outs = []
    for r in range(NSA_REP):
        s = jnp.where(vis, _dot_nt(qn_ref[:, r * LANES:(r + 1) * LANES], kc), NEG)
        m = jnp.max(s, axis=-1, keepdims=True)
        e = jnp.exp2(s - m)
        pn = e * jnp.where(sees_any, 1.0 / jnp.sum(e, axis=-1, keepdims=True), 0.0)
        psum = psum + pn
        outs.append(_dot(pn.astype(BF16), vc))
    for k in range(NSA_REP // 2):
        o_ref[:, k * LANES:(k + 1) * LANES] = jnp.where(lane < HEAD_DIM, outs[2 * k], outs[2 * k + 1])

    p_hi = psum.astype(BF16)
    p_lo = (psum - p_hi.astype(F32)).astype(BF16)
    ovt = ovt_ref[...]
    imp = _dot_nt(ovt, p_hi) + _dot_nt(ovt, p_lo)
    n_slc = LANES - HEAD_DIM
    jb = lax.broadcasted_iota(jnp.int32, (LANES, tq), 0) - HEAD_DIM
    cur = (i * tq + lax.broadcasted_iota(jnp.int32, (LANES, tq), 1)) // SLC_LEN
    forced = (jb == 0) | (jb == cur) | (jb == cur - 1)
    imp = jnp.where(forced, 1e9, jnp.where(jb > cur, -1e9, imp))
    groups = [imp[HEAD_DIM + 8 * v:HEAD_DIM + 8 * (v + 1)] for v in range(n_slc // 8)]
    sub = lax.broadcasted_iota(jnp.int32, (8, tq), 0)
    ranks = [jnp.zeros((8, tq), F32) for _ in groups]
    for a in range(n_slc):
        row = jnp.broadcast_to(groups[a // 8][a % 8:a % 8 + 1], (8, tq))
        for v in range(len(groups)):
            if v > a // 8:
                inc = jnp.where(row >= groups[v], 1.0, 0.0)
            elif v < a // 8:
                inc = jnp.where(row > groups[v], 1.0, 0.0)
            else:
                tie = jnp.where(sub > a % 8, 1.0, 0.0)
                inc = jnp.where(row > groups[v], 1.0, jnp.where(row == groups[v], tie, 0.0))
            ranks[v] = ranks[v] + inc
    bias_t = jnp.concatenate(
        [jnp.zeros((HEAD_DIM, tq), F32)]
        + [jnp.where(rk < SLC_TOPK, 0.0, SEL_NEG) for rk in ranks], axis=0)
    bias = bias_t.T
    for r in range(NSA_REP):
        qa_ref[:, r * LANES:(r + 1) * LANES] = (
            qr_ref[:, r * LANES:(r + 1) * LANES].astype(F32) + bias).astype(BF16)


def _overlap_t(seq):
    n_cmp = (seq - CMP_LEN) // CMP_STRIDE + 1
    n_slc = seq // SLC_LEN
    assert n_slc == LANES - HEAD_DIM, "selection blocks must fill the bias half of a slab"
    c0 = CMP_STRIDE * np.arange(n_cmp)[:, None]
    s0 = SLC_LEN * np.arange(n_slc)[None, :]
    ov = np.clip(np.minimum(c0 + CMP_LEN, s0 + SLC_LEN) - np.maximum(c0, s0), 0, None) / CMP_LEN
    out = np.zeros((LANES, seq // CMP_STRIDE), np.float32)
    out[HEAD_DIM:, :n_cmp] = ov.T
    return jnp.asarray(out, BF16)


def _cmp_select(qn, qr, cmp_kv, batch, seq):
    T = qn.shape[0]
    nq = seq // TQ_SELECT
    rows = seq // CMP_STRIDE
    gw = NSA_REP * LANES
    qspec = pl.BlockSpec((TQ_SELECT, gw), lambda b, g, i: (b * nq + i, g))
    return pl.pallas_call(
        functools.partial(_cmp_select_kernel, tq=TQ_SELECT), grid=(batch, NSA_GROUPS, nq),
        in_specs=[qspec, qspec,
                  pl.BlockSpec((1, 1, rows, LANES), lambda b, g, i: (b, g, 0, 0)),
                  pl.BlockSpec((1, 1, rows, LANES), lambda b, g, i: (b, NSA_GROUPS + g, 0, 0)),
                  _const_spec((LANES, rows))],
        out_specs=[pl.BlockSpec((TQ_SELECT, NSA_REP * HEAD_DIM), lambda b, g, i: (b * nq + i, g)), qspec],
        out_shape=[jax.ShapeDtypeStruct((T, NSA_HEADS * HEAD_DIM), F32),
                   jax.ShapeDtypeStruct((T, NSA_HEADS * LANES), BF16)],
        compiler_params=_cparams(3), name="nsa_cmp_select",
    )(qn, qr, cmp_kv, cmp_kv, _overlap_t(seq))


def _merge_kernel(x_ref, oc_ref, os_ref, ow_ref, gl_ref, of_ref, om_ref, mg_ref,
                  e_ref, wn_ref, wf_ref, wm_ref, wo_ref, o_ref):
    d = x_ref.shape[1]
    nw = oc_ref.shape[1]
    mg = mg_ref[...]
    merged = mg[:, d:2 * d] * _dot(of_ref[...], wf_ref[...]) + mg[:, 2 * d:] * _dot(om_ref[...], wm_ref[...])
    g = _dot(jax.nn.sigmoid(gl_ref[...]).astype(BF16), e_ref[...])
    o_nsa = g[:, :nw] * oc_ref[...] + g[:, nw:2 * nw] * os_ref[...] + g[:, 2 * nw:] * ow_ref[...]
    merged = merged + mg[:, :d] * _dot(o_nsa.astype(BF16), wn_ref[...])
    o_ref[...] = x_ref[...] + _dot(merged.astype(BF16), wo_ref[...])


def _gate_expand():
    e = np.zeros((LANES, 3 * NSA_HEADS * HEAD_DIM), np.float32)
    for h in range(NSA_HEADS):
        for c in range(3):
            e[h * 3 + c, c * NSA_HEADS * HEAD_DIM + h * HEAD_DIM:
              c * NSA_HEADS * HEAD_DIM + (h + 1) * HEAD_DIM] = 1.0
    return jnp.asarray(e, BF16)


def _merge(x2, o_cmp, o_slc, o_win, gate_l, o_fox, o_mla, mg, p):
    T, D = x2.shape
    tok = lambda a: pl.BlockSpec((TM, a.shape[1]), lambda i: (i, 0))
    acts = [x2, o_cmp, o_slc, o_win, gate_l, o_fox, o_mla, mg]
    weights = [_gate_expand(), p["w_br_nsa"].astype(BF16), p["w_br_fox"].astype(BF16),
               p["w_br_mla"].astype(BF16), p["w_out"].astype(BF16)]
    return pl.pallas_call(
        _merge_kernel, grid=(T // TM,),
        in_specs=[tok(a) for a in acts] + [_const_spec(w.shape) for w in weights],
        out_specs=pl.BlockSpec((TM, D), lambda i: (i, 0)),
        out_shape=jax.ShapeDtypeStruct((T, D), F32),
        compiler_params=_cparams(1), name="merge_out",
    )(*acts, *weights)


def _ffn_kernel(x_ref, g_ref, wu_ref, wv_ref, cw_ref, cb_ref, wd_ref, fin_ref, o_ref, tail_ref,
                *, seq, n_chunks, final):
    tm = x_ref.shape[0]
    x = x_ref[...]
    hb = _rms(x, g_ref[...]).astype(BF16)

    @pl.when((pl.program_id(0) * tm) % seq == 0)
    def _():
        tail_ref[...] = jnp.zeros(tail_ref.shape, F32)

    cw = wu_ref.shape[1] // n_chunks
    row8 = lax.broadcasted_iota(jnp.int32, (8, cw), 0)
    acc = jnp.zeros(x.shape, F32)
    for c in range(n_chunks):
        sl = slice(c * cw, (c + 1) * cw)
        u = _dot(hb, wu_ref[:, sl])
        v = _dot(hb, wv_ref[:, sl])
        prev = tail_ref[:, sl]
        tail_ref[:, sl] = u[tm - 8:]
        u1 = pltpu.roll(u, 1, 0)
        u2 = pltpu.roll(u, 2, 0)
        h1 = jnp.where(row8 < 1, pltpu.roll(prev, 1, 0), u1[:8])
        h2 = jnp.where(row8 < 2, pltpu.roll(prev, 2, 0), u2[:8])
        u1 = jnp.concatenate([h1, u1[8:]], axis=0)
        u2 = jnp.concatenate([h2, u2[8:]], axis=0)
        uc = cw_ref[0:1, sl] * u2 + cw_ref[1:2, sl] * u1 + cw_ref[2:3, sl] * u + cb_ref[:, sl]
        act = (uc * jax.nn.sigmoid(uc) * v).astype(BF16)
        acc = acc + _dot(act, wd_ref[sl, :])
    y = x + acc
    if final:
        y = _rms(y, fin_ref[...])
    o_ref[...] = y


def _ffn(x2, p, final_norm, seq, final):
    T, D = x2.shape
    w_up = p["w_up"]
    wu, wv = w_up[:, :D_FF].astype(BF16), w_up[:, D_FF:].astype(BF16)
    row = lambda a: a.reshape(1, -1).astype(F32)
    cwp = jnp.pad(p["conv_w"].astype(F32), ((0, 8 - CONV_W), (0, 0)))
    weights = [row(p["ffn_norm"]), wu, wv, cwp, row(p["conv_b"]), p["w_down"].astype(BF16),
               row(final_norm)]
    return pl.pallas_call(
        functools.partial(_ffn_kernel, seq=seq, n_chunks=2, final=final), grid=(T // TM,),
        in_specs=[pl.BlockSpec((TM, D), lambda i: (i, 0))] + [_const_spec(w.shape) for w in weights],
        out_specs=pl.BlockSpec((TM, D), lambda i: (i, 0)),
        out_shape=jax.ShapeDtypeStruct((T, D), F32),
        scratch_shapes=[pltpu.VMEM((8, D_FF), F32)],
        compiler_params=_cparams(1), name="conv_ffn",
    )(x2, *weights)


_LAYER_PARAMS = ("mix_norm", "w_in", "b_forget", "cmp_pe_k", "cmp_w1_k", "cmp_w2_k", "cmp_pe_v",
                 "cmp_w1_v", "cmp_w2_v", "mla_q_norm", "mla_w_uq", "mla_kv_norm", "mla_w_ukv",
                 "w_br_nsa", "w_br_fox", "w_br_mla", "w_out", "ffn_norm", "w_up", "conv_w",
                 "conv_b", "w_down")


def _mixer_layer(h, tabs, p, batch, seq):
    (qn, qr, kvc, ksl, vsl_t, kwin, vwin_t, gate_l,
     fq, fk, fv_t, mq, mk, mv_t, mg) = _inproj(h, tabs, p, seq)
    cmp_kv = _compress(kvc, p, batch, seq)
    o_cmp, qa = _cmp_select(qn, qr, cmp_kv, batch, seq)
    group = (0,) * NSA_REP
    pair = (0, 1)
    o_slc, o_win = _attention([(qa, ksl, vsl_t, False), (qa, kwin, vwin_t, True)], batch, seq, group,
                              F32, "nsa_slc_win_attn")
    o_fox, o_mla = _attention([(fq, fk, fv_t, False), (mq, mk, mv_t, False)], batch, seq, pair, BF16,
                              "fox_mla_attn")
    return _merge(h, o_cmp, o_slc, o_win, gate_l, o_fox, o_mla, mg, p)


def kernel(x, positions, mix_norm, w_in, b_forget, cmp_pe_k, cmp_w1_k, cmp_w2_k, cmp_pe_v, cmp_w1_v,
           cmp_w2_v, mla_q_norm, mla_w_uq, mla_kv_norm, mla_w_ukv, w_br_nsa, w_br_fox, w_br_mla,
           w_out, ffn_norm, w_up, conv_w, conv_b, w_down, final_norm):
    batch, seq, d = x.shape
    stacked = dict(zip(_LAYER_PARAMS, (
        mix_norm, w_in, b_forget, cmp_pe_k, cmp_w1_k, cmp_w2_k, cmp_pe_v, cmp_w1_v, cmp_w2_v,
        mla_q_norm, mla_w_uq, mla_kv_norm, mla_w_ukv, w_br_nsa, w_br_fox, w_br_mla, w_out,
        ffn_norm, w_up, conv_w, conv_b, w_down)))
    depth = w_in.shape[0]
    tabs = _rope_tables(positions)
    h = x.reshape(batch * seq, d)
    for l in range(depth):
        p = {k: v[l] for k, v in stacked.items()}
        h = _mixer_layer(h, tabs, p, batch, seq)
        h = _ffn(h, p, final_norm, seq, final=(l == depth - 1))
    return h.reshape(batch, seq, d)
```

```python
import functools

import numpy as np
import jax
import jax.numpy as jnp
from jax import lax
from jax.experimental import pallas as pl
from jax.experimental.pallas import tpu as pltpu

F32 = jnp.float32
BF16 = jnp.bfloat16

HEAD_DIM = 64
ROPE_THETA = 500000.0
PARTIAL_ROT = HEAD_DIM // 4
NORM_EPS = 1e-6
NSA_HEADS = 8
NSA_GROUPS = 2
NSA_REP = NSA_HEADS // NSA_GROUPS
CMP_LEN = 32
CMP_STRIDE = 16
SLC_LEN = 64
SLC_TOPK = 16
WIN = 512
FOX_HEADS = 8
MLA_HEADS = 8
MLA_Q_LORA = 384
MLA_KV_LORA = 256
MLA_NOPE = 64
MLA_ROPE = 32
MLA_V = 64
D_FF = 2816
D_MODEL = 1024
CONV_W = 3

LANES = 128
LOG2E = 1.4426950408889634
NEG = -1e30
SEL_NEG = -1e9
VMEM_LIMIT = 52 * 1024 * 1024

TM = 256
TQ = 1024
TQ_SELECT = 512
TK = 512
QC = 256
LOOP_TILES = 4
SKEW = 2
DECAY_TERMS = 3


def _cparams(n_axes):
    return pltpu.CompilerParams(dimension_semantics=("arbitrary",) * n_axes,
                                vmem_limit_bytes=VMEM_LIMIT)


def _const_spec(shape):
    nd = len(shape)
    return pl.BlockSpec(shape, lambda *_: (0,) * nd, pipeline_mode=pl.Buffered(1))


def _rms(x, g):
    return x * lax.rsqrt(jnp.mean(x * x, axis=-1, keepdims=True) + NORM_EPS) * g


def _dot(a, b):
    return jnp.dot(a, b, preferred_element_type=F32)


def _dot_nt(a, b):
    return lax.dot_general(a, b, (((1,), (1,)), ((), ())), preferred_element_type=F32)


def _rope_tables_kernel(pos_ref, freq_ref, g16_ref, g32_ref, c16_o, s16_o, c32_o, s32_o):
    ang = pos_ref[...].astype(F32) * freq_ref[...]
    cos, sin = jnp.cos(ang), jnp.sin(ang)
    lane = lax.broadcasted_iota(jnp.int32, ang.shape, 1)
    low = lane < PARTIAL_ROT
    high = (lane >= HEAD_DIM) & (lane < HEAD_DIM + PARTIAL_ROT)
    mla = (lane >= MLA_NOPE) & (lane < MLA_NOPE + MLA_ROPE)
    c16_o[...] = jnp.where(low, cos, jnp.where(high, pltpu.roll(cos, HEAD_DIM, 1), 1.0))
    s16_o[...] = jnp.where(low, sin, jnp.where(high, pltpu.roll(sin, HEAD_DIM, 1), 0.0)) * g16_ref[...]
    c32_o[...] = jnp.where(mla, pltpu.roll(cos, MLA_ROPE, 1), 1.0)
    s32_o[...] = jnp.where(mla, pltpu.roll(sin, MLA_ROPE, 1), 0.0) * g32_ref[...]


def _rope_tables(positions):
    T = positions.size
    half16, half32 = PARTIAL_ROT // 2, MLA_ROPE // 2
    inv16 = ROPE_THETA ** (-jnp.arange(half16, dtype=F32) / half16)
    inv32 = ROPE_THETA ** (-jnp.arange(half32, dtype=F32) / half32)
    lane = np.arange(LANES)
    freq = jnp.where(lane < PARTIAL_ROT, inv16[lane % half16],
                     jnp.where((lane >= MLA_ROPE) & (lane < 2 * MLA_ROPE), inv32[lane % half32], 0.0))[None, :]
    d = lane % HEAD_DIM
    g16 = np.where(d < PARTIAL_ROT, np.where(d < half16, -1.0, 1.0), 0.0).astype(np.float32)[None, :]
    r = lane - MLA_NOPE
    g32 = np.where((r >= 0) & (r < MLA_ROPE), np.where(r < half32, -1.0, 1.0), 0.0).astype(np.float32)[None, :]
    pos = jnp.broadcast_to(positions.reshape(T, 1), (T, LANES))
    tt = 1024
    row = pl.BlockSpec((tt, LANES), lambda i: (i, 0))
    vec = _const_spec((1, LANES))
    out = jax.ShapeDtypeStruct((T, LANES), F32)
    return pl.pallas_call(
        _rope_tables_kernel, grid=(T // tt,),
        in_specs=[row, vec, vec, vec], out_specs=[row] * 4, out_shape=[out] * 4,
        compiler_params=_cparams(1), name="rope_tables",
    )(pos, freq, jnp.asarray(g16), jnp.asarray(g32))


def _rot16(x, cos, sin):
    lane = lax.broadcasted_iota(jnp.int32, x.shape, 1)
    first = (lane % HEAD_DIM) < (PARTIAL_ROT // 2)
    sw = jnp.where(first, pltpu.roll(x, LANES - PARTIAL_ROT // 2, 1), pltpu.roll(x, PARTIAL_ROT // 2, 1))
    return x * cos + sw * sin


def _rot32(x, cos, sin):
    lane = lax.broadcasted_iota(jnp.int32, x.shape, 1)
    first = lane < (MLA_NOPE + MLA_ROPE // 2)
    sw = jnp.where(first, pltpu.roll(x, LANES - MLA_ROPE // 2, 1), pltpu.roll(x, MLA_ROPE // 2, 1))
    return x * cos + sw * sin


_FF_LANE = 3 * NSA_HEADS
_IN_COLS = {}
for _name, _n in (("q", NSA_HEADS * HEAD_DIM), ("kvc", 4 * HEAD_DIM), ("ksl", LANES), ("kwin", LANES),
                  ("misc", LANES), ("fq", FOX_HEADS * HEAD_DIM), ("fk", FOX_HEADS * HEAD_DIM),
                  ("lat", MLA_Q_LORA + MLA_KV_LORA), ("mg", 3 * D_MODEL)):
    _IN_COLS[_name] = (sum(n for _, n in _IN_COLS.values()), _n)
_IN_ROWS = {"vsl": (0, LANES), "vwin": (LANES, LANES), "fv": (2 * LANES, FOX_HEADS * HEAD_DIM)}


def _inproj_kernel(x_ref, g_ref, c16_ref, s16_ref, c32_ref, s32_ref,
                   w_ref, wt_ref, bf_ref, tri_ref, place_ref,
                   qnrm_ref, wuq_ref, kvnrm_ref, wuk_ref, wuv_ref,
                   qn_o, qr_o, kvc_o, ksl_o, vsl_o, kwin_o, vwin_o, gate_o,
                   fq_o, fk_o, fv_o, mq_o, mk_o, mv_o, mg_o, fcarry_ref, *, seq):
    tm = x_ref.shape[0]
    seq_start = (pl.program_id(0) * tm) % seq == 0
    hb = _rms(x_ref[...], g_ref[...]).astype(BF16)
    c16, s16 = c16_ref[...], s16_ref[...]
    c32, s32 = c32_ref[...], s32_ref[...]
    lane = lax.broadcasted_iota(jnp.int32, (tm, LANES), 1)
    low = lane < HEAD_DIM

    def proj(name):
        a, n = _IN_COLS[name]
        return _dot(hb, w_ref[:, a:a + n])

    def proj_t(name):
        a, n = _IN_ROWS[name]
        return _dot_nt(wt_ref[a:a + n, :], hb)

    def slabs(z):
        return [z[:, k * LANES:(k + 1) * LANES] for k in range(z.shape[1] // LANES)]

    def split(z, fill=0.0):
        return jnp.where(low, z, fill), jnp.where(low, pltpu.roll(z, HEAD_DIM, 1), fill)

    def store_heads(o_ref, pair_slabs, fill=0.0, extra=None):
        for k, z in enumerate(pair_slabs):
            for e, s in enumerate(split(z, fill)):
                h = 2 * k + e
                if extra is not None:
                    s = s + extra[:, h * LANES:(h + 1) * LANES]
                o_ref[:, h * LANES:(h + 1) * LANES] = s.astype(o_ref.dtype)

    misc = proj("misc")
    gate_o[...] = misc

    @pl.when(seq_start)
    def _():
        fcarry_ref[...] = jnp.zeros(fcarry_ref.shape, F32)

    def bf16_terms(v):
        packed = None
        for t in range(DECAY_TERMS):
            part = v.astype(BF16).astype(F32)
            v = v - part
            part = part if t == 0 else pltpu.roll(part, FOX_HEADS * t, 1)
            packed = part if packed is None else packed + part
        return packed.astype(BF16)

    zf = misc + bf_ref[...]
    is_f = (lane >= _FF_LANE) & (lane < _FF_LANE + FOX_HEADS)
    logf_terms = bf16_terms(
        jnp.where(is_f, jnp.minimum(zf, 0.0) - jnp.log1p(jnp.exp(-jnp.abs(zf))), 0.0))

    zq = slabs(proj("q"))
    store_heads(qn_o, zq)
    store_heads(qr_o, [_rot16(z, c16, s16) for z in zq])

    terms = _dot(tri_ref[...], logf_terms)
    fsum = sum(terms if t == 0 else pltpu.roll(terms, LANES - FOX_HEADS * t, 1)
               for t in range(DECAY_TERMS))
    fsum = jnp.where(is_f, fsum, 0.0) + fcarry_ref[0:1]
    fcarry_ref[...] = jnp.broadcast_to(fsum[tm - 1:tm], fcarry_ref.shape)
    decay_terms = bf16_terms(jnp.where(is_f, -LOG2E * fsum, 0.0))

    for k, z in enumerate(slabs(proj("kvc"))):
        for e, s in enumerate(split(z)):
            kvc_o[2 * k + e] = s[:, :HEAD_DIM]

    lat = proj("lat")
    cq = _rms(lat[:, :MLA_Q_LORA], qnrm_ref[...]).astype(BF16)
    ckv = _rms(lat[:, MLA_Q_LORA:], kvnrm_ref[...]).astype(BF16)

    tok = (pl.program_id(0) * tm) % seq + lax.broadcasted_iota(jnp.int32, (tm, LANES), 0)
    onehot = jnp.where(lane - HEAD_DIM == tok // SLC_LEN, 1.0, 0.0)
    store_heads(ksl_o, [_rot16(proj("ksl"), c16, s16)], fill=onehot)

    store_heads(kwin_o, [_rot16(proj("kwin"), c16, s16)])

    mg_o[...] = jax.nn.sigmoid(proj("mg"))

    ones = jnp.where((lane >= HEAD_DIM) & (lane < HEAD_DIM + DECAY_TERMS), 1.0, 0.0)
    store_heads(fq_o, slabs(proj("fq")), fill=ones)
    decay = _dot(decay_terms, place_ref[...])

    kpe = jnp.where(low, 0.0, _rot32(misc, c32, s32))
    store_heads(fk_o, slabs(proj("fk")), extra=decay)
    zuq = _dot(cq, wuq_ref[...])
    for k, z in enumerate(slabs(zuq)):
        mq_o[:, k * LANES:(k + 1) * LANES] = _rot32(z, c32, s32).astype(BF16)
    zuk = _dot(ckv, wuk_ref[...])
    for k, z in enumerate(slabs(zuk)):
        mk_o[:, k * LANES:(k + 1) * LANES] = (z + kpe).astype(BF16)
    mv_o[0] = _dot_nt(wuv_ref[...], ckv).astype(BF16)

    vsl_o[0] = proj_t("vsl").astype(BF16)
    vwin_o[0] = proj_t("vwin").astype(BF16)
    fv_o[0] = proj_t("fv").astype(BF16)


def _pad_halves(w, first_only=True):
    K, n = w.shape[0], w.shape[1] // HEAD_DIM
    w = w.reshape(K, n, HEAD_DIM)
    other = jnp.zeros_like(w) if first_only else w
    return jnp.concatenate([w, other], axis=-1).reshape(K, n * LANES)


def _inproj(x2, tabs, p, seq):
    T, D = x2.shape
    w_in = p["w_in"]
    scale = HEAD_DIM ** -0.5 * LOG2E
    o = 0

    def take(n):
        nonlocal o
        w = w_in[:, o:o + n]
        o += n
        return w

    nsa_w, kvw = NSA_HEADS * HEAD_DIM, NSA_GROUPS * HEAD_DIM
    w_q = take(nsa_w)
    w_kc, w_vc, w_ks, w_vs, w_kw, w_vw = [take(kvw) for _ in range(6)]
    w_g = take(3 * NSA_HEADS)
    w_fq, w_fk, w_fv = [take(FOX_HEADS * HEAD_DIM) for _ in range(3)]
    w_ff = take(FOX_HEADS)
    w_cq, w_ckv, w_kr = take(MLA_Q_LORA), take(MLA_KV_LORA), take(MLA_ROPE)
    w_mg = take(3 * D)

    bf = lambda w: w.astype(BF16)
    row = lambda a: a.reshape(1, -1).astype(F32)
    w_misc = jnp.concatenate([w_g, w_ff, jnp.zeros((D, MLA_NOPE - _FF_LANE - FOX_HEADS), F32), w_kr,
                              jnp.zeros((D, LANES - MLA_NOPE - MLA_ROPE), F32)], axis=1)
    w_tok = bf(jnp.concatenate([w_q * scale, w_kc, w_vc, w_ks, w_kw, w_misc, w_fq * scale, w_fk,
                                w_cq, w_ckv, w_mg], axis=1))
    w_tr = bf(jnp.concatenate([w_vs, w_vw, w_fv], axis=1).T)
    b_f = jnp.pad(row(p["b_forget"]), ((0, 0), (_FF_LANE, LANES - _FF_LANE - FOX_HEADS)))
    tri = jnp.asarray(np.tril(np.ones((TM, TM), np.float32)), BF16)
    place = np.zeros((LANES, FOX_HEADS * LANES), np.float32)
    for t in range(DECAY_TERMS):
        for h in range(FOX_HEADS):
            place[_FF_LANE + FOX_HEADS * t + h, h * LANES + HEAD_DIM + t] = 1.0
    place = jnp.asarray(place, BF16)
    dq = MLA_NOPE + MLA_ROPE
    wuq = (p["mla_w_uq"] * (dq ** -0.5 * LOG2E)).reshape(MLA_Q_LORA, MLA_HEADS, dq)
    wuq = bf(jnp.pad(wuq, ((0, 0), (0, 0), (0, LANES - dq))).reshape(MLA_Q_LORA, MLA_HEADS * LANES))
    wukv = p["mla_w_ukv"].reshape(MLA_KV_LORA, MLA_HEADS, MLA_NOPE + MLA_V)
    wuk = bf(_pad_halves(wukv[:, :, :MLA_NOPE].reshape(MLA_KV_LORA, MLA_HEADS * MLA_NOPE)))
    wuv = bf(wukv[:, :, MLA_NOPE:].reshape(MLA_KV_LORA, MLA_HEADS * MLA_V).T)
    assert w_tok.shape[1] == sum(n for _, n in _IN_COLS.values())

    weights = [w_tok, w_tr, b_f, tri, place,
               row(p["mla_q_norm"]), wuq, row(p["mla_kv_norm"]), wuk, wuv]
    n = T // TM
    tok = lambda c: pl.BlockSpec((TM, c), lambda i: (i, 0))
    in_specs = ([tok(D), _const_spec((1, D))] + [tok(LANES)] * 4
                + [_const_spec(w.shape) for w in weights])
    tokens, transposed = "tokens", "transposed"
    outs = [
        (tokens, NSA_HEADS * LANES, BF16), (tokens, NSA_HEADS * LANES, BF16), None,
        (tokens, NSA_GROUPS * LANES, BF16), (transposed, NSA_GROUPS * HEAD_DIM, BF16),
        (tokens, NSA_GROUPS * LANES, BF16), (transposed, NSA_GROUPS * HEAD_DIM, BF16), (tokens, LANES, F32),
        (tokens, FOX_HEADS * LANES, BF16), (tokens, FOX_HEADS * LANES, BF16),
        (transposed, FOX_HEADS * HEAD_DIM, BF16),
        (tokens, MLA_HEADS * LANES, BF16), (tokens, MLA_HEADS * LANES, BF16),
        (transposed, MLA_HEADS * MLA_V, BF16), (tokens, 3 * D, F32)]
    out_shape, out_specs = [], []
    for spec in outs:
        if spec is None:
            out_shape.append(jax.ShapeDtypeStruct((4, T, HEAD_DIM), F32))
            out_specs.append(pl.BlockSpec((4, TM, HEAD_DIM), lambda i: (0, i, 0)))
        elif spec[0] == transposed:
            out_shape.append(jax.ShapeDtypeStruct((n, spec[1], TM), spec[2]))
            out_specs.append(pl.BlockSpec((1, spec[1], TM), lambda i: (i, 0, 0)))
        else:
            out_shape.append(jax.ShapeDtypeStruct((T, spec[1]), spec[2]))
            out_specs.append(tok(spec[1]))
    return pl.pallas_call(
        functools.partial(_inproj_kernel, seq=seq), grid=(n,),
        in_specs=in_specs, out_specs=out_specs, out_shape=out_shape,
        scratch_shapes=[pltpu.VMEM((8, LANES), F32)],
        compiler_params=_cparams(1), name="inproj",
    )(x2, row(p["mix_norm"]), *tabs, *weights)


def _attn_kernel(*refs, tq, tk, q_of_stream, window_of_stream, k_of_head, values_by_unit):
    n_streams, n_q = len(q_of_stream), max(q_of_stream) + 1
    q_refs, kv_refs = refs[:n_q], refs[n_q:n_q + 2 * n_streams]
    outs = refs[n_q + 2 * n_streams:n_q + 3 * n_streams]
    qt_ref, m_ref, l_ref, acc_ref = refs[n_q + 3 * n_streams:]
    i = pl.program_id(2)
    hs = len(k_of_head)
    nh = n_streams * hs
    k_of = lambda h: kv_refs[2 * (h // hs)]
    vt_of = lambda h: kv_refs[2 * (h // hs) + 1]
    qt_of = lambda h: q_of_stream[h // hs] * hs + h % hs
    per_q = tq // tk
    tv = kv_refs[1].shape[2]
    v_sub = tk // tv
    for qi in range(n_q):
        for r in range(hs):
            qt_ref[qi * hs + r] = q_refs[qi][:, r * LANES:(r + 1) * LANES].astype(F32).T.astype(BF16)
    dist0 = (i * tq + lax.broadcasted_iota(jnp.int32, (tk, tq), 1)
             - lax.broadcasted_iota(jnp.int32, (tk, tq), 0))

    m_ref[...] = jnp.full(m_ref.shape, NEG, F32)
    l_ref[...] = jnp.zeros(l_ref.shape, F32)
    acc_ref[...] = jnp.zeros(acc_ref.shape, F32)

    def logits(j, lo, n, h, cols):
        r0 = pl.multiple_of(j * tk + lo, tv)
        kk = k_of_head[h % hs]
        return _dot(k_of(h)[pl.ds(r0, n), kk * LANES:(kk + 1) * LANES], qt_ref[qt_of(h), :, cols])

    def softmax(s, h, cols, mask):
        if mask is not None:
            s = jnp.where(mask, s, NEG)
        m_old = m_ref[h, :, cols]
        m_new = jnp.maximum(m_old, jnp.max(s, axis=0, keepdims=True))
        alpha = jnp.exp2(m_old - m_new)
        p = jnp.exp2(s - m_new)
        l_ref[h, :, cols] = alpha * l_ref[h, :, cols] + jnp.sum(p, axis=0, keepdims=True)
        m_ref[h, :, cols] = m_new
        return p.astype(BF16), alpha

    def accumulate(j, lo, n, h, cols, p, alpha):
        if values_by_unit:
            half = pl.ds(pl.multiple_of(pl.program_id(1) * HEAD_DIM, HEAD_DIM), HEAD_DIM)
        else:
            half = slice((h % 2) * HEAD_DIM, (h % 2 + 1) * HEAD_DIM)
        pv = sum(_dot(vt_of(h)[j * v_sub + lo // tv + c, half, :], p[c * tv:(c + 1) * tv])
                 for c in range(n // tv))
        acc_ref[h, :, cols] = alpha * acc_ref[h, :, cols] + pv

    def sweep(tiles):
        work = []
        for heads, j, n_tiles, mask, span in tiles:
            for h in heads:
                for c in range(tq // QC):
                    lo, n, masked = (0, n_tiles * tk, False) if span is None else span(c)
                    if n > 0:
                        work.append((j, lo, n, mask if masked else None, h, slice(c * QC, (c + 1) * QC)))
        s_of, p_of = {}, {}
        for t in range(len(work) + 2 * SKEW):
            if 0 <= t - 2 * SKEW < len(work):
                j, lo, n, _, h, cols = work[t - 2 * SKEW]
                accumulate(j, lo, n, h, cols, *p_of.pop(t - 2 * SKEW))
            if 0 <= t - SKEW < len(work):
                _, lo, n, mask, h, cols = work[t - SKEW]
                p_of[t - SKEW] = softmax(s_of.pop(t - SKEW), h, cols,
                                         None if mask is None else mask(slice(lo, lo + n), cols))
            if t < len(work):
                j, lo, n, _, h, cols = work[t]
                s_of[t] = logits(j, lo, n, h, cols)

    def visible(j, window, valid=None):
        def mask(keys, cols):
            dist = dist0[keys, cols] - j * tk
            if not window:
                return dist >= 0
            if valid is not None:
                dist = dist + jnp.where(valid, 0, WIN)
            return lax.bitcast_convert_type(dist, jnp.uint32) < WIN
        return mask

    assert tk == 2 * QC and WIN % tk == 0 and LOOP_TILES % per_q == 0
    causal = [h for h in range(nh) if not window_of_stream[h // hs]]
    windowed = [h for h in range(nh) if window_of_stream[h // hs]]
    first = i * per_q

    def causal_span(d):
        def span(c):
            hi = min(tk, (c + 1) * QC - d * tk)
            return 0, hi, d * tk + hi > c * QC + 1
        return span

    def window_span(d):
        def span(c):
            lo = max(0, (c * QC - WIN + 1) // tv * tv - d * tk)
            return lo, min(tk, (c + 1) * QC - d * tk) - lo, True
        return span

    one_step = len(causal) * (tq // QC) >= 8

    def unmasked(j0, n):
        tiles = []
        while n > 0:
            k = 2 if (one_step and n >= 2) else 1
            tiles.append((causal, j0, k, None, None))
            j0, n = j0 + k, n - k
        return tiles

    def last_tiles(n_left):
        tiles = []
        if causal:
            tiles += unmasked(first - n_left, n_left)
            tiles += [(causal, first + d, 1, visible(first + d, False), causal_span(d)) for d in range(per_q)]
        if windowed:
            tiles += [(windowed, first + d, 1, visible(first + d, True), window_span(d))
                      for d in reversed(range(per_q))]
            tiles += [(windowed, jnp.maximum(first + d, 0), 1,
                       visible(first + d, True, valid=first + d >= 0), window_span(d))
                      for d in range(-(WIN // tk), 0)]
        return tiles

    if causal:
        def body(jj, c):
            sweep(unmasked(LOOP_TILES * jj, LOOP_TILES))
            return c
        lax.fori_loop(0, first // LOOP_TILES, body, 0)
        for n_left in range(0, LOOP_TILES, per_q):
            @pl.when(first % LOOP_TILES == n_left)
            def _():
                sweep(last_tiles(n_left))
    else:
        sweep(last_tiles(0))

    for h in range(0, nh, 2):
        a = acc_ref[h] * (1.0 / l_ref[h])
        b = acc_ref[h + 1] * (1.0 / l_ref[h + 1])
        o_ref, k = outs[h // hs], (h % hs) // 2
        o_ref[:, k * LANES:(k + 1) * LANES] = jnp.concatenate([a, b], axis=0).T.astype(o_ref.dtype)


def _attention(streams, batch, seq, k_of_head, out_dtype, name):
    qs = []
    for q, _, _, _ in streams:
        if not any(q is x for x in qs):
            qs.append(q)
    q_of_stream = tuple(next(n for n, x in enumerate(qs) if x is q) for q, _, _, _ in streams)
    T = qs[0].shape[0]
    nq = seq // TQ
    hs, nk = len(k_of_head), max(k_of_head) + 1
    nh = hs * len(streams)
    units = qs[0].shape[1] // (hs * LANES)
    by_unit = nk == 1
    assert not by_unit or units * HEAD_DIM == LANES, "GQA value blocks must fill one slab"
    in_specs = [pl.BlockSpec((TQ, hs * LANES), lambda b, u, i: (b * nq + i, u)) for _ in qs]
    args = list(qs)
    for q, k, vt, _ in streams:
        assert q.shape[1] == units * hs * LANES and k.shape[1] == units * nk * LANES
        assert vt.shape == (T // TM, units * (HEAD_DIM if by_unit else LANES), TM)
        in_specs += [pl.BlockSpec((seq, nk * LANES), lambda b, u, i: (b, u)),
                     pl.BlockSpec((seq // TM, LANES, TM),
                                  (lambda b, u, i: (b, 0, 0)) if by_unit else (lambda b, u, i: (b, u, 0)))]
        args += [k, vt]
    return pl.pallas_call(
        functools.partial(_attn_kernel, tq=TQ, tk=TK, q_of_stream=q_of_stream,
                          window_of_stream=tuple(w for _, _, _, w in streams),
                          k_of_head=k_of_head, values_by_unit=by_unit),
        grid=(batch, units, nq), in_specs=in_specs,
        out_specs=[pl.BlockSpec((TQ, hs // 2 * LANES), lambda b, u, i: (b * nq + i, u))] * len(streams),
        out_shape=[jax.ShapeDtypeStruct((T, units * hs // 2 * LANES), out_dtype)] * len(streams),
        scratch_shapes=[pltpu.VMEM((len(qs) * hs, LANES, TQ), BF16), pltpu.VMEM((nh, 1, TQ), F32),
                        pltpu.VMEM((nh, 1, TQ), F32), pltpu.VMEM((nh, HEAD_DIM, TQ), F32)],
        compiler_params=_cparams(3), name=name,
    )(*args)


def _gelu_tanh(x):
    return 0.5 * x * (1.0 + jnp.tanh(np.float32(np.sqrt(2.0 / np.pi)) * (x + 0.044715 * (x * x * x))))


def _compress_kernel(x_ref, pe_ref, w1_ref, w2_ref, o_ref):
    x = x_ref[0]
    half = x.shape[1]
    top = _dot((x + pe_ref[0, :, :half]).astype(BF16), w1_ref[0, :half, :])
    bot = _dot((x + pe_ref[0, :, half:]).astype(BF16), w1_ref[0, half:, :])
    hid = top + pltpu.roll(bot, bot.shape[0] - 1, 0)
    o_ref[0, 0] = _dot(_gelu_tanh(hid).astype(BF16), w2_ref[0]).astype(o_ref.dtype)


def _compress(kvc, p, batch, seq):
    T = kvc.shape[1]
    rows = seq // CMP_STRIDE
    x = kvc.reshape(4, T // CMP_STRIDE, CMP_STRIDE * HEAD_DIM)
    pe = jnp.stack([p["cmp_pe_k"].reshape(1, -1), p["cmp_pe_v"].reshape(1, -1)]).astype(F32)
    w1 = jnp.stack([p["cmp_w1_k"], p["cmp_w1_v"]]).astype(BF16)
    w2k = jnp.pad(p["cmp_w2_k"], ((0, 0), (0, LANES - HEAD_DIM)))
    w2v = jnp.concatenate([p["cmp_w2_v"], p["cmp_w2_v"]], axis=1)
    w2 = jnp.stack([w2k, w2v]).astype(BF16)
    return pl.pallas_call(
        _compress_kernel, grid=(4, batch),
        in_specs=[pl.BlockSpec((1, rows, x.shape[2]), lambda j, b: (j, b, 0)),
                  pl.BlockSpec((1,) + pe.shape[1:], lambda j, b: (j // 2, 0, 0)),
                  pl.BlockSpec((1,) + w1.shape[1:], lambda j, b: (j // 2, 0, 0)),
                  pl.BlockSpec((1,) + w2.shape[1:], lambda j, b: (j // 2, 0, 0))],
        out_specs=pl.BlockSpec((1, 1, rows, LANES), lambda j, b: (b, j, 0, 0)),
        out_shape=jax.ShapeDtypeStruct((batch, 4, rows, LANES), BF16),
        compiler_params=_cparams(2), name="nsa_compress",
    )(x, pe, w1, w2)


def _cmp_select_kernel(qn_ref, qr_ref, kc_ref, vc_ref, ovt_ref, o_ref, qa_ref, *, tq):
    i = pl.program_id(2)
    n_cmp = kc_ref.shape[2]
    kc, vc = kc_ref[0, 0], vc_ref[0, 0]
    t_row = i * tq + lax.broadcasted_iota(jnp.int32, (tq, n_cmp), 0)
    blk_end = CMP_STRIDE * lax.broadcasted_iota(jnp.int32, (tq, n_cmp), 1) + (CMP_LEN - 1)
    vis = blk_end <= t_row
    lane = lax.broadcasted_iota(jnp.int32, (tq, LANES), 1)
    sees_any = t_row[:, :1] >= CMP_LEN - 1
    psum = jnp.zeros((tq, n_cmp), F32)
    outs = []
    for r in range(NSA_REP):
        s = jnp.where(vis, _dot_nt(qn_ref[:, r * LANES:(r + 1) * LANES], kc), NEG)
        m = jnp.max(s, axis=-1, keepdims=True)
        e = jnp.exp2(s - m)
        pn = e * jnp.where(sees_any, 1.0 / jnp.sum(e, axis=-1, keepdims=True), 0.0)
        psum = psum + pn
        outs.append(_dot(pn.astype(BF16), vc))
    for k in range(NSA_REP // 2):
        o_ref[:, k * LANES:(k + 1) * LANES] = jnp.where(lane < HEAD_DIM, outs[2 * k], outs[2 * k + 1])

    p_hi = psum.astype(BF16)
    p_lo = (psum - p_hi.astype(F32)).astype(BF16)
    ovt = ovt_ref[...]
    imp = _dot_nt(ovt, p_hi) + _dot_nt(ovt, p_lo)
    n_slc = LANES - HEAD_DIM
    jb = lax.broadcasted_iota(jnp.int32, (LANES, tq), 0) - HEAD_DIM
    cur = (i * tq + lax.broadcasted_iota(jnp.int32, (LANES, tq), 1)) // SLC_LEN
    forced = (jb == 0) | (jb == cur) | (jb == cur - 1)
    imp = jnp.where(forced, 1e9, jnp.where(jb > cur, -1e9, imp))
    groups = [imp[HEAD_DIM + 8 * v:HEAD_DIM + 8 * (v + 1)] for v in range(n_slc // 8)]
    sub = lax.broadcasted_iota(jnp.int32, (8, tq), 0)
    ranks = [jnp.zeros((8, tq), F32) for _ in groups]
    for a in range(n_slc):
        row = jnp.broadcast_to(groups[a // 8][a % 8:a % 8 + 1], (8, tq))
        for v in range(len(groups)):
            if v > a // 8:
                inc = jnp.where(row >= groups[v], 1.0, 0.0)
            elif v < a // 8:
                inc = jnp.where(row > groups[v], 1.0, 0.0)
            else:
                tie = jnp.where(sub > a % 8, 1.0, 0.0)
                inc = jnp.where(row > groups[v], 1.0, jnp.where(row == groups[v], tie, 0.0))
            ranks[v] = ranks[v] + inc
    bias_t = jnp.concatenate(
        [jnp.zeros((HEAD_DIM, tq), F32)]
        + [jnp.where(rk < SLC_TOPK, 0.0, SEL_NEG) for rk in ranks], axis=0)
    bias = bias_t.T
    for r in range(NSA_REP):
        qa_ref[:, r * LANES:(r + 1) * LANES] = (
            qr_ref[:, r * LANES:(r + 1) * LANES].astype(F32) + bias).astype(BF16)


def _overlap_t(seq):
    n_cmp = (seq - CMP_LEN) // CMP_STRIDE + 1
    n_slc = seq // SLC_LEN
    assert n_slc == LANES - HEAD_DIM, "selection blocks must fill the bias half of a slab"
    c0 = CMP_STRIDE * np.arange(n_cmp)[:, None]
    s0 = SLC_LEN * np.arange(n_slc)[None, :]
    ov = np.clip(np.minimum(c0 + CMP_LEN, s0 + SLC_LEN) - np.maximum(c0, s0), 0, None) / CMP_LEN
    out = np.zeros((LANES, seq // CMP_STRIDE), np.float32)
    out[HEAD_DIM:, :n_cmp] = ov.T
    return jnp.asarray(out, BF16)


def _cmp_select(qn, qr, cmp_kv, batch, seq):
    T = qn.shape[0]
    nq = seq // TQ_SELECT
    rows = seq // CMP_STRIDE
    gw = NSA_REP * LANES
    qspec = pl.BlockSpec((TQ_SELECT, gw), lambda b, g, i: (b * nq + i, g))
    return pl.pallas_call(
        functools.partial(_cmp_select_kernel, tq=TQ_SELECT), grid=(batch, NSA_GROUPS, nq),
        in_specs=[qspec, qspec,
                  pl.BlockSpec((1, 1, rows, LANES), lambda b, g, i: (b, g, 0, 0)),
                  pl.BlockSpec((1, 1, rows, LANES), lambda b, g, i: (b, NSA_GROUPS + g, 0, 0)),
                  _const_spec((LANES, rows))],
        out_specs=[pl.BlockSpec((TQ_SELECT, NSA_REP * HEAD_DIM), lambda b, g, i: (b * nq + i, g)), qspec],
        out_shape=[jax.ShapeDtypeStruct((T, NSA_HEADS * HEAD_DIM), F32),
                   jax.ShapeDtypeStruct((T, NSA_HEADS * LANES), BF16)],
        compiler_params=_cparams(3), name="nsa_cmp_select",
    )(qn, qr, cmp_kv, cmp_kv, _overlap_t(seq))


def _merge_kernel(x_ref, oc_ref, os_ref, ow_ref, gl_ref, of_ref, om_ref, mg_ref,
                  e_ref, wn_ref, wf_ref, wm_ref, wo_ref, o_ref):
    d = x_ref.shape[1]
    nw = oc_ref.shape[1]
    mg = mg_ref[...]
    merged = mg[:, d:2 * d] * _dot(of_ref[...], wf_ref[...]) + mg[:, 2 * d:] * _dot(om_ref[...], wm_ref[...])
    g = _dot(jax.nn.sigmoid(gl_ref[...]).astype(BF16), e_ref[...])
    o_nsa = g[:, :nw] * oc_ref[...] + g[:, nw:2 * nw] * os_ref[...] + g[:, 2 * nw:] * ow_ref[...]
    merged = merged + mg[:, :d] * _dot(o_nsa.astype(BF16), wn_ref[...])
    o_ref[...] = x_ref[...] + _dot(merged.astype(BF16), wo_ref[...])


def _gate_expand():
    e = np.zeros((LANES, 3 * NSA_HEADS * HEAD_DIM), np.float32)
    for h in range(NSA_HEADS):
        for c in range(3):
            e[h * 3 + c, c * NSA_HEADS * HEAD_DIM + h * HEAD_DIM:
              c * NSA_HEADS * HEAD_DIM + (h + 1) * HEAD_DIM] = 1.0
    return jnp.asarray(e, BF16)


def _merge(x2, o_cmp, o_slc, o_win, gate_l, o_fox, o_mla, mg, p):
    T, D = x2.shape
    tok = lambda a: pl.BlockSpec((TM, a.shape[1]), lambda i: (i, 0))
    acts = [x2, o_cmp, o_slc, o_win, gate_l, o_fox, o_mla, mg]
    weights = [_gate_expand(), p["w_br_nsa"].astype(BF16), p["w_br_fox"].astype(BF16),
               p["w_br_mla"].astype(BF16), p["w_out"].astype(BF16)]
    return pl.pallas_call(
        _merge_kernel, grid=(T // TM,),
        in_specs=[tok(a) for a in acts] + [_const_spec(w.shape) for w in weights],
        out_specs=pl.BlockSpec((TM, D), lambda i: (i, 0)),
        out_shape=jax.ShapeDtypeStruct((T, D), F32),
        compiler_params=_cparams(1), name="merge_out",
    )(*acts, *weights)


def _ffn_kernel(x_ref, g_ref, wu_ref, wv_ref, cw_ref, cb_ref, wd_ref, fin_ref, o_ref, tail_ref,
                *, seq, n_chunks, final):
    tm = x_ref.shape[0]
    x = x_ref[...]
    hb = _rms(x, g_ref[...]).astype(BF16)

    @pl.when((pl.program_id(0) * tm) % seq == 0)
    def _():
        tail_ref[...] = jnp.zeros(tail_ref.shape, F32)

    cw = wu_ref.shape[1] // n_chunks
    row8 = lax.broadcasted_iota(jnp.int32, (8, cw), 0)
    acc = jnp.zeros(x.shape, F32)
    for c in range(n_chunks):
        sl = slice(c * cw, (c + 1) * cw)
        u = _dot(hb, wu_ref[:, sl])
        v = _dot(hb, wv_ref[:, sl])
        prev = tail_ref[:, sl]
        tail_ref[:, sl] = u[tm - 8:]
        u1 = pltpu.roll(u, 1, 0)
        u2 = pltpu.roll(u, 2, 0)
        h1 = jnp.where(row8 < 1, pltpu.roll(prev, 1, 0), u1[:8])
        h2 = jnp.where(row8 < 2, pltpu.roll(prev, 2, 0), u2[:8])
        u1 = jnp.concatenate([h1, u1[8:]], axis=0)
        u2 = jnp.concatenate([h2, u2[8:]], axis=0)
        uc = cw_ref[0:1, sl] * u2 + cw_ref[1:2, sl] * u1 + cw_ref[2:3, sl] * u + cb_ref[:, sl]
        act = (uc * jax.nn.sigmoid(uc) * v).astype(BF16)
        acc = acc + _dot(act, wd_ref[sl, :])
    y = x + acc
    if final:
        y = _rms(y, fin_ref[...])
    o_ref[...] = y


def _ffn(x2, p, final_norm, seq, final):
    T, D = x2.shape
    w_up = p["w_up"]
    wu, wv = w_up[:, :D_FF].astype(BF16), w_up[:, D_FF:].astype(BF16)
    row = lambda a: a.reshape(1, -1).astype(F32)
    cwp = jnp.pad(p["conv_w"].astype(F32), ((0, 8 - CONV_W), (0, 0)))
    weights = [row(p["ffn_norm"]), wu, wv, cwp, row(p["conv_b"]), p["w_down"].astype(BF16),
               row(final_norm)]
    return pl.pallas_call(
        functools.partial(_ffn_kernel, seq=seq, n_chunks=2, final=final), grid=(T // TM,),
        in_specs=[pl.BlockSpec((TM, D), lambda i: (i, 0))] + [_const_spec(w.shape) for w in weights],
        out_specs=pl.BlockSpec((TM, D), lambda i: (i, 0)),
        out_shape=jax.ShapeDtypeStruct((T, D), F32),
        scratch_shapes=[pltpu.VMEM((8, D_FF), F32)],
        compiler_params=_cparams(1), name="conv_ffn",
    )(x2, *weights)


_LAYER_PARAMS = ("mix_norm", "w_in", "b_forget", "cmp_pe_k", "cmp_w1_k", "cmp_w2_k", "cmp_pe_v",
                 "cmp_w1_v", "cmp_w2_v", "mla_q_norm", "mla_w_uq", "mla_kv_norm", "mla_w_ukv",
                 "w_br_nsa", "w_br_fox", "w_br_mla", "w_out", "ffn_norm", "w_up", "conv_w",
                 "conv_b", "w_down")


def _mixer_layer(h, tabs, p, batch, seq):
    (qn, qr, kvc, ksl, vsl_t, kwin, vwin_t, gate_l,
     fq, fk, fv_t, mq, mk, mv_t, mg) = _inproj(h, tabs, p, seq)
    cmp_kv = _compress(kvc, p, batch, seq)
    o_cmp, qa = _cmp_select(qn, qr, cmp_kv, batch, seq)
    group = (0,) * NSA_REP
    pair = (0, 1)
    o_slc, o_win = _attention([(qa, ksl, vsl_t, False), (qa, kwin, vwin_t, True)], batch, seq, group,
                              F32, "nsa_slc_win_attn")
    o_fox, o_mla = _attention([(fq, fk, fv_t, False), (mq, mk, mv_t, False)], batch, seq, pair, BF16,
                              "fox_mla_attn")
    return _merge(h, o_cmp, o_slc, o_win, gate_l, o_fox, o_mla, mg, p)


def kernel(x, positions, mix_norm, w_in, b_forget, cmp_pe_k, cmp_w1_k, cmp_w2_k, cmp_pe_v, cmp_w1_v,
           cmp_w2_v, mla_q_norm, mla_w_uq, mla_kv_norm, mla_w_ukv, w_br_nsa, w_br_fox, w_br_mla,
           w_out, ffn_norm, w_up, conv_w, conv_b, w_down, final_norm):
    batch, seq, d = x.shape
    stacked = dict(zip(_LAYER_PARAMS, (
        mix_norm, w_in, b_forget, cmp_pe_k, cmp_w1_k, cmp_w2_k, cmp_pe_v, cmp_w1_v, cmp_w2_v,
        mla_q_norm, mla_w_uq, mla_kv_norm, mla_w_ukv, w_br_nsa, w_br_fox, w_br_mla, w_out,
        ffn_norm, w_up, conv_w, conv_b, w_down)))
    depth = w_in.shape[0]
    tabs = _rope_tables(positions)
    h = x.reshape(batch * seq, d)
    for l in range(depth):
        p = {k: v[l] for k, v in stacked.items()}
        h = _mixer_layer(h, tabs, p, batch, seq)
        h = _ffn(h, p, final_norm, seq, final=(l == depth - 1))
    return h.reshape(batch, seq, d)
```

```python
import functools

import numpy as np
import jax
import jax.numpy as jnp
from jax import lax
from jax.experimental import pallas as pl
from jax.experimental.pallas import tpu as pltpu

F32 = jnp.float32
BF16 = jnp.bfloat16

HEAD_DIM = 64
ROPE_THETA = 500000.0
PARTIAL_ROT = HEAD_DIM // 4
NORM_EPS = 1e-6
NSA_HEADS = 8
NSA_GROUPS = 2
NSA_REP = NSA_HEADS // NSA_GROUPS
CMP_LEN = 32
CMP_STRIDE = 16
SLC_LEN = 64
SLC_TOPK = 16
WIN = 512
FOX_HEADS = 8
MLA_HEADS = 8
MLA_Q_LORA = 384
MLA_KV_LORA = 256
MLA_NOPE = 64
MLA_ROPE = 32
MLA_V = 64
D_FF = 2816
D_MODEL = 1024
CONV_W = 3

LANES = 128
LOG2E = 1.4426950408889634
NEG = -1e30
SEL_NEG = -1e9
VMEM_LIMIT = 52 * 1024 * 1024

TM = 256
TQ = 1024
TQ_SELECT = 512
TK = 512
QC = 256
LOOP_TILES = 4
SKEW = 2
DECAY_TERMS = 3


def _cparams(n_axes):
    return pltpu.CompilerParams(dimension_semantics=("arbitrary",) * n_axes,
                                vmem_limit_bytes=VMEM_LIMIT)


def _const_spec(shape):
    nd = len(shape)
    return pl.BlockSpec(shape, lambda *_: (0,) * nd, pipeline_mode=pl.Buffered(1))


def _rms(x, g):
    return x * lax.rsqrt(jnp.mean(x * x, axis=-1, keepdims=True) + NORM_EPS) * g


def _dot(a, b):
    return jnp.dot(a, b, preferred_element_type=F32)


def _dot_nt(a, b):
    return lax.dot_general(a, b, (((1,), (1,)), ((), ())), preferred_element_type=F32)


def _rope_tables_kernel(pos_ref, freq_ref, g16_ref, g32_ref, c16_o, s16_o, c32_o, s32_o):
    ang = pos_ref[...].astype(F32) * freq_ref[...]
    cos, sin = jnp.cos(ang), jnp.sin(ang)
    lane = lax.broadcasted_iota(jnp.int32, ang.shape, 1)
    low = lane < PARTIAL_ROT
    high = (lane >= HEAD_DIM) & (lane < HEAD_DIM + PARTIAL_ROT)
    mla = (lane >= MLA_NOPE) & (lane < MLA_NOPE + MLA_ROPE)
    c16_o[...] = jnp.where(low, cos, jnp.where(high, pltpu.roll(cos, HEAD_DIM, 1), 1.0))
    s16_o[...] = jnp.where(low, sin, jnp.where(high, pltpu.roll(sin, HEAD_DIM, 1), 0.0)) * g16_ref[...]
    c32_o[...] = jnp.where(mla, pltpu.roll(cos, MLA_ROPE, 1), 1.0)
    s32_o[...] = jnp.where(mla, pltpu.roll(sin, MLA_ROPE, 1), 0.0) * g32_ref[...]


def _rope_tables(positions):
    T = positions.size
    half16, half32 = PARTIAL_ROT // 2, MLA_ROPE // 2
    inv16 = ROPE_THETA ** (-jnp.arange(half16, dtype=F32) / half16)
    inv32 = ROPE_THETA ** (-jnp.arange(half32, dtype=F32) / half32)
    lane = np.arange(LANES)
    freq = jnp.where(lane < PARTIAL_ROT, inv16[lane % half16],
                     jnp.where((lane >= MLA_ROPE) & (lane < 2 * MLA_ROPE), inv32[lane % half32], 0.0))[None, :]
    d = lane % HEAD_DIM
    g16 = np.where(d < PARTIAL_ROT, np.where(d < half16, -1.0, 1.0), 0.0).astype(np.float32)[None, :]
    r = lane - MLA_NOPE
    g32 = np.where((r >= 0) & (r < MLA_ROPE), np.where(r < half32, -1.0, 1.0), 0.0).astype(np.float32)[None, :]
    pos = jnp.broadcast_to(positions.reshape(T, 1), (T, LANES))
    tt = 1024
    row = pl.BlockSpec((tt, LANES), lambda i: (i, 0))
    vec = _const_spec((1, LANES))
    out = jax.ShapeDtypeStruct((T, LANES), F32)
    return pl.pallas_call(
        _rope_tables_kernel, grid=(T // tt,),
        in_specs=[row, vec, vec, vec], out_specs=[row] * 4, out_shape=[out] * 4,
        compiler_params=_cparams(1), name="rope_tables",
    )(pos, freq, jnp.asarray(g16), jnp.asarray(g32))


def _rot16(x, cos, sin):
    lane = lax.broadcasted_iota(jnp.int32, x.shape, 1)
    first = (lane % HEAD_DIM) < (PARTIAL_ROT // 2)
    sw = jnp.where(first, pltpu.roll(x, LANES - PARTIAL_ROT // 2, 1), pltpu.roll(x, PARTIAL_ROT // 2, 1))
    return x * cos + sw * sin


def _rot32(x, cos, sin):
    lane = lax.broadcasted_iota(jnp.int32, x.shape, 1)
    first = lane < (MLA_NOPE + MLA_ROPE // 2)
    sw = jnp.where(first, pltpu.roll(x, LANES - MLA_ROPE // 2, 1), pltpu.roll(x, MLA_ROPE // 2, 1))
    return x * cos + sw * sin


_FF_LANE = 3 * NSA_HEADS
_IN_COLS = {}
for _name, _n in (("q", NSA_HEADS * HEAD_DIM), ("kvc", 4 * HEAD_DIM), ("ksl", LANES), ("kwin", LANES),
                  ("misc", LANES), ("fq", FOX_HEADS * HEAD_DIM), ("fk", FOX_HEADS * HEAD_DIM),
                  ("lat", MLA_Q_LORA + MLA_KV_LORA), ("mg", 3 * D_MODEL)):
    _IN_COLS[_name] = (sum(n for _, n in _IN_COLS.values()), _n)
_IN_ROWS = {"vsl": (0, LANES), "vwin": (LANES, LANES), "fv": (2 * LANES, FOX_HEADS * HEAD_DIM)}


def _inproj_kernel(x_ref, g_ref, c16_ref, s16_ref, c32_ref, s32_ref,
                   w_ref, wt_ref, bf_ref, tri_ref, place_ref,
                   qnrm_ref, wuq_ref, kvnrm_ref, wuk_ref, wuv_ref,
                   qn_o, qr_o, kvc_o, ksl_o, vsl_o, kwin_o, vwin_o, gate_o,
                   fq_o, fk_o, fv_o, mq_o, mk_o, mv_o, mg_o, fcarry_ref, *, seq):
    tm = x_ref.shape[0]
    seq_start = (pl.program_id(0) * tm) % seq == 0
    hb = _rms(x_ref[...], g_ref[...]).astype(BF16)
    c16, s16 = c16_ref[...], s16_ref[...]
    c32, s32 = c32_ref[...], s32_ref[...]
    lane = lax.broadcasted_iota(jnp.int32, (tm, LANES), 1)
    low = lane < HEAD_DIM

    def proj(name):
        a, n = _IN_COLS[name]
        return _dot(hb, w_ref[:, a:a + n])

    def proj_t(name):
        a, n = _IN_ROWS[name]
        return _dot_nt(wt_ref[a:a + n, :], hb)

    def slabs(z):
        return [z[:, k * LANES:(k + 1) * LANES] for k in range(z.shape[1] // LANES)]

    def split(z, fill=0.0):
        return jnp.where(low, z, fill), jnp.where(low, pltpu.roll(z, HEAD_DIM, 1), fill)

    def store_heads(o_ref, pair_slabs, fill=0.0, extra=None):
        for k, z in enumerate(pair_slabs):
            for e, s in enumerate(split(z, fill)):
                h = 2 * k + e
                if extra is not None:
                    s = s + extra[:, h * LANES:(h + 1) * LANES]
                o_ref[:, h * LANES:(h + 1) * LANES] = s.astype(o_ref.dtype)

    misc = proj("misc")
    gate_o[...] = misc

    @pl.when(seq_start)
    def _():
        fcarry_ref[...] = jnp.zeros(fcarry_ref.shape, F32)

    def bf16_terms(v):
        packed = None
        for t in range(DECAY_TERMS):
            part = v.astype(BF16).astype(F32)
            v = v - part
            part = part if t == 0 else pltpu.roll(part, FOX_HEADS * t, 1)
            packed = part if packed is None else packed + part
        return packed.astype(BF16)

    zf = misc + bf_ref[...]
    is_f = (lane >= _FF_LANE) & (lane < _FF_LANE + FOX_HEADS)
    logf_terms = bf16_terms(
        jnp.where(is_f, jnp.minimum(zf, 0.0) - jnp.log1p(jnp.exp(-jnp.abs(zf))), 0.0))

    zq = slabs(proj("q"))
    store_heads(qn_o, zq)
    store_heads(qr_o, [_rot16(z, c16, s16) for z in zq])

    terms = _dot(tri_ref[...], logf_terms)
    fsum = sum(terms if t == 0 else pltpu.roll(terms, LANES - FOX_HEADS * t, 1)
               for t in range(DECAY_TERMS))
    fsum = jnp.where(is_f, fsum, 0.0) + fcarry_ref[0:1]
    fcarry_ref[...] = jnp.broadcast_to(fsum[tm - 1:tm], fcarry_ref.shape)
    decay_terms = bf16_terms(jnp.where(is_f, -LOG2E * fsum, 0.0))

    for k, z in enumerate(slabs(proj("kvc"))):
        for e, s in enumerate(split(z)):
            kvc_o[2 * k + e] = s[:, :HEAD_DIM]

    tok = (pl.program_id(0) * tm) % seq + lax.broadcasted_iota(jnp.int32, (tm, LANES), 0)
    onehot = jnp.where(lane - HEAD_DIM == tok // SLC_LEN, 1.0, 0.0)
    store_heads(ksl_o, [_rot16(proj("ksl"), c16, s16)], fill=onehot)
    vsl_o[0] = proj_t("vsl").astype(BF16)

    store_heads(kwin_o, [_rot16(proj("kwin"), c16, s16)])
    vwin_o[0] = proj_t("vwin").astype(BF16)

    ones = jnp.where((lane >= HEAD_DIM) & (lane < HEAD_DIM + DECAY_TERMS), 1.0, 0.0)
    store_heads(fq_o, slabs(proj("fq")), fill=ones)
    fv_o[0] = proj_t("fv").astype(BF16)
    decay = _dot(decay_terms, place_ref[...])

    lat = proj("lat")
    cq = _rms(lat[:, :MLA_Q_LORA], qnrm_ref[...]).astype(BF16)
    ckv = _rms(lat[:, MLA_Q_LORA:], kvnrm_ref[...]).astype(BF16)
    kpe = jnp.where(low, 0.0, _rot32(misc, c32, s32))
    store_heads(fk_o, slabs(proj("fk")), extra=decay)
    zuq = _dot(cq, wuq_ref[...])
    for k, z in enumerate(slabs(zuq)):
        mq_o[:, k * LANES:(k + 1) * LANES] = _rot32(z, c32, s32).astype(BF16)
    zuk = _dot(ckv, wuk_ref[...])
    for k, z in enumerate(slabs(zuk)):
        mk_o[:, k * LANES:(k + 1) * LANES] = (z + kpe).astype(BF16)
    mv_o[0] = _dot_nt(wuv_ref[...], ckv).astype(BF16)

    mg_o[...] = jax.nn.sigmoid(proj("mg"))


def _pad_halves(w, first_only=True):
    K, n = w.shape[0], w.shape[1] // HEAD_DIM
    w = w.reshape(K, n, HEAD_DIM)
    other = jnp.zeros_like(w) if first_only else w
    return jnp.concatenate([w, other], axis=-1).reshape(K, n * LANES)


def _inproj(x2, tabs, p, seq):
    T, D = x2.shape
    w_in = p["w_in"]
    scale = HEAD_DIM ** -0.5 * LOG2E
    o = 0

    def take(n):
        nonlocal o
        w = w_in[:, o:o + n]
        o += n
        return w

    nsa_w, kvw = NSA_HEADS * HEAD_DIM, NSA_GROUPS * HEAD_DIM
    w_q = take(nsa_w)
    w_kc, w_vc, w_ks, w_vs, w_kw, w_vw = [take(kvw) for _ in range(6)]
    w_g = take(3 * NSA_HEADS)
    w_fq, w_fk, w_fv = [take(FOX_HEADS * HEAD_DIM) for _ in range(3)]
    w_ff = take(FOX_HEADS)
    w_cq, w_ckv, w_kr = take(MLA_Q_LORA), take(MLA_KV_LORA), take(MLA_ROPE)
    w_mg = take(3 * D)

    bf = lambda w: w.astype(BF16)
    row = lambda a: a.reshape(1, -1).astype(F32)
    w_misc = jnp.concatenate([w_g, w_ff, jnp.zeros((D, MLA_NOPE - _FF_LANE - FOX_HEADS), F32), w_kr,
                              jnp.zeros((D, LANES - MLA_NOPE - MLA_ROPE), F32)], axis=1)
    w_tok = bf(jnp.concatenate([w_q * scale, w_kc, w_vc, w_ks, w_kw, w_misc, w_fq * scale, w_fk,
                                w_cq, w_ckv, w_mg], axis=1))
    w_tr = bf(jnp.concatenate([w_vs, w_vw, w_fv], axis=1).T)
    b_f = jnp.pad(row(p["b_forget"]), ((0, 0), (_FF_LANE, LANES - _FF_LANE - FOX_HEADS)))
    tri = jnp.asarray(np.tril(np.ones((TM, TM), np.float32)), BF16)
    place = np.zeros((LANES, FOX_HEADS * LANES), np.float32)
    for t in range(DECAY_TERMS):
        for h in range(FOX_HEADS):
            place[_FF_LANE + FOX_HEADS * t + h, h * LANES + HEAD_DIM + t] = 1.0
    place = jnp.asarray(place, BF16)
    dq = MLA_NOPE + MLA_ROPE
    wuq = (p["mla_w_uq"] * (dq ** -0.5 * LOG2E)).reshape(MLA_Q_LORA, MLA_HEADS, dq)
    wuq = bf(jnp.pad(wuq, ((0, 0), (0, 0), (0, LANES - dq))).reshape(MLA_Q_LORA, MLA_HEADS * LANES))
    wukv = p["mla_w_ukv"].reshape(MLA_KV_LORA, MLA_HEADS, MLA_NOPE + MLA_V)
    wuk = bf(_pad_halves(wukv[:, :, :MLA_NOPE].reshape(MLA_KV_LORA, MLA_HEADS * MLA_NOPE)))
    wuv = bf(wukv[:, :, MLA_NOPE:].reshape(MLA_KV_LORA, MLA_HEADS * MLA_V).T)
    assert w_tok.shape[1] == sum(n for _, n in _IN_COLS.values())

    weights = [w_tok, w_tr, b_f, tri, place,
               row(p["mla_q_norm"]), wuq, row(p["mla_kv_norm"]), wuk, wuv]
    n = T // TM
    tok = lambda c: pl.BlockSpec((TM, c), lambda i: (i, 0))
    in_specs = ([tok(D), _const_spec((1, D))] + [tok(LANES)] * 4
                + [_const_spec(w.shape) for w in weights])
    tokens, transposed = "tokens", "transposed"
    outs = [
        (tokens, NSA_HEADS * LANES, BF16), (tokens, NSA_HEADS * LANES, BF16), None,
        (tokens, NSA_GROUPS * LANES, BF16), (transposed, NSA_GROUPS * HEAD_DIM, BF16),
        (tokens, NSA_GROUPS * LANES, BF16), (transposed, NSA_GROUPS * HEAD_DIM, BF16), (tokens, LANES, F32),
        (tokens, FOX_HEADS * LANES, BF16), (tokens, FOX_HEADS * LANES, BF16),
        (transposed, FOX_HEADS * HEAD_DIM, BF16),
        (tokens, MLA_HEADS * LANES, BF16), (tokens, MLA_HEADS * LANES, BF16),
        (transposed, MLA_HEADS * MLA_V, BF16), (tokens, 3 * D, F32)]
    out_shape, out_specs = [], []
    for spec in outs:
        if spec is None:
            out_shape.append(jax.ShapeDtypeStruct((4, T, HEAD_DIM), F32))
            out_specs.append(pl.BlockSpec((4, TM, HEAD_DIM), lambda i: (0, i, 0)))
        elif spec[0] == transposed:
            out_shape.append(jax.ShapeDtypeStruct((n, spec[1], TM), spec[2]))
            out_specs.append(pl.BlockSpec((1, spec[1], TM), lambda i: (i, 0, 0)))
        else:
            out_shape.append(jax.ShapeDtypeStruct((T, spec[1]), spec[2]))
            out_specs.append(tok(spec[1]))
    return pl.pallas_call(
        functools.partial(_inproj_kernel, seq=seq), grid=(n,),
        in_specs=in_specs, out_specs=out_specs, out_shape=out_shape,
        scratch_shapes=[pltpu.VMEM((8, LANES), F32)],
        compiler_params=_cparams(1), name="inproj",
    )(x2, row(p["mix_norm"]), *tabs, *weights)


def _attn_kernel(*refs, tq, tk, q_of_stream, window_of_stream, k_of_head, values_by_unit):
    n_streams, n_q = len(q_of_stream), max(q_of_stream) + 1
    q_refs, kv_refs = refs[:n_q], refs[n_q:n_q + 2 * n_streams]
    outs = refs[n_q + 2 * n_streams:n_q + 3 * n_streams]
    qt_ref, m_ref, l_ref, acc_ref = refs[n_q + 3 * n_streams:]
    i = pl.program_id(2)
    hs = len(k_of_head)
    nh = n_streams * hs
    k_of = lambda h: kv_refs[2 * (h // hs)]
    vt_of = lambda h: kv_refs[2 * (h // hs) + 1]
    qt_of = lambda h: q_of_stream[h // hs] * hs + h % hs
    per_q = tq // tk
    tv = kv_refs[1].shape[2]
    v_sub = tk // tv
    for qi in range(n_q):
        for r in range(hs):
            qt_ref[qi * hs + r] = q_refs[qi][:, r * LANES:(r + 1) * LANES].astype(F32).T.astype(BF16)
    dist0 = (i * tq + lax.broadcasted_iota(jnp.int32, (tk, tq), 1)
             - lax.broadcasted_iota(jnp.int32, (tk, tq), 0))

    m_ref[...] = jnp.full(m_ref.shape, NEG, F32)
    l_ref[...] = jnp.zeros(l_ref.shape, F32)
    acc_ref[...] = jnp.zeros(acc_ref.shape, F32)

    def logits(j, lo, n, h, cols):
        r0 = pl.multiple_of(j * tk + lo, tv)
        kk = k_of_head[h % hs]
        return _dot(k_of(h)[pl.ds(r0, n), kk * LANES:(kk + 1) * LANES], qt_ref[qt_of(h), :, cols])

    def softmax(s, h, cols, mask):
        if mask is not None:
            s = jnp.where(mask, s, NEG)
        m_old = m_ref[h, :, cols]
        m_new = jnp.maximum(m_old, jnp.max(s, axis=0, keepdims=True))
        alpha = jnp.exp2(m_old - m_new)
        p = jnp.exp2(s - m_new)
        l_ref[h, :, cols] = alpha * l_ref[h, :, cols] + jnp.sum(p, axis=0, keepdims=True)
        m_ref[h, :, cols] = m_new
        return p.astype(BF16), alpha

    def accumulate(j, lo, n, h, cols, p, alpha):
        if values_by_unit:
            half = pl.ds(pl.multiple_of(pl.program_id(1) * HEAD_DIM, HEAD_DIM), HEAD_DIM)
        else:
            half = slice((h % hs) * HEAD_DIM, (h % hs + 1) * HEAD_DIM)
        pv = sum(_dot(vt_of(h)[j * v_sub + lo // tv + c, half, :], p[c * tv:(c + 1) * tv])
                 for c in range(n // tv))
        acc_ref[h, :, cols] = alpha * acc_ref[h, :, cols] + pv

    def sweep(tiles):
        work = []
        for heads, j, n_tiles, mask, span in tiles:
            for h in heads:
                for c in range(tq // QC):
                    lo, n, masked = (0, n_tiles * tk, False) if span is None else span(c)
                    if n > 0:
                        work.append((j, lo, n, mask if masked else None, h, slice(c * QC, (c + 1) * QC)))
        s_of, p_of = {}, {}
        for t in range(len(work) + 2 * SKEW):
            if 0 <= t - 2 * SKEW < len(work):
                j, lo, n, _, h, cols = work[t - 2 * SKEW]
                accumulate(j, lo, n, h, cols, *p_of.pop(t - 2 * SKEW))
            if 0 <= t - SKEW < len(work):
                _, lo, n, mask, h, cols = work[t - SKEW]
                p_of[t - SKEW] = softmax(s_of.pop(t - SKEW), h, cols,
                                         None if mask is None else mask(slice(lo, lo + n), cols))
            if t < len(work):
                j, lo, n, _, h, cols = work[t]
                s_of[t] = logits(j, lo, n, h, cols)

    def visible(j, window, valid=None):
        def mask(keys, cols):
            dist = dist0[keys, cols] - j * tk
            if not window:
                return dist >= 0
            if valid is not None:
                dist = dist + jnp.where(valid, 0, WIN)
            return lax.bitcast_convert_type(dist, jnp.uint32) < WIN
        return mask

    assert tk == 2 * QC and WIN % tk == 0 and LOOP_TILES % per_q == 0
    causal = [h for h in range(nh) if not window_of_stream[h // hs]]
    windowed = [h for h in range(nh) if window_of_stream[h // hs]]
    first = i * per_q

    def causal_span(d):
        def span(c):
            hi = min(tk, (c + 1) * QC - d * tk)
            return 0, hi, d * tk + hi > c * QC + 1
        return span

    def window_span(d):
        def span(c):
            lo = max(0, (c * QC - WIN + 1) // tv * tv - d * tk)
            return lo, min(tk, (c + 1) * QC - d * tk) - lo, True
        return span

    one_step = len(causal) * (tq // QC) >= 8

    def unmasked(j0, n):
        tiles = []
        while n > 0:
            k = 2 if (one_step and n >= 2) else 1
            tiles.append((causal, j0, k, None, None))
            j0, n = j0 + k, n - k
        return tiles

    def last_tiles(n_left):
        tiles = []
        if causal:
            tiles += unmasked(first - n_left, n_left)
            tiles += [(causal, first + d, 1, visible(first + d, False), causal_span(d)) for d in range(per_q)]
        if windowed:
            tiles += [(windowed, first + d, 1, visible(first + d, True), window_span(d))
                      for d in reversed(range(per_q))]
            tiles += [(windowed, jnp.maximum(first + d, 0), 1,
                       visible(first + d, True, valid=first + d >= 0), window_span(d))
                      for d in range(-(WIN // tk), 0)]
        return tiles

    if causal:
        def body(jj, c):
            sweep(unmasked(LOOP_TILES * jj, LOOP_TILES))
            return c
        lax.fori_loop(0, first // LOOP_TILES, body, 0)
        for n_left in range(0, LOOP_TILES, per_q):
            @pl.when(first % LOOP_TILES == n_left)
            def _():
                sweep(last_tiles(n_left))
    else:
        sweep(last_tiles(0))

    for h in range(0, nh, 2):
        a = acc_ref[h] * (1.0 / l_ref[h])
        b = acc_ref[h + 1] * (1.0 / l_ref[h + 1])
        o_ref, k = outs[h // hs], (h % hs) // 2
        o_ref[:, k * LANES:(k + 1) * LANES] = jnp.concatenate([a, b], axis=0).T.astype(o_ref.dtype)


def _attention(streams, batch, seq, k_of_head, out_dtype, name):
    qs = []
    for q, _, _, _ in streams:
        if not any(q is x for x in qs):
            qs.append(q)
    q_of_stream = tuple(next(n for n, x in enumerate(qs) if x is q) for q, _, _, _ in streams)
    T = qs[0].shape[0]
    nq = seq // TQ
    hs, nk = len(k_of_head), max(k_of_head) + 1
    nh = hs * len(streams)
    units = qs[0].shape[1] // (hs * LANES)
    by_unit = nk == 1
    assert not by_unit or units * HEAD_DIM == LANES, "GQA value blocks must fill one slab"
    in_specs = [pl.BlockSpec((TQ, hs * LANES), lambda b, u, i: (b * nq + i, u)) for _ in qs]
    args = list(qs)
    for q, k, vt, _ in streams:
        assert q.shape[1] == units * hs * LANES and k.shape[1] == units * nk * LANES
        assert vt.shape == (T // TM, units * HEAD_DIM * (1 if by_unit else hs), TM)
        in_specs += [pl.BlockSpec((seq, nk * LANES), lambda b, u, i: (b, u)),
                     pl.BlockSpec((seq // TM, LANES if by_unit else hs * HEAD_DIM, TM),
                                  (lambda b, u, i: (b, 0, 0)) if by_unit else (lambda b, u, i: (b, u, 0)))]
        args += [k, vt]
    return pl.pallas_call(
        functools.partial(_attn_kernel, tq=TQ, tk=TK, q_of_stream=q_of_stream,
                          window_of_stream=tuple(w for _, _, _, w in streams),
                          k_of_head=k_of_head, values_by_unit=by_unit),
        grid=(batch, units, nq), in_specs=in_specs,
        out_specs=[pl.BlockSpec((TQ, hs // 2 * LANES), lambda b, u, i: (b * nq + i, u))] * len(streams),
        out_shape=[jax.ShapeDtypeStruct((T, units * hs // 2 * LANES), out_dtype)] * len(streams),
        scratch_shapes=[pltpu.VMEM((len(qs) * hs, LANES, TQ), BF16), pltpu.VMEM((nh, 1, TQ), F32),
                        pltpu.VMEM((nh, 1, TQ), F32), pltpu.VMEM((nh, HEAD_DIM, TQ), F32)],
        compiler_params=_cparams(3), name=name,
    )(*args)


def _gelu_tanh(x):
    return 0.5 * x * (1.0 + jnp.tanh(np.float32(np.sqrt(2.0 / np.pi)) * (x + 0.044715 * (x * x * x))))


def _compress_kernel(x_ref, pe_ref, w1_ref, w2_ref, o_ref):
    x = x_ref[0]
    half = x.shape[1]
    top = _dot((x + pe_ref[0, :, :half]).astype(BF16), w1_ref[0, :half, :])
    bot = _dot((x + pe_ref[0, :, half:]).astype(BF16), w1_ref[0, half:, :])
    hid = top + pltpu.roll(bot, bot.shape[0] - 1, 0)
    o_ref[0, 0] = _dot(_gelu_tanh(hid).astype(BF16), w2_ref[0]).astype(o_ref.dtype)


def _compress(kvc, p, batch, seq):
    T = kvc.shape[1]
    rows = seq // CMP_STRIDE
    x = kvc.reshape(4, T // CMP_STRIDE, CMP_STRIDE * HEAD_DIM)
    pe = jnp.stack([p["cmp_pe_k"].reshape(1, -1), p["cmp_pe_v"].reshape(1, -1)]).astype(F32)
    w1 = jnp.stack([p["cmp_w1_k"], p["cmp_w1_v"]]).astype(BF16)
    w2k = jnp.pad(p["cmp_w2_k"], ((0, 0), (0, LANES - HEAD_DIM)))
    w2v = jnp.concatenate([p["cmp_w2_v"], p["cmp_w2_v"]], axis=1)
    w2 = jnp.stack([w2k, w2v]).astype(BF16)
    return pl.pallas_call(
        _compress_kernel, grid=(4, batch),
        in_specs=[pl.BlockSpec((1, rows, x.shape[2]), lambda j, b: (j, b, 0)),
                  pl.BlockSpec((1,) + pe.shape[1:], lambda j, b: (j // 2, 0, 0)),
                  pl.BlockSpec((1,) + w1.shape[1:], lambda j, b: (j // 2, 0, 0)),
                  pl.BlockSpec((1,) + w2.shape[1:], lambda j, b: (j // 2, 0, 0))],
        out_specs=pl.BlockSpec((1, 1, rows, LANES), lambda j, b: (b, j, 0, 0)),
        out_shape=jax.ShapeDtypeStruct((batch, 4, rows, LANES), BF16),
        compiler_params=_cparams(2), name="nsa_compress",
    )(x, pe, w1, w2)


def _cmp_select_kernel(qn_ref, qr_ref, kc_ref, vc_ref, ovt_ref, o_ref, qa_ref, *, tq):
    i = pl.program_id(2)
    n_cmp = kc_ref.shape[2]
    kc, vc = kc_ref[0, 0], vc_ref[0, 0]
    t_row = i * tq + lax.broadcasted_iota(jnp.int32, (tq, n_cmp), 0)
    blk_end = CMP_STRIDE * lax.broadcasted_iota(jnp.int32, (tq, n_cmp), 1) + (CMP_LEN - 1)
    vis = blk_end <= t_row
    lane = lax.broadcasted_iota(jnp.int32, (tq, LANES), 1)
    sees_any = t_row[:, :1] >= CMP_LEN - 1
    psum = jnp.zeros((tq, n_cmp), F32)
    outs = []
    for r in range(NSA_REP):
        s = jnp.where(vis, _dot_nt(qn_ref[:, r * LANES:(r + 1) * LANES], kc), NEG)
        m = jnp.max(s, axis=-1, keepdims=True)
        e = jnp.exp2(s - m)
        pn = e * jnp.where(sees_any, 1.0 / jnp.sum(e, axis=-1, keepdims=True), 0.0)
        psum = psum + pn
        outs.append(_dot(pn.astype(BF16), vc))
    for k in range(NSA_REP // 2):
        o_ref[:, k * LANES:(k + 1) * LANES] = jnp.where(lane < HEAD_DIM, outs[2 * k], outs[2 * k + 1])

    p_hi = psum.astype(BF16)
    p_lo = (psum - p_hi.astype(F32)).astype(BF16)
    ovt = ovt_ref[...]
    imp = _dot_nt(ovt, p_hi) + _dot_nt(ovt, p_lo)
    n_slc = LANES - HEAD_DIM
    jb = lax.broadcasted_iota(jnp.int32, (LANES, tq), 0) - HEAD_DIM
    cur = (i * tq + lax.broadcasted_iota(jnp.int32, (LANES, tq), 1)) // SLC_LEN
    forced = (jb == 0) | (jb == cur) | (jb == cur - 1)
    imp = jnp.where(forced, 1e9, jnp.where(jb > cur, -1e9, imp))
    groups = [imp[HEAD_DIM + 8 * v:HEAD_DIM + 8 * (v + 1)] for v in range(n_slc // 8)]
    sub = lax.broadcasted_iota(jnp.int32, (8, tq), 0)
    ranks = [jnp.zeros((8, tq), F32) for _ in groups]
    for a in range(n_slc):
        row = jnp.broadcast_to(groups[a // 8][a % 8:a % 8 + 1], (8, tq))
        for v in range(len(groups)):
            if v > a // 8:
                inc = jnp.where(row >= groups[v], 1.0, 0.0)
            elif v < a // 8:
                inc = jnp.where(row > groups[v], 1.0, 0.0)
            else:
                tie = jnp.where(sub > a % 8, 1.0, 0.0)
                inc = jnp.where(row > groups[v], 1.0, jnp.where(row == groups[v], tie, 0.0))
            ranks[v] = ranks[v] + inc
    bias_t = jnp.concatenate(
        [jnp.zeros((HEAD_DIM, tq), F32)]
        + [jnp.where(rk < SLC_TOPK, 0.0, SEL_NEG) for rk in ranks], axis=0)
    bias = bias_t.T
    for r in range(NSA_REP):
        qa_ref[:, r * LANES:(r + 1) * LANES] = (
            qr_ref[:, r * LANES:(r + 1) * LANES].astype(F32) + bias).astype(BF16)


def _overlap_t(seq):
    n_cmp = (seq - CMP_LEN) // CMP_STRIDE + 1
    n_slc = seq // SLC_LEN
    assert n_slc == LANES - HEAD_DIM, "selection blocks must fill the bias half of a slab"
    c0 = CMP_STRIDE * np.arange(n_cmp)[:, None]
    s0 = SLC_LEN * np.arange(n_slc)[None, :]
    ov = np.clip(np.minimum(c0 + CMP_LEN, s0 + SLC_LEN) - np.maximum(c0, s0), 0, None) / CMP_LEN
    out = np.zeros((LANES, seq // CMP_STRIDE), np.float32)
    out[HEAD_DIM:, :n_cmp] = ov.T
    return jnp.asarray(out, BF16)


def _cmp_select(qn, qr, cmp_kv, batch, seq):
    T = qn.shape[0]
    nq = seq // TQ_SELECT
    rows = seq // CMP_STRIDE
    gw = NSA_REP * LANES
    qspec = pl.BlockSpec((TQ_SELECT, gw), lambda b, g, i: (b * nq + i, g))
    return pl.pallas_call(
        functools.partial(_cmp_select_kernel, tq=TQ_SELECT), grid=(batch, NSA_GROUPS, nq),
        in_specs=[qspec, qspec,
                  pl.BlockSpec((1, 1, rows, LANES), lambda b, g, i: (b, g, 0, 0)),
                  pl.BlockSpec((1, 1, rows, LANES), lambda b, g, i: (b, NSA_GROUPS + g, 0, 0)),
                  _const_spec((LANES, rows))],
        out_specs=[pl.BlockSpec((TQ_SELECT, NSA_REP * HEAD_DIM), lambda b, g, i: (b * nq + i, g)), qspec],
        out_shape=[jax.ShapeDtypeStruct((T, NSA_HEADS * HEAD_DIM), F32),
                   jax.ShapeDtypeStruct((T, NSA_HEADS * LANES), BF16)],
        compiler_params=_cparams(3), name="nsa_cmp_select",
    )(qn, qr, cmp_kv, cmp_kv, _overlap_t(seq))


def _merge_kernel(x_ref, oc_ref, os_ref, ow_ref, gl_ref, of_ref, om_ref, mg_ref,
                  e_ref, wn_ref, wf_ref, wm_ref, wo_ref, o_ref):
    d = x_ref.shape[1]
    nw = oc_ref.shape[1]
    mg = mg_ref[...]
    merged = mg[:, d:2 * d] * _dot(of_ref[...], wf_ref[...]) + mg[:, 2 * d:] * _dot(om_ref[...], wm_ref[...])
    g = _dot(jax.nn.sigmoid(gl_ref[...]).astype(BF16), e_ref[...])
    o_nsa = g[:, :nw] * oc_ref[...] + g[:, nw:2 * nw] * os_ref[...] + g[:, 2 * nw:] * ow_ref[...]
    merged = merged + mg[:, :d] * _dot(o_nsa.astype(BF16), wn_ref[...])
    o_ref[...] = x_ref[...] + _dot(merged.astype(BF16), wo_ref[...])


def _gate_expand():
    e = np.zeros((LANES, 3 * NSA_HEADS * HEAD_DIM), np.float32)
    for h in range(NSA_HEADS):
        for c in range(3):
            e[h * 3 + c, c * NSA_HEADS * HEAD_DIM + h * HEAD_DIM:
              c * NSA_HEADS * HEAD_DIM + (h + 1) * HEAD_DIM] = 1.0
    return jnp.asarray(e, BF16)


def _merge(x2, o_cmp, o_slc, o_win, gate_l, o_fox, o_mla, mg, p):
    T, D = x2.shape
    tok = lambda a: pl.BlockSpec((TM, a.shape[1]), lambda i: (i, 0))
    acts = [x2, o_cmp, o_slc, o_win, gate_l, o_fox, o_mla, mg]
    weights = [_gate_expand(), p["w_br_nsa"].astype(BF16), p["w_br_fox"].astype(BF16),
               p["w_br_mla"].astype(BF16), p["w_out"].astype(BF16)]
    return pl.pallas_call(
        _merge_kernel, grid=(T // TM,),
        in_specs=[tok(a) for a in acts] + [_const_spec(w.shape) for w in weights],
        out_specs=pl.BlockSpec((TM, D), lambda i: (i, 0)),
        out_shape=jax.ShapeDtypeStruct((T, D), F32),
        compiler_params=_cparams(1), name="merge_out",
    )(*acts, *weights)


def _ffn_kernel(x_ref, g_ref, wu_ref, wv_ref, cw_ref, cb_ref, wd_ref, fin_ref, o_ref, tail_ref,
                *, seq, n_chunks, final):
    tm = x_ref.shape[0]
    x = x_ref[...]
    hb = _rms(x, g_ref[...]).astype(BF16)

    @pl.when((pl.program_id(0) * tm) % seq == 0)
    def _():
        tail_ref[...] = jnp.zeros(tail_ref.shape, F32)

    cw = wu_ref.shape[1] // n_chunks
    row8 = lax.broadcasted_iota(jnp.int32, (8, cw), 0)
    acc = jnp.zeros(x.shape, F32)
    for c in range(n_chunks):
        sl = slice(c * cw, (c + 1) * cw)
        u = _dot(hb, wu_ref[:, sl])
        v = _dot(hb, wv_ref[:, sl])
        prev = tail_ref[:, sl]
        tail_ref[:, sl] = u[tm - 8:]
        u1 = pltpu.roll(u, 1, 0)
        u2 = pltpu.roll(u, 2, 0)
        h1 = jnp.where(row8 < 1, pltpu.roll(prev, 1, 0), u1[:8])
        h2 = jnp.where(row8 < 2, pltpu.roll(prev, 2, 0), u2[:8])
        u1 = jnp.concatenate([h1, u1[8:]], axis=0)
        u2 = jnp.concatenate([h2, u2[8:]], axis=0)
        uc = cw_ref[0:1, sl] * u2 + cw_ref[1:2, sl] * u1 + cw_ref[2:3, sl] * u + cb_ref[:, sl]
        act = (uc * jax.nn.sigmoid(uc) * v).astype(BF16)
        acc = acc + _dot(act, wd_ref[sl, :])
    y = x + acc
    if final:
        y = _rms(y, fin_ref[...])
    o_ref[...] = y


def _ffn(x2, p, final_norm, seq, final):
    T, D = x2.shape
    w_up = p["w_up"]
    wu, wv = w_up[:, :D_FF].astype(BF16), w_up[:, D_FF:].astype(BF16)
    row = lambda a: a.reshape(1, -1).astype(F32)
    cwp = jnp.pad(p["conv_w"].astype(F32), ((0, 8 - CONV_W), (0, 0)))
    weights = [row(p["ffn_norm"]), wu, wv, cwp, row(p["conv_b"]), p["w_down"].astype(BF16),
               row(final_norm)]
    return pl.pallas_call(
        functools.partial(_ffn_kernel, seq=seq, n_chunks=2, final=final), grid=(T // TM,),
        in_specs=[pl.BlockSpec((TM, D), lambda i: (i, 0))] + [_const_spec(w.shape) for w in weights],
        out_specs=pl.BlockSpec((TM, D), lambda i: (i, 0)),
        out_shape=jax.ShapeDtypeStruct((T, D), F32),
        scratch_shapes=[pltpu.VMEM((8, D_FF), F32)],
        compiler_params=_cparams(1), name="conv_ffn",
    )(x2, *weights)


_LAYER_PARAMS = ("mix_norm", "w_in", "b_forget", "cmp_pe_k", "cmp_w1_k", "cmp_w2_k", "cmp_pe_v",
                 "cmp_w1_v", "cmp_w2_v", "mla_q_norm", "mla_w_uq", "mla_kv_norm", "mla_w_ukv",
                 "w_br_nsa", "w_br_fox", "w_br_mla", "w_out", "ffn_norm", "w_up", "conv_w",
                 "conv_b", "w_down")


def _mixer_layer(h, tabs, p, batch, seq):
    (qn, qr, kvc, ksl, vsl_t, kwin, vwin_t, gate_l,
     fq, fk, fv_t, mq, mk, mv_t, mg) = _inproj(h, tabs, p, seq)
    cmp_kv = _compress(kvc, p, batch, seq)
    o_cmp, qa = _cmp_select(qn, qr, cmp_kv, batch, seq)
    group = (0,) * NSA_REP
    own = (0, 1, 2, 3)
    o_slc, o_win = _attention([(qa, ksl, vsl_t, False), (qa, kwin, vwin_t, True)], batch, seq, group,
                              F32, "nsa_slc_win_attn")
    o_fox, o_mla = _attention([(fq, fk, fv_t, False), (mq, mk, mv_t, False)], batch, seq, own, BF16,
                              "fox_mla_attn")
    return _merge(h, o_cmp, o_slc, o_win, gate_l, o_fox, o_mla, mg, p)


def kernel(x, positions, mix_norm, w_in, b_forget, cmp_pe_k, cmp_w1_k, cmp_w2_k, cmp_pe_v, cmp_w1_v,
           cmp_w2_v, mla_q_norm, mla_w_uq, mla_kv_norm, mla_w_ukv, w_br_nsa, w_br_fox, w_br_mla,
           w_out, ffn_norm, w_up, conv_w, conv_b, w_down, final_norm):
    batch, seq, d = x.shape
    stacked = dict(zip(_LAYER_PARAMS, (
        mix_norm, w_in, b_forget, cmp_pe_k, cmp_w1_k, cmp_w2_k, cmp_pe_v, cmp_w1_v, cmp_w2_v,
        mla_q_norm, mla_w_uq, mla_kv_norm, mla_w_ukv, w_br_nsa, w_br_fox, w_br_mla, w_out,
        ffn_norm, w_up, conv_w, conv_b, w_down)))
    depth = w_in.shape[0]
    tabs = _rope_tables(positions)
    h = x.reshape(batch * seq, d)
    for l in range(depth):
        p = {k: v[l] for k, v in stacked.items()}
        h = _mixer_layer(h, tabs, p, batch, seq)
        h = _ffn(h, p, final_norm, seq, final=(l == depth - 1))
    return h.reshape(batch, seq, d)
```

```python
import functools

import numpy as np
import jax
import jax.numpy as jnp
from jax import lax
from jax.experimental import pallas as pl
from jax.experimental.pallas import tpu as pltpu

F32 = jnp.float32
BF16 = jnp.bfloat16

HEAD_DIM = 64
ROPE_THETA = 500000.0
PARTIAL_ROT = HEAD_DIM // 4
NORM_EPS = 1e-6
NSA_HEADS = 8
NSA_GROUPS = 2
NSA_REP = NSA_HEADS // NSA_GROUPS
CMP_LEN = 32
CMP_STRIDE = 16
SLC_LEN = 64
SLC_TOPK = 16
WIN = 512
FOX_HEADS = 8
MLA_HEADS = 8
MLA_Q_LORA = 384
MLA_KV_LORA = 256
MLA_NOPE = 64
MLA_ROPE = 32
MLA_V = 64
D_FF = 2816
D_MODEL = 1024
CONV_W = 3

LANES = 128
LOG2E = 1.4426950408889634
NEG = -1e30
SEL_NEG = -1e9
VMEM_LIMIT = 52 * 1024 * 1024

TM = 256
TQ = 1024
TQ_PAIRS = 2048
TQ_SELECT = 512
TK = 512
QC = 256
LOOP_TILES = 4
SKEW = 2
DECAY_TERMS = 3


def _cparams(n_axes):
    return pltpu.CompilerParams(dimension_semantics=("arbitrary",) * n_axes,
                                vmem_limit_bytes=VMEM_LIMIT)


def _const_spec(shape):
    nd = len(shape)
    return pl.BlockSpec(shape, lambda *_: (0,) * nd, pipeline_mode=pl.Buffered(1))


def _rms(x, g):
    return x * lax.rsqrt(jnp.mean(x * x, axis=-1, keepdims=True) + NORM_EPS) * g


def _dot(a, b):
    return jnp.dot(a, b, preferred_element_type=F32)


def _dot_nt(a, b):
    return lax.dot_general(a, b, (((1,), (1,)), ((), ())), preferred_element_type=F32)


def _rope_tables_kernel(pos_ref, freq_ref, g16_ref, g32_ref, c16_o, s16_o, c32_o, s32_o):
    ang = pos_ref[...].astype(F32) * freq_ref[...]
    cos, sin = jnp.cos(ang), jnp.sin(ang)
    lane = lax.broadcasted_iota(jnp.int32, ang.shape, 1)
    low = lane < PARTIAL_ROT
    high = (lane >= HEAD_DIM) & (lane < HEAD_DIM + PARTIAL_ROT)
    mla = (lane >= MLA_NOPE) & (lane < MLA_NOPE + MLA_ROPE)
    c16_o[...] = jnp.where(low, cos, jnp.where(high, pltpu.roll(cos, HEAD_DIM, 1), 1.0))
    s16_o[...] = jnp.where(low, sin, jnp.where(high, pltpu.roll(sin, HEAD_DIM, 1), 0.0)) * g16_ref[...]
    c32_o[...] = jnp.where(mla, pltpu.roll(cos, MLA_ROPE, 1), 1.0)
    s32_o[...] = jnp.where(mla, pltpu.roll(sin, MLA_ROPE, 1), 0.0) * g32_ref[...]


def _rope_tables(positions):
    T = positions.size
    half16, half32 = PARTIAL_ROT // 2, MLA_ROPE // 2
    inv16 = ROPE_THETA ** (-jnp.arange(half16, dtype=F32) / half16)
    inv32 = ROPE_THETA ** (-jnp.arange(half32, dtype=F32) / half32)
    lane = np.arange(LANES)
    freq = jnp.where(lane < PARTIAL_ROT, inv16[lane % half16],
                     jnp.where((lane >= MLA_ROPE) & (lane < 2 * MLA_ROPE), inv32[lane % half32], 0.0))[None, :]
    d = lane % HEAD_DIM
    g16 = np.where(d < PARTIAL_ROT, np.where(d < half16, -1.0, 1.0), 0.0).astype(np.float32)[None, :]
    r = lane - MLA_NOPE
    g32 = np.where((r >= 0) & (r < MLA_ROPE), np.where(r < half32, -1.0, 1.0), 0.0).astype(np.float32)[None, :]
    pos = jnp.broadcast_to(positions.reshape(T, 1), (T, LANES))
    tt = 1024
    row = pl.BlockSpec((tt, LANES), lambda i: (i, 0))
    vec = _const_spec((1, LANES))
    out = jax.ShapeDtypeStruct((T, LANES), F32)
    return pl.pallas_call(
        _rope_tables_kernel, grid=(T // tt,),
        in_specs=[row, vec, vec, vec], out_specs=[row] * 4, out_shape=[out] * 4,
        compiler_params=_cparams(1), name="rope_tables",
    )(pos, freq, jnp.asarray(g16), jnp.asarray(g32))


def _rot16(x, cos, sin):
    lane = lax.broadcasted_iota(jnp.int32, x.shape, 1)
    first = (lane % HEAD_DIM) < (PARTIAL_ROT // 2)
    sw = jnp.where(first, pltpu.roll(x, LANES - PARTIAL_ROT // 2, 1), pltpu.roll(x, PARTIAL_ROT // 2, 1))
    return x * cos + sw * sin


def _rot32(x, cos, sin):
    lane = lax.broadcasted_iota(jnp.int32, x.shape, 1)
    first = lane < (MLA_NOPE + MLA_ROPE // 2)
    sw = jnp.where(first, pltpu.roll(x, LANES - MLA_ROPE // 2, 1), pltpu.roll(x, MLA_ROPE // 2, 1))
    return x * cos + sw * sin


_FF_LANE = 3 * NSA_HEADS
_IN_COLS = {}
for _name, _n in (("q", NSA_HEADS * HEAD_DIM), ("kvc", 4 * HEAD_DIM), ("ksl", LANES), ("kwin", LANES),
                  ("misc", LANES), ("fq", FOX_HEADS * HEAD_DIM), ("fk", FOX_HEADS * HEAD_DIM),
                  ("lat", MLA_Q_LORA + MLA_KV_LORA), ("mg", 3 * D_MODEL)):
    _IN_COLS[_name] = (sum(n for _, n in _IN_COLS.values()), _n)
_IN_ROWS = {"vsl": (0, LANES), "vwin": (LANES, LANES), "fv": (2 * LANES, FOX_HEADS * HEAD_DIM)}


def _inproj_kernel(x_ref, g_ref, c16_ref, s16_ref, c32_ref, s32_ref,
                   w_ref, wt_ref, bf_ref, tri_ref, place_ref,
                   qnrm_ref, wuq_ref, kvnrm_ref, wuk_ref, wuv_ref,
                   qn_o, qr_o, kvc_o, ksl_o, vsl_o, kwin_o, vwin_o, gate_o,
                   fq_o, fk_o, fv_o, mq_o, mk_o, mv_o, mg_o, fcarry_ref, *, seq):
    tm = x_ref.shape[0]
    seq_start = (pl.program_id(0) * tm) % seq == 0
    hb = _rms(x_ref[...], g_ref[...]).astype(BF16)
    c16, s16 = c16_ref[...], s16_ref[...]
    c32, s32 = c32_ref[...], s32_ref[...]
    lane = lax.broadcasted_iota(jnp.int32, (tm, LANES), 1)
    low = lane < HEAD_DIM

    def proj(name):
        a, n = _IN_COLS[name]
        return _dot(hb, w_ref[:, a:a + n])

    def proj_t(name):
        a, n = _IN_ROWS[name]
        return _dot_nt(wt_ref[a:a + n, :], hb)

    def slabs(z):
        return [z[:, k * LANES:(k + 1) * LANES] for k in range(z.shape[1] // LANES)]

    def split(z, fill=0.0):
        return jnp.where(low, z, fill), jnp.where(low, pltpu.roll(z, HEAD_DIM, 1), fill)

    def store_heads(o_ref, pair_slabs, fill=0.0, extra=None):
        for k, z in enumerate(pair_slabs):
            for e, s in enumerate(split(z, fill)):
                h = 2 * k + e
                if extra is not None:
                    s = s + extra[:, h * LANES:(h + 1) * LANES]
                o_ref[:, h * LANES:(h + 1) * LANES] = s.astype(o_ref.dtype)

    misc = proj("misc")
    gate_o[...] = misc

    @pl.when(seq_start)
    def _():
        fcarry_ref[...] = jnp.zeros(fcarry_ref.shape, F32)

    def bf16_terms(v):
        packed = None
        for t in range(DECAY_TERMS):
            part = v.astype(BF16).astype(F32)
            v = v - part
            part = part if t == 0 else pltpu.roll(part, FOX_HEADS * t, 1)
            packed = part if packed is None else packed + part
        return packed.astype(BF16)

    zf = misc + bf_ref[...]
    is_f = (lane >= _FF_LANE) & (lane < _FF_LANE + FOX_HEADS)
    logf_terms = bf16_terms(
        jnp.where(is_f, jnp.minimum(zf, 0.0) - jnp.log1p(jnp.exp(-jnp.abs(zf))), 0.0))

    zq = slabs(proj("q"))
    store_heads(qn_o, zq)
    store_heads(qr_o, [_rot16(z, c16, s16) for z in zq])

    terms = _dot(tri_ref[...], logf_terms)
    fsum = sum(terms if t == 0 else pltpu.roll(terms, LANES - FOX_HEADS * t, 1)
               for t in range(DECAY_TERMS))
    fsum = jnp.where(is_f, fsum, 0.0) + fcarry_ref[0:1]
    fcarry_ref[...] = jnp.broadcast_to(fsum[tm - 1:tm], fcarry_ref.shape)
    decay_terms = bf16_terms(jnp.where(is_f, -LOG2E * fsum, 0.0))

    for k, z in enumerate(slabs(proj("kvc"))):
        for e, s in enumerate(split(z)):
            kvc_o[2 * k + e] = s[:, :HEAD_DIM]

    tok = (pl.program_id(0) * tm) % seq + lax.broadcasted_iota(jnp.int32, (tm, LANES), 0)
    onehot = jnp.where(lane - HEAD_DIM == tok // SLC_LEN, 1.0, 0.0)
    store_heads(ksl_o, [_rot16(proj("ksl"), c16, s16)], fill=onehot)
    vsl_o[0] = proj_t("vsl").astype(BF16)

    store_heads(kwin_o, [_rot16(proj("kwin"), c16, s16)])
    vwin_o[0] = proj_t("vwin").astype(BF16)

    ones = jnp.where((lane >= HEAD_DIM) & (lane < HEAD_DIM + DECAY_TERMS), 1.0, 0.0)
    store_heads(fq_o, slabs(proj("fq")), fill=ones)
    fv_o[0] = proj_t("fv").astype(BF16)
    decay = _dot(decay_terms, place_ref[...])

    lat = proj("lat")
    cq = _rms(lat[:, :MLA_Q_LORA], qnrm_ref[...]).astype(BF16)
    ckv = _rms(lat[:, MLA_Q_LORA:], kvnrm_ref[...]).astype(BF16)
    kpe = jnp.where(low, 0.0, _rot32(misc, c32, s32))
    store_heads(fk_o, slabs(proj("fk")), extra=decay)
    zuq = _dot(cq, wuq_ref[...])
    for k, z in enumerate(slabs(zuq)):
        mq_o[:, k * LANES:(k + 1) * LANES] = _rot32(z, c32, s32).astype(BF16)
    zuk = _dot(ckv, wuk_ref[...])
    for k, z in enumerate(slabs(zuk)):
        mk_o[:, k * LANES:(k + 1) * LANES] = (z + kpe).astype(BF16)
    mv_o[0] = _dot_nt(wuv_ref[...], ckv).astype(BF16)

    mg_o[...] = jax.nn.sigmoid(proj("mg"))


def _pad_halves(w, first_only=True):
    K, n = w.shape[0], w.shape[1] // HEAD_DIM
    w = w.reshape(K, n, HEAD_DIM)
    other = jnp.zeros_like(w) if first_only else w
    return jnp.concatenate([w, other], axis=-1).reshape(K, n * LANES)


def _inproj(x2, tabs, p, seq):
    T, D = x2.shape
    w_in = p["w_in"]
    scale = HEAD_DIM ** -0.5 * LOG2E
    o = 0

    def take(n):
        nonlocal o
        w = w_in[:, o:o + n]
        o += n
        return w

    nsa_w, kvw = NSA_HEADS * HEAD_DIM, NSA_GROUPS * HEAD_DIM
    w_q = take(nsa_w)
    w_kc, w_vc, w_ks, w_vs, w_kw, w_vw = [take(kvw) for _ in range(6)]
    w_g = take(3 * NSA_HEADS)
    w_fq, w_fk, w_fv = [take(FOX_HEADS * HEAD_DIM) for _ in range(3)]
    w_ff = take(FOX_HEADS)
    w_cq, w_ckv, w_kr = take(MLA_Q_LORA), take(MLA_KV_LORA), take(MLA_ROPE)
    w_mg = take(3 * D)

    bf = lambda w: w.astype(BF16)
    row = lambda a: a.reshape(1, -1).astype(F32)
    w_misc = jnp.concatenate([w_g, w_ff, jnp.zeros((D, MLA_NOPE - _FF_LANE - FOX_HEADS), F32), w_kr,
                              jnp.zeros((D, LANES - MLA_NOPE - MLA_ROPE), F32)], axis=1)
    w_tok = bf(jnp.concatenate([w_q * scale, w_kc, w_vc, w_ks, w_kw, w_misc, w_fq * scale, w_fk,
                                w_cq, w_ckv, w_mg], axis=1))
    w_tr = bf(jnp.concatenate([w_vs, w_vw, w_fv], axis=1).T)
    b_f = jnp.pad(row(p["b_forget"]), ((0, 0), (_FF_LANE, LANES - _FF_LANE - FOX_HEADS)))
    tri = jnp.asarray(np.tril(np.ones((TM, TM), np.float32)), BF16)
    place = np.zeros((LANES, FOX_HEADS * LANES), np.float32)
    for t in range(DECAY_TERMS):
        for h in range(FOX_HEADS):
            place[_FF_LANE + FOX_HEADS * t + h, h * LANES + HEAD_DIM + t] = 1.0
    place = jnp.asarray(place, BF16)
    dq = MLA_NOPE + MLA_ROPE
    wuq = (p["mla_w_uq"] * (dq ** -0.5 * LOG2E)).reshape(MLA_Q_LORA, MLA_HEADS, dq)
    wuq = bf(jnp.pad(wuq, ((0, 0), (0, 0), (0, LANES - dq))).reshape(MLA_Q_LORA, MLA_HEADS * LANES))
    wukv = p["mla_w_ukv"].reshape(MLA_KV_LORA, MLA_HEADS, MLA_NOPE + MLA_V)
    wuk = bf(_pad_halves(wukv[:, :, :MLA_NOPE].reshape(MLA_KV_LORA, MLA_HEADS * MLA_NOPE)))
    wuv = bf(wukv[:, :, MLA_NOPE:].reshape(MLA_KV_LORA, MLA_HEADS * MLA_V).T)
    assert w_tok.shape[1] == sum(n for _, n in _IN_COLS.values())

    weights = [w_tok, w_tr, b_f, tri, place,
               row(p["mla_q_norm"]), wuq, row(p["mla_kv_norm"]), wuk, wuv]
    n = T // TM
    tok = lambda c: pl.BlockSpec((TM, c), lambda i: (i, 0))
    in_specs = ([tok(D), _const_spec((1, D))] + [tok(LANES)] * 4
                + [_const_spec(w.shape) for w in weights])
    tokens, transposed = "tokens", "transposed"
    outs = [
        (tokens, NSA_HEADS * LANES, BF16), (tokens, NSA_HEADS * LANES, BF16), None,
        (tokens, NSA_GROUPS * LANES, BF16), (transposed, NSA_GROUPS * HEAD_DIM, BF16),
        (tokens, NSA_GROUPS * LANES, BF16), (transposed, NSA_GROUPS * HEAD_DIM, BF16), (tokens, LANES, F32),
        (tokens, FOX_HEADS * LANES, BF16), (tokens, FOX_HEADS * LANES, BF16),
        (transposed, FOX_HEADS * HEAD_DIM, BF16),
        (tokens, MLA_HEADS * LANES, BF16), (tokens, MLA_HEADS * LANES, BF16),
        (transposed, MLA_HEADS * MLA_V, BF16), (tokens, 3 * D, F32)]
    out_shape, out_specs = [], []
    for spec in outs:
        if spec is None:
            out_shape.append(jax.ShapeDtypeStruct((4, T, HEAD_DIM), F32))
            out_specs.append(pl.BlockSpec((4, TM, HEAD_DIM), lambda i: (0, i, 0)))
        elif spec[0] == transposed:
            out_shape.append(jax.ShapeDtypeStruct((n, spec[1], TM), spec[2]))
            out_specs.append(pl.BlockSpec((1, spec[1], TM), lambda i: (i, 0, 0)))
        else:
            out_shape.append(jax.ShapeDtypeStruct((T, spec[1]), spec[2]))
            out_specs.append(tok(spec[1]))
    return pl.pallas_call(
        functools.partial(_inproj_kernel, seq=seq), grid=(n,),
        in_specs=in_specs, out_specs=out_specs, out_shape=out_shape,
        scratch_shapes=[pltpu.VMEM((8, LANES), F32)],
        compiler_params=_cparams(1), name="inproj",
    )(x2, row(p["mix_norm"]), *tabs, *weights)


def _attn_kernel(*refs, tq, tk, q_of_stream, window_of_stream, k_of_head, values_by_unit):
    n_streams, n_q = len(q_of_stream), max(q_of_stream) + 1
    q_refs, kv_refs = refs[:n_q], refs[n_q:n_q + 2 * n_streams]
    outs = refs[n_q + 2 * n_streams:n_q + 3 * n_streams]
    qt_ref, m_ref, l_ref, acc_ref = refs[n_q + 3 * n_streams:]
    i = pl.program_id(2)
    hs = len(k_of_head)
    nh = n_streams * hs
    k_of = lambda h: kv_refs[2 * (h // hs)]
    vt_of = lambda h: kv_refs[2 * (h // hs) + 1]
    qt_of = lambda h: q_of_stream[h // hs] * hs + h % hs
    per_q = tq // tk
    tv = kv_refs[1].shape[2]
    v_sub = tk // tv
    for qi in range(n_q):
        for r in range(hs):
            qt_ref[qi * hs + r] = q_refs[qi][:, r * LANES:(r + 1) * LANES].astype(F32).T.astype(BF16)
    dist0 = (i * tq + lax.broadcasted_iota(jnp.int32, (tk, tq), 1)
             - lax.broadcasted_iota(jnp.int32, (tk, tq), 0))

    m_ref[...] = jnp.full(m_ref.shape, NEG, F32)
    l_ref[...] = jnp.zeros(l_ref.shape, F32)
    acc_ref[...] = jnp.zeros(acc_ref.shape, F32)

    def logits(j, lo, n, h, cols):
        r0 = pl.multiple_of(j * tk + lo, tv)
        kk = k_of_head[h % hs]
        return _dot(k_of(h)[pl.ds(r0, n), kk * LANES:(kk + 1) * LANES], qt_ref[qt_of(h), :, cols])

    def softmax(s, h, cols, mask):
        if mask is not None:
            s = jnp.where(mask, s, NEG)
        m_old = m_ref[h, :, cols]
        m_new = jnp.maximum(m_old, jnp.max(s, axis=0, keepdims=True))
        alpha = jnp.exp2(m_old - m_new)
        p = jnp.exp2(s - m_new)
        l_ref[h, :, cols] = alpha * l_ref[h, :, cols] + jnp.sum(p, axis=0, keepdims=True)
        m_ref[h, :, cols] = m_new
        return p.astype(BF16), alpha

    def accumulate(j, lo, n, h, cols, p, alpha):
        if values_by_unit:
            half = pl.ds(pl.multiple_of(pl.program_id(1) * HEAD_DIM, HEAD_DIM), HEAD_DIM)
        else:
            half = slice((h % 2) * HEAD_DIM, (h % 2 + 1) * HEAD_DIM)
        pv = sum(_dot(vt_of(h)[j * v_sub + lo // tv + c, half, :], p[c * tv:(c + 1) * tv])
                 for c in range(n // tv))
        acc_ref[h, :, cols] = alpha * acc_ref[h, :, cols] + pv

    def sweep(tiles):
        work = []
        for heads, j, n_tiles, mask, span in tiles:
            for h in heads:
                for c in range(tq // QC):
                    lo, n, masked = (0, n_tiles * tk, False) if span is None else span(c)
                    if n > 0:
                        work.append((j, lo, n, mask if masked else None, h, slice(c * QC, (c + 1) * QC)))
        s_of, p_of = {}, {}
        for t in range(len(work) + 2 * SKEW):
            if 0 <= t - 2 * SKEW < len(work):
                j, lo, n, _, h, cols = work[t - 2 * SKEW]
                accumulate(j, lo, n, h, cols, *p_of.pop(t - 2 * SKEW))
            if 0 <= t - SKEW < len(work):
                _, lo, n, mask, h, cols = work[t - SKEW]
                p_of[t - SKEW] = softmax(s_of.pop(t - SKEW), h, cols,
                                         None if mask is None else mask(slice(lo, lo + n), cols))
            if t < len(work):
                j, lo, n, _, h, cols = work[t]
                s_of[t] = logits(j, lo, n, h, cols)

    def visible(j, window, valid=None):
        def mask(keys, cols):
            dist = dist0[keys, cols] - j * tk
            if not window:
                return dist >= 0
            if valid is not None:
                dist = dist + jnp.where(valid, 0, WIN)
            return lax.bitcast_convert_type(dist, jnp.uint32) < WIN
        return mask

    assert tk == 2 * QC and WIN % tk == 0 and LOOP_TILES % per_q == 0
    causal = [h for h in range(nh) if not window_of_stream[h // hs]]
    windowed = [h for h in range(nh) if window_of_stream[h // hs]]
    first = i * per_q

    def causal_span(d):
        def span(c):
            hi = min(tk, (c + 1) * QC - d * tk)
            return 0, hi, d * tk + hi > c * QC + 1
        return span

    def window_span(d):
        def span(c):
            lo = max(0, (c * QC - WIN + 1) // tv * tv - d * tk)
            return lo, min(tk, (c + 1) * QC - d * tk) - lo, True
        return span

    one_step = len(causal) * (tq // QC) >= 8

    def unmasked(j0, n):
        tiles = []
        while n > 0:
            k = 2 if (one_step and n >= 2) else 1
            tiles.append((causal, j0, k, None, None))
            j0, n = j0 + k, n - k
        return tiles

    def last_tiles(n_left):
        tiles = []
        if causal:
            tiles += unmasked(first - n_left, n_left)
            tiles += [(causal, first + d, 1, visible(first + d, False), causal_span(d)) for d in range(per_q)]
        if windowed:
            tiles += [(windowed, first + d, 1, visible(first + d, True), window_span(d))
                      for d in reversed(range(per_q))]
            tiles += [(windowed, jnp.maximum(first + d, 0), 1,
                       visible(first + d, True, valid=first + d >= 0), window_span(d))
                      for d in range(-(WIN // tk), 0)]
        return tiles

    if causal:
        def body(jj, c):
            sweep(unmasked(LOOP_TILES * jj, LOOP_TILES))
            return c
        lax.fori_loop(0, first // LOOP_TILES, body, 0)
        for n_left in range(0, LOOP_TILES, per_q):
            @pl.when(first % LOOP_TILES == n_left)
            def _():
                sweep(last_tiles(n_left))
    else:
        sweep(last_tiles(0))

    for h in range(0, nh, 2):
        a = acc_ref[h] * (1.0 / l_ref[h])
        b = acc_ref[h + 1] * (1.0 / l_ref[h + 1])
        o_ref, k = outs[h // hs], (h % hs) // 2
        o_ref[:, k * LANES:(k + 1) * LANES] = jnp.concatenate([a, b], axis=0).T.astype(o_ref.dtype)


def _attention(streams, batch, seq, k_of_head, out_dtype, name, tq):
    qs = []
    for q, _, _, _ in streams:
        if not any(q is x for x in qs):
            qs.append(q)
    q_of_stream = tuple(next(n for n, x in enumerate(qs) if x is q) for q, _, _, _ in streams)
    T = qs[0].shape[0]
    nq = seq // tq
    hs, nk = len(k_of_head), max(k_of_head) + 1
    nh = hs * len(streams)
    units = qs[0].shape[1] // (hs * LANES)
    by_unit = nk == 1
    assert not by_unit or units * HEAD_DIM == LANES, "GQA value blocks must fill one slab"
    in_specs = [pl.BlockSpec((tq, hs * LANES), lambda b, u, i: (b * nq + i, u)) for _ in qs]
    args = list(qs)
    for q, k, vt, _ in streams:
        assert q.shape[1] == units * hs * LANES and k.shape[1] == units * nk * LANES
        assert vt.shape == (T // TM, units * (HEAD_DIM if by_unit else LANES), TM)
        in_specs += [pl.BlockSpec((seq, nk * LANES), lambda b, u, i: (b, u)),
                     pl.BlockSpec((seq // TM, LANES, TM),
                                  (lambda b, u, i: (b, 0, 0)) if by_unit else (lambda b, u, i: (b, u, 0)))]
        args += [k, vt]
    return pl.pallas_call(
        functools.partial(_attn_kernel, tq=tq, tk=TK, q_of_stream=q_of_stream,
                          window_of_stream=tuple(w for _, _, _, w in streams),
                          k_of_head=k_of_head, values_by_unit=by_unit),
        grid=(batch, units, nq), in_specs=in_specs,
        out_specs=[pl.BlockSpec((tq, hs // 2 * LANES), lambda b, u, i: (b * nq + i, u))] * len(streams),
        out_shape=[jax.ShapeDtypeStruct((T, units * hs // 2 * LANES), out_dtype)] * len(streams),
        scratch_shapes=[pltpu.VMEM((len(qs) * hs, LANES, tq), BF16), pltpu.VMEM((nh, 1, tq), F32),
                        pltpu.VMEM((nh, 1, tq), F32), pltpu.VMEM((nh, HEAD_DIM, tq), F32)],
        compiler_params=_cparams(3), name=name,
    )(*args)


def _gelu_tanh(x):
    return 0.5 * x * (1.0 + jnp.tanh(np.float32(np.sqrt(2.0 / np.pi)) * (x + 0.044715 * (x * x * x))))


def _compress_kernel(x_ref, pe_ref, w1_ref, w2_ref, o_ref):
    x = x_ref[0]
    half = x.shape[1]
    top = _dot((x + pe_ref[0, :, :half]).astype(BF16), w1_ref[0, :half, :])
    bot = _dot((x + pe_ref[0, :, half:]).astype(BF16), w1_ref[0, half:, :])
    hid = top + pltpu.roll(bot, bot.shape[0] - 1, 0)
    o_ref[0, 0] = _dot(_gelu_tanh(hid).astype(BF16), w2_ref[0]).astype(o_ref.dtype)


def _compress(kvc, p, batch, seq):
    T = kvc.shape[1]
    rows = seq // CMP_STRIDE
    x = kvc.reshape(4, T // CMP_STRIDE, CMP_STRIDE * HEAD_DIM)
    pe = jnp.stack([p["cmp_pe_k"].reshape(1, -1), p["cmp_pe_v"].reshape(1, -1)]).astype(F32)
    w1 = jnp.stack([p["cmp_w1_k"], p["cmp_w1_v"]]).astype(BF16)
    w2k = jnp.pad(p["cmp_w2_k"], ((0, 0), (0, LANES - HEAD_DIM)))
    w2v = jnp.concatenate([p["cmp_w2_v"], p["cmp_w2_v"]], axis=1)
    w2 = jnp.stack([w2k, w2v]).astype(BF16)
    return pl.pallas_call(
        _compress_kernel, grid=(4, batch),
        in_specs=[pl.BlockSpec((1, rows, x.shape[2]), lambda j, b: (j, b, 0)),
                  pl.BlockSpec((1,) + pe.shape[1:], lambda j, b: (j // 2, 0, 0)),
                  pl.BlockSpec((1,) + w1.shape[1:], lambda j, b: (j // 2, 0, 0)),
                  pl.BlockSpec((1,) + w2.shape[1:], lambda j, b: (j // 2, 0, 0))],
        out_specs=pl.BlockSpec((1, 1, rows, LANES), lambda j, b: (b, j, 0, 0)),
        out_shape=jax.ShapeDtypeStruct((batch, 4, rows, LANES), BF16),
        compiler_params=_cparams(2), name="nsa_compress",
    )(x, pe, w1, w2)


def _cmp_select_kernel(qn_ref, qr_ref, kc_ref, vc_ref, ovt_ref, o_ref, qa_ref, *, tq):
    i = pl.program_id(2)
    n_cmp = kc_ref.shape[2]
    kc, vc = kc_ref[0, 0], vc_ref[0, 0]
    t_row = i * tq + lax.broadcasted_iota(jnp.int32, (tq, n_cmp), 0)
    blk_end = CMP_STRIDE * lax.broadcasted_iota(jnp.int32, (tq, n_cmp), 1) + (CMP_LEN - 1)
    vis = blk_end <= t_row
    lane = lax.broadcasted_iota(jnp.int32, (tq, LANES), 1)
    sees_any = t_row[:, :1] >= CMP_LEN - 1
    psum = jnp.zeros((tq, n_cmp), F32)
    outs = []
    for r in range(NSA_REP):
        s = jnp.where(vis, _dot_nt(qn_ref[:, r * LANES:(r + 1) * LANES], kc), NEG)
        m = jnp.max(s, axis=-1, keepdims=True)
        e = jnp.exp2(s - m)
        pn = e * jnp.where(sees_any, 1.0 / jnp.sum(e, axis=-1, keepdims=True), 0.0)
        psum = psum + pn
        outs.append(_dot(pn.astype(BF16), vc))
    for k in range(NSA_REP // 2):
        o_ref[:, k * LANES:(k + 1) * LANES] = jnp.where(lane < HEAD_DIM, outs[2 * k], outs[2 * k + 1])

    p_hi = psum.astype(BF16)
    p_lo = (psum - p_hi.astype(F32)).astype(BF16)
    ovt = ovt_ref[...]
    imp = _dot_nt(ovt, p_hi) + _dot_nt(ovt, p_lo)
    n_slc = LANES - HEAD_DIM
    jb = lax.broadcasted_iota(jnp.int32, (LANES, tq), 0) - HEAD_DIM
    cur = (i * tq + lax.broadcasted_iota(jnp.int32, (LANES, tq), 1)) // SLC_LEN
    forced = (jb == 0) | (jb == cur) | (jb == cur - 1)
    imp = jnp.where(forced, 1e9, jnp.where(jb > cur, -1e9, imp))
    groups = [imp[HEAD_DIM + 8 * v:HEAD_DIM + 8 * (v + 1)] for v in range(n_slc // 8)]
    sub = lax.broadcasted_iota(jnp.int32, (8, tq), 0)
    ranks = [jnp.zeros((8, tq), F32) for _ in groups]
    for a in range(n_slc):
        row = jnp.broadcast_to(groups[a // 8][a % 8:a % 8 + 1], (8, tq))
        for v in range(len(groups)):
            if v > a // 8:
                inc = jnp.where(row >= groups[v], 1.0, 0.0)
            elif v < a // 8:
                inc = jnp.where(row > groups[v], 1.0, 0.0)
            else:
                tie = jnp.where(sub > a % 8, 1.0, 0.0)
                inc = jnp.where(row > groups[v], 1.0, jnp.where(row == groups[v], tie, 0.0))
            ranks[v] = ranks[v] + inc
    bias_t = jnp.concatenate(
        [jnp.zeros((HEAD_DIM, tq), F32)]
        + [jnp.where(rk < SLC_TOPK, 0.0, SEL_NEG) for rk in ranks], axis=0)
    bias = bias_t.T
    for r in range(NSA_REP):
        qa_ref[:, r * LANES:(r + 1) * LANES] = (
            qr_ref[:, r * LANES:(r + 1) * LANES].astype(F32) + bias).astype(BF16)


def _overlap_t(seq):
    n_cmp = (seq - CMP_LEN) // CMP_STRIDE + 1
    n_slc = seq // SLC_LEN
    assert n_slc == LANES - HEAD_DIM, "selection blocks must fill the bias half of a slab"
    c0 = CMP_STRIDE * np.arange(n_cmp)[:, None]
    s0 = SLC_LEN * np.arange(n_slc)[None, :]
    ov = np.clip(np.minimum(c0 + CMP_LEN, s0 + SLC_LEN) - np.maximum(c0, s0), 0, None) / CMP_LEN
    out = np.zeros((LANES, seq // CMP_STRIDE), np.float32)
    out[HEAD_DIM:, :n_cmp] = ov.T
    return jnp.asarray(out, BF16)


def _cmp_select(qn, qr, cmp_kv, batch, seq):
    T = qn.shape[0]
    nq = seq // TQ_SELECT
    rows = seq // CMP_STRIDE
    gw = NSA_REP * LANES
    qspec = pl.BlockSpec((TQ_SELECT, gw), lambda b, g, i: (b * nq + i, g))
    return pl.pallas_call(
        functools.partial(_cmp_select_kernel, tq=TQ_SELECT), grid=(batch, NSA_GROUPS, nq),
        in_specs=[qspec, qspec,
                  pl.BlockSpec((1, 1, rows, LANES), lambda b, g, i: (b, g, 0, 0)),
                  pl.BlockSpec((1, 1, rows, LANES), lambda b, g, i: (b, NSA_GROUPS + g, 0, 0)),
                  _const_spec((LANES, rows))],
        out_specs=[pl.BlockSpec((TQ_SELECT, NSA_REP * HEAD_DIM), lambda b, g, i: (b * nq + i, g)), qspec],
        out_shape=[jax.ShapeDtypeStruct((T, NSA_HEADS * HEAD_DIM), F32),
                   jax.ShapeDtypeStruct((T, NSA_HEADS * LANES), BF16)],
        compiler_params=_cparams(3), name="nsa_cmp_select",
    )(qn, qr, cmp_kv, cmp_kv, _overlap_t(seq))


def _merge_kernel(x_ref, oc_ref, os_ref, ow_ref, gl_ref, of_ref, om_ref, mg_ref,
                  e_ref, wn_ref, wf_ref, wm_ref, wo_ref, o_ref):
    d = x_ref.shape[1]
    nw = oc_ref.shape[1]
    mg = mg_ref[...]
    merged = mg[:, d:2 * d] * _dot(of_ref[...], wf_ref[...]) + mg[:, 2 * d:] * _dot(om_ref[...], wm_ref[...])
    g = _dot(jax.nn.sigmoid(gl_ref[...]).astype(BF16), e_ref[...])
    o_nsa = g[:, :nw] * oc_ref[...] + g[:, nw:2 * nw] * os_ref[...] + g[:, 2 * nw:] * ow_ref[...]
    merged = merged + mg[:, :d] * _dot(o_nsa.astype(BF16), wn_ref[...])
    o_ref[...] = x_ref[...] + _dot(merged.astype(BF16), wo_ref[...])


def _gate_expand():
    e = np.zeros((LANES, 3 * NSA_HEADS * HEAD_DIM), np.float32)
    for h in range(NSA_HEADS):
        for c in range(3):
            e[h * 3 + c, c * NSA_HEADS * HEAD_DIM + h * HEAD_DIM:
              c * NSA_HEADS * HEAD_DIM + (h + 1) * HEAD_DIM] = 1.0
    return jnp.asarray(e, BF16)


def _merge(x2, o_cmp, o_slc, o_win, gate_l, o_fox, o_mla, mg, p):
    T, D = x2.shape
    tok = lambda a: pl.BlockSpec((TM, a.shape[1]), lambda i: (i, 0))
    acts = [x2, o_cmp, o_slc, o_win, gate_l, o_fox, o_mla, mg]
    weights = [_gate_expand(), p["w_br_nsa"].astype(BF16), p["w_br_fox"].astype(BF16),
               p["w_br_mla"].astype(BF16), p["w_out"].astype(BF16)]
    return pl.pallas_call(
        _merge_kernel, grid=(T // TM,),
        in_specs=[tok(a) for a in acts] + [_const_spec(w.shape) for w in weights],
        out_specs=pl.BlockSpec((TM, D), lambda i: (i, 0)),
        out_shape=jax.ShapeDtypeStruct((T, D), F32),
        compiler_params=_cparams(1), name="merge_out",
    )(*acts, *weights)


def _ffn_kernel(x_ref, g_ref, wu_ref, wv_ref, cw_ref, cb_ref, wd_ref, fin_ref, o_ref, tail_ref,
                *, seq, n_chunks, final):
    tm = x_ref.shape[0]
    x = x_ref[...]
    hb = _rms(x, g_ref[...]).astype(BF16)

    @pl.when((pl.program_id(0) * tm) % seq == 0)
    def _():
        tail_ref[...] = jnp.zeros(tail_ref.shape, F32)

    cw = wu_ref.shape[1] // n_chunks
    row8 = lax.broadcasted_iota(jnp.int32, (8, cw), 0)
    acc = jnp.zeros(x.shape, F32)
    for c in range(n_chunks):
        sl = slice(c * cw, (c + 1) * cw)
        u = _dot(hb, wu_ref[:, sl])
        v = _dot(hb, wv_ref[:, sl])
        prev = tail_ref[:, sl]
        tail_ref[:, sl] = u[tm - 8:]
        u1 = pltpu.roll(u, 1, 0)
        u2 = pltpu.roll(u, 2, 0)
        h1 = jnp.where(row8 < 1, pltpu.roll(prev, 1, 0), u1[:8])
        h2 = jnp.where(row8 < 2, pltpu.roll(prev, 2, 0), u2[:8])
        u1 = jnp.concatenate([h1, u1[8:]], axis=0)
        u2 = jnp.concatenate([h2, u2[8:]], axis=0)
        uc = cw_ref[0:1, sl] * u2 + cw_ref[1:2, sl] * u1 + cw_ref[2:3, sl] * u + cb_ref[:, sl]
        act = (uc * jax.nn.sigmoid(uc) * v).astype(BF16)
        acc = acc + _dot(act, wd_ref[sl, :])
    y = x + acc
    if final:
        y = _rms(y, fin_ref[...])
    o_ref[...] = y


def _ffn(x2, p, final_norm, seq, final):
    T, D = x2.shape
    w_up = p["w_up"]
    wu, wv = w_up[:, :D_FF].astype(BF16), w_up[:, D_FF:].astype(BF16)
    row = lambda a: a.reshape(1, -1).astype(F32)
    cwp = jnp.pad(p["conv_w"].astype(F32), ((0, 8 - CONV_W), (0, 0)))
    weights = [row(p["ffn_norm"]), wu, wv, cwp, row(p["conv_b"]), p["w_down"].astype(BF16),
               row(final_norm)]
    return pl.pallas_call(
        functools.partial(_ffn_kernel, seq=seq, n_chunks=2, final=final), grid=(T // TM,),
        in_specs=[pl.BlockSpec((TM, D), lambda i: (i, 0))] + [_const_spec(w.shape) for w in weights],
        out_specs=pl.BlockSpec((TM, D), lambda i: (i, 0)),
        out_shape=jax.ShapeDtypeStruct((T, D), F32),
        scratch_shapes=[pltpu.VMEM((8, D_FF), F32)],
        compiler_params=_cparams(1), name="conv_ffn",
    )(x2, *weights)


_LAYER_PARAMS = ("mix_norm", "w_in", "b_forget", "cmp_pe_k", "cmp_w1_k", "cmp_w2_k", "cmp_pe_v",
                 "cmp_w1_v", "cmp_w2_v", "mla_q_norm", "mla_w_uq", "mla_kv_norm", "mla_w_ukv",
                 "w_br_nsa", "w_br_fox", "w_br_mla", "w_out", "ffn_norm", "w_up", "conv_w",
                 "conv_b", "w_down")


def _mixer_layer(h, tabs, p, batch, seq):
    (qn, qr, kvc, ksl, vsl_t, kwin, vwin_t, gate_l,
     fq, fk, fv_t, mq, mk, mv_t, mg) = _inproj(h, tabs, p, seq)
    cmp_kv = _compress(kvc, p, batch, seq)
    o_cmp, qa = _cmp_select(qn, qr, cmp_kv, batch, seq)
    group = (0,) * NSA_REP
    pair = (0, 1)
    o_slc, o_win = _attention([(qa, ksl, vsl_t, False), (qa, kwin, vwin_t, True)], batch, seq, group,
                              F32, "nsa_slc_win_attn", TQ)
    o_fox, o_mla = _attention([(fq, fk, fv_t, False), (mq, mk, mv_t, False)], batch, seq, pair, BF16,
                              "fox_mla_attn", TQ_PAIRS)
    return _merge(h, o_cmp, o_slc, o_win, gate_l, o_fox, o_mla, mg, p)


def kernel(x, positions, mix_norm, w_in, b_forget, cmp_pe_k, cmp_w1_k, cmp_w2_k, cmp_pe_v, cmp_w1_v,
           cmp_w2_v, mla_q_norm, mla_w_uq, mla_kv_norm, mla_w_ukv, w_br_nsa, w_br_fox, w_br_mla,
           w_out, ffn_norm, w_up, conv_w, conv_b, w_down, final_norm):
    batch, seq, d = x.shape
    stacked = dict(zip(_LAYER_PARAMS, (
        mix_norm, w_in, b_forget, cmp_pe_k, cmp_w1_k, cmp_w2_k, cmp_pe_v, cmp_w1_v, cmp_w2_v,
        mla_q_norm, mla_w_uq, mla_kv_norm, mla_w_ukv, w_br_nsa, w_br_fox, w_br_mla, w_out,
        ffn_norm, w_up, conv_w, conv_b, w_down)))
    depth = w_in.shape[0]
    tabs = _rope_tables(positions)
    h = x.reshape(batch * seq, d)
    for l in range(depth):
        p = {k: v[l] for k, v in stacked.items()}
        h = _mixer_layer(h, tabs, p, batch, seq)
        h = _ffn(h, p, final_norm, seq, final=(l == depth - 1))
    return h.reshape(batch, seq, d)
```

```python
import functools

import numpy as np
import jax
import jax.numpy as jnp
from jax import lax
from jax.experimental import pallas as pl
from jax.experimental.pallas import tpu as pltpu

F32 = jnp.float32
BF16 = jnp.bfloat16

HEAD_DIM = 64
ROPE_THETA = 500000.0
PARTIAL_ROT = HEAD_DIM // 4
NORM_EPS = 1e-6
NSA_HEADS = 8
NSA_GROUPS = 2
NSA_REP = NSA_HEADS // NSA_GROUPS
CMP_LEN = 32
CMP_STRIDE = 16
SLC_LEN = 64
SLC_TOPK = 16
WIN = 512
FOX_HEADS = 8
MLA_HEADS = 8
MLA_Q_LORA = 384
MLA_KV_LORA = 256
MLA_NOPE = 64
MLA_ROPE = 32
MLA_V = 64
D_FF = 2816
D_MODEL = 1024
CONV_W = 3

LANES = 128
LOG2E = 1.4426950408889634
NEG = -1e30
SEL_NEG = -1e9
VMEM_LIMIT = 52 * 1024 * 1024

TM = 256
TQ = 1024
TQ_PAIRS = 2048
TQ_SELECT = 512
TK = 512
QC = 256
LOOP_TILES = 4
SKEW = 2
DECAY_TERMS = 3


def _cparams(n_axes):
    return pltpu.CompilerParams(dimension_semantics=("arbitrary",) * n_axes,
                                vmem_limit_bytes=VMEM_LIMIT)


def _const_spec(shape):
    nd = len(shape)
    return pl.BlockSpec(shape, lambda *_: (0,) * nd, pipeline_mode=pl.Buffered(1))


def _rms(x, g):
    return x * lax.rsqrt(jnp.mean(x * x, axis=-1, keepdims=True) + NORM_EPS) * g


def _dot(a, b):
    return jnp.dot(a, b, preferred_element_type=F32)


def _dot_nt(a, b):
    return lax.dot_general(a, b, (((1,), (1,)), ((), ())), preferred_element_type=F32)


def _rope_tables_kernel(pos_ref, freq_ref, g16_ref, g32_ref, c16_o, s16_o, c32_o, s32_o):
    ang = pos_ref[...].astype(F32) * freq_ref[...]
    cos, sin = jnp.cos(ang), jnp.sin(ang)
    lane = lax.broadcasted_iota(jnp.int32, ang.shape, 1)
    low = lane < PARTIAL_ROT
    high = (lane >= HEAD_DIM) & (lane < HEAD_DIM + PARTIAL_ROT)
    mla = (lane >= MLA_NOPE) & (lane < MLA_NOPE + MLA_ROPE)
    c16_o[...] = jnp.where(low, cos, jnp.where(high, pltpu.roll(cos, HEAD_DIM, 1), 1.0))
    s16_o[...] = jnp.where(low, sin, jnp.where(high, pltpu.roll(sin, HEAD_DIM, 1), 0.0)) * g16_ref[...]
    c32_o[...] = jnp.where(mla, pltpu.roll(cos, MLA_ROPE, 1), 1.0)
    s32_o[...] = jnp.where(mla, pltpu.roll(sin, MLA_ROPE, 1), 0.0) * g32_ref[...]


def _rope_tables(positions):
    T = positions.size
    half16, half32 = PARTIAL_ROT // 2, MLA_ROPE // 2
    inv16 = ROPE_THETA ** (-jnp.arange(half16, dtype=F32) / half16)
    inv32 = ROPE_THETA ** (-jnp.arange(half32, dtype=F32) / half32)
    lane = np.arange(LANES)
    freq = jnp.where(lane < PARTIAL_ROT, inv16[lane % half16],
                     jnp.where((lane >= MLA_ROPE) & (lane < 2 * MLA_ROPE), inv32[lane % half32], 0.0))[None, :]
    d = lane % HEAD_DIM
    g16 = np.where(d < PARTIAL_ROT, np.where(d < half16, -1.0, 1.0), 0.0).astype(np.float32)[None, :]
    r = lane - MLA_NOPE
    g32 = np.where((r >= 0) & (r < MLA_ROPE), np.where(r < half32, -1.0, 1.0), 0.0).astype(np.float32)[None, :]
    pos = jnp.broadcast_to(positions.reshape(T, 1), (T, LANES))
    tt = 1024
    row = pl.BlockSpec((tt, LANES), lambda i: (i, 0))
    vec = _const_spec((1, LANES))
    out = jax.ShapeDtypeStruct((T, LANES), F32)
    return pl.pallas_call(
        _rope_tables_kernel, grid=(T // tt,),
        in_specs=[row, vec, vec, vec], out_specs=[row] * 4, out_shape=[out] * 4,
        compiler_params=_cparams(1), name="rope_tables",
    )(pos, freq, jnp.asarray(g16), jnp.asarray(g32))


def _rot16(x, cos, sin):
    lane = lax.broadcasted_iota(jnp.int32, x.shape, 1)
    first = (lane % HEAD_DIM) < (PARTIAL_ROT // 2)
    sw = jnp.where(first, pltpu.roll(x, LANES - PARTIAL_ROT // 2, 1), pltpu.roll(x, PARTIAL_ROT // 2, 1))
    return x * cos + sw * sin


def _rot32(x, cos, sin):
    lane = lax.broadcasted_iota(jnp.int32, x.shape, 1)
    first = lane < (MLA_NOPE + MLA_ROPE // 2)
    sw = jnp.where(first, pltpu.roll(x, LANES - MLA_ROPE // 2, 1), pltpu.roll(x, MLA_ROPE // 2, 1))
    return x * cos + sw * sin


_FF_LANE = 3 * NSA_HEADS
_IN_COLS = {}
for _name, _n in (("q", NSA_HEADS * HEAD_DIM), ("kvc", 4 * HEAD_DIM), ("ksl", LANES), ("kwin", LANES),
                  ("misc", LANES), ("fq", FOX_HEADS * HEAD_DIM), ("fk", FOX_HEADS * HEAD_DIM),
                  ("lat", MLA_Q_LORA + MLA_KV_LORA), ("mg", 3 * D_MODEL)):
    _IN_COLS[_name] = (sum(n for _, n in _IN_COLS.values()), _n)
_IN_ROWS = {"vsl": (0, LANES), "vwin": (LANES, LANES), "fv": (2 * LANES, FOX_HEADS * HEAD_DIM)}


def _inproj_kernel(x_ref, g_ref, c16_ref, s16_ref, c32_ref, s32_ref,
                   w_ref, wt_ref, bf_ref, tri_ref, place_ref,
                   qnrm_ref, wuq_ref, kvnrm_ref, wuk_ref, wuv_ref,
                   qn_o, qr_o, kvc_o, ksl_o, vsl_o, kwin_o, vwin_o, gate_o,
                   fq_o, fk_o, fv_o, mq_o, mk_o, mv_o, mg_o, fcarry_ref, *, seq):
    tm = x_ref.shape[0]
    seq_start = (pl.program_id(0) * tm) % seq == 0
    hb = _rms(x_ref[...], g_ref[...]).astype(BF16)
    c16, s16 = c16_ref[...], s16_ref[...]
    c32, s32 = c32_ref[...], s32_ref[...]
    lane = lax.broadcasted_iota(jnp.int32, (tm, LANES), 1)
    low = lane < HEAD_DIM

    def proj(name):
        a, n = _IN_COLS[name]
        return _dot(hb, w_ref[:, a:a + n])

    def proj_t(name):
        a, n = _IN_ROWS[name]
        return _dot_nt(wt_ref[a:a + n, :], hb)

    def slabs(z):
        return [z[:, k * LANES:(k + 1) * LANES] for k in range(z.shape[1] // LANES)]

    def split(z, fill=0.0):
        return jnp.where(low, z, fill), jnp.where(low, pltpu.roll(z, HEAD_DIM, 1), fill)

    def store_heads(o_ref, pair_slabs, fill=0.0, extra=None):
        for k, z in enumerate(pair_slabs):
            for e, s in enumerate(split(z, fill)):
                h = 2 * k + e
                if extra is not None:
                    s = s + extra[:, h * LANES:(h + 1) * LANES]
                o_ref[:, h * LANES:(h + 1) * LANES] = s.astype(o_ref.dtype)

    misc = proj("misc")
    gate_o[...] = misc

    @pl.when(seq_start)
    def _():
        fcarry_ref[...] = jnp.zeros(fcarry_ref.shape, F32)

    def bf16_terms(v):
        packed = None
        for t in range(DECAY_TERMS):
            part = v.astype(BF16).astype(F32)
            v = v - part
            part = part if t == 0 else pltpu.roll(part, FOX_HEADS * t, 1)
            packed = part if packed is None else packed + part
        return packed.astype(BF16)

    zf = misc + bf_ref[...]
    is_f = (lane >= _FF_LANE) & (lane < _FF_LANE + FOX_HEADS)
    logf_terms = bf16_terms(
        jnp.where(is_f, jnp.minimum(zf, 0.0) - jnp.log1p(jnp.exp(-jnp.abs(zf))), 0.0))

    zq = slabs(proj("q"))
    store_heads(qn_o, zq)
    store_heads(qr_o, [_rot16(z, c16, s16) for z in zq])

    terms = _dot(tri_ref[...], logf_terms)
    fsum = sum(terms if t == 0 else pltpu.roll(terms, LANES - FOX_HEADS * t, 1)
               for t in range(DECAY_TERMS))
    fsum = jnp.where(is_f, fsum, 0.0) + fcarry_ref[0:1]
    fcarry_ref[...] = jnp.broadcast_to(fsum[tm - 1:tm], fcarry_ref.shape)
    decay_terms = bf16_terms(jnp.where(is_f, -LOG2E * fsum, 0.0))

    for k, z in enumerate(slabs(proj("kvc"))):
        for e, s in enumerate(split(z)):
            kvc_o[2 * k + e] = s[:, :HEAD_DIM]

    tok = (pl.program_id(0) * tm) % seq + lax.broadcasted_iota(jnp.int32, (tm, LANES), 0)
    onehot = jnp.where(lane - HEAD_DIM == tok // SLC_LEN, 1.0, 0.0)
    store_heads(ksl_o, [_rot16(proj("ksl"), c16, s16)], fill=onehot)
    vsl_o[0] = proj_t("vsl").astype(BF16)

    store_heads(kwin_o, [_rot16(proj("kwin"), c16, s16)])
    vwin_o[0] = proj_t("vwin").astype(BF16)

    ones_q = jnp.where((lane >= HEAD_DIM) & (lane < HEAD_DIM + DECAY_TERMS), 1.0, 0.0)
    ones_k = pltpu.roll(ones_q, DECAY_TERMS, 1)
    zfq = slabs(proj("fq"))
    fv_o[0] = proj_t("fv").astype(BF16)
    decay = _dot(decay_terms, place_ref[...])
    grow = jnp.concatenate([pltpu.roll(-d, DECAY_TERMS, 1) for d in slabs(decay)], axis=1)
    store_heads(fq_o, zfq, fill=ones_q, extra=grow)

    lat = proj("lat")
    cq = _rms(lat[:, :MLA_Q_LORA], qnrm_ref[...]).astype(BF16)
    ckv = _rms(lat[:, MLA_Q_LORA:], kvnrm_ref[...]).astype(BF16)
    kpe = jnp.where(low, 0.0, _rot32(misc, c32, s32))
    store_heads(fk_o, slabs(proj("fk")), fill=ones_k, extra=decay)
    zuq = _dot(cq, wuq_ref[...])
    for k, z in enumerate(slabs(zuq)):
        mq_o[:, k * LANES:(k + 1) * LANES] = _rot32(z, c32, s32).astype(BF16)
    zuk = _dot(ckv, wuk_ref[...])
    for k, z in enumerate(slabs(zuk)):
        mk_o[:, k * LANES:(k + 1) * LANES] = (z + kpe).astype(BF16)
    mv_o[0] = _dot_nt(wuv_ref[...], ckv).astype(BF16)

    mg_o[...] = jax.nn.sigmoid(proj("mg"))


def _pad_halves(w, first_only=True):
    K, n = w.shape[0], w.shape[1] // HEAD_DIM
    w = w.reshape(K, n, HEAD_DIM)
    other = jnp.zeros_like(w) if first_only else w
    return jnp.concatenate([w, other], axis=-1).reshape(K, n * LANES)


def _inproj(x2, tabs, p, seq):
    T, D = x2.shape
    w_in = p["w_in"]
    scale = HEAD_DIM ** -0.5 * LOG2E
    o = 0

    def take(n):
        nonlocal o
        w = w_in[:, o:o + n]
        o += n
        return w

    nsa_w, kvw = NSA_HEADS * HEAD_DIM, NSA_GROUPS * HEAD_DIM
    w_q = take(nsa_w)
    w_kc, w_vc, w_ks, w_vs, w_kw, w_vw = [take(kvw) for _ in range(6)]
    w_g = take(3 * NSA_HEADS)
    w_fq, w_fk, w_fv = [take(FOX_HEADS * HEAD_DIM) for _ in range(3)]
    w_ff = take(FOX_HEADS)
    w_cq, w_ckv, w_kr = take(MLA_Q_LORA), take(MLA_KV_LORA), take(MLA_ROPE)
    w_mg = take(3 * D)

    bf = lambda w: w.astype(BF16)
    row = lambda a: a.reshape(1, -1).astype(F32)
    w_misc = jnp.concatenate([w_g, w_ff, jnp.zeros((D, MLA_NOPE - _FF_LANE - FOX_HEADS), F32), w_kr,
                              jnp.zeros((D, LANES - MLA_NOPE - MLA_ROPE), F32)], axis=1)
    w_tok = bf(jnp.concatenate([w_q * scale, w_kc, w_vc, w_ks, w_kw, w_misc, w_fq * scale, w_fk,
                                w_cq, w_ckv, w_mg], axis=1))
    w_tr = bf(jnp.concatenate([w_vs, w_vw, w_fv], axis=1).T)
    b_f = jnp.pad(row(p["b_forget"]), ((0, 0), (_FF_LANE, LANES - _FF_LANE - FOX_HEADS)))
    tri = jnp.asarray(np.tril(np.ones((TM, TM), np.float32)), BF16)
    place = np.zeros((LANES, FOX_HEADS * LANES), np.float32)
    for t in range(DECAY_TERMS):
        for h in range(FOX_HEADS):
            place[_FF_LANE + FOX_HEADS * t + h, h * LANES + HEAD_DIM + t] = 1.0
    place = jnp.asarray(place, BF16)
    dq = MLA_NOPE + MLA_ROPE
    wuq = (p["mla_w_uq"] * (dq ** -0.5 * LOG2E)).reshape(MLA_Q_LORA, MLA_HEADS, dq)
    wuq = bf(jnp.pad(wuq, ((0, 0), (0, 0), (0, LANES - dq))).reshape(MLA_Q_LORA, MLA_HEADS * LANES))
    wukv = p["mla_w_ukv"].reshape(MLA_KV_LORA, MLA_HEADS, MLA_NOPE + MLA_V)
    wuk = bf(_pad_halves(wukv[:, :, :MLA_NOPE].reshape(MLA_KV_LORA, MLA_HEADS * MLA_NOPE)))
    wuv = bf(wukv[:, :, MLA_NOPE:].reshape(MLA_KV_LORA, MLA_HEADS * MLA_V).T)
    assert w_tok.shape[1] == sum(n for _, n in _IN_COLS.values())

    weights = [w_tok, w_tr, b_f, tri, place,
               row(p["mla_q_norm"]), wuq, row(p["mla_kv_norm"]), wuk, wuv]
    n = T // TM
    tok = lambda c: pl.BlockSpec((TM, c), lambda i: (i, 0))
    in_specs = ([tok(D), _const_spec((1, D))] + [tok(LANES)] * 4
                + [_const_spec(w.shape) for w in weights])
    tokens, transposed = "tokens", "transposed"
    outs = [
        (tokens, NSA_HEADS * LANES, BF16), (tokens, NSA_HEADS * LANES, BF16), None,
        (tokens, NSA_GROUPS * LANES, BF16), (transposed, NSA_GROUPS * HEAD_DIM, BF16),
        (tokens, NSA_GROUPS * LANES, BF16), (transposed, NSA_GROUPS * HEAD_DIM, BF16), (tokens, LANES, F32),
        (tokens, FOX_HEADS * LANES, BF16), (tokens, FOX_HEADS * LANES, BF16),
        (transposed, FOX_HEADS * HEAD_DIM, BF16),
        (tokens, MLA_HEADS * LANES, BF16), (tokens, MLA_HEADS * LANES, BF16),
        (transposed, MLA_HEADS * MLA_V, BF16), (tokens, 3 * D, F32)]
    out_shape, out_specs = [], []
    for spec in outs:
        if spec is None:
            out_shape.append(jax.ShapeDtypeStruct((4, T, HEAD_DIM), F32))
            out_specs.append(pl.BlockSpec((4, TM, HEAD_DIM), lambda i: (0, i, 0)))
        elif spec[0] == transposed:
            out_shape.append(jax.ShapeDtypeStruct((n, spec[1], TM), spec[2]))
            out_specs.append(pl.BlockSpec((1, spec[1], TM), lambda i: (i, 0, 0)))
        else:
            out_shape.append(jax.ShapeDtypeStruct((T, spec[1]), spec[2]))
            out_specs.append(tok(spec[1]))
    return pl.pallas_call(
        functools.partial(_inproj_kernel, seq=seq), grid=(n,),
        in_specs=in_specs, out_specs=out_specs, out_shape=out_shape,
        scratch_shapes=[pltpu.VMEM((8, LANES), F32)],
        compiler_params=_cparams(1), name="inproj",
    )(x2, row(p["mix_norm"]), *tabs, *weights)


def _attn_kernel(*refs, tq, tk, q_of_stream, window_of_stream, k_of_head, values_by_unit):
    n_streams, n_q = len(q_of_stream), max(q_of_stream) + 1
    q_refs, kv_refs = refs[:n_q], refs[n_q:n_q + 2 * n_streams]
    outs = refs[n_q + 2 * n_streams:n_q + 3 * n_streams]
    qt_ref, m_ref, l_ref, acc_ref = refs[n_q + 3 * n_streams:]
    i = pl.program_id(2)
    hs = len(k_of_head)
    nh = n_streams * hs
    k_of = lambda h: kv_refs[2 * (h // hs)]
    vt_of = lambda h: kv_refs[2 * (h // hs) + 1]
    qt_of = lambda h: q_of_stream[h // hs] * hs + h % hs
    per_q = tq // tk
    tv = kv_refs[1].shape[2]
    v_sub = tk // tv
    for qi in range(n_q):
        for r in range(hs):
            qt_ref[qi * hs + r] = q_refs[qi][:, r * LANES:(r + 1) * LANES].astype(F32).T.astype(BF16)
    dist0 = (i * tq + lax.broadcasted_iota(jnp.int32, (tk, tq), 1)
             - lax.broadcasted_iota(jnp.int32, (tk, tq), 0))

    m_ref[...] = jnp.full(m_ref.shape, NEG, F32)
    l_ref[...] = jnp.zeros(l_ref.shape, F32)
    acc_ref[...] = jnp.zeros(acc_ref.shape, F32)

    def logits(j, lo, n, h, cols):
        r0 = pl.multiple_of(j * tk + lo, tv)
        kk = k_of_head[h % hs]
        return _dot(k_of(h)[pl.ds(r0, n), kk * LANES:(kk + 1) * LANES], qt_ref[qt_of(h), :, cols])

    def softmax(s, h, cols, mask):
        if mask is not None:
            s = jnp.where(mask, s, NEG)
        m_old = m_ref[h, :, cols]
        m_new = jnp.maximum(m_old, jnp.max(s, axis=0, keepdims=True))
        alpha = jnp.exp2(m_old - m_new)
        p = jnp.exp2(s - m_new)
        l_ref[h, :, cols] = alpha * l_ref[h, :, cols] + jnp.sum(p, axis=0, keepdims=True)
        m_ref[h, :, cols] = m_new
        return p.astype(BF16), alpha

    def accumulate(j, lo, n, h, cols, p, alpha):
        if values_by_unit:
            half = pl.ds(pl.multiple_of(pl.program_id(1) * HEAD_DIM, HEAD_DIM), HEAD_DIM)
        else:
            half = slice((h % 2) * HEAD_DIM, (h % 2 + 1) * HEAD_DIM)
        pv = sum(_dot(vt_of(h)[j * v_sub + lo // tv + c, half, :], p[c * tv:(c + 1) * tv])
                 for c in range(n // tv))
        acc_ref[h, :, cols] = alpha * acc_ref[h, :, cols] + pv

    def sweep(tiles):
        work = []
        for heads, j, n_tiles, mask, span in tiles:
            for h in heads:
                for c in range(tq // QC):
                    lo, n, masked = (0, n_tiles * tk, False) if span is None else span(c)
                    if n > 0:
                        work.append((j, lo, n, mask if masked else None, h, slice(c * QC, (c + 1) * QC)))
        s_of, p_of = {}, {}
        for t in range(len(work) + 2 * SKEW):
            if 0 <= t - 2 * SKEW < len(work):
                j, lo, n, _, h, cols = work[t - 2 * SKEW]
                accumulate(j, lo, n, h, cols, *p_of.pop(t - 2 * SKEW))
            if 0 <= t - SKEW < len(work):
                _, lo, n, mask, h, cols = work[t - SKEW]
                p_of[t - SKEW] = softmax(s_of.pop(t - SKEW), h, cols,
                                         None if mask is None else mask(slice(lo, lo + n), cols))
            if t < len(work):
                j, lo, n, _, h, cols = work[t]
                s_of[t] = logits(j, lo, n, h, cols)

    def visible(j, window, valid=None):
        def mask(keys, cols):
            dist = dist0[keys, cols] - j * tk
            if not window:
                return dist >= 0
            if valid is not None:
                dist = dist + jnp.where(valid, 0, WIN)
            return lax.bitcast_convert_type(dist, jnp.uint32) < WIN
        return mask

    assert tk == 2 * QC and WIN % tk == 0 and LOOP_TILES % per_q == 0
    causal = [h for h in range(nh) if not window_of_stream[h // hs]]
    windowed = [h for h in range(nh) if window_of_stream[h // hs]]
    first = i * per_q

    def causal_span(d):
        def span(c):
            hi = min(tk, (c + 1) * QC - d * tk)
            return 0, hi, d * tk + hi > c * QC + 1
        return span

    def window_span(d):
        def span(c):
            lo = max(0, (c * QC - WIN + 1) // tv * tv - d * tk)
            return lo, min(tk, (c + 1) * QC - d * tk) - lo, True
        return span

    one_step = len(causal) * (tq // QC) >= 8

    def unmasked(j0, n):
        tiles = []
        while n > 0:
            k = 2 if (one_step and n >= 2) else 1
            tiles.append((causal, j0, k, None, None))
            j0, n = j0 + k, n - k
        return tiles

    def last_tiles(n_left):
        tiles = []
        if causal:
            tiles += unmasked(first - n_left, n_left)
            tiles += [(causal, first + d, 1, visible(first + d, False), causal_span(d)) for d in range(per_q)]
        if windowed:
            tiles += [(windowed, first + d, 1, visible(first + d, True), window_span(d))
                      for d in reversed(range(per_q))]
            tiles += [(windowed, jnp.maximum(first + d, 0), 1,
                       visible(first + d, True, valid=first + d >= 0), window_span(d))
                      for d in range(-(WIN // tk), 0)]
        return tiles

    if causal:
        def body(jj, c):
            sweep(unmasked(LOOP_TILES * jj, LOOP_TILES))
            return c
        lax.fori_loop(0, first // LOOP_TILES, body, 0)
        for n_left in range(0, LOOP_TILES, per_q):
            @pl.when(first % LOOP_TILES == n_left)
            def _():
                sweep(last_tiles(n_left))
    else:
        sweep(last_tiles(0))

    for h in range(0, nh, 2):
        a = acc_ref[h] * (1.0 / l_ref[h])
        b = acc_ref[h + 1] * (1.0 / l_ref[h + 1])
        o_ref, k = outs[h // hs], (h % hs) // 2
        o_ref[:, k * LANES:(k + 1) * LANES] = jnp.concatenate([a, b], axis=0).T.astype(o_ref.dtype)


def _attention(streams, batch, seq, k_of_head, out_dtype, name, tq):
    qs = []
    for q, _, _, _ in streams:
        if not any(q is x for x in qs):
            qs.append(q)
    q_of_stream = tuple(next(n for n, x in enumerate(qs) if x is q) for q, _, _, _ in streams)
    T = qs[0].shape[0]
    nq = seq // tq
    hs, nk = len(k_of_head), max(k_of_head) + 1
    nh = hs * len(streams)
    units = qs[0].shape[1] // (hs * LANES)
    by_unit = nk == 1
    assert not by_unit or units * HEAD_DIM == LANES, "GQA value blocks must fill one slab"
    in_specs = [pl.BlockSpec((tq, hs * LANES), lambda b, u, i: (b * nq + i, u)) for _ in qs]
    args = list(qs)
    for q, k, vt, _ in streams:
        assert q.shape[1] == units * hs * LANES and k.shape[1] == units * nk * LANES
        assert vt.shape == (T // TM, units * (HEAD_DIM if by_unit else LANES), TM)
        in_specs += [pl.BlockSpec((seq, nk * LANES), lambda b, u, i: (b, u)),
                     pl.BlockSpec((seq // TM, LANES, TM),
                                  (lambda b, u, i: (b, 0, 0)) if by_unit else (lambda b, u, i: (b, u, 0)))]
        args += [k, vt]
    return pl.pallas_call(
        functools.partial(_attn_kernel, tq=tq, tk=TK, q_of_stream=q_of_stream,
                          window_of_stream=tuple(w for _, _, _, w in streams),
                          k_of_head=k_of_head, values_by_unit=by_unit),
        grid=(batch, units, nq), in_specs=in_specs,
        out_specs=[pl.BlockSpec((tq, hs // 2 * LANES), lambda b, u, i: (b * nq + i, u))] * len(streams),
        out_shape=[jax.ShapeDtypeStruct((T, units * hs // 2 * LANES), out_dtype)] * len(streams),
        scratch_shapes=[pltpu.VMEM((len(qs) * hs, LANES, tq), BF16), pltpu.VMEM((nh, 1, tq), F32),
                        pltpu.VMEM((nh, 1, tq), F32), pltpu.VMEM((nh, HEAD_DIM, tq), F32)],
        compiler_params=_cparams(3), name=name,
    )(*args)


def _gelu_tanh(x):
    return 0.5 * x * (1.0 + jnp.tanh(np.float32(np.sqrt(2.0 / np.pi)) * (x + 0.044715 * (x * x * x))))


def _compress_kernel(x_ref, pe_ref, w1_ref, w2_ref, o_ref):
    x = x_ref[0]
    half = x.shape[1]
    top = _dot((x + pe_ref[0, :, :half]).astype(BF16), w1_ref[0, :half, :])
    bot = _dot((x + pe_ref[0, :, half:]).astype(BF16), w1_ref[0, half:, :])
    hid = top + pltpu.roll(bot, bot.shape[0] - 1, 0)
    o_ref[0, 0] = _dot(_gelu_tanh(hid).astype(BF16), w2_ref[0]).astype(o_ref.dtype)


def _compress(kvc, p, batch, seq):
    T = kvc.shape[1]
    rows = seq // CMP_STRIDE
    x = kvc.reshape(4, T // CMP_STRIDE, CMP_STRIDE * HEAD_DIM)
    pe = jnp.stack([p["cmp_pe_k"].reshape(1, -1), p["cmp_pe_v"].reshape(1, -1)]).astype(F32)
    w1 = jnp.stack([p["cmp_w1_k"], p["cmp_w1_v"]]).astype(BF16)
    w2k = jnp.pad(p["cmp_w2_k"], ((0, 0), (0, LANES - HEAD_DIM)))
    w2v = jnp.concatenate([p["cmp_w2_v"], p["cmp_w2_v"]], axis=1)
    w2 = jnp.stack([w2k, w2v]).astype(BF16)
    return pl.pallas_call(
        _compress_kernel, grid=(4, batch),
        in_specs=[pl.BlockSpec((1, rows, x.shape[2]), lambda j, b: (j, b, 0)),
                  pl.BlockSpec((1,) + pe.shape[1:], lambda j, b: (j // 2, 0, 0)),
                  pl.BlockSpec((1,) + w1.shape[1:], lambda j, b: (j // 2, 0, 0)),
                  pl.BlockSpec((1,) + w2.shape[1:], lambda j, b: (j // 2, 0, 0))],
        out_specs=pl.BlockSpec((1, 1, rows, LANES), lambda j, b: (b, j, 0, 0)),
        out_shape=jax.ShapeDtypeStruct((batch, 4, rows, LANES), BF16),
        compiler_params=_cparams(2), name="nsa_compress",
    )(x, pe, w1, w2)


def _cmp_select_kernel(qn_ref, qr_ref, kc_ref, vc_ref, ovt_ref, o_ref, qa_ref, *, tq):
    i = pl.program_id(2)
    n_cmp = kc_ref.shape[2]
    kc, vc = kc_ref[0, 0], vc_ref[0, 0]
    t_row = i * tq + lax.broadcasted_iota(jnp.int32, (tq, n_cmp), 0)
    blk_end = CMP_STRIDE * lax.broadcasted_iota(jnp.int32, (tq, n_cmp), 1) + (CMP_LEN - 1)
    vis = blk_end <= t_row
    lane = lax.broadcasted_iota(jnp.int32, (tq, LANES), 1)
    sees_any = t_row[:, :1] >= CMP_LEN - 1
    psum = jnp.zeros((tq, n_cmp), F32)
    outs = []
    for r in range(NSA_REP):
        s = jnp.where(vis, _dot_nt(qn_ref[:, r * LANES:(r + 1) * LANES], kc), NEG)
        m = jnp.max(s, axis=-1, keepdims=True)
        e = jnp.exp2(s - m)
        pn = e * jnp.where(sees_any, 1.0 / jnp.sum(e, axis=-1, keepdims=True), 0.0)
        psum = psum + pn
        outs.append(_dot(pn.astype(BF16), vc))
    for k in range(NSA_REP // 2):
        o_ref[:, k * LANES:(k + 1) * LANES] = jnp.where(lane < HEAD_DIM, outs[2 * k], outs[2 * k + 1])

    p_hi = psum.astype(BF16)
    p_lo = (psum - p_hi.astype(F32)).astype(BF16)
    ovt = ovt_ref[...]
    imp = _dot_nt(ovt, p_hi) + _dot_nt(ovt, p_lo)
    n_slc = LANES - HEAD_DIM
    jb = lax.broadcasted_iota(jnp.int32, (LANES, tq), 0) - HEAD_DIM
    cur = (i * tq + lax.broadcasted_iota(jnp.int32, (LANES, tq), 1)) // SLC_LEN
    forced = (jb == 0) | (jb == cur) | (jb == cur - 1)
    imp = jnp.where(forced, 1e9, jnp.where(jb > cur, -1e9, imp))
    groups = [imp[HEAD_DIM + 8 * v:HEAD_DIM + 8 * (v + 1)] for v in range(n_slc // 8)]
    sub = lax.broadcasted_iota(jnp.int32, (8, tq), 0)
    ranks = [jnp.zeros((8, tq), F32) for _ in groups]
    for a in range(n_slc):
        row = jnp.broadcast_to(groups[a // 8][a % 8:a % 8 + 1], (8, tq))
        for v in range(len(groups)):
            if v > a // 8:
                inc = jnp.where(row >= groups[v], 1.0, 0.0)
            elif v < a // 8:
                inc = jnp.where(row > groups[v], 1.0, 0.0)
            else:
                tie = jnp.where(sub > a % 8, 1.0, 0.0)
                inc = jnp.where(row > groups[v], 1.0, jnp.where(row == groups[v], tie, 0.0))
            ranks[v] = ranks[v] + inc
    bias_t = jnp.concatenate(
        [jnp.zeros((HEAD_DIM, tq), F32)]
        + [jnp.where(rk < SLC_TOPK, 0.0, SEL_NEG) for rk in ranks], axis=0)
    bias = bias_t.T
    for r in range(NSA_REP):
        qa_ref[:, r * LANES:(r + 1) * LANES] = (
            qr_ref[:, r * LANES:(r + 1) * LANES].astype(F32) + bias).astype(BF16)


def _overlap_t(seq):
    n_cmp = (seq - CMP_LEN) // CMP_STRIDE + 1
    n_slc = seq // SLC_LEN
    assert n_slc == LANES - HEAD_DIM, "selection blocks must fill the bias half of a slab"
    c0 = CMP_STRIDE * np.arange(n_cmp)[:, None]
    s0 = SLC_LEN * np.arange(n_slc)[None, :]
    ov = np.clip(np.minimum(c0 + CMP_LEN, s0 + SLC_LEN) - np.maximum(c0, s0), 0, None) / CMP_LEN
    out = np.zeros((LANES, seq // CMP_STRIDE), np.float32)
    out[HEAD_DIM:, :n_cmp] = ov.T
    return jnp.asarray(out, BF16)


def _cmp_select(qn, qr, cmp_kv, batch, seq):
    T = qn.shape[0]
    nq = seq // TQ_SELECT
    rows = seq // CMP_STRIDE
    gw = NSA_REP * LANES
    qspec = pl.BlockSpec((TQ_SELECT, gw), lambda b, g, i: (b * nq + i, g))
    return pl.pallas_call(
        functools.partial(_cmp_select_kernel, tq=TQ_SELECT), grid=(batch, NSA_GROUPS, nq),
        in_specs=[qspec, qspec,
                  pl.BlockSpec((1, 1, rows, LANES), lambda b, g, i: (b, g, 0, 0)),
                  pl.BlockSpec((1, 1, rows, LANES), lambda b, g, i: (b, NSA_GROUPS + g, 0, 0)),
                  _const_spec((LANES, rows))],
        out_specs=[pl.BlockSpec((TQ_SELECT, NSA_REP * HEAD_DIM), lambda b, g, i: (b * nq + i, g)), qspec],
        out_shape=[jax.ShapeDtypeStruct((T, NSA_HEADS * HEAD_DIM), F32),
                   jax.ShapeDtypeStruct((T, NSA_HEADS * LANES), BF16)],
        compiler_params=_cparams(3), name="nsa_cmp_select",
    )(qn, qr, cmp_kv, cmp_kv, _overlap_t(seq))


def _merge_kernel(x_ref, oc_ref, os_ref, ow_ref, gl_ref, of_ref, om_ref, mg_ref,
                  e_ref, wn_ref, wf_ref, wm_ref, wo_ref, o_ref):
    d = x_ref.shape[1]
    nw = oc_ref.shape[1]
    mg = mg_ref[...]
    merged = mg[:, d:2 * d] * _dot(of_ref[...], wf_ref[...]) + mg[:, 2 * d:] * _dot(om_ref[...], wm_ref[...])
    g = _dot(jax.nn.sigmoid(gl_ref[...]).astype(BF16), e_ref[...])
    o_nsa = g[:, :nw] * oc_ref[...] + g[:, nw:2 * nw] * os_ref[...] + g[:, 2 * nw:] * ow_ref[...]
    merged = merged + mg[:, :d] * _dot(o_nsa.astype(BF16), wn_ref[...])
    o_ref[...] = x_ref[...] + _dot(merged.astype(BF16), wo_ref[...])


def _gate_expand():
    e = np.zeros((LANES, 3 * NSA_HEADS * HEAD_DIM), np.float32)
    for h in range(NSA_HEADS):
        for c in range(3):
            e[h * 3 + c, c * NSA_HEADS * HEAD_DIM + h * HEAD_DIM:
              c * NSA_HEADS * HEAD_DIM + (h + 1) * HEAD_DIM] = 1.0
    return jnp.asarray(e, BF16)


def _merge(x2, o_cmp, o_slc, o_win, gate_l, o_fox, o_mla, mg, p):
    T, D = x2.shape
    tok = lambda a: pl.BlockSpec((TM, a.shape[1]), lambda i: (i, 0))
    acts = [x2, o_cmp, o_slc, o_win, gate_l, o_fox, o_mla, mg]
    weights = [_gate_expand(), p["w_br_nsa"].astype(BF16), p["w_br_fox"].astype(BF16),
               p["w_br_mla"].astype(BF16), p["w_out"].astype(BF16)]
    return pl.pallas_call(
        _merge_kernel, grid=(T // TM,),
        in_specs=[tok(a) for a in acts] + [_const_spec(w.shape) for w in weights],
        out_specs=pl.BlockSpec((TM, D), lambda i: (i, 0)),
        out_shape=jax.ShapeDtypeStruct((T, D), F32),
        compiler_params=_cparams(1), name="merge_out",
    )(*acts, *weights)


def _ffn_kernel(x_ref, g_ref, wu_ref, wv_ref, cw_ref, cb_ref, wd_ref, fin_ref, o_ref, tail_ref,
                *, seq, n_chunks, final):
    tm = x_ref.shape[0]
    x = x_ref[...]
    hb = _rms(x, g_ref[...]).astype(BF16)

    @pl.when((pl.program_id(0) * tm) % seq == 0)
    def _():
        tail_ref[...] = jnp.zeros(tail_ref.shape, F32)

    cw = wu_ref.shape[1] // n_chunks
    row8 = lax.broadcasted_iota(jnp.int32, (8, cw), 0)
    acc = jnp.zeros(x.shape, F32)
    for c in range(n_chunks):
        sl = slice(c * cw, (c + 1) * cw)
        u = _dot(hb, wu_ref[:, sl])
        v = _dot(hb, wv_ref[:, sl])
        prev = tail_ref[:, sl]
        tail_ref[:, sl] = u[tm - 8:]
        u1 = pltpu.roll(u, 1, 0)
        u2 = pltpu.roll(u, 2, 0)
        h1 = jnp.where(row8 < 1, pltpu.roll(prev, 1, 0), u1[:8])
        h2 = jnp.where(row8 < 2, pltpu.roll(prev, 2, 0), u2[:8])
        u1 = jnp.concatenate([h1, u1[8:]], axis=0)
        u2 = jnp.concatenate([h2, u2[8:]], axis=0)
        uc = cw_ref[0:1, sl] * u2 + cw_ref[1:2, sl] * u1 + cw_ref[2:3, sl] * u + cb_ref[:, sl]
        act = (uc * jax.nn.sigmoid(uc) * v).astype(BF16)
        acc = acc + _dot(act, wd_ref[sl, :])
    y = x + acc
    if final:
        y = _rms(y, fin_ref[...])
    o_ref[...] = y


def _ffn(x2, p, final_norm, seq, final):
    T, D = x2.shape
    w_up = p["w_up"]
    wu, wv = w_up[:, :D_FF].astype(BF16), w_up[:, D_FF:].astype(BF16)
    row = lambda a: a.reshape(1, -1).astype(F32)
    cwp = jnp.pad(p["conv_w"].astype(F32), ((0, 8 - CONV_W), (0, 0)))
    weights = [row(p["ffn_norm"]), wu, wv, cwp, row(p["conv_b"]), p["w_down"].astype(BF16),
               row(final_norm)]
    return pl.pallas_call(
        functools.partial(_ffn_kernel, seq=seq, n_chunks=2, final=final), grid=(T // TM,),
        in_specs=[pl.BlockSpec((TM, D), lambda i: (i, 0))] + [_const_spec(w.shape) for w in weights],
        out_specs=pl.BlockSpec((TM, D), lambda i: (i, 0)),
        out_shape=jax.ShapeDtypeStruct((T, D), F32),
        scratch_shapes=[pltpu.VMEM((8, D_FF), F32)],
        compiler_params=_cparams(1), name="conv_ffn",
    )(x2, *weights)


_LAYER_PARAMS = ("mix_norm", "w_in", "b_forget", "cmp_pe_k", "cmp_w1_k", "cmp_w2_k", "cmp_pe_v",
                 "cmp_w1_v", "cmp_w2_v", "mla_q_norm", "mla_w_uq", "mla_kv_norm", "mla_w_ukv",
                 "w_br_nsa", "w_br_fox", "w_br_mla", "w_out", "ffn_norm", "w_up", "conv_w",
                 "conv_b", "w_down")


def _mixer_layer(h, tabs, p, batch, seq):
    (qn, qr, kvc, ksl, vsl_t, kwin, vwin_t, gate_l,
     fq, fk, fv_t, mq, mk, mv_t, mg) = _inproj(h, tabs, p, seq)
    cmp_kv = _compress(kvc, p, batch, seq)
    o_cmp, qa = _cmp_select(qn, qr, cmp_kv, batch, seq)
    group = (0,) * NSA_REP
    pair = (0, 1)
    o_slc, o_win = _attention([(qa, ksl, vsl_t, False), (qa, kwin, vwin_t, True)], batch, seq, group,
                              F32, "nsa_slc_win_attn", TQ)
    o_fox, o_mla = _attention([(fq, fk, fv_t, False), (mq, mk, mv_t, False)], batch, seq, pair, BF16,
                              "fox_mla_attn", TQ_PAIRS)
    return _merge(h, o_cmp, o_slc, o_win, gate_l, o_fox, o_mla, mg, p)


def kernel(x, positions, mix_norm, w_in, b_forget, cmp_pe_k, cmp_w1_k, cmp_w2_k, cmp_pe_v, cmp_w1_v,
           cmp_w2_v, mla_q_norm, mla_w_uq, mla_kv_norm, mla_w_ukv, w_br_nsa, w_br_fox, w_br_mla,
           w_out, ffn_norm, w_up, conv_w, conv_b, w_down, final_norm):
    batch, seq, d = x.shape
    stacked = dict(zip(_LAYER_PARAMS, (
        mix_norm, w_in, b_forget, cmp_pe_k, cmp_w1_k, cmp_w2_k, cmp_pe_v, cmp_w1_v, cmp_w2_v,
        mla_q_norm, mla_w_uq, mla_kv_norm, mla_w_ukv, w_br_nsa, w_br_fox, w_br_mla, w_out,
        ffn_norm, w_up, conv_w, conv_b, w_down)))
    depth = w_in.shape[0]
    tabs = _rope_tables(positions)
    h = x.reshape(batch * seq, d)
    for l in range(depth):
        p = {k: v[l] for k, v in stacked.items()}
        h = _mixer_layer(h, tabs, p, batch, seq)
        h = _ffn(h, p, final_norm, seq, final=(l == depth - 1))
    return h.reshape(batch, seq, d)
```
